```python
import math
import jax, jax.numpy as jnp
from jax import lax
import numpy as np

D_MODEL = 1024
BATCH = 2
SEQ = 16384
DEPTH = 1
DEC_BATCH = 8
DEC_SEQ = 4096
PAST_LEN = 128

ATTN_GROUPS = ((128, 1), (512, 4), (2048, 16))
N_GROUPS = 3
HEADS_PER_GROUP = 8
N_HEADS_A = N_GROUPS * HEADS_PER_GROUP
HEAD_DIM_A = 64
ATTN_QBLOCK = 64
W_A = N_HEADS_A * HEAD_DIM_A
D_ATTN_OUT = HEADS_PER_GROUP * HEAD_DIM_A
N_HEADS_B = 4
DK_B = 128
DV_B = 128
HGRN_CHUNK = 64
D_HGRN_OUT = N_HEADS_B * DV_B
IN_SPLITS = (W_A, W_A, W_A, N_HEADS_B * DK_B, N_HEADS_B * DK_B, N_HEADS_B * DK_B,
             N_HEADS_B * DV_B, N_HEADS_B * DV_B)
D_IN = sum(IN_SPLITS)
N_EXPERTS = 32
TOP_K = 4
D_FF = 1024
SWIGLU_LIMIT = 7.0
SWIGLU_ALPHA = 1.702
MOE_BLOCK = 128
EPS = 1e-6

kernel_name = "hybrid_dilated_attn_hgrn2_moe_encoder"


def rms_norm(x, g):
    xf = x.astype(jnp.float32)
    xf = xf * lax.rsqrt(jnp.mean(xf * xf, axis=-1, keepdims=True) + EPS)
    return (xf * g.astype(jnp.float32)).astype(x.dtype)


def alibi_slopes(n):
    return jnp.asarray(2.0 ** (-8.0 * (np.arange(n) + 1) / n), dtype=jnp.float32)


def dilated_window_attention(q, k, v, dilation, n_side, slopes):
    B, S, H, Dh = q.shape
    L = S // dilation

    def to_sub(t):
        return t.reshape(B, L, dilation, H, Dh).transpose(0, 2, 1, 3, 4).reshape(B * dilation, L, H, Dh)

    qs, ks, vs = to_sub(q), to_sub(k), to_sub(v)
    blk = math.gcd(L, ATTN_QBLOCK)
    nblk = L // blk
    win = blk + 2 * n_side
    pad = ((0, 0), (n_side, n_side), (0, 0), (0, 0))
    kp, vp = jnp.pad(ks, pad), jnp.pad(vs, pad)
    idx = jnp.arange(nblk)[:, None] * blk + jnp.arange(win)[None, :]
    kg, vg = kp[:, idx], vp[:, idx]
    qb = qs.reshape(B * dilation, nblk, blk, H, Dh)
    logits = jnp.einsum('bnqhd,bnkhd->bnhqk', qb, kg,
                        preferred_element_type=jnp.float32) / math.sqrt(Dh)
    rel = jnp.arange(win)[None, :] - jnp.arange(blk)[:, None] - n_side
    kpos = idx - n_side
    valid = (jnp.abs(rel) <= n_side)[None] & ((kpos >= 0) & (kpos < L))[:, None, :]
    alibi = -slopes[:, None, None] * (dilation * jnp.abs(rel)).astype(jnp.float32)
    logits = jnp.where(valid[None, :, None], logits + alibi[None, None], -jnp.inf)
    lse = jax.nn.logsumexp(logits, axis=-1)
    p = jnp.exp(logits - lse[..., None])
    o = jnp.einsum('bnhqk,bnkhd->bnqhd', p.astype(v.dtype), vg)
    o = o.reshape(B, dilation, L, H, Dh).transpose(0, 2, 1, 3, 4).reshape(B, S, H, Dh)
    lse = lse.transpose(0, 1, 3, 2).reshape(B, dilation, L, H).transpose(0, 2, 1, 3).reshape(B, S, H)
    return o, lse


def attention_branch(qa, ka, va, q_gain, k_gain):
    B, S = qa.shape[:2]
    qa, ka = rms_norm(qa, q_gain), rms_norm(ka, k_gain)
    slopes = alibi_slopes(N_HEADS_A)
    outs, lses = [], []
    for g, (window, dil) in enumerate(ATTN_GROUPS):
        sl = slice(g * HEADS_PER_GROUP, (g + 1) * HEADS_PER_GROUP)
        o, l = dilated_window_attention(qa[:, :, sl], ka[:, :, sl], va[:, :, sl],
                                        dil, window // (2 * dil), slopes[sl])
        outs.append(o)
        lses.append(l)
    o = jnp.stack(outs, 0)
    wts = jax.nn.softmax(jnp.stack(lses, 0), axis=0)
    o = jnp.sum(wts[..., None] * o.astype(jnp.float32), axis=0)
    return o.reshape(B, S, D_ATTN_OUT).astype(qa.dtype)


def hgrn2_scan(q, k, v, log_f):
    B, S, H, DK = q.shape
    DV = v.shape[-1]
    C = HGRN_CHUNK
    nc = S // C

    def chunks(t):
        return t.astype(jnp.float32).reshape(B, nc, C, H, t.shape[-1]).transpose(1, 0, 3, 2, 4)

    qc, kc, vc = chunks(q), chunks(k), chunks(v)
    bc = jnp.cumsum(chunks(log_f), axis=3)
    tri = jnp.tril(jnp.ones((C, C), dtype=bool))

    def step(state, xs):
        q_, k_, v_, b_ = xs
        inter = jnp.einsum('bhtk,bhkv->bhtv', q_ * jnp.exp(b_), state)
        diff = b_[:, :, :, None, :] - b_[:, :, None, :, :]
        decay = jnp.exp(jnp.where(tri[:, :, None], diff, -jnp.inf))
        scores = jnp.einsum('bhtk,bhtsk,bhsk->bhts', q_, decay, k_)
        intra = jnp.einsum('bhts,bhsv->bhtv', scores, v_)
        b_last = b_[:, :, -1:, :]
        state = (jnp.exp(b_last[:, :, 0, :])[..., None] * state
                 + jnp.einsum('bhsk,bhsv->bhkv', k_ * jnp.exp(b_last - b_), v_))
        return state, inter + intra

    s0 = jnp.zeros((B, H, DK, DV), jnp.float32)
    _, o = lax.scan(step, s0, (qc, kc, vc, bc))
    return o.transpose(1, 0, 3, 2, 4).reshape(B, S, H, DV)


def hgrn2_branch(qb, ff, fb, ib, og, lb_f, lb_b, o_gain):
    B, S = qb.shape[:2]
    q = jax.nn.silu(qb.astype(jnp.float32)) * (DK_B ** -0.5)

    def gates(z, lb):
        f = lb + (1.0 - lb) * jax.nn.sigmoid(z.astype(jnp.float32))
        return jnp.log(f), 1.0 - f

    logf_f, k_f = gates(ff, lb_f)
    logf_b, k_b = gates(fb, lb_b)
    o_fwd = hgrn2_scan(q, k_f, ib, logf_f)
    o_bwd = jnp.flip(hgrn2_scan(jnp.flip(q, 1), jnp.flip(k_b, 1), jnp.flip(ib, 1),
                                jnp.flip(logf_b, 1)), axis=1)
    o = rms_norm(o_fwd + o_bwd, o_gain) * jax.nn.silu(og.astype(jnp.float32))
    return o.reshape(B, S, D_HGRN_OUT).astype(qb.dtype)


def moe_ffn(x, w_router, b_router, w_gu, b_gu, w_down, b_down):
    T, D = x.shape
    logits = x.astype(jnp.float32) @ w_router.astype(jnp.float32) + b_router.astype(jnp.float32)
    top_vals, top_idx = lax.top_k(logits, TOP_K)
    gate_w = jax.nn.softmax(top_vals, axis=-1)
    n = T * TOP_K
    e_flat = top_idx.reshape(n).astype(jnp.int32)
    g_flat = gate_w.reshape(n)
    tok_flat = jnp.arange(n, dtype=jnp.int32) // TOP_K
    order = jnp.argsort(e_flat, stable=True)
    e_sorted = e_flat[order]
    sizes = jnp.bincount(e_flat, length=N_EXPERTS)
    starts = jnp.cumsum(sizes) - sizes
    pad_sizes = (sizes + MOE_BLOCK - 1) // MOE_BLOCK * MOE_BLOCK
    pad_ends = jnp.cumsum(pad_sizes)
    pad_starts = pad_ends - pad_sizes
    dest = pad_starts[e_sorted] + (jnp.arange(n) - starts[e_sorted])
    n_blocks = -(-n // MOE_BLOCK) + N_EXPERTS
    P = n_blocks * MOE_BLOCK
    buf_tok = jnp.full((P,), T, jnp.int32).at[dest].set(tok_flat[order])
    buf_w = jnp.zeros((P,), jnp.float32).at[dest].set(g_flat[order])
    block_e = jnp.minimum(jnp.searchsorted(pad_ends, jnp.arange(n_blocks) * MOE_BLOCK, side='right'),
                          N_EXPERTS - 1)
    xp = jnp.concatenate([x, jnp.zeros((1, D), x.dtype)], axis=0)
    xb = xp[buf_tok].reshape(n_blocks, MOE_BLOCK, D)

    def expert_block(args):
        xblk, e = args
        h = xblk @ w_gu[e] + b_gu[e]
        gate, up = h[:, :D_FF], h[:, D_FF:]
        gate = jnp.minimum(gate, SWIGLU_LIMIT)
        up = jnp.clip(up, -SWIGLU_LIMIT, SWIGLU_LIMIT)
        glu = gate * jax.nn.sigmoid(SWIGLU_ALPHA * gate)
        return ((up + 1.0) * glu) @ w_down[e] + b_down[e]

    yb = lax.map(expert_block, (xb, block_e)).reshape(P, D)
    y = jnp.zeros((T + 1, D), jnp.float32).at[buf_tok].add(yb.astype(jnp.float32) * buf_w[:, None])
    return y[:T].astype(x.dtype)


def encoder_layer(x, l, norm_mix, w_in, q_gain, k_gain, hgrn_lb, hgrn_o_gain, w_gate,
                  w_proj_a, w_proj_b, w_out, norm_moe, w_router, b_router, w_gu, b_gu, w_down, b_down):
    B, S, D = x.shape
    xn = rms_norm(x, norm_mix[l])
    proj = xn @ w_in[l]
    qa, ka, va, qb, ff, fb, ib, og = jnp.split(proj, [int(c) for c in np.cumsum(IN_SPLITS)[:-1]], axis=-1)
    ha = (B, S, N_HEADS_A, HEAD_DIM_A)
    hk = (B, S, N_HEADS_B, DK_B)
    hv = (B, S, N_HEADS_B, DV_B)
    attn_out = attention_branch(qa.reshape(ha), ka.reshape(ha), va.reshape(ha), q_gain[l], k_gain[l])
    lb = jnp.cumsum(jax.nn.softmax(hgrn_lb.astype(jnp.float32), axis=1), axis=1)
    hgrn_out = hgrn2_branch(qb.reshape(hk), ff.reshape(hk), fb.reshape(hk), ib.reshape(hv), og.reshape(hv),
                            lb[0, l].reshape(N_HEADS_B, DK_B), lb[1, l].reshape(N_HEADS_B, DK_B),
                            hgrn_o_gain[l])
    gates = jax.nn.sigmoid((xn @ w_gate[l]).astype(jnp.float32))
    mixed = gates[..., :D] * (attn_out @ w_proj_a[l]) + gates[..., D:] * (hgrn_out @ w_proj_b[l])
    h = x + mixed.astype(x.dtype) @ w_out[l]
    hn = rms_norm(h, norm_moe[l])
    y = moe_ffn(hn.reshape(B * S, D), w_router[l], b_router[l], w_gu[l], b_gu[l], w_down[l], b_down[l])
    return h + y.reshape(B, S, D)


def setup_inputs(seed: int = 0) -> dict:
    key = jax.random.key(seed)
    ks = jax.random.split(key, 20)
    f32 = jnp.float32

    def nrm(k, shape, scale):
        return jax.random.normal(k, shape, f32) * scale

    D = D_MODEL
    return {
        "x_prompt": nrm(ks[0], (BATCH, SEQ, D), 1.0),
        "x_sample": nrm(ks[1], (DEC_BATCH, DEC_SEQ, D), 1.0),
        "norm_mix": 1.0 + nrm(ks[2], (DEPTH, D), 0.02),
        "w_in": nrm(ks[3], (DEPTH, D, D_IN), D ** -0.5),
        "q_gain": 1.0 + nrm(ks[4], (DEPTH, N_HEADS_A, HEAD_DIM_A), 0.02),
        "k_gain": 1.0 + nrm(ks[5], (DEPTH, N_HEADS_A, HEAD_DIM_A), 0.02),
        "hgrn_lb": nrm(ks[6], (2, DEPTH + 1, N_HEADS_B * DK_B), 0.1),
        "hgrn_o_gain": 1.0 + nrm(ks[7], (DEPTH, N_HEADS_B, DV_B), 0.02),
        "w_gate": nrm(ks[8], (DEPTH, D, 2 * D), D ** -0.5),
        "w_proj_a": nrm(ks[9], (DEPTH, D_ATTN_OUT, D), D_ATTN_OUT ** -0.5),
        "w_proj_b": nrm(ks[10], (DEPTH, D_HGRN_OUT, D), D_HGRN_OUT ** -0.5),
        "w_out": nrm(ks[11], (DEPTH, D, D), D ** -0.5),
        "norm_moe": 1.0 + nrm(ks[12], (DEPTH, D), 0.02),
        "w_router": nrm(ks[13], (DEPTH, D, N_EXPERTS), D ** -0.5),
        "b_router": nrm(ks[14], (DEPTH, N_EXPERTS), 0.01),
        "w_gu": nrm(ks[15], (DEPTH, N_EXPERTS, D, 2 * D_FF), D ** -0.5),
        "b_gu": nrm(ks[16], (DEPTH, N_EXPERTS, 2 * D_FF), 0.01),
        "w_down": nrm(ks[17], (DEPTH, N_EXPERTS, D_FF, D), D_FF ** -0.5),
        "b_down": nrm(ks[18], (DEPTH, N_EXPERTS, D), 0.01),
    }


def reference(x_prompt, x_sample, norm_mix, w_in, q_gain, k_gain, hgrn_lb, hgrn_o_gain, w_gate,
              w_proj_a, w_proj_b, w_out, norm_moe, w_router, b_router, w_gu, b_gu, w_down, b_down):
    y_prompt = x_prompt
    y_sample = x_sample
    for l in range(DEPTH):
        y_prompt = encoder_layer(y_prompt, l, norm_mix, w_in, q_gain, k_gain, hgrn_lb, hgrn_o_gain, w_gate,
                                 w_proj_a, w_proj_b, w_out, norm_moe, w_router, b_router, w_gu, b_gu,
                                 w_down, b_down)
        y_sample = encoder_layer(y_sample, l, norm_mix, w_in, q_gain, k_gain, hgrn_lb, hgrn_o_gain, w_gate,
                                 w_proj_a, w_proj_b, w_out, norm_moe, w_router, b_router, w_gu, b_gu,
                                 w_down, b_down)
    return (y_prompt, y_sample)
```

```python
import functools
import math

import jax
import jax.numpy as jnp
import numpy as np
from jax import lax
from jax.experimental import pallas as pl
from jax.experimental.pallas import tpu as pltpu

F32 = jnp.float32
BF16 = jnp.bfloat16

D_MODEL = 1024
ATTN_GROUPS = ((128, 1), (512, 4), (2048, 16))
HEADS_PER_GROUP = 8
N_HEADS_A = 24
HEAD_DIM_A = 64
W_A = N_HEADS_A * HEAD_DIM_A
GROUP_W = HEADS_PER_GROUP * HEAD_DIM_A
N_SIDE = 64
N_HEADS_B = 4
DK_B = 128
HGRN_CHUNK = 64
D_B = N_HEADS_B * DK_B
D_IN = 3 * W_A + 5 * D_B
N_EXPERTS = 32
TOP_K = 4
D_FF = 1024
SWIGLU_LIMIT = 7.0
SWIGLU_ALPHA = 1.702
EPS = 1e-6
NEG = -1e30

LANES = 128
VMEM_LIMIT = 56 * 1024 * 1024

TM_IN = 256
PIECE = 512
TQ = 128
TS_HGRN = 512
TM_MIX = 256
TM_ROW = 256
BM = 512


def _cparams(sem):
    return pltpu.CompilerParams(dimension_semantics=sem, vmem_limit_bytes=VMEM_LIMIT)


def _split_bf16(x):
    hi = x.astype(BF16)
    lo = (x - hi.astype(F32)).astype(BF16)
    return hi, lo


def _inproj_kernel(x_ref, nw_ref, w_ref, qg_ref, kg_ref, e_ref, et_ref, lbf_ref, lbb_ref,
                   qa_ref, ka_ref, va_ref, qb_ref, lff_ref, lfb_ref, ib_ref, og_ref, gt_ref):
    x = x_ref[...]
    ms = jnp.mean(x * x, axis=-1, keepdims=True)
    xn = (x * lax.rsqrt(ms + EPS) * nw_ref[...]).astype(BF16)

    def proj(col):
        return jnp.dot(xn, w_ref[:, col:col + PIECE], preferred_element_type=F32)

    def head_norm(y, gain):
        ss = jnp.dot((y * y).astype(BF16), e_ref[...], preferred_element_type=F32)
        inv = lax.rsqrt(ss * (1.0 / HEAD_DIM_A) + EPS)
        hi, lo = _split_bf16(inv)
        inv_full = (jnp.dot(hi, et_ref[...], preferred_element_type=F32)
                    + jnp.dot(lo, et_ref[...], preferred_element_type=F32))
        return y * inv_full * gain

    for p in range(W_A // PIECE):
        c = p * PIECE
        qa_ref[:, c:c + PIECE] = head_norm(proj(c), qg_ref[:, c:c + PIECE]).astype(BF16)
        ka_ref[:, c:c + PIECE] = head_norm(proj(W_A + c), kg_ref[:, c:c + PIECE]).astype(BF16)
        va_ref[:, c:c + PIECE] = proj(2 * W_A + c).astype(BF16)
    base = 3 * W_A
    qb = proj(base)
    qb_ref[...] = (qb * jax.nn.sigmoid(qb) * (DK_B ** -0.5)).astype(BF16)
    for dst, lb_ref, off in ((lff_ref, lbf_ref, D_B), (lfb_ref, lbb_ref, 2 * D_B)):
        lb = lb_ref[...]
        f = lb + (1.0 - lb) * jax.nn.sigmoid(proj(base + off))
        dst[...] = jnp.log(f)
    ib_ref[...] = proj(base + 3 * D_B).astype(BF16)
    og = proj(base + 4 * D_B)
    og_ref[...] = (og * jax.nn.sigmoid(og)).astype(BF16)
    for p in range(2 * D_MODEL // PIECE):
        c = p * PIECE
        gt_ref[:, c:c + PIECE] = jax.nn.sigmoid(proj(D_IN + c)).astype(BF16)


def _inproj(x2, nw, w_all, qg, kg, e_mat, et_mat, lbf, lbb):
    t = x2.shape[0]
    tm = TM_IN
    n_all = w_all.shape[1]
    row = lambda i: (i, 0)
    const = lambda i: (0, 0)
    widths = (W_A, W_A, W_A, D_B, D_B, D_B, D_B, D_B, 2 * D_MODEL)
    dtypes = (BF16, BF16, BF16, BF16, F32, F32, BF16, BF16, BF16)
    return pl.pallas_call(
        _inproj_kernel,
        out_shape=tuple(jax.ShapeDtypeStruct((t, w), dt) for w, dt in zip(widths, dtypes)),
        grid=(t // tm,),
        in_specs=[
            pl.BlockSpec((tm, D_MODEL), row),
            pl.BlockSpec((1, D_MODEL), const),
            pl.BlockSpec((D_MODEL, n_all), const),
            pl.BlockSpec((1, W_A), const),
            pl.BlockSpec((1, W_A), const),
            pl.BlockSpec((PIECE, LANES), const),
            pl.BlockSpec((LANES, PIECE), const),
            pl.BlockSpec((1, D_B), const),
            pl.BlockSpec((1, D_B), const),
        ],
        out_specs=tuple(pl.BlockSpec((tm, w), row) for w in widths),
        compiler_params=_cparams(("parallel",)),
        name="inproj",
    )(x2, nw, w_all, qg, kg, e_mat, et_mat, lbf, lbb)


def _attn_kernel(q_ref, kp_ref, kc_ref, kn_ref, vp_ref, vc_ref, vn_ref, bias_ref,
                 o_ref, lse_ref, *, tq, sub_len):
    i = pl.program_id(2)
    nk = tq + 2 * N_SIDE
    kk = jnp.concatenate([kp_ref[0], kc_ref[0], kn_ref[0]], axis=0)
    vv = jnp.concatenate([vp_ref[0], vc_ref[0], vn_ref[0]], axis=0)
    q = q_ref[0]
    kpos = i * tq - N_SIDE + lax.broadcasted_iota(jnp.int32, (1, nk), 1)
    colbias = jnp.where((kpos >= 0) & (kpos < sub_len), 0.0, NEG).astype(F32)
    lane = lax.broadcasted_iota(jnp.int32, (tq, LANES), 1)
    lse_tile = jnp.zeros((tq, LANES), F32)
    for h in range(HEADS_PER_GROUP):
        sl = slice(h * HEAD_DIM_A, (h + 1) * HEAD_DIM_A)
        s = lax.dot_general(q[:, sl], kk[:, sl], (((1,), (1,)), ((), ())),
                            preferred_element_type=F32)
        s = s + bias_ref[h] + colbias
        m = jnp.max(s, axis=-1, keepdims=True)
        p = jnp.exp(s - m)
        l = jnp.sum(p, axis=-1, keepdims=True)
        o = jnp.dot(p.astype(BF16), vv[:, sl], preferred_element_type=F32)
        o_ref[0, :, sl] = (o * (1.0 / l)).astype(o_ref.dtype)
        lse_tile = jnp.where(lane == h, m + jnp.log(l), lse_tile)
    lse_ref[0] = lse_tile


def _attn_bias(tq, dil, slopes):
    nk = tq + 2 * N_SIDE
    rel = np.arange(nk)[None, :] - N_SIDE - np.arange(tq)[:, None]
    band = np.abs(rel) <= N_SIDE
    alibi = -slopes[:, None, None] * (dil * np.abs(rel)).astype(np.float32)[None]
    return jnp.asarray(np.where(band[None], alibi, NEG).astype(np.float32))


def _attention_group(qa, ka, va, g, dil):
    b, s, _ = qa.shape
    sub_len = s // dil
    tq = min(TQ, sub_len)
    hb = tq // N_SIDE
    n_halo = sub_len // N_SIDE
    slopes = (2.0 ** (-8.0 * (np.arange(N_HEADS_A) + 1) / N_HEADS_A)).astype(np.float32)
    bias = _attn_bias(tq, dil, slopes[g * HEADS_PER_GROUP:(g + 1) * HEADS_PER_GROUP])
    nk = tq + 2 * N_SIDE
    ncol = W_A // GROUP_W

    def view(a):
        return a.reshape(b, sub_len, dil * W_A)

    cur = lambda bi, r, i: (bi, i, r * ncol + g)
    prev = lambda bi, r, i: (bi, jnp.maximum(i * hb - 1, 0), r * ncol + g)
    nxt = lambda bi, r, i: (bi, jnp.minimum((i + 1) * hb, n_halo - 1), r * ncol + g)
    out_map = lambda bi, r, i: (bi, i, r)
    blk_q = (1, tq, GROUP_W)
    blk_h = (1, N_SIDE, GROUP_W)
    o, lse = pl.pallas_call(
        functools.partial(_attn_kernel, tq=tq, sub_len=sub_len),
        out_shape=(jax.ShapeDtypeStruct((b, sub_len, dil * GROUP_W), BF16),
                   jax.ShapeDtypeStruct((b, sub_len, dil * LANES), F32)),
        grid=(b, dil, sub_len // tq),
        in_specs=[
            pl.BlockSpec(blk_q, cur),
            pl.BlockSpec(blk_h, prev), pl.BlockSpec(blk_q, cur), pl.BlockSpec(blk_h, nxt),
            pl.BlockSpec(blk_h, prev), pl.BlockSpec(blk_q, cur), pl.BlockSpec(blk_h, nxt),
            pl.BlockSpec((HEADS_PER_GROUP, tq, nk), lambda bi, r, i: (0, 0, 0)),
        ],
        out_specs=(pl.BlockSpec((1, tq, GROUP_W), out_map),
                   pl.BlockSpec((1, tq, LANES), out_map)),
        compiler_params=_cparams(("parallel", "parallel", "parallel")),
        name=f"attn_d{dil}",
    )(view(qa), view(ka), view(ka), view(ka), view(va), view(va), view(va), bias)
    return o.reshape(b, s, GROUP_W), lse.reshape(b, s, LANES)


def _hgrn_kernel(qf_ref, lf_ref, vf_ref, qr_ref, lr_ref, vr_ref, of_ref, or_ref,
                 sf_ref, sr_ref, *, ts):
    c_len = HGRN_CHUNK
    nc = ts // c_len

    @pl.when(pl.program_id(1) == 0)
    def _():
        sf_ref[...] = jnp.zeros_like(sf_ref)
        sr_ref[...] = jnp.zeros_like(sr_ref)

    r_i = lax.broadcasted_iota(jnp.int32, (c_len, c_len), 0)
    c_i = lax.broadcasted_iota(jnp.int32, (c_len, c_len), 1)
    lower = r_i >= c_i
    upper = r_i <= c_i
    tri_f = jnp.where(lower, 1.0, 0.0).astype(BF16)
    tri_r = jnp.where(upper, 1.0, 0.0).astype(BF16)

    def cumsum(tri, x):
        hi = x.astype(BF16)
        r1 = x - hi.astype(F32)
        mid = r1.astype(BF16)
        lo = (r1 - mid.astype(F32)).astype(BF16)
        d = lambda a: jnp.dot(tri, a, preferred_element_type=F32)
        return d(hi) + d(mid) + d(lo)

    def chunk(q_ref, l_ref, v_ref, o_ref, s_ref, row0, h, tri, mask, end_row):
        rows = pl.ds(row0, c_len)
        cols = slice(h * DK_B, (h + 1) * DK_B)
        lf = l_ref[0, rows, cols]
        q = q_ref[0, rows, cols].astype(F32)
        v = v_ref[0, rows, cols]
        k = 1.0 - jnp.exp(lf)
        bcum = cumsum(tri, lf)
        btot = bcum[end_row:end_row + 1, :]
        qt = (q * jnp.exp(bcum)).astype(BF16)
        kt = (k * jnp.exp(-bcum)).astype(BF16)
        sc = lax.dot_general(qt, kt, (((1,), (1,)), ((), ())), preferred_element_type=F32)
        sc = jnp.where(mask, sc, 0.0).astype(BF16)
        st = s_ref[h]
        o = (jnp.dot(sc, v, preferred_element_type=F32)
             + lax.dot_general(qt, st.astype(BF16), (((1,), (1,)), ((), ())),
                               preferred_element_type=F32))
        o_ref[0, rows, cols] = o
        kd = (k * jnp.exp(btot - bcum)).astype(BF16)
        upd = lax.dot_general(v, kd, (((0,), (0,)), ((), ())), preferred_element_type=F32)
        s_ref[h] = st * jnp.exp(btot) + upd

    def body(c, carry):
        row_f = pl.multiple_of(c * c_len, c_len)
        row_r = pl.multiple_of((nc - 1 - c) * c_len, c_len)
        for h in range(N_HEADS_B):
            chunk(qf_ref, lf_ref, vf_ref, of_ref, sf_ref, row_f, h, tri_f, lower, c_len - 1)
            chunk(qr_ref, lr_ref, vr_ref, or_ref, sr_ref, row_r, h, tri_r, upper, 0)
        return carry

    lax.fori_loop(0, nc, body, 0)


def _hgrn(qb, lff, lfb, ib):
    b, s, _ = qb.shape
    ts = min(TS_HGRN, s)
    nt = s // ts
    fwd = lambda bi, j: (bi, j, 0)
    rev = lambda bi, j: (bi, nt - 1 - j, 0)
    blk = (1, ts, D_B)
    return pl.pallas_call(
        functools.partial(_hgrn_kernel, ts=ts),
        out_shape=(jax.ShapeDtypeStruct((b, s, D_B), F32), jax.ShapeDtypeStruct((b, s, D_B), F32)),
        grid=(b, nt),
        in_specs=[pl.BlockSpec(blk, fwd), pl.BlockSpec(blk, fwd), pl.BlockSpec(blk, fwd),
                  pl.BlockSpec(blk, rev), pl.BlockSpec(blk, rev), pl.BlockSpec(blk, rev)],
        out_specs=(pl.BlockSpec(blk, fwd), pl.BlockSpec(blk, rev)),
        scratch_shapes=[pltpu.VMEM((N_HEADS_B, DK_B, DK_B), F32),
                        pltpu.VMEM((N_HEADS_B, DK_B, DK_B), F32)],
        compiler_params=_cparams(("parallel", "arbitrary")),
        name="hgrn",
    )(qb, lff, ib, qb, lfb, ib)


def _mix_kernel(x_ref, o1_ref, o2_ref, o3_ref, l1_ref, l2_ref, l3_ref, of_ref, ob_ref, og_ref,
                gt_ref, wa_ref, wb_ref, wo_ref, eh_ref, ogain_ref, nmoe_ref, wrh_ref, wrl_ref,
                br_ref, cnt_ref,
                h_ref, hn_ref, ti_ref, tw_ref, rk_ref, cnt_out_ref, run_ref, *, tm):
    i = pl.program_id(0)

    @pl.when(i == 0)
    def _():
        run_ref[...] = cnt_ref[...]

    l1, l2, l3 = l1_ref[...], l2_ref[...], l3_ref[...]
    mx = jnp.maximum(jnp.maximum(l1, l2), l3)
    e1, e2, e3 = jnp.exp(l1 - mx), jnp.exp(l2 - mx), jnp.exp(l3 - mx)
    inv_den = 1.0 / (e1 + e2 + e3)

    def expand(w):
        hi, lo = _split_bf16(w)
        return (jnp.dot(hi, eh_ref[...], preferred_element_type=F32)
                + jnp.dot(lo, eh_ref[...], preferred_element_type=F32))

    attn = (expand(e1 * inv_den) * o1_ref[...].astype(F32)
            + expand(e2 * inv_den) * o2_ref[...].astype(F32)
            + expand(e3 * inv_den) * o3_ref[...].astype(F32))

    o = of_ref[...] + ob_ref[...]
    parts = []
    for h in range(N_HEADS_B):
        oh = o[:, h * DK_B:(h + 1) * DK_B]
        ms = jnp.mean(oh * oh, axis=-1, keepdims=True)
        parts.append(oh * lax.rsqrt(ms + EPS))
    hg = jnp.concatenate(parts, axis=-1) * ogain_ref[...] * og_ref[...].astype(F32)

    pa = jnp.dot(attn.astype(BF16), wa_ref[...], preferred_element_type=F32)
    pb = jnp.dot(hg.astype(BF16), wb_ref[...], preferred_element_type=F32)
    mixed = (gt_ref[:, :D_MODEL].astype(F32) * pa + gt_ref[:, D_MODEL:].astype(F32) * pb)
    h = x_ref[...] + jnp.dot(mixed.astype(BF16), wo_ref[...], preferred_element_type=F32)
    h_ref[...] = h
    ms = jnp.mean(h * h, axis=-1, keepdims=True)
    hn = h * lax.rsqrt(ms + EPS) * nmoe_ref[...]
    hn_ref[...] = hn

    hi, lo = _split_bf16(hn)
    lg = (jnp.dot(hi, wrh_ref[...], preferred_element_type=F32)
          + jnp.dot(lo, wrh_ref[...], preferred_element_type=F32)
          + jnp.dot(hi, wrl_ref[...], preferred_element_type=F32)) + br_ref[...]
    lane = lax.broadcasted_iota(jnp.int32, (tm, LANES), 1)
    vals, idxs = [], []
    onehot = jnp.zeros((tm, LANES), F32)
    for _ in range(TOP_K):
        m = jnp.max(lg, axis=-1, keepdims=True)
        idx = jnp.min(jnp.where(lg == m, lane, LANES), axis=-1, keepdims=True)
        sel = lane == idx
        onehot = jnp.where(sel, 1.0, onehot)
        lg = jnp.where(sel, NEG * 2, lg)
        vals.append(m)
        idxs.append(idx)
    exps = [jnp.exp(v - vals[0]) for v in vals]
    inv = 1.0 / (exps[0] + exps[1] + exps[2] + exps[3])

    r_i = lax.broadcasted_iota(jnp.int32, (tm, tm), 0)
    c_i = lax.broadcasted_iota(jnp.int32, (tm, tm), 1)
    tri = jnp.where(r_i > c_i, 1.0, 0.0).astype(BF16)
    before = jnp.dot(tri, onehot.astype(BF16), preferred_element_type=F32) + run_ref[...]
    ti = jnp.zeros((tm, LANES), jnp.int32)
    tw = jnp.zeros((tm, LANES), F32)
    rk = jnp.zeros((tm, LANES), F32)
    for k in range(TOP_K):
        rank_k = jnp.sum(jnp.where(lane == idxs[k], before, 0.0), axis=-1, keepdims=True)
        ti = jnp.where(lane == k, idxs[k], ti)
        tw = jnp.where(lane == k, exps[k] * inv, tw)
        rk = jnp.where(lane == k, rank_k, rk)
    ti_ref[...] = ti
    tw_ref[...] = tw
    rk_ref[...] = rk.astype(jnp.int32)
    run_new = run_ref[...] + jnp.sum(onehot, axis=0, keepdims=True)
    run_ref[...] = run_new
    cnt_out_ref[...] = run_new


def _mix(x2, o1, o2, o3, l1, l2, l3, of, ob, og, gt, wa, wb, wo, eh, ogain, nmoe, wrh, wrl, br, cnt):
    t = x2.shape[0]
    tm = TM_MIX
    row = lambda i: (i, 0)
    const = lambda i: (0, 0)
    rb = lambda w: pl.BlockSpec((tm, w), row)
    cb = lambda a: pl.BlockSpec(a.shape, const)
    return pl.pallas_call(
        functools.partial(_mix_kernel, tm=tm),
        out_shape=(jax.ShapeDtypeStruct((t, D_MODEL), F32),
                   jax.ShapeDtypeStruct((t, D_MODEL), F32),
                   jax.ShapeDtypeStruct((t, LANES), jnp.int32),
                   jax.ShapeDtypeStruct((t, LANES), F32),
                   jax.ShapeDtypeStruct((t, LANES), jnp.int32),
                   jax.ShapeDtypeStruct((1, LANES), F32)),
        grid=(t // tm,),
        in_specs=[rb(D_MODEL), rb(GROUP_W), rb(GROUP_W), rb(GROUP_W), rb(LANES), rb(LANES), rb(LANES),
                  rb(D_B), rb(D_B), rb(D_B), rb(2 * D_MODEL),
                  cb(wa), cb(wb), cb(wo), cb(eh), cb(ogain), cb(nmoe), cb(wrh), cb(wrl), cb(br), cb(cnt)],
        out_specs=(rb(D_MODEL), rb(D_MODEL), rb(LANES), rb(LANES), rb(LANES),
                   pl.BlockSpec((1, LANES), const)),
        scratch_shapes=[pltpu.VMEM((1, LANES), F32)],
        compiler_params=_cparams(("arbitrary",)),
        name="mix",
    )(x2, o1, o2, o3, l1, l2, l3, of, ob, og, gt, wa, wb, wo, eh, ogain, nmoe, wrh, wrl, br, cnt)


def _dispatch_kernel(dest_ref, hn_ref, xs_in_ref, xs_ref, sem, *, tm):
    del xs_in_ref

    def issue(r, carry):
        for k in range(TOP_K):
            d = dest_ref[r * TOP_K + k]
            pltpu.make_async_copy(hn_ref.at[pl.ds(r, 1)], xs_ref.at[pl.ds(d, 1)], sem).start()
        return carry

    lax.fori_loop(0, tm, issue, 0)
    for _ in range(TOP_K):
        pltpu.make_async_copy(hn_ref, xs_ref.at[pl.ds(0, tm)], sem).wait()


def _dispatch(dest_flat, hn, xs):
    t = hn.shape[0]
    tm = TM_ROW
    return pl.pallas_call(
        functools.partial(_dispatch_kernel, tm=tm),
        out_shape=jax.ShapeDtypeStruct(xs.shape, xs.dtype),
        grid=(t // tm,),
        in_specs=[pl.BlockSpec((tm * TOP_K,), lambda i: (i,), memory_space=pltpu.SMEM),
                  pl.BlockSpec((tm, D_MODEL), lambda i: (i, 0)),
                  pl.BlockSpec(memory_space=pl.ANY)],
        out_specs=pl.BlockSpec(memory_space=pl.ANY),
        scratch_shapes=[pltpu.SemaphoreType.DMA],
        input_output_aliases={2: 0},
        compiler_params=_cparams(("arbitrary",)),
        name="dispatch",
    )(dest_flat, hn, xs)


def _combine_kernel(dest_ref, h_ref, tw_ref, ys_ref, y_ref, buf_ref, sem, *, tm):
    def issue(r, carry):
        for k in range(TOP_K):
            d = dest_ref[r * TOP_K + k]
            pltpu.make_async_copy(ys_ref.at[pl.ds(d, 1)], buf_ref.at[k, pl.ds(r, 1)], sem).start()
        return carry

    lax.fori_loop(0, tm, issue, 0)
    for k in range(TOP_K):
        pltpu.make_async_copy(ys_ref.at[pl.ds(0, tm)], buf_ref.at[k], sem).wait()
    tw = tw_ref[...]
    acc = h_ref[...]
    for k in range(TOP_K):
        acc = acc + tw[:, k:k + 1] * buf_ref[k]
    y_ref[...] = acc


def _combine(dest_flat, h, tw, ys):
    t = h.shape[0]
    tm = TM_ROW
    return pl.pallas_call(
        functools.partial(_combine_kernel, tm=tm),
        out_shape=jax.ShapeDtypeStruct((t, D_MODEL), F32),
        grid=(t // tm,),
        in_specs=[pl.BlockSpec((tm * TOP_K,), lambda i: (i,), memory_space=pltpu.SMEM),
                  pl.BlockSpec((tm, D_MODEL), lambda i: (i, 0)),
                  pl.BlockSpec((tm, LANES), lambda i: (i, 0)),
                  pl.BlockSpec(memory_space=pl.ANY)],
        out_specs=pl.BlockSpec((tm, D_MODEL), lambda i: (i, 0)),
        scratch_shapes=[pltpu.VMEM((TOP_K, tm, D_MODEL), F32), pltpu.SemaphoreType.DMA],
        compiler_params=_cparams(("arbitrary",)),
        name="combine",
    )(dest_flat, h, tw, ys)


def _experts_kernel(be_ref, nused_ref, xs_ref, wgu_ref, bgu_ref, wd_ref, bd_ref, ys_ref):
    del be_ref
    i = pl.program_id(0)

    @pl.when(i < nused_ref[0])
    def _():
        x = xs_ref[...].astype(BF16)
        hh = jnp.dot(x, wgu_ref[0], preferred_element_type=F32) + bgu_ref[0]
        gate = jnp.minimum(hh[:, :D_FF], SWIGLU_LIMIT)
        up = jnp.clip(hh[:, D_FF:], -SWIGLU_LIMIT, SWIGLU_LIMIT)
        glu = gate * jax.nn.sigmoid(SWIGLU_ALPHA * gate)
        act = ((up + 1.0) * glu).astype(BF16)
        ys_ref[...] = jnp.dot(act, wd_ref[0], preferred_element_type=F32) + bd_ref[0]

    @pl.when(i >= nused_ref[0])
    def _():
        ys_ref[...] = jnp.zeros_like(ys_ref)


def _experts(block_e, nused, xs, wgu, bgu, wd, bd):
    p = xs.shape[0]
    nb = p // BM
    emap3 = lambda i, be, nu: (be[i], 0, 0)
    return pl.pallas_call(
        _experts_kernel,
        out_shape=jax.ShapeDtypeStruct((p, D_MODEL), F32),
        grid_spec=pltpu.PrefetchScalarGridSpec(
            num_scalar_prefetch=2,
            grid=(nb,),
            in_specs=[pl.BlockSpec((BM, D_MODEL), lambda i, be, nu: (i, 0)),
                      pl.BlockSpec((1, D_MODEL, 2 * D_FF), emap3),
                      pl.BlockSpec((1, 1, 2 * D_FF), emap3),
                      pl.BlockSpec((1, D_FF, D_MODEL), emap3),
                      pl.BlockSpec((1, 1, D_MODEL), emap3)],
            out_specs=pl.BlockSpec((BM, D_MODEL), lambda i, be, nu: (i, 0)),
        ),
        compiler_params=_cparams(("arbitrary",)),
        name="experts",
    )(block_e, nused, xs, wgu, bgu, wd, bd)


def _head_indicator(n_cols, head_dim):
    e = np.zeros((n_cols, LANES), np.float32)
    e[np.arange(n_cols), np.arange(n_cols) // head_dim] = 1.0
    return e


def _mixer(x, prm):
    b, s, d = x.shape
    t = b * s
    x2 = x.reshape(t, d)
    qa, ka, va, qb, lff, lfb, ib, og, gt = _inproj(
        x2, prm["norm_mix"], prm["w_all"], prm["q_gain"], prm["k_gain"], prm["e_in"], prm["et_in"],
        prm["lb_f"], prm["lb_b"])
    r3 = lambda a: a.reshape(b, s, a.shape[-1])
    outs, lses = [], []
    for g, (_, dil) in enumerate(ATTN_GROUPS):
        o, lse = _attention_group(r3(qa), r3(ka), r3(va), g, dil)
        outs.append(o.reshape(t, GROUP_W))
        lses.append(lse.reshape(t, LANES))
    of, ob = _hgrn(r3(qb), r3(lff), r3(lfb), r3(ib))
    return x2, outs, lses, of.reshape(t, D_B), ob.reshape(t, D_B), og, gt


def kernel(x_prompt, x_sample, norm_mix, w_in, q_gain, k_gain, hgrn_lb, hgrn_o_gain, w_gate, w_proj_a,
           w_proj_b, w_out, norm_moe, w_router, b_router, w_gu, b_gu, w_down, b_down):
    l = 0
    lb = jnp.cumsum(jax.nn.softmax(hgrn_lb.astype(F32), axis=1), axis=1)
    wr = jnp.zeros((D_MODEL, LANES), F32).at[:, :N_EXPERTS].set(w_router[l])
    wr_hi = wr.astype(BF16)
    prm = {
        "norm_mix": norm_mix[l].reshape(1, D_MODEL),
        "w_all": jnp.concatenate([w_in[l], w_gate[l]], axis=1).astype(BF16),
        "q_gain": q_gain[l].reshape(1, W_A) * (HEAD_DIM_A ** -0.5),
        "k_gain": k_gain[l].reshape(1, W_A),
        "e_in": jnp.asarray(_head_indicator(PIECE, HEAD_DIM_A), BF16),
        "et_in": jnp.asarray(_head_indicator(PIECE, HEAD_DIM_A).T, BF16),
        "lb_f": lb[0, l].reshape(1, D_B),
        "lb_b": lb[1, l].reshape(1, D_B),
    }
    wa, wb, wo = w_proj_a[l].astype(BF16), w_proj_b[l].astype(BF16), w_out[l].astype(BF16)
    eh = jnp.asarray(_head_indicator(GROUP_W, HEAD_DIM_A).T, BF16)
    ogain = hgrn_o_gain[l].reshape(1, D_B)
    nmoe = norm_moe[l].reshape(1, D_MODEL)
    wr_lo = (wr - wr_hi.astype(F32)).astype(BF16)
    br = jnp.full((1, LANES), NEG, F32).at[0, :N_EXPERTS].set(b_router[l])

    cnt = jnp.zeros((1, LANES), F32)
    per_batch = []
    for x in (x_prompt, x_sample):
        x2, outs, lses, of, ob, og, gt = _mixer(x, prm)
        h, hn, ti, tw, rk, cnt = _mix(x2, outs[0], outs[1], outs[2], lses[0], lses[1], lses[2], of, ob,
                                      og, gt, wa, wb, wo, eh, ogain, nmoe, wr_hi, wr_lo, br, cnt)
        per_batch.append((x.shape, h, hn, ti, tw, rk))

    n_tok = sum(pb[1].shape[0] for pb in per_batch)
    sizes = cnt[0, :N_EXPERTS].astype(jnp.int32)
    pad_sizes = (sizes + BM - 1) // BM * BM
    pad_ends = jnp.cumsum(pad_sizes)
    pad_starts = pad_ends - pad_sizes
    nb = (n_tok * TOP_K) // BM + N_EXPERTS
    block_e = jnp.minimum(jnp.searchsorted(pad_ends, jnp.arange(nb, dtype=jnp.int32) * BM, side="right"),
                          N_EXPERTS - 1).astype(jnp.int32)
    nused = (pad_ends[-1:] // BM).astype(jnp.int32)

    xs = jnp.zeros((nb * BM, D_MODEL), F32)
    dests = []
    for _, _, hn, ti, _, rk in per_batch:
        dest = (pad_starts[ti[:, :TOP_K]] + rk[:, :TOP_K]).reshape(-1).astype(jnp.int32)
        dests.append(dest)
        xs = _dispatch(dest, hn, xs)
    ys = _experts(block_e, nused, xs, w_gu[l].astype(BF16), b_gu[l].reshape(N_EXPERTS, 1, 2 * D_FF),
                  w_down[l].astype(BF16), b_down[l].reshape(N_EXPERTS, 1, D_MODEL))
    results = []
    for (shape, h, _, _, tw, _), dest in zip(per_batch, dests):
        results.append(_combine(dest, h, tw, ys).reshape(shape))
    return tuple(results)
```

```python
import functools
import math

import jax
import jax.numpy as jnp
import numpy as np
from jax import lax
from jax.experimental import pallas as pl
from jax.experimental.pallas import tpu as pltpu

F32 = jnp.float32
BF16 = jnp.bfloat16

D_MODEL = 1024
ATTN_GROUPS = ((128, 1), (512, 4), (2048, 16))
HEADS_PER_GROUP = 8
N_HEADS_A = 24
HEAD_DIM_A = 64
W_A = N_HEADS_A * HEAD_DIM_A
GROUP_W = HEADS_PER_GROUP * HEAD_DIM_A
N_SIDE = 64
N_HEADS_B = 4
DK_B = 128
HGRN_CHUNK = 64
D_B = N_HEADS_B * DK_B
D_IN = 3 * W_A + 5 * D_B
N_EXPERTS = 32
TOP_K = 4
D_FF = 1024
SWIGLU_LIMIT = 7.0
SWIGLU_ALPHA = 1.702
EPS = 1e-6
NEG = -1e30

LANES = 128
VMEM_LIMIT = 56 * 1024 * 1024

TM_IN = 256
PIECE = 512
TQ = 128
TS_HGRN = 512
TM_MIX = 256
TM_ROW = 256
BM = 512


def _cparams(sem):
    return pltpu.CompilerParams(dimension_semantics=sem, vmem_limit_bytes=VMEM_LIMIT)


def _split_bf16(x):
    hi = x.astype(BF16)
    lo = (x - hi.astype(F32)).astype(BF16)
    return hi, lo


def _inproj_kernel(x_ref, nw_ref, w_ref, qg_ref, kg_ref, e_ref, et_ref, lbf_ref, lbb_ref,
                   q0_ref, q1_ref, q2_ref, k0_ref, k1_ref, k2_ref, v0_ref, v1_ref, v2_ref,
                   qb_ref, lff_ref, lfb_ref, ib_ref, og_ref, gt_ref, scr_ref, *, tm):
    q_refs, k_refs, v_refs = (q0_ref, q1_ref, q2_ref), (k0_ref, k1_ref, k2_ref), (v0_ref, v1_ref, v2_ref)
    x = x_ref[...]
    ms = jnp.mean(x * x, axis=-1, keepdims=True)
    xn = (x * lax.rsqrt(ms + EPS) * nw_ref[...]).astype(BF16)

    def proj(col):
        return jnp.dot(xn, w_ref[:, col:col + PIECE], preferred_element_type=F32)

    def head_norm(y, gain):
        ss = jnp.dot((y * y).astype(BF16), e_ref[...], preferred_element_type=F32)
        inv = lax.rsqrt(ss * (1.0 / HEAD_DIM_A) + EPS)
        hi, lo = _split_bf16(inv)
        inv_full = (jnp.dot(hi, et_ref[...], preferred_element_type=F32)
                    + jnp.dot(lo, et_ref[...], preferred_element_type=F32))
        return y * inv_full * gain

    def store_group(dst_ref, y, dil):
        if dil == 1:
            dst_ref[0, 0] = y.astype(BF16)
            return
        for c in range(GROUP_W // LANES):
            scr_ref[c] = y[:, c * LANES:(c + 1) * LANES]
        for r in range(dil):
            rows = pl.ds(r, tm // dil, stride=dil)
            dst_ref[0, r] = jnp.concatenate(
                [scr_ref[c, rows, :] for c in range(GROUP_W // LANES)], axis=-1).astype(BF16)

    for g, (_, dil) in enumerate(ATTN_GROUPS):
        c = g * GROUP_W
        store_group(q_refs[g], head_norm(proj(c), qg_ref[:, c:c + GROUP_W]), dil)
        store_group(k_refs[g], head_norm(proj(W_A + c), kg_ref[:, c:c + GROUP_W]), dil)
        store_group(v_refs[g], proj(2 * W_A + c), dil)
    base = 3 * W_A
    qb = proj(base)
    qb_ref[...] = (qb * jax.nn.sigmoid(qb) * (DK_B ** -0.5)).astype(BF16)
    for dst, lb_ref, off in ((lff_ref, lbf_ref, D_B), (lfb_ref, lbb_ref, 2 * D_B)):
        lb = lb_ref[...]
        f = lb + (1.0 - lb) * jax.nn.sigmoid(proj(base + off))
        dst[...] = jnp.log(f)
    ib_ref[...] = proj(base + 3 * D_B).astype(BF16)
    og = proj(base + 4 * D_B)
    og_ref[...] = (og * jax.nn.sigmoid(og)).astype(BF16)
    for p in range(2 * D_MODEL // PIECE):
        c = p * PIECE
        gt_ref[:, c:c + PIECE] = jax.nn.sigmoid(proj(D_IN + c)).astype(BF16)


def _inproj(x2, b, s, nw, w_all, qg, kg, e_mat, et_mat, lbf, lbb):
    t = x2.shape[0]
    tm = TM_IN
    tps = s // tm
    n_all = w_all.shape[1]
    row = lambda i: (i, 0)
    const = lambda i: (0, 0)
    seq = lambda i: (i // tps, 0, i % tps, 0)
    widths = (D_B, D_B, D_B, D_B, D_B, 2 * D_MODEL)
    dtypes = (BF16, F32, F32, BF16, BF16, BF16)
    grp_shapes = [jax.ShapeDtypeStruct((b, dil, s // dil, GROUP_W), BF16) for _, dil in ATTN_GROUPS] * 3
    grp_specs = [pl.BlockSpec((1, dil, tm // dil, GROUP_W), seq) for _, dil in ATTN_GROUPS] * 3
    return pl.pallas_call(
        functools.partial(_inproj_kernel, tm=tm),
        out_shape=tuple(grp_shapes) + tuple(jax.ShapeDtypeStruct((t, w), dt) for w, dt in zip(widths, dtypes)),
        grid=(t // tm,),
        in_specs=[
            pl.BlockSpec((tm, D_MODEL), row),
            pl.BlockSpec((1, D_MODEL), const),
            pl.BlockSpec((D_MODEL, n_all), const),
            pl.BlockSpec((1, W_A), const),
            pl.BlockSpec((1, W_A), const),
            pl.BlockSpec((PIECE, LANES), const),
            pl.BlockSpec((LANES, PIECE), const),
            pl.BlockSpec((1, D_B), const),
            pl.BlockSpec((1, D_B), const),
        ],
        out_specs=tuple(grp_specs) + tuple(pl.BlockSpec((tm, w), row) for w in widths),
        scratch_shapes=[pltpu.VMEM((GROUP_W // LANES, tm, LANES), F32)],
        compiler_params=_cparams(("parallel",)),
        name="inproj",
    )(x2, nw, w_all, qg, kg, e_mat, et_mat, lbf, lbb)


def _attn_kernel(q_ref, kp_ref, kc_ref, kn_ref, vp_ref, vc_ref, vn_ref, bias_ref,
                 o_ref, lse_ref, *, tq, sub_len):
    i = pl.program_id(2)
    nk = tq + 2 * N_SIDE
    kk = jnp.concatenate([kp_ref[...], kc_ref[...], kn_ref[...]], axis=0)
    vv = jnp.concatenate([vp_ref[...], vc_ref[...], vn_ref[...]], axis=0)
    q = q_ref[...]
    kpos = i * tq - N_SIDE + lax.broadcasted_iota(jnp.int32, (1, nk), 1)
    colbias = jnp.where((kpos >= 0) & (kpos < sub_len), 0.0, NEG).astype(F32)
    lane = lax.broadcasted_iota(jnp.int32, (tq, LANES), 1)
    lse_tile = jnp.zeros((tq, LANES), F32)
    for h in range(HEADS_PER_GROUP):
        sl = slice(h * HEAD_DIM_A, (h + 1) * HEAD_DIM_A)
        s = lax.dot_general(q[:, sl], kk[:, sl], (((1,), (1,)), ((), ())),
                            preferred_element_type=F32)
        s = s + bias_ref[h] + colbias
        m = jnp.max(s, axis=-1, keepdims=True)
        p = jnp.exp(s - m)
        l = jnp.sum(p, axis=-1, keepdims=True)
        o = jnp.dot(p.astype(BF16), vv[:, sl], preferred_element_type=F32)
        o_ref[:, sl] = (o * (1.0 / l)).astype(o_ref.dtype)
        lse_tile = jnp.where(lane == h, m + jnp.log(l), lse_tile)
    lse_ref[...] = lse_tile


def _attn_bias(tq, dil, slopes):
    nk = tq + 2 * N_SIDE
    rel = np.arange(nk)[None, :] - N_SIDE - np.arange(tq)[:, None]
    band = np.abs(rel) <= N_SIDE
    alibi = -slopes[:, None, None] * (dil * np.abs(rel)).astype(np.float32)[None]
    return jnp.asarray(np.where(band[None], alibi, NEG).astype(np.float32))


def _attention_group(q, k, v, g):
    b, dil, sub_len, _ = q.shape
    tq = min(TQ, sub_len)
    hb = tq // N_SIDE
    n_halo = sub_len // N_SIDE
    slopes = (2.0 ** (-8.0 * (np.arange(N_HEADS_A) + 1) / N_HEADS_A)).astype(np.float32)
    bias = _attn_bias(tq, dil, slopes[g * HEADS_PER_GROUP:(g + 1) * HEADS_PER_GROUP])
    nk = tq + 2 * N_SIDE
    cur = lambda bi, r, i: (bi, r, i, 0)
    prev = lambda bi, r, i: (bi, r, jnp.maximum(i * hb - 1, 0), 0)
    nxt = lambda bi, r, i: (bi, r, jnp.minimum((i + 1) * hb, n_halo - 1), 0)
    blk_q = (None, None, tq, GROUP_W)
    blk_h = (None, None, N_SIDE, GROUP_W)
    return pl.pallas_call(
        functools.partial(_attn_kernel, tq=tq, sub_len=sub_len),
        out_shape=(jax.ShapeDtypeStruct((b, dil, sub_len, GROUP_W), BF16),
                   jax.ShapeDtypeStruct((b, dil, sub_len, LANES), F32)),
        grid=(b, dil, sub_len // tq),
        in_specs=[
            pl.BlockSpec(blk_q, cur),
            pl.BlockSpec(blk_h, prev), pl.BlockSpec(blk_q, cur), pl.BlockSpec(blk_h, nxt),
            pl.BlockSpec(blk_h, prev), pl.BlockSpec(blk_q, cur), pl.BlockSpec(blk_h, nxt),
            pl.BlockSpec((HEADS_PER_GROUP, tq, nk), lambda bi, r, i: (0, 0, 0)),
        ],
        out_specs=(pl.BlockSpec((None, None, tq, GROUP_W), cur),
                   pl.BlockSpec((None, None, tq, LANES), cur)),
        compiler_params=_cparams(("parallel", "parallel", "parallel")),
        name=f"attn_d{dil}",
    )(q, k, k, k, v, v, v, bias)


def _hgrn_kernel(qf_ref, lf_ref, vf_ref, qr_ref, lr_ref, vr_ref, of_ref, or_ref,
                 sf_ref, sr_ref, *, ts):
    c_len = HGRN_CHUNK
    nc = ts // c_len

    @pl.when(pl.program_id(1) == 0)
    def _():
        sf_ref[...] = jnp.zeros_like(sf_ref)
        sr_ref[...] = jnp.zeros_like(sr_ref)

    r_i = lax.broadcasted_iota(jnp.int32, (c_len, c_len), 0)
    c_i = lax.broadcasted_iota(jnp.int32, (c_len, c_len), 1)
    lower = r_i >= c_i
    upper = r_i <= c_i
    tri_f = jnp.where(lower, 1.0, 0.0).astype(BF16)
    tri_r = jnp.where(upper, 1.0, 0.0).astype(BF16)

    def cumsum(tri, x):
        hi = x.astype(BF16)
        r1 = x - hi.astype(F32)
        mid = r1.astype(BF16)
        lo = (r1 - mid.astype(F32)).astype(BF16)
        d = lambda a: jnp.dot(tri, a, preferred_element_type=F32)
        return d(hi) + d(mid) + d(lo)

    def chunk(q_ref, l_ref, v_ref, o_ref, s_ref, row0, h, tri, mask, end_row):
        rows = pl.ds(row0, c_len)
        cols = slice(h * DK_B, (h + 1) * DK_B)
        lf = l_ref[0, rows, cols]
        q = q_ref[0, rows, cols].astype(F32)
        v = v_ref[0, rows, cols]
        k = 1.0 - jnp.exp(lf)
        bcum = cumsum(tri, lf)
        btot = bcum[end_row:end_row + 1, :]
        qt = (q * jnp.exp(bcum)).astype(BF16)
        kt = (k * jnp.exp(-bcum)).astype(BF16)
        sc = lax.dot_general(qt, kt, (((1,), (1,)), ((), ())), preferred_element_type=F32)
        sc = jnp.where(mask, sc, 0.0).astype(BF16)
        st = s_ref[h]
        o = (jnp.dot(sc, v, preferred_element_type=F32)
             + lax.dot_general(qt, st.astype(BF16), (((1,), (1,)), ((), ())),
                               preferred_element_type=F32))
        o_ref[0, rows, cols] = o
        kd = (k * jnp.exp(btot - bcum)).astype(BF16)
        upd = lax.dot_general(v, kd, (((0,), (0,)), ((), ())), preferred_element_type=F32)
        s_ref[h] = st * jnp.exp(btot) + upd

    def body(c, carry):
        row_f = pl.multiple_of(c * c_len, c_len)
        row_r = pl.multiple_of((nc - 1 - c) * c_len, c_len)
        for h in range(N_HEADS_B):
            chunk(qf_ref, lf_ref, vf_ref, of_ref, sf_ref, row_f, h, tri_f, lower, c_len - 1)
            chunk(qr_ref, lr_ref, vr_ref, or_ref, sr_ref, row_r, h, tri_r, upper, 0)
        return carry

    lax.fori_loop(0, nc, body, 0)


def _hgrn(qb, lff, lfb, ib):
    b, s, _ = qb.shape
    ts = min(TS_HGRN, s)
    nt = s // ts
    fwd = lambda bi, j: (bi, j, 0)
    rev = lambda bi, j: (bi, nt - 1 - j, 0)
    blk = (1, ts, D_B)
    return pl.pallas_call(
        functools.partial(_hgrn_kernel, ts=ts),
        out_shape=(jax.ShapeDtypeStruct((b, s, D_B), F32), jax.ShapeDtypeStruct((b, s, D_B), F32)),
        grid=(b, nt),
        in_specs=[pl.BlockSpec(blk, fwd), pl.BlockSpec(blk, fwd), pl.BlockSpec(blk, fwd),
                  pl.BlockSpec(blk, rev), pl.BlockSpec(blk, rev), pl.BlockSpec(blk, rev)],
        out_specs=(pl.BlockSpec(blk, fwd), pl.BlockSpec(blk, rev)),
        scratch_shapes=[pltpu.VMEM((N_HEADS_B, DK_B, DK_B), F32),
                        pltpu.VMEM((N_HEADS_B, DK_B, DK_B), F32)],
        compiler_params=_cparams(("parallel", "arbitrary")),
        name="hgrn",
    )(qb, lff, ib, qb, lfb, ib)


def _mix_kernel(x_ref, o1_ref, o2_ref, o3_ref, l1_ref, l2_ref, l3_ref, of_ref, ob_ref, og_ref,
                gt_ref, wa_ref, wb_ref, wo_ref, eh_ref, ogain_ref, nmoe_ref, wrh_ref, wrl_ref,
                br_ref, cnt_ref,
                h_ref, hn_ref, ti_ref, tw_ref, rk_ref, cnt_out_ref, run_ref, so_ref, sl_ref, *, tm):
    i = pl.program_id(0)

    @pl.when(i == 0)
    def _():
        run_ref[...] = cnt_ref[...]

    def token_major(src_ref, scr_ref, dil):
        if dil == 1:
            return src_ref[0].astype(F32)
        n_chunk = scr_ref.shape[0]
        for r in range(dil):
            blk = src_ref[r].astype(F32)
            for c in range(n_chunk):
                scr_ref[c, pl.ds(r, tm // dil, stride=dil), :] = blk[:, c * LANES:(c + 1) * LANES]
        return jnp.concatenate([scr_ref[c] for c in range(n_chunk)], axis=-1)

    dils = [dil for _, dil in ATTN_GROUPS]
    l1, l2, l3 = (token_major(r, sl_ref, d) for r, d in zip((l1_ref, l2_ref, l3_ref), dils))
    mx = jnp.maximum(jnp.maximum(l1, l2), l3)
    e1, e2, e3 = jnp.exp(l1 - mx), jnp.exp(l2 - mx), jnp.exp(l3 - mx)
    inv_den = 1.0 / (e1 + e2 + e3)

    def expand(w):
        hi, lo = _split_bf16(w)
        return (jnp.dot(hi, eh_ref[...], preferred_element_type=F32)
                + jnp.dot(lo, eh_ref[...], preferred_element_type=F32))

    attn = expand(e1 * inv_den) * token_major(o1_ref, so_ref, dils[0])
    attn = attn + expand(e2 * inv_den) * token_major(o2_ref, so_ref, dils[1])
    attn = attn + expand(e3 * inv_den) * token_major(o3_ref, so_ref, dils[2])

    o = of_ref[...] + ob_ref[...]
    parts = []
    for h in range(N_HEADS_B):
        oh = o[:, h * DK_B:(h + 1) * DK_B]
        ms = jnp.mean(oh * oh, axis=-1, keepdims=True)
        parts.append(oh * lax.rsqrt(ms + EPS))
    hg = jnp.concatenate(parts, axis=-1) * ogain_ref[...] * og_ref[...].astype(F32)

    pa = jnp.dot(attn.astype(BF16), wa_ref[...], preferred_element_type=F32)
    pb = jnp.dot(hg.astype(BF16), wb_ref[...], preferred_element_type=F32)
    mixed = (gt_ref[:, :D_MODEL].astype(F32) * pa + gt_ref[:, D_MODEL:].astype(F32) * pb)
    h = x_ref[...] + jnp.dot(mixed.astype(BF16), wo_ref[...], preferred_element_type=F32)
    h_ref[...] = h
    ms = jnp.mean(h * h, axis=-1, keepdims=True)
    hn = h * lax.rsqrt(ms + EPS) * nmoe_ref[...]
    hn_ref[...] = hn

    hi, lo = _split_bf16(hn)
    lg = (jnp.dot(hi, wrh_ref[...], preferred_element_type=F32)
          + jnp.dot(lo, wrh_ref[...], preferred_element_type=F32)
          + jnp.dot(hi, wrl_ref[...], preferred_element_type=F32)) + br_ref[...]
    lane = lax.broadcasted_iota(jnp.int32, (tm, LANES), 1)
    vals, idxs = [], []
    onehot = jnp.zeros((tm, LANES), F32)
    for _ in range(TOP_K):
        m = jnp.max(lg, axis=-1, keepdims=True)
        idx = jnp.min(jnp.where(lg == m, lane, LANES), axis=-1, keepdims=True)
        sel = lane == idx
        onehot = jnp.where(sel, 1.0, onehot)
        lg = jnp.where(sel, NEG * 2, lg)
        vals.append(m)
        idxs.append(idx)
    exps = [jnp.exp(v - vals[0]) for v in vals]
    inv = 1.0 / (exps[0] + exps[1] + exps[2] + exps[3])

    r_i = lax.broadcasted_iota(jnp.int32, (tm, tm), 0)
    c_i = lax.broadcasted_iota(jnp.int32, (tm, tm), 1)
    tri = jnp.where(r_i > c_i, 1.0, 0.0).astype(BF16)
    before = jnp.dot(tri, onehot.astype(BF16), preferred_element_type=F32) + run_ref[...]
    ti = jnp.zeros((tm, LANES), jnp.int32)
    tw = jnp.zeros((tm, LANES), F32)
    rk = jnp.zeros((tm, LANES), F32)
    for k in range(TOP_K):
        rank_k = jnp.sum(jnp.where(lane == idxs[k], before, 0.0), axis=-1, keepdims=True)
        ti = jnp.where(lane == k, idxs[k], ti)
        tw = jnp.where(lane == k, exps[k] * inv, tw)
        rk = jnp.where(lane == k, rank_k, rk)
    ti_ref[...] = ti
    tw_ref[...] = tw
    rk_ref[...] = rk.astype(jnp.int32)
    run_new = run_ref[...] + jnp.sum(onehot, axis=0, keepdims=True)
    run_ref[...] = run_new
    cnt_out_ref[...] = run_new


def _mix(x2, s, o1, o2, o3, l1, l2, l3, of, ob, og, gt, wa, wb, wo, eh, ogain, nmoe, wrh, wrl, br, cnt):
    t = x2.shape[0]
    tm = TM_MIX
    tps = s // tm
    row = lambda i: (i, 0)
    const = lambda i: (0, 0)
    seq = lambda i: (i // tps, 0, i % tps, 0)
    rb = lambda w: pl.BlockSpec((tm, w), row)
    cb = lambda a: pl.BlockSpec(a.shape, const)
    gb = lambda a: pl.BlockSpec((None, a.shape[1], tm // a.shape[1], a.shape[3]), seq)
    return pl.pallas_call(
        functools.partial(_mix_kernel, tm=tm),
        out_shape=(jax.ShapeDtypeStruct((t, D_MODEL), F32),
                   jax.ShapeDtypeStruct((t, D_MODEL), F32),
                   jax.ShapeDtypeStruct((t, LANES), jnp.int32),
                   jax.ShapeDtypeStruct((t, LANES), F32),
                   jax.ShapeDtypeStruct((t, LANES), jnp.int32),
                   jax.ShapeDtypeStruct((1, LANES), F32)),
        grid=(t // tm,),
        in_specs=[rb(D_MODEL), gb(o1), gb(o2), gb(o3), gb(l1), gb(l2), gb(l3),
                  rb(D_B), rb(D_B), rb(D_B), rb(2 * D_MODEL),
                  cb(wa), cb(wb), cb(wo), cb(eh), cb(ogain), cb(nmoe), cb(wrh), cb(wrl), cb(br), cb(cnt)],
        out_specs=(rb(D_MODEL), rb(D_MODEL), rb(LANES), rb(LANES), rb(LANES),
                   pl.BlockSpec((1, LANES), const)),
        scratch_shapes=[pltpu.VMEM((1, LANES), F32), pltpu.VMEM((GROUP_W // LANES, tm, LANES), F32),
                        pltpu.VMEM((1, tm, LANES), F32)],
        compiler_params=_cparams(("arbitrary",)),
        name="mix",
    )(x2, o1, o2, o3, l1, l2, l3, of, ob, og, gt, wa, wb, wo, eh, ogain, nmoe, wrh, wrl, br, cnt)


def _dispatch_kernel(dest_ref, hn_ref, xs_in_ref, xs_ref, sem, *, tm):
    del xs_in_ref

    def issue(r, carry):
        for k in range(TOP_K):
            d = dest_ref[r * TOP_K + k]
            pltpu.make_async_copy(hn_ref.at[pl.ds(r, 1)], xs_ref.at[pl.ds(d, 1)], sem).start()
        return carry

    lax.fori_loop(0, tm, issue, 0)
    for _ in range(TOP_K):
        pltpu.make_async_copy(hn_ref, xs_ref.at[pl.ds(0, tm)], sem).wait()


def _dispatch(dest_flat, hn, xs):
    t = hn.shape[0]
    tm = TM_ROW
    return pl.pallas_call(
        functools.partial(_dispatch_kernel, tm=tm),
        out_shape=jax.ShapeDtypeStruct(xs.shape, xs.dtype),
        grid=(t // tm,),
        in_specs=[pl.BlockSpec((tm * TOP_K,), lambda i: (i,), memory_space=pltpu.SMEM),
                  pl.BlockSpec((tm, D_MODEL), lambda i: (i, 0)),
                  pl.BlockSpec(memory_space=pl.ANY)],
        out_specs=pl.BlockSpec(memory_space=pl.ANY),
        scratch_shapes=[pltpu.SemaphoreType.DMA],
        input_output_aliases={2: 0},
        compiler_params=_cparams(("arbitrary",)),
        name="dispatch",
    )(dest_flat, hn, xs)


def _combine_kernel(dest_ref, h_ref, tw_ref, ys_ref, y_ref, buf_ref, sem, *, tm):
    def issue(r, carry):
        for k in range(TOP_K):
            d = dest_ref[r * TOP_K + k]
            pltpu.make_async_copy(ys_ref.at[pl.ds(d, 1)], buf_ref.at[k, pl.ds(r, 1)], sem).start()
        return carry

    lax.fori_loop(0, tm, issue, 0)
    for k in range(TOP_K):
        pltpu.make_async_copy(ys_ref.at[pl.ds(0, tm)], buf_ref.at[k], sem).wait()
    tw = tw_ref[...]
    acc = h_ref[...]
    for k in range(TOP_K):
        acc = acc + tw[:, k:k + 1] * buf_ref[k]
    y_ref[...] = acc


def _combine(dest_flat, h, tw, ys):
    t = h.shape[0]
    tm = TM_ROW
    return pl.pallas_call(
        functools.partial(_combine_kernel, tm=tm),
        out_shape=jax.ShapeDtypeStruct((t, D_MODEL), F32),
        grid=(t // tm,),
        in_specs=[pl.BlockSpec((tm * TOP_K,), lambda i: (i,), memory_space=pltpu.SMEM),
                  pl.BlockSpec((tm, D_MODEL), lambda i: (i, 0)),
                  pl.BlockSpec((tm, LANES), lambda i: (i, 0)),
                  pl.BlockSpec(memory_space=pl.ANY)],
        out_specs=pl.BlockSpec((tm, D_MODEL), lambda i: (i, 0)),
        scratch_shapes=[pltpu.VMEM((TOP_K, tm, D_MODEL), F32), pltpu.SemaphoreType.DMA],
        compiler_params=_cparams(("arbitrary",)),
        name="combine",
    )(dest_flat, h, tw, ys)


def _experts_kernel(be_ref, nused_ref, xs_ref, wgu_ref, bgu_ref, wd_ref, bd_ref, ys_ref):
    del be_ref
    i = pl.program_id(0)

    @pl.when(i < nused_ref[0])
    def _():
        x = xs_ref[...].astype(BF16)
        hh = jnp.dot(x, wgu_ref[0], preferred_element_type=F32) + bgu_ref[0]
        gate = jnp.minimum(hh[:, :D_FF], SWIGLU_LIMIT)
        up = jnp.clip(hh[:, D_FF:], -SWIGLU_LIMIT, SWIGLU_LIMIT)
        glu = gate * jax.nn.sigmoid(SWIGLU_ALPHA * gate)
        act = ((up + 1.0) * glu).astype(BF16)
        ys_ref[...] = jnp.dot(act, wd_ref[0], preferred_element_type=F32) + bd_ref[0]

    @pl.when(i >= nused_ref[0])
    def _():
        ys_ref[...] = jnp.zeros_like(ys_ref)


def _experts(block_e, nused, xs, wgu, bgu, wd, bd):
    p = xs.shape[0]
    nb = p // BM
    emap3 = lambda i, be, nu: (be[i], 0, 0)
    return pl.pallas_call(
        _experts_kernel,
        out_shape=jax.ShapeDtypeStruct((p, D_MODEL), F32),
        grid_spec=pltpu.PrefetchScalarGridSpec(
            num_scalar_prefetch=2,
            grid=(nb,),
            in_specs=[pl.BlockSpec((BM, D_MODEL), lambda i, be, nu: (i, 0)),
                      pl.BlockSpec((1, D_MODEL, 2 * D_FF), emap3),
                      pl.BlockSpec((1, 1, 2 * D_FF), emap3),
                      pl.BlockSpec((1, D_FF, D_MODEL), emap3),
                      pl.BlockSpec((1, 1, D_MODEL), emap3)],
            out_specs=pl.BlockSpec((BM, D_MODEL), lambda i, be, nu: (i, 0)),
        ),
        compiler_params=_cparams(("arbitrary",)),
        name="experts",
    )(block_e, nused, xs, wgu, bgu, wd, bd)


def _head_indicator(n_cols, head_dim):
    e = np.zeros((n_cols, LANES), np.float32)
    e[np.arange(n_cols), np.arange(n_cols) // head_dim] = 1.0
    return e


def _mixer(x, prm):
    b, s, d = x.shape
    t = b * s
    x2 = x.reshape(t, d)
    res = _inproj(x2, b, s, prm["norm_mix"], prm["w_all"], prm["q_gain"], prm["k_gain"], prm["e_in"],
                  prm["et_in"], prm["lb_f"], prm["lb_b"])
    n_grp = len(ATTN_GROUPS)
    qs, ks, vs = res[:n_grp], res[n_grp:2 * n_grp], res[2 * n_grp:3 * n_grp]
    qb, lff, lfb, ib, og, gt = res[3 * n_grp:]
    r3 = lambda a: a.reshape(b, s, a.shape[-1])
    outs, lses = [], []
    for g in range(n_grp):
        o, lse = _attention_group(qs[g], ks[g], vs[g], g)
        outs.append(o)
        lses.append(lse)
    of, ob = _hgrn(r3(qb), r3(lff), r3(lfb), r3(ib))
    return x2, outs, lses, of.reshape(t, D_B), ob.reshape(t, D_B), og, gt


def kernel(x_prompt, x_sample, norm_mix, w_in, q_gain, k_gain, hgrn_lb, hgrn_o_gain, w_gate, w_proj_a,
           w_proj_b, w_out, norm_moe, w_router, b_router, w_gu, b_gu, w_down, b_down):
    l = 0
    lb = jnp.cumsum(jax.nn.softmax(hgrn_lb.astype(F32), axis=1), axis=1)
    wr = jnp.zeros((D_MODEL, LANES), F32).at[:, :N_EXPERTS].set(w_router[l])
    wr_hi = wr.astype(BF16)
    prm = {
        "norm_mix": norm_mix[l].reshape(1, D_MODEL),
        "w_all": jnp.concatenate([w_in[l], w_gate[l]], axis=1).astype(BF16),
        "q_gain": q_gain[l].reshape(1, W_A) * (HEAD_DIM_A ** -0.5),
        "k_gain": k_gain[l].reshape(1, W_A),
        "e_in": jnp.asarray(_head_indicator(PIECE, HEAD_DIM_A), BF16),
        "et_in": jnp.asarray(_head_indicator(PIECE, HEAD_DIM_A).T, BF16),
        "lb_f": lb[0, l].reshape(1, D_B),
        "lb_b": lb[1, l].reshape(1, D_B),
    }
    wa, wb, wo = w_proj_a[l].astype(BF16), w_proj_b[l].astype(BF16), w_out[l].astype(BF16)
    eh = jnp.asarray(_head_indicator(GROUP_W, HEAD_DIM_A).T, BF16)
    ogain = hgrn_o_gain[l].reshape(1, D_B)
    nmoe = norm_moe[l].reshape(1, D_MODEL)
    wr_lo = (wr - wr_hi.astype(F32)).astype(BF16)
    br = jnp.full((1, LANES), NEG, F32).at[0, :N_EXPERTS].set(b_router[l])

    cnt = jnp.zeros((1, LANES), F32)
    per_batch = []
    for x in (x_prompt, x_sample):
        x2, outs, lses, of, ob, og, gt = _mixer(x, prm)
        h, hn, ti, tw, rk, cnt = _mix(x2, x.shape[1], outs[0], outs[1], outs[2], lses[0], lses[1], lses[2],
                                      of, ob, og, gt, wa, wb, wo, eh, ogain, nmoe, wr_hi, wr_lo, br, cnt)
        per_batch.append((x.shape, h, hn, ti, tw, rk))

    n_tok = sum(pb[1].shape[0] for pb in per_batch)
    sizes = cnt[0, :N_EXPERTS].astype(jnp.int32)
    pad_sizes = (sizes + BM - 1) // BM * BM
    pad_ends = jnp.cumsum(pad_sizes)
    pad_starts = pad_ends - pad_sizes
    nb = (n_tok * TOP_K) // BM + N_EXPERTS
    block_start = jnp.arange(nb, dtype=jnp.int32) * BM
    block_e = jnp.minimum(jnp.sum(pad_ends[None, :] <= block_start[:, None], axis=1),
                          N_EXPERTS - 1).astype(jnp.int32)
    nused = (pad_ends[-1:] // BM).astype(jnp.int32)

    xs = jnp.zeros((nb * BM, D_MODEL), F32)
    dests = []
    for _, _, hn, ti, _, rk in per_batch:
        dest = (pad_starts[ti[:, :TOP_K]] + rk[:, :TOP_K]).reshape(-1).astype(jnp.int32)
        dests.append(dest)
        xs = _dispatch(dest, hn, xs)
    ys = _experts(block_e, nused, xs, w_gu[l].astype(BF16), b_gu[l].reshape(N_EXPERTS, 1, 2 * D_FF),
                  w_down[l].astype(BF16), b_down[l].reshape(N_EXPERTS, 1, D_MODEL))
    results = []
    for (shape, h, _, _, tw, _), dest in zip(per_batch, dests):
        results.append(_combine(dest, h, tw, ys).reshape(shape))
    return tuple(results)
```

```python
import functools
import math

import jax
import jax.numpy as jnp
import numpy as np
from jax import lax
from jax.experimental import pallas as pl
from jax.experimental.pallas import tpu as pltpu

F32 = jnp.float32
BF16 = jnp.bfloat16

D_MODEL = 1024
ATTN_GROUPS = ((128, 1), (512, 4), (2048, 16))
HEADS_PER_GROUP = 8
N_HEADS_A = 24
HEAD_DIM_A = 64
W_A = N_HEADS_A * HEAD_DIM_A
GROUP_W = HEADS_PER_GROUP * HEAD_DIM_A
N_SIDE = 64
N_HEADS_B = 4
DK_B = 128
HGRN_CHUNK = 64
D_B = N_HEADS_B * DK_B
D_IN = 3 * W_A + 5 * D_B
N_EXPERTS = 32
TOP_K = 4
D_FF = 1024
SWIGLU_LIMIT = 7.0
SWIGLU_ALPHA = 1.702
EPS = 1e-6
NEG = -1e30

LANES = 128
VMEM_LIMIT = 56 * 1024 * 1024

TM_IN = 256
PIECE = 512
TQ = 512
SQ = 128
TS_HGRN = 512
TM_MIX = 256
BM = 512
SUBLANES = 8
ROW_SUB = D_MODEL // LANES
assert ROW_SUB == SUBLANES
SEG_BLOCK = 1024
SEG_CHUNK = 16


def _cparams(sem):
    return pltpu.CompilerParams(dimension_semantics=sem, vmem_limit_bytes=VMEM_LIMIT)


def _split_bf16(x):
    hi = x.astype(BF16)
    lo = (x - hi.astype(F32)).astype(BF16)
    return hi, lo


def _inproj_kernel(x_ref, nw_ref, w_ref, qg_ref, kg_ref, e_ref, et_ref, lbf_ref, lbb_ref,
                   q0_ref, q1_ref, q2_ref, k0_ref, k1_ref, k2_ref, v0_ref, v1_ref, v2_ref,
                   qb_ref, lff_ref, lfb_ref, ib_ref, og_ref, gt_ref, scr_ref, *, tm):
    q_refs, k_refs, v_refs = (q0_ref, q1_ref, q2_ref), (k0_ref, k1_ref, k2_ref), (v0_ref, v1_ref, v2_ref)
    x = x_ref[...]
    ms = jnp.mean(x * x, axis=-1, keepdims=True)
    xn = (x * lax.rsqrt(ms + EPS) * nw_ref[...]).astype(BF16)

    def proj(col):
        return jnp.dot(xn, w_ref[:, col:col + PIECE], preferred_element_type=F32)

    def head_norm(y, gain):
        ss = jnp.dot((y * y).astype(BF16), e_ref[...], preferred_element_type=F32)
        inv = lax.rsqrt(ss * (1.0 / HEAD_DIM_A) + EPS)
        hi, lo = _split_bf16(inv)
        inv_full = (jnp.dot(hi, et_ref[...], preferred_element_type=F32)
                    + jnp.dot(lo, et_ref[...], preferred_element_type=F32))
        return y * inv_full * gain

    def store_group(dst_ref, y, dil):
        if dil == 1:
            dst_ref[0, 0] = y.astype(BF16)
            return
        for c in range(GROUP_W // LANES):
            scr_ref[c] = y[:, c * LANES:(c + 1) * LANES]
        for r in range(dil):
            rows = pl.ds(r, tm // dil, stride=dil)
            dst_ref[0, r] = jnp.concatenate(
                [scr_ref[c, rows, :] for c in range(GROUP_W // LANES)], axis=-1).astype(BF16)

    for g, (_, dil) in enumerate(ATTN_GROUPS):
        c = g * GROUP_W
        store_group(q_refs[g], head_norm(proj(c), qg_ref[:, c:c + GROUP_W]), dil)
        store_group(k_refs[g], head_norm(proj(W_A + c), kg_ref[:, c:c + GROUP_W]), dil)
        store_group(v_refs[g], proj(2 * W_A + c), dil)
    base = 3 * W_A
    qb = proj(base)
    qb_ref[...] = (qb * jax.nn.sigmoid(qb) * (DK_B ** -0.5)).astype(BF16)
    for dst, lb_ref, off in ((lff_ref, lbf_ref, D_B), (lfb_ref, lbb_ref, 2 * D_B)):
        lb = lb_ref[...]
        f = lb + (1.0 - lb) * jax.nn.sigmoid(proj(base + off))
        dst[...] = jnp.log(f)
    ib_ref[...] = proj(base + 3 * D_B).astype(BF16)
    og = proj(base + 4 * D_B)
    og_ref[...] = (og * jax.nn.sigmoid(og)).astype(BF16)
    for p in range(2 * D_MODEL // PIECE):
        c = p * PIECE
        gt_ref[:, c:c + PIECE] = jax.nn.sigmoid(proj(D_IN + c)).astype(BF16)


def _inproj(x2, b, s, nw, w_all, qg, kg, e_mat, et_mat, lbf, lbb):
    t = x2.shape[0]
    tm = TM_IN
    tps = s // tm
    n_all = w_all.shape[1]
    row = lambda i: (i, 0)
    const = lambda i: (0, 0)
    seq = lambda i: (i // tps, 0, i % tps, 0)
    widths = (D_B, D_B, D_B, D_B, D_B, 2 * D_MODEL)
    dtypes = (BF16, F32, F32, BF16, BF16, BF16)
    grp_shapes = [jax.ShapeDtypeStruct((b, dil, s // dil, GROUP_W), BF16) for _, dil in ATTN_GROUPS] * 3
    grp_specs = [pl.BlockSpec((1, dil, tm // dil, GROUP_W), seq) for _, dil in ATTN_GROUPS] * 3
    return pl.pallas_call(
        functools.partial(_inproj_kernel, tm=tm),
        out_shape=tuple(grp_shapes) + tuple(jax.ShapeDtypeStruct((t, w), dt) for w, dt in zip(widths, dtypes)),
        grid=(t // tm,),
        in_specs=[
            pl.BlockSpec((tm, D_MODEL), row),
            pl.BlockSpec((1, D_MODEL), const),
            pl.BlockSpec((D_MODEL, n_all), const),
            pl.BlockSpec((1, W_A), const),
            pl.BlockSpec((1, W_A), const),
            pl.BlockSpec((PIECE, LANES), const),
            pl.BlockSpec((LANES, PIECE), const),
            pl.BlockSpec((1, D_B), const),
            pl.BlockSpec((1, D_B), const),
        ],
        out_specs=tuple(grp_specs) + tuple(pl.BlockSpec((tm, w), row) for w in widths),
        scratch_shapes=[pltpu.VMEM((GROUP_W // LANES, tm, LANES), F32)],
        compiler_params=_cparams(("parallel",)),
        name="inproj",
    )(x2, nw, w_all, qg, kg, e_mat, et_mat, lbf, lbb)


def _attn_kernel(q_ref, kp_ref, kc_ref, kn_ref, vp_ref, vc_ref, vn_ref, bias_ref,
                 o_ref, lse_ref, *, tq, sub_len):
    i = pl.program_id(2)
    sq = bias_ref.shape[1] // 2
    nk = sq + 2 * N_SIDE
    kk = jnp.concatenate([kp_ref[...], kc_ref[...], kn_ref[...]], axis=0)
    vv = jnp.concatenate([vp_ref[...], vc_ref[...], vn_ref[...]], axis=0)
    lane = lax.broadcasted_iota(jnp.int32, (sq, LANES), 1)
    low = lane < HEAD_DIM_A
    ones = jnp.ones((nk, LANES), BF16)
    zero = jnp.zeros((sq, LANES), BF16)
    for j in range(tq // sq):
        kpos = i * tq + j * sq - N_SIDE + lax.broadcasted_iota(jnp.int32, (1, nk), 1)
        colbias = jnp.where((kpos >= 0) & (kpos < sub_len), 0.0, NEG).astype(F32)
        lse_tile = jnp.zeros((sq, LANES), F32)
        for pr in range(HEADS_PER_GROUP // 2):
            cols = slice(pr * LANES, (pr + 1) * LANES)
            q2 = q_ref[j * sq:(j + 1) * sq, cols]
            k2 = kk[j * sq:j * sq + nk, cols]
            v2 = jnp.concatenate([vv[j * sq:j * sq + nk, cols], ones], axis=1)
            q_st = jnp.concatenate([jnp.where(low, q2, zero), jnp.where(low, zero, q2)], axis=0)
            s = lax.dot_general(q_st, k2, (((1,), (1,)), ((), ())), preferred_element_type=F32)
            s = s + bias_ref[pr] + colbias
            m = jnp.max(s, axis=-1, keepdims=True)
            p = jnp.exp(s - m).astype(BF16)
            r = jnp.dot(p, v2, preferred_element_type=F32)
            o2 = jnp.where(low, r[:sq, :LANES] / r[:sq, LANES:], r[sq:, :LANES] / r[sq:, LANES:])
            o_ref[j * sq:(j + 1) * sq, cols] = o2.astype(o_ref.dtype)
            lse = m + jnp.log(r[:, LANES:LANES + 1])
            lse_tile = jnp.where(lane == 2 * pr, lse[:sq], lse_tile)
            lse_tile = jnp.where(lane == 2 * pr + 1, lse[sq:], lse_tile)
        lse_ref[j * sq:(j + 1) * sq, :] = lse_tile


def _attn_bias(sq, dil, slopes):
    nk = sq + 2 * N_SIDE
    rel = np.arange(nk)[None, :] - N_SIDE - np.arange(sq)[:, None]
    band = np.abs(rel) <= N_SIDE
    alibi = -slopes[:, None, None] * (dil * np.abs(rel)).astype(np.float32)[None]
    bias = np.where(band[None], alibi, NEG).astype(np.float32)
    return jnp.asarray(bias.reshape(HEADS_PER_GROUP // 2, 2 * sq, nk))


def _attention_group(q, k, v, g):
    b, dil, sub_len, _ = q.shape
    tq = min(TQ, sub_len)
    sq = min(SQ, sub_len)
    hb = tq // N_SIDE
    n_halo = sub_len // N_SIDE
    slopes = (2.0 ** (-8.0 * (np.arange(N_HEADS_A) + 1) / N_HEADS_A)).astype(np.float32)
    bias = _attn_bias(sq, dil, slopes[g * HEADS_PER_GROUP:(g + 1) * HEADS_PER_GROUP])
    cur = lambda bi, r, i: (bi, r, i, 0)
    prev = lambda bi, r, i: (bi, r, jnp.maximum(i * hb - 1, 0), 0)
    nxt = lambda bi, r, i: (bi, r, jnp.minimum((i + 1) * hb, n_halo - 1), 0)
    blk_q = (None, None, tq, GROUP_W)
    blk_h = (None, None, N_SIDE, GROUP_W)
    return pl.pallas_call(
        functools.partial(_attn_kernel, tq=tq, sub_len=sub_len),
        out_shape=(jax.ShapeDtypeStruct((b, dil, sub_len, GROUP_W), BF16),
                   jax.ShapeDtypeStruct((b, dil, sub_len, LANES), F32)),
        grid=(b, dil, sub_len // tq),
        in_specs=[
            pl.BlockSpec(blk_q, cur),
            pl.BlockSpec(blk_h, prev), pl.BlockSpec(blk_q, cur), pl.BlockSpec(blk_h, nxt),
            pl.BlockSpec(blk_h, prev), pl.BlockSpec(blk_q, cur), pl.BlockSpec(blk_h, nxt),
            pl.BlockSpec(bias.shape, lambda bi, r, i: (0, 0, 0)),
        ],
        out_specs=(pl.BlockSpec((None, None, tq, GROUP_W), cur),
                   pl.BlockSpec((None, None, tq, LANES), cur)),
        compiler_params=_cparams(("parallel", "parallel", "parallel")),
        name=f"attn_d{dil}",
    )(q, k, k, k, v, v, v, bias)


def _hgrn_kernel(qf_ref, lf_ref, vf_ref, qr_ref, lr_ref, vr_ref, of_ref, or_ref,
                 sf_ref, sr_ref, *, ts):
    c_len = HGRN_CHUNK
    nc = ts // c_len

    @pl.when(pl.program_id(1) == 0)
    def _():
        sf_ref[...] = jnp.zeros_like(sf_ref)
        sr_ref[...] = jnp.zeros_like(sr_ref)

    r_i = lax.broadcasted_iota(jnp.int32, (c_len, c_len), 0)
    c_i = lax.broadcasted_iota(jnp.int32, (c_len, c_len), 1)
    lower = r_i >= c_i
    upper = r_i <= c_i
    tri_f = jnp.where(lower, 1.0, 0.0).astype(BF16)
    tri_r = jnp.where(upper, 1.0, 0.0).astype(BF16)

    def cumsum(tri, x):
        hi = x.astype(BF16)
        r1 = x - hi.astype(F32)
        mid = r1.astype(BF16)
        lo = (r1 - mid.astype(F32)).astype(BF16)
        d = lambda a: jnp.dot(tri, a, preferred_element_type=F32)
        return d(hi) + d(mid) + d(lo)

    def chunk(q_ref, l_ref, v_ref, o_ref, s_ref, row0, h, tri, mask, end_row):
        rows = pl.ds(row0, c_len)
        cols = slice(h * DK_B, (h + 1) * DK_B)
        lf = l_ref[0, rows, cols]
        q = q_ref[0, rows, cols].astype(F32)
        v = v_ref[0, rows, cols]
        k = 1.0 - jnp.exp(lf)
        bcum = cumsum(tri, lf)
        btot = bcum[end_row:end_row + 1, :]
        qt = (q * jnp.exp(bcum)).astype(BF16)
        kt = (k * jnp.exp(-bcum)).astype(BF16)
        sc = lax.dot_general(qt, kt, (((1,), (1,)), ((), ())), preferred_element_type=F32)
        sc = jnp.where(mask, sc, 0.0).astype(BF16)
        st = s_ref[h]
        o = (jnp.dot(sc, v, preferred_element_type=F32)
             + lax.dot_general(qt, st.astype(BF16), (((1,), (1,)), ((), ())),
                               preferred_element_type=F32))
        o_ref[0, rows, cols] = o
        kd = (k * jnp.exp(btot - bcum)).astype(BF16)
        upd = lax.dot_general(v, kd, (((0,), (0,)), ((), ())), preferred_element_type=F32)
        s_ref[h] = st * jnp.exp(btot) + upd

    def body(c, carry):
        row_f = pl.multiple_of(c * c_len, c_len)
        row_r = pl.multiple_of((nc - 1 - c) * c_len, c_len)
        for h in range(N_HEADS_B):
            chunk(qf_ref, lf_ref, vf_ref, of_ref, sf_ref, row_f, h, tri_f, lower, c_len - 1)
            chunk(qr_ref, lr_ref, vr_ref, or_ref, sr_ref, row_r, h, tri_r, upper, 0)
        return carry

    lax.fori_loop(0, nc, body, 0)


def _hgrn(qb, lff, lfb, ib):
    b, s, _ = qb.shape
    ts = min(TS_HGRN, s)
    nt = s // ts
    fwd = lambda bi, j: (bi, j, 0)
    rev = lambda bi, j: (bi, nt - 1 - j, 0)
    blk = (1, ts, D_B)
    return pl.pallas_call(
        functools.partial(_hgrn_kernel, ts=ts),
        out_shape=(jax.ShapeDtypeStruct((b, s, D_B), F32), jax.ShapeDtypeStruct((b, s, D_B), F32)),
        grid=(b, nt),
        in_specs=[pl.BlockSpec(blk, fwd), pl.BlockSpec(blk, fwd), pl.BlockSpec(blk, fwd),
                  pl.BlockSpec(blk, rev), pl.BlockSpec(blk, rev), pl.BlockSpec(blk, rev)],
        out_specs=(pl.BlockSpec(blk, fwd), pl.BlockSpec(blk, rev)),
        scratch_shapes=[pltpu.VMEM((N_HEADS_B, DK_B, DK_B), F32),
                        pltpu.VMEM((N_HEADS_B, DK_B, DK_B), F32)],
        compiler_params=_cparams(("parallel", "arbitrary")),
        name="hgrn",
    )(qb, lff, ib, qb, lfb, ib)


def _mix_kernel(x_ref, o1_ref, o2_ref, o3_ref, l1_ref, l2_ref, l3_ref, of_ref, ob_ref, og_ref,
                gt_ref, wa_ref, wb_ref, wo_ref, eh_ref, ogain_ref, nmoe_ref, wrh_ref, wrl_ref,
                br_ref, cnt_ref,
                h_ref, xt_ref, tw_ref, ps_ref, ct_ref, rt_ref, cnt_out_ref, run_ref, so_ref, sl_ref, *, tm):
    i = pl.program_id(0)

    @pl.when(i == 0)
    def _():
        run_ref[...] = cnt_ref[...]

    def token_major(src_ref, scr_ref, dil):
        if dil == 1:
            return src_ref[0].astype(F32)
        n_chunk = scr_ref.shape[0]
        for r in range(dil):
            blk = src_ref[r].astype(F32)
            for c in range(n_chunk):
                scr_ref[c, pl.ds(r, tm // dil, stride=dil), :] = blk[:, c * LANES:(c + 1) * LANES]
        return jnp.concatenate([scr_ref[c] for c in range(n_chunk)], axis=-1)

    dils = [dil for _, dil in ATTN_GROUPS]
    l1, l2, l3 = (token_major(r, sl_ref, d) for r, d in zip((l1_ref, l2_ref, l3_ref), dils))
    mx = jnp.maximum(jnp.maximum(l1, l2), l3)
    e1, e2, e3 = jnp.exp(l1 - mx), jnp.exp(l2 - mx), jnp.exp(l3 - mx)
    inv_den = 1.0 / (e1 + e2 + e3)

    def expand(w):
        hi, lo = _split_bf16(w)
        return (jnp.dot(hi, eh_ref[...], preferred_element_type=F32)
                + jnp.dot(lo, eh_ref[...], preferred_element_type=F32))

    attn = expand(e1 * inv_den) * token_major(o1_ref, so_ref, dils[0])
    attn = attn + expand(e2 * inv_den) * token_major(o2_ref, so_ref, dils[1])
    attn = attn + expand(e3 * inv_den) * token_major(o3_ref, so_ref, dils[2])

    o = of_ref[...] + ob_ref[...]
    parts = []
    for h in range(N_HEADS_B):
        oh = o[:, h * DK_B:(h + 1) * DK_B]
        ms = jnp.mean(oh * oh, axis=-1, keepdims=True)
        parts.append(oh * lax.rsqrt(ms + EPS))
    hg = jnp.concatenate(parts, axis=-1) * ogain_ref[...] * og_ref[...].astype(F32)

    pa = jnp.dot(attn.astype(BF16), wa_ref[...], preferred_element_type=F32)
    pb = jnp.dot(hg.astype(BF16), wb_ref[...], preferred_element_type=F32)
    mixed = (gt_ref[:, :D_MODEL].astype(F32) * pa + gt_ref[:, D_MODEL:].astype(F32) * pb)
    h = x_ref[...] + jnp.dot(mixed.astype(BF16), wo_ref[...], preferred_element_type=F32)
    h_ref[...] = h
    ms = jnp.mean(h * h, axis=-1, keepdims=True)
    hn = h * lax.rsqrt(ms + EPS) * nmoe_ref[...]

    hi, lo = _split_bf16(hn)
    lg = (jnp.dot(hi, wrh_ref[...], preferred_element_type=F32)
          + jnp.dot(lo, wrh_ref[...], preferred_element_type=F32)
          + jnp.dot(hi, wrl_ref[...], preferred_element_type=F32)) + br_ref[...]
    lane = lax.broadcasted_iota(jnp.int32, (tm, LANES), 1)
    vals, idxs = [], []
    onehot = jnp.zeros((tm, LANES), F32)
    for _ in range(TOP_K):
        m = jnp.max(lg, axis=-1, keepdims=True)
        idx = jnp.min(jnp.where(lg == m, lane, LANES), axis=-1, keepdims=True)
        sel = lane == idx
        onehot = jnp.where(sel, 1.0, onehot)
        lg = jnp.where(sel, NEG * 2, lg)
        vals.append(m)
        idxs.append(idx)
    exps = [jnp.exp(v - vals[0]) for v in vals]
    inv = 1.0 / (exps[0] + exps[1] + exps[2] + exps[3])

    r_i = lax.broadcasted_iota(jnp.int32, (tm, tm), 0)
    c_i = lax.broadcasted_iota(jnp.int32, (tm, tm), 1)
    tri = jnp.where(r_i > c_i, 1.0, 0.0).astype(BF16)
    local = jnp.dot(tri, onehot.astype(BF16), preferred_element_type=F32)
    cnt_tile = jnp.sum(onehot, axis=0, keepdims=True)
    e_r = lax.broadcasted_iota(jnp.int32, (LANES, LANES), 0)
    e_c = lax.broadcasted_iota(jnp.int32, (LANES, LANES), 1)
    before_e = jnp.where(e_r < e_c, 1.0, 0.0).astype(BF16)
    off = jnp.dot(jnp.broadcast_to(cnt_tile, (8, LANES)).astype(BF16), before_e,
                  preferred_element_type=F32)[0:1]
    slot = off + local
    tw = jnp.zeros((tm, LANES), F32)
    ps = jnp.full((tm, LANES), -1.0, F32)
    for k in range(TOP_K):
        slot_k = jnp.sum(jnp.where(lane == idxs[k], slot, 0.0), axis=-1, keepdims=True)
        tw = jnp.where(lane == k, exps[k] * inv, tw)
        ps = jnp.where(lane == k, slot_k, ps)
    tw_ref[...] = tw
    ps_ref[...] = ps.astype(jnp.int32)

    ps_t = ps.T
    row_id = lax.broadcasted_iota(jnp.int32, (tm * TOP_K, tm), 0).astype(F32)
    perm = jnp.zeros((tm * TOP_K, tm), F32)
    for k in range(TOP_K):
        perm = jnp.where(row_id == ps_t[k:k + 1, :], 1.0, perm)
    perm = perm.astype(BF16)
    rows = jnp.dot(perm, hi, preferred_element_type=F32)
    for c in range(D_MODEL // LANES):
        xt_ref[pl.ds(c, tm * TOP_K, stride=ROW_SUB), :] = rows[:, c * LANES:(c + 1) * LANES]

    ct_ref[0] = cnt_tile
    rt_ref[0] = run_ref[...]
    run_new = run_ref[...] + cnt_tile
    run_ref[...] = run_new
    cnt_out_ref[...] = run_new


def _mix(x2, s, o1, o2, o3, l1, l2, l3, of, ob, og, gt, wa, wb, wo, eh, ogain, nmoe, wrh, wrl, br, cnt):
    t = x2.shape[0]
    tm = TM_MIX
    tps = s // tm
    row = lambda i: (i, 0)
    const = lambda i: (0, 0)
    seq = lambda i: (i // tps, 0, i % tps, 0)
    rb = lambda w: pl.BlockSpec((tm, w), row)
    cb = lambda a: pl.BlockSpec(a.shape, const)
    gb = lambda a: pl.BlockSpec((None, a.shape[1], tm // a.shape[1], a.shape[3]), seq)
    return pl.pallas_call(
        functools.partial(_mix_kernel, tm=tm),
        out_shape=(jax.ShapeDtypeStruct((t, D_MODEL), F32),
                   jax.ShapeDtypeStruct((t * TOP_K * ROW_SUB, LANES), F32),
                   jax.ShapeDtypeStruct((t, LANES), F32),
                   jax.ShapeDtypeStruct((t, LANES), jnp.int32),
                   jax.ShapeDtypeStruct((t // tm, 1, LANES), F32),
                   jax.ShapeDtypeStruct((t // tm, 1, LANES), F32),
                   jax.ShapeDtypeStruct((1, LANES), F32)),
        grid=(t // tm,),
        in_specs=[rb(D_MODEL), gb(o1), gb(o2), gb(o3), gb(l1), gb(l2), gb(l3),
                  rb(D_B), rb(D_B), rb(D_B), rb(2 * D_MODEL),
                  cb(wa), cb(wb), cb(wo), cb(eh), cb(ogain), cb(nmoe), cb(wrh), cb(wrl), cb(br), cb(cnt)],
        out_specs=(rb(D_MODEL), pl.BlockSpec((tm * TOP_K * ROW_SUB, LANES), row), rb(LANES), rb(LANES),
                   pl.BlockSpec((1, 1, LANES), lambda i: (i, 0, 0)),
                   pl.BlockSpec((1, 1, LANES), lambda i: (i, 0, 0)),
                   pl.BlockSpec((1, LANES), const)),
        scratch_shapes=[pltpu.VMEM((1, LANES), F32), pltpu.VMEM((GROUP_W // LANES, tm, LANES), F32),
                        pltpu.VMEM((1, tm, LANES), F32)],
        compiler_params=_cparams(("arbitrary",)),
        name="mix",
    )(x2, o1, o2, o3, l1, l2, l3, of, ob, og, gt, wa, wb, wo, eh, ogain, nmoe, wrh, wrl, br, cnt)


def _segcopy_kernel(src_tbl, dst_tbl, len_tbl, src_ref, *rest):
    dst_ref, sem = rest[-2:]

    def piece(src_row, dst_row, n_rows):
        size = n_rows * ROW_SUB
        src_at = src_row * ROW_SUB
        dst_at = dst_row * ROW_SUB
        if not isinstance(src_row, int):
            src_at, dst_at = pl.multiple_of(src_at, ROW_SUB), pl.multiple_of(dst_at, ROW_SUB)
        return pltpu.make_async_copy(src_ref.at[pl.ds(src_at, size)], dst_ref.at[pl.ds(dst_at, size)], sem)

    def segment(sidx, total):
        s0, d0, n = src_tbl[sidx], dst_tbl[sidx], len_tbl[sidx]
        n_bulk = jnp.right_shift(n, SEG_CHUNK.bit_length() - 1)

        def bulk(j, carry):
            piece(s0 + j * SEG_CHUNK, d0 + j * SEG_CHUNK, SEG_CHUNK).start()
            return carry

        lax.fori_loop(0, n_bulk, bulk, 0)
        done = n_bulk * SEG_CHUNK
        bit = SEG_CHUNK // 2
        while bit >= 1:
            has = (n & bit) != 0

            @pl.when(has)
            def _(done=done, bit=bit):
                piece(s0 + done, d0 + done, bit).start()

            done = done + jnp.where(has, bit, 0)
            bit //= 2
        return total + n

    total = lax.fori_loop(0, SEG_BLOCK, segment, 0)
    max_rows = min(src_ref.shape[0], dst_ref.shape[0]) // ROW_SUB
    bit = 1
    while bit <= max_rows:
        @pl.when((total & bit) != 0)
        def _(bit=bit):
            piece(0, 0, bit).wait()

        bit *= 2


def _segcopy(src_tbl, dst_tbl, len_tbl, src, dst_rows, name, dst_init=None):
    n_seg = src_tbl.shape[0]
    tbl = pl.BlockSpec((SEG_BLOCK,), lambda i: (i,), memory_space=pltpu.SMEM)
    hbm = pl.BlockSpec(memory_space=pl.ANY)
    args = (src_tbl, dst_tbl, len_tbl, src) + (() if dst_init is None else (dst_init,))
    return pl.pallas_call(
        _segcopy_kernel,
        out_shape=jax.ShapeDtypeStruct((dst_rows * ROW_SUB, LANES), F32),
        grid=(n_seg // SEG_BLOCK,),
        in_specs=[tbl, tbl, tbl] + [hbm] * (len(args) - 3),
        out_specs=hbm,
        scratch_shapes=[pltpu.SemaphoreType.DMA],
        input_output_aliases={} if dst_init is None else {4: 0},
        compiler_params=_cparams(("arbitrary",)),
        name=name,
    )(*args)


def _combine_kernel(yt_ref, ps_ref, tw_ref, h_ref, y_ref, *, tm):
    n_rows = tm * TOP_K
    ysorted = jnp.concatenate(
        [yt_ref[pl.ds(c, n_rows, stride=ROW_SUB), :] for c in range(ROW_SUB)], axis=-1).astype(BF16)
    col = lax.broadcasted_iota(jnp.int32, (tm, n_rows), 1)
    ps = ps_ref[...]
    tw = tw_ref[...]
    pw = jnp.zeros((tm, n_rows), F32)
    for k in range(TOP_K):
        pw = jnp.where(col == ps[:, k:k + 1], tw[:, k:k + 1], pw)
    hi, lo = _split_bf16(pw)
    y_ref[...] = (h_ref[...] + jnp.dot(hi, ysorted, preferred_element_type=F32)
                  + jnp.dot(lo, ysorted, preferred_element_type=F32))


def _combine(yt, ps, tw, h):
    t = h.shape[0]
    tm = TM_MIX
    row = lambda i: (i, 0)
    return pl.pallas_call(
        functools.partial(_combine_kernel, tm=tm),
        out_shape=jax.ShapeDtypeStruct((t, D_MODEL), F32),
        grid=(t // tm,),
        in_specs=[pl.BlockSpec((tm * TOP_K * ROW_SUB, LANES), row),
                  pl.BlockSpec((tm, LANES), row),
                  pl.BlockSpec((tm, LANES), row),
                  pl.BlockSpec((tm, D_MODEL), row)],
        out_specs=pl.BlockSpec((tm, D_MODEL), row),
        compiler_params=_cparams(("parallel",)),
        name="combine",
    )(yt, ps, tw, h)


def _experts_kernel(be_ref, nused_ref, xs_ref, wgu_ref, bgu_ref, wd_ref, bd_ref, ys_ref):
    del be_ref
    i = pl.program_id(0)

    @pl.when(i < nused_ref[0])
    def _():
        x = jnp.concatenate(
            [xs_ref[pl.ds(c, BM, stride=ROW_SUB), :] for c in range(ROW_SUB)], axis=-1).astype(BF16)
        hh = jnp.dot(x, wgu_ref[0], preferred_element_type=F32) + bgu_ref[0]
        gate = jnp.minimum(hh[:, :D_FF], SWIGLU_LIMIT)
        up = jnp.clip(hh[:, D_FF:], -SWIGLU_LIMIT, SWIGLU_LIMIT)
        glu = gate * jax.nn.sigmoid(SWIGLU_ALPHA * gate)
        act = ((up + 1.0) * glu).astype(BF16)
        y = jnp.dot(act, wd_ref[0], preferred_element_type=F32) + bd_ref[0]
        for c in range(ROW_SUB):
            ys_ref[pl.ds(c, BM, stride=ROW_SUB), :] = y[:, c * LANES:(c + 1) * LANES]

    @pl.when(i >= nused_ref[0])
    def _():
        ys_ref[...] = jnp.zeros_like(ys_ref)


def _experts(block_e, nused, xs, wgu, bgu, wd, bd):
    nb = xs.shape[0] // (BM * ROW_SUB)
    emap3 = lambda i, be, nu: (be[i], 0, 0)
    rows = pl.BlockSpec((BM * ROW_SUB, LANES), lambda i, be, nu: (i, 0))
    return pl.pallas_call(
        _experts_kernel,
        out_shape=jax.ShapeDtypeStruct(xs.shape, F32),
        grid_spec=pltpu.PrefetchScalarGridSpec(
            num_scalar_prefetch=2,
            grid=(nb,),
            in_specs=[rows,
                      pl.BlockSpec((1, D_MODEL, 2 * D_FF), emap3),
                      pl.BlockSpec((1, 1, 2 * D_FF), emap3),
                      pl.BlockSpec((1, D_FF, D_MODEL), emap3),
                      pl.BlockSpec((1, 1, D_MODEL), emap3)],
            out_specs=rows,
        ),
        compiler_params=_cparams(("arbitrary",)),
        name="experts",
    )(block_e, nused, xs, wgu, bgu, wd, bd)


def _head_indicator(n_cols, head_dim):
    e = np.zeros((n_cols, LANES), np.float32)
    e[np.arange(n_cols), np.arange(n_cols) // head_dim] = 1.0
    return e


def _mixer(x, prm):
    b, s, d = x.shape
    t = b * s
    x2 = x.reshape(t, d)
    res = _inproj(x2, b, s, prm["norm_mix"], prm["w_all"], prm["q_gain"], prm["k_gain"], prm["e_in"],
                  prm["et_in"], prm["lb_f"], prm["lb_b"])
    n_grp = len(ATTN_GROUPS)
    qs, ks, vs = res[:n_grp], res[n_grp:2 * n_grp], res[2 * n_grp:3 * n_grp]
    qb, lff, lfb, ib, og, gt = res[3 * n_grp:]
    r3 = lambda a: a.reshape(b, s, a.shape[-1])
    outs, lses = [], []
    for g in range(n_grp):
        o, lse = _attention_group(qs[g], ks[g], vs[g], g)
        outs.append(o)
        lses.append(lse)
    of, ob = _hgrn(r3(qb), r3(lff), r3(lfb), r3(ib))
    return x2, outs, lses, of.reshape(t, D_B), ob.reshape(t, D_B), og, gt


def kernel(x_prompt, x_sample, norm_mix, w_in, q_gain, k_gain, hgrn_lb, hgrn_o_gain, w_gate, w_proj_a,
           w_proj_b, w_out, norm_moe, w_router, b_router, w_gu, b_gu, w_down, b_down):
    l = 0
    lb = jnp.cumsum(jax.nn.softmax(hgrn_lb.astype(F32), axis=1), axis=1)
    wr = jnp.zeros((D_MODEL, LANES), F32).at[:, :N_EXPERTS].set(w_router[l])
    wr_hi = wr.astype(BF16)
    prm = {
        "norm_mix": norm_mix[l].reshape(1, D_MODEL),
        "w_all": jnp.concatenate([w_in[l], w_gate[l]], axis=1).astype(BF16),
        "q_gain": q_gain[l].reshape(1, W_A) * (HEAD_DIM_A ** -0.5),
        "k_gain": k_gain[l].reshape(1, W_A),
        "e_in": jnp.asarray(_head_indicator(PIECE, HEAD_DIM_A), BF16),
        "et_in": jnp.asarray(_head_indicator(PIECE, HEAD_DIM_A).T, BF16),
        "lb_f": lb[0, l].reshape(1, D_B),
        "lb_b": lb[1, l].reshape(1, D_B),
    }
    wa, wb, wo = w_proj_a[l].astype(BF16), w_proj_b[l].astype(BF16), w_out[l].astype(BF16)
    eh = jnp.asarray(_head_indicator(GROUP_W, HEAD_DIM_A).T, BF16)
    ogain = hgrn_o_gain[l].reshape(1, D_B)
    nmoe = norm_moe[l].reshape(1, D_MODEL)
    wr_lo = (wr - wr_hi.astype(F32)).astype(BF16)
    br = jnp.full((1, LANES), NEG, F32).at[0, :N_EXPERTS].set(b_router[l])

    cnt = jnp.zeros((1, LANES), F32)
    per_batch = []
    for x in (x_prompt, x_sample):
        x2, outs, lses, of, ob, og, gt = _mixer(x, prm)
        h, xt, tw, ps, ct, rt, cnt = _mix(x2, x.shape[1], outs[0], outs[1], outs[2], lses[0], lses[1], lses[2],
                                          of, ob, og, gt, wa, wb, wo, eh, ogain, nmoe, wr_hi, wr_lo, br, cnt)
        per_batch.append((x.shape, h, xt, tw, ps, ct, rt))

    n_tok = sum(pb[1].shape[0] for pb in per_batch)
    sizes = cnt[0, :N_EXPERTS].astype(jnp.int32)
    pad_sizes = (sizes + BM - 1) // BM * BM
    pad_ends = jnp.cumsum(pad_sizes)
    pad_starts = pad_ends - pad_sizes
    nb = (n_tok * TOP_K) // BM + N_EXPERTS
    block_start = jnp.arange(nb, dtype=jnp.int32) * BM
    block_e = jnp.minimum(jnp.sum(pad_ends[None, :] <= block_start[:, None], axis=1),
                          N_EXPERTS - 1).astype(jnp.int32)
    nused = (pad_ends[-1:] // BM).astype(jnp.int32)

    tables = []
    for _, h, _, _, _, ct, rt in per_batch:
        n_tiles = ct.shape[0]
        cnt_te = ct[:, 0, :N_EXPERTS].astype(jnp.int32)
        tile_row = (jnp.arange(n_tiles, dtype=jnp.int32) * (TM_MIX * TOP_K))[:, None]
        tile_side = tile_row + jnp.cumsum(cnt_te, axis=1) - cnt_te
        expert_side = pad_starts[None, :] + rt[:, 0, :N_EXPERTS].astype(jnp.int32)
        n_seg = n_tiles * N_EXPERTS
        pad = (-n_seg) % SEG_BLOCK
        flat = lambda a: jnp.pad(a.reshape(-1), (0, pad))
        tables.append((flat(tile_side), flat(expert_side), flat(cnt_te)))

    xs = jnp.zeros((nb * BM * ROW_SUB, LANES), F32)
    for (_, _, xt, _, _, _, _), (tile_side, expert_side, seg_len) in zip(per_batch, tables):
        xs = _segcopy(tile_side, expert_side, seg_len, xt, nb * BM, "dispatch", dst_init=xs)
    ys = _experts(block_e, nused, xs, w_gu[l].astype(BF16), b_gu[l].reshape(N_EXPERTS, 1, 2 * D_FF),
                  w_down[l].astype(BF16), b_down[l].reshape(N_EXPERTS, 1, D_MODEL))
    results = []
    for (shape, h, _, tw, ps, _, _), (tile_side, expert_side, seg_len) in zip(per_batch, tables):
        yt = _segcopy(expert_side, tile_side, seg_len, ys, h.shape[0] * TOP_K, "undispatch")
        results.append(_combine(yt, ps, tw, h).reshape(shape))
    return tuple(results)
```

```python
import functools
import math

import jax
import jax.numpy as jnp
import numpy as np
from jax import lax
from jax.experimental import pallas as pl
from jax.experimental.pallas import tpu as pltpu

F32 = jnp.float32
BF16 = jnp.bfloat16

D_MODEL = 1024
ATTN_GROUPS = ((128, 1), (512, 4), (2048, 16))
HEADS_PER_GROUP = 8
N_HEADS_A = 24
HEAD_DIM_A = 64
W_A = N_HEADS_A * HEAD_DIM_A
GROUP_W = HEADS_PER_GROUP * HEAD_DIM_A
N_SIDE = 64
N_HEADS_B = 4
DK_B = 128
HGRN_CHUNK = 64
D_B = N_HEADS_B * DK_B
D_IN = 3 * W_A + 5 * D_B
N_EXPERTS = 32
TOP_K = 4
D_FF = 1024
SWIGLU_LIMIT = 7.0
SWIGLU_ALPHA = 1.702
EPS = 1e-6
NEG = -1e30

LANES = 128
VMEM_LIMIT = 56 * 1024 * 1024

TM_IN = 256
PIECE = 512
TQ = 512
SQ = 128
TS_HGRN = 512
TM_MIX = 256
BM = 512
SUBLANES = 8
ROW_SUB = D_MODEL // LANES
assert ROW_SUB == SUBLANES
SEG_CHUNK = 16


def _cparams(sem):
    return pltpu.CompilerParams(dimension_semantics=sem, vmem_limit_bytes=VMEM_LIMIT)


def _split_bf16(x):
    hi = x.astype(BF16)
    lo = (x - hi.astype(F32)).astype(BF16)
    return hi, lo


def _inproj_kernel(x_ref, nw_ref, w_ref, qg_ref, kg_ref, e_ref, et_ref, lbf_ref, lbb_ref,
                   q0_ref, q1_ref, q2_ref, k0_ref, k1_ref, k2_ref, v0_ref, v1_ref, v2_ref,
                   qb_ref, lff_ref, lfb_ref, ib_ref, og_ref, gt_ref, scr_ref, *, tm):
    q_refs, k_refs, v_refs = (q0_ref, q1_ref, q2_ref), (k0_ref, k1_ref, k2_ref), (v0_ref, v1_ref, v2_ref)
    x = x_ref[...]
    ms = jnp.mean(x * x, axis=-1, keepdims=True)
    xn = (x * lax.rsqrt(ms + EPS) * nw_ref[...]).astype(BF16)

    def proj(col):
        return jnp.dot(xn, w_ref[:, col:col + PIECE], preferred_element_type=F32)

    def head_norm(y, gain):
        ss = jnp.dot((y * y).astype(BF16), e_ref[...], preferred_element_type=F32)
        inv = lax.rsqrt(ss * (1.0 / HEAD_DIM_A) + EPS)
        hi, lo = _split_bf16(inv)
        inv_full = (jnp.dot(hi, et_ref[...], preferred_element_type=F32)
                    + jnp.dot(lo, et_ref[...], preferred_element_type=F32))
        return y * inv_full * gain

    def store_group(dst_ref, y, dil):
        if dil == 1:
            dst_ref[0, 0] = y.astype(BF16)
            return
        for c in range(GROUP_W // LANES):
            scr_ref[c] = y[:, c * LANES:(c + 1) * LANES]
        for r in range(dil):
            rows = pl.ds(r, tm // dil, stride=dil)
            dst_ref[0, r] = jnp.concatenate(
                [scr_ref[c, rows, :] for c in range(GROUP_W // LANES)], axis=-1).astype(BF16)

    for g, (_, dil) in enumerate(ATTN_GROUPS):
        c = g * GROUP_W
        store_group(q_refs[g], head_norm(proj(c), qg_ref[:, c:c + GROUP_W]), dil)
        store_group(k_refs[g], head_norm(proj(W_A + c), kg_ref[:, c:c + GROUP_W]), dil)
        store_group(v_refs[g], proj(2 * W_A + c), dil)
    base = 3 * W_A
    qb = proj(base)
    qb_ref[...] = (qb * jax.nn.sigmoid(qb) * (DK_B ** -0.5)).astype(BF16)
    for dst, lb_ref, off in ((lff_ref, lbf_ref, D_B), (lfb_ref, lbb_ref, 2 * D_B)):
        lb = lb_ref[...]
        f = lb + (1.0 - lb) * jax.nn.sigmoid(proj(base + off))
        dst[...] = jnp.log(f)
    ib_ref[...] = proj(base + 3 * D_B).astype(BF16)
    og = proj(base + 4 * D_B)
    og_ref[...] = (og * jax.nn.sigmoid(og)).astype(BF16)
    for p in range(2 * D_MODEL // PIECE):
        c = p * PIECE
        gt_ref[:, c:c + PIECE] = jax.nn.sigmoid(proj(D_IN + c)).astype(BF16)


def _inproj(x2, b, s, nw, w_all, qg, kg, e_mat, et_mat, lbf, lbb):
    t = x2.shape[0]
    tm = TM_IN
    tps = s // tm
    n_all = w_all.shape[1]
    row = lambda i: (i, 0)
    const = lambda i: (0, 0)
    seq = lambda i: (i // tps, 0, i % tps, 0)
    widths = (D_B, D_B, D_B, D_B, D_B, 2 * D_MODEL)
    dtypes = (BF16, F32, F32, BF16, BF16, BF16)
    grp_shapes = [jax.ShapeDtypeStruct((b, dil, s // dil, GROUP_W), BF16) for _, dil in ATTN_GROUPS] * 3
    grp_specs = [pl.BlockSpec((1, dil, tm // dil, GROUP_W), seq) for _, dil in ATTN_GROUPS] * 3
    return pl.pallas_call(
        functools.partial(_inproj_kernel, tm=tm),
        out_shape=tuple(grp_shapes) + tuple(jax.ShapeDtypeStruct((t, w), dt) for w, dt in zip(widths, dtypes)),
        grid=(t // tm,),
        in_specs=[
            pl.BlockSpec((tm, D_MODEL), row),
            pl.BlockSpec((1, D_MODEL), const),
            pl.BlockSpec((D_MODEL, n_all), const),
            pl.BlockSpec((1, W_A), const),
            pl.BlockSpec((1, W_A), const),
            pl.BlockSpec((PIECE, LANES), const),
            pl.BlockSpec((LANES, PIECE), const),
            pl.BlockSpec((1, D_B), const),
            pl.BlockSpec((1, D_B), const),
        ],
        out_specs=tuple(grp_specs) + tuple(pl.BlockSpec((tm, w), row) for w in widths),
        scratch_shapes=[pltpu.VMEM((GROUP_W // LANES, tm, LANES), F32)],
        compiler_params=_cparams(("parallel",)),
        name="inproj",
    )(x2, nw, w_all, qg, kg, e_mat, et_mat, lbf, lbb)


def _attn_kernel(q_ref, kp_ref, kc_ref, kn_ref, vp_ref, vc_ref, vn_ref, bias_ref,
                 o_ref, lse_ref, *, tq, sub_len):
    i = pl.program_id(2)
    sq = bias_ref.shape[1] // 2
    nk = sq + 2 * N_SIDE
    kk = jnp.concatenate([kp_ref[...], kc_ref[...], kn_ref[...]], axis=0)
    vv = jnp.concatenate([vp_ref[...], vc_ref[...], vn_ref[...]], axis=0)
    lane = lax.broadcasted_iota(jnp.int32, (sq, LANES), 1)
    low = lane < HEAD_DIM_A
    ones = jnp.ones((nk, LANES), BF16)
    zero = jnp.zeros((sq, LANES), BF16)
    for j in range(tq // sq):
        kpos = i * tq + j * sq - N_SIDE + lax.broadcasted_iota(jnp.int32, (1, nk), 1)
        colbias = jnp.where((kpos >= 0) & (kpos < sub_len), 0.0, NEG).astype(F32)
        lse_tile = jnp.zeros((sq, LANES), F32)
        for pr in range(HEADS_PER_GROUP // 2):
            cols = slice(pr * LANES, (pr + 1) * LANES)
            q2 = q_ref[j * sq:(j + 1) * sq, cols]
            k2 = kk[j * sq:j * sq + nk, cols]
            v2 = jnp.concatenate([vv[j * sq:j * sq + nk, cols], ones], axis=1)
            q_st = jnp.concatenate([jnp.where(low, q2, zero), jnp.where(low, zero, q2)], axis=0)
            s = lax.dot_general(q_st, k2, (((1,), (1,)), ((), ())), preferred_element_type=F32)
            s = s + bias_ref[pr] + colbias
            m = jnp.max(s, axis=-1, keepdims=True)
            p = jnp.exp(s - m).astype(BF16)
            r = jnp.dot(p, v2, preferred_element_type=F32)
            o2 = jnp.where(low, r[:sq, :LANES] / r[:sq, LANES:], r[sq:, :LANES] / r[sq:, LANES:])
            o_ref[j * sq:(j + 1) * sq, cols] = o2.astype(o_ref.dtype)
            lse = m + jnp.log(r[:, LANES:LANES + 1])
            lse_tile = jnp.where(lane == 2 * pr, lse[:sq], lse_tile)
            lse_tile = jnp.where(lane == 2 * pr + 1, lse[sq:], lse_tile)
        lse_ref[j * sq:(j + 1) * sq, :] = lse_tile


def _attn_bias(sq, dil, slopes):
    nk = sq + 2 * N_SIDE
    rel = np.arange(nk)[None, :] - N_SIDE - np.arange(sq)[:, None]
    band = np.abs(rel) <= N_SIDE
    alibi = -slopes[:, None, None] * (dil * np.abs(rel)).astype(np.float32)[None]
    bias = np.where(band[None], alibi, NEG).astype(np.float32)
    return jnp.asarray(bias.reshape(HEADS_PER_GROUP // 2, 2 * sq, nk))


def _attention_group(q, k, v, g):
    b, dil, sub_len, _ = q.shape
    tq = min(TQ, sub_len)
    sq = min(SQ, sub_len)
    hb = tq // N_SIDE
    n_halo = sub_len // N_SIDE
    slopes = (2.0 ** (-8.0 * (np.arange(N_HEADS_A) + 1) / N_HEADS_A)).astype(np.float32)
    bias = _attn_bias(sq, dil, slopes[g * HEADS_PER_GROUP:(g + 1) * HEADS_PER_GROUP])
    cur = lambda bi, r, i: (bi, r, i, 0)
    prev = lambda bi, r, i: (bi, r, jnp.maximum(i * hb - 1, 0), 0)
    nxt = lambda bi, r, i: (bi, r, jnp.minimum((i + 1) * hb, n_halo - 1), 0)
    blk_q = (None, None, tq, GROUP_W)
    blk_h = (None, None, N_SIDE, GROUP_W)
    return pl.pallas_call(
        functools.partial(_attn_kernel, tq=tq, sub_len=sub_len),
        out_shape=(jax.ShapeDtypeStruct((b, dil, sub_len, GROUP_W), BF16),
                   jax.ShapeDtypeStruct((b, dil, sub_len, LANES), F32)),
        grid=(b, dil, sub_len // tq),
        in_specs=[
            pl.BlockSpec(blk_q, cur),
            pl.BlockSpec(blk_h, prev), pl.BlockSpec(blk_q, cur), pl.BlockSpec(blk_h, nxt),
            pl.BlockSpec(blk_h, prev), pl.BlockSpec(blk_q, cur), pl.BlockSpec(blk_h, nxt),
            pl.BlockSpec(bias.shape, lambda bi, r, i: (0, 0, 0)),
        ],
        out_specs=(pl.BlockSpec((None, None, tq, GROUP_W), cur),
                   pl.BlockSpec((None, None, tq, LANES), cur)),
        compiler_params=_cparams(("parallel", "parallel", "parallel")),
        name=f"attn_d{dil}",
    )(q, k, k, k, v, v, v, bias)


def _hgrn_kernel(qf_ref, lf_ref, vf_ref, qr_ref, lr_ref, vr_ref, of_ref, or_ref,
                 sf_ref, sr_ref, *, ts):
    c_len = HGRN_CHUNK
    nc = ts // c_len

    @pl.when(pl.program_id(1) == 0)
    def _():
        sf_ref[...] = jnp.zeros_like(sf_ref)
        sr_ref[...] = jnp.zeros_like(sr_ref)

    r_i = lax.broadcasted_iota(jnp.int32, (c_len, c_len), 0)
    c_i = lax.broadcasted_iota(jnp.int32, (c_len, c_len), 1)
    lower = r_i >= c_i
    upper = r_i <= c_i
    tri_f = jnp.where(lower, 1.0, 0.0).astype(BF16)
    tri_r = jnp.where(upper, 1.0, 0.0).astype(BF16)

    def cumsum(tri, x):
        hi = x.astype(BF16)
        r1 = x - hi.astype(F32)
        mid = r1.astype(BF16)
        lo = (r1 - mid.astype(F32)).astype(BF16)
        d = lambda a: jnp.dot(tri, a, preferred_element_type=F32)
        return d(hi) + d(mid) + d(lo)

    def chunk(q_ref, l_ref, v_ref, o_ref, s_ref, row0, h, tri, mask, end_row):
        rows = pl.ds(row0, c_len)
        cols = slice(h * DK_B, (h + 1) * DK_B)
        lf = l_ref[0, rows, cols]
        q = q_ref[0, rows, cols].astype(F32)
        v = v_ref[0, rows, cols]
        k = 1.0 - jnp.exp(lf)
        bcum = cumsum(tri, lf)
        btot = bcum[end_row:end_row + 1, :]
        qt = (q * jnp.exp(bcum)).astype(BF16)
        kt = (k * jnp.exp(-bcum)).astype(BF16)
        sc = lax.dot_general(qt, kt, (((1,), (1,)), ((), ())), preferred_element_type=F32)
        sc = jnp.where(mask, sc, 0.0).astype(BF16)
        st = s_ref[h]
        o = (jnp.dot(sc, v, preferred_element_type=F32)
             + lax.dot_general(qt, st.astype(BF16), (((1,), (1,)), ((), ())),
                               preferred_element_type=F32))
        o_ref[0, rows, cols] = o
        kd = (k * jnp.exp(btot - bcum)).astype(BF16)
        upd = lax.dot_general(v, kd, (((0,), (0,)), ((), ())), preferred_element_type=F32)
        s_ref[h] = st * jnp.exp(btot) + upd

    def body(c, carry):
        row_f = pl.multiple_of(c * c_len, c_len)
        row_r = pl.multiple_of((nc - 1 - c) * c_len, c_len)
        for h in range(N_HEADS_B):
            chunk(qf_ref, lf_ref, vf_ref, of_ref, sf_ref, row_f, h, tri_f, lower, c_len - 1)
            chunk(qr_ref, lr_ref, vr_ref, or_ref, sr_ref, row_r, h, tri_r, upper, 0)
        return carry

    lax.fori_loop(0, nc, body, 0)


def _hgrn(qb, lff, lfb, ib):
    b, s, _ = qb.shape
    ts = min(TS_HGRN, s)
    nt = s // ts
    fwd = lambda bi, j: (bi, j, 0)
    rev = lambda bi, j: (bi, nt - 1 - j, 0)
    blk = (1, ts, D_B)
    return pl.pallas_call(
        functools.partial(_hgrn_kernel, ts=ts),
        out_shape=(jax.ShapeDtypeStruct((b, s, D_B), F32), jax.ShapeDtypeStruct((b, s, D_B), F32)),
        grid=(b, nt),
        in_specs=[pl.BlockSpec(blk, fwd), pl.BlockSpec(blk, fwd), pl.BlockSpec(blk, fwd),
                  pl.BlockSpec(blk, rev), pl.BlockSpec(blk, rev), pl.BlockSpec(blk, rev)],
        out_specs=(pl.BlockSpec(blk, fwd), pl.BlockSpec(blk, rev)),
        scratch_shapes=[pltpu.VMEM((N_HEADS_B, DK_B, DK_B), F32),
                        pltpu.VMEM((N_HEADS_B, DK_B, DK_B), F32)],
        compiler_params=_cparams(("parallel", "arbitrary")),
        name="hgrn",
    )(qb, lff, ib, qb, lfb, ib)


def _mix_kernel(x_ref, o1_ref, o2_ref, o3_ref, l1_ref, l2_ref, l3_ref, of_ref, ob_ref, og_ref,
                gt_ref, wa_ref, wb_ref, wo_ref, eh_ref, ogain_ref, nmoe_ref, wrh_ref, wrl_ref,
                br_ref, cnt_ref,
                h_ref, xt_ref, tw_ref, ps_ref, ct_ref, rt_ref, cnt_out_ref, run_ref, so_ref, sl_ref, *, tm):
    i = pl.program_id(0)

    @pl.when(i == 0)
    def _():
        run_ref[...] = cnt_ref[...]

    def token_major(src_ref, scr_ref, dil):
        if dil == 1:
            return src_ref[0].astype(F32)
        n_chunk = scr_ref.shape[0]
        for r in range(dil):
            blk = src_ref[r].astype(F32)
            for c in range(n_chunk):
                scr_ref[c, pl.ds(r, tm // dil, stride=dil), :] = blk[:, c * LANES:(c + 1) * LANES]
        return jnp.concatenate([scr_ref[c] for c in range(n_chunk)], axis=-1)

    dils = [dil for _, dil in ATTN_GROUPS]
    l1, l2, l3 = (token_major(r, sl_ref, d) for r, d in zip((l1_ref, l2_ref, l3_ref), dils))
    mx = jnp.maximum(jnp.maximum(l1, l2), l3)
    e1, e2, e3 = jnp.exp(l1 - mx), jnp.exp(l2 - mx), jnp.exp(l3 - mx)
    inv_den = 1.0 / (e1 + e2 + e3)

    def expand(w):
        hi, lo = _split_bf16(w)
        return (jnp.dot(hi, eh_ref[...], preferred_element_type=F32)
                + jnp.dot(lo, eh_ref[...], preferred_element_type=F32))

    attn = expand(e1 * inv_den) * token_major(o1_ref, so_ref, dils[0])
    attn = attn + expand(e2 * inv_den) * token_major(o2_ref, so_ref, dils[1])
    attn = attn + expand(e3 * inv_den) * token_major(o3_ref, so_ref, dils[2])

    o = of_ref[...] + ob_ref[...]
    parts = []
    for h in range(N_HEADS_B):
        oh = o[:, h * DK_B:(h + 1) * DK_B]
        ms = jnp.mean(oh * oh, axis=-1, keepdims=True)
        parts.append(oh * lax.rsqrt(ms + EPS))
    hg = jnp.concatenate(parts, axis=-1) * ogain_ref[...] * og_ref[...].astype(F32)

    pa = jnp.dot(attn.astype(BF16), wa_ref[...], preferred_element_type=F32)
    pb = jnp.dot(hg.astype(BF16), wb_ref[...], preferred_element_type=F32)
    mixed = (gt_ref[:, :D_MODEL].astype(F32) * pa + gt_ref[:, D_MODEL:].astype(F32) * pb)
    h = x_ref[...] + jnp.dot(mixed.astype(BF16), wo_ref[...], preferred_element_type=F32)
    h_ref[...] = h
    ms = jnp.mean(h * h, axis=-1, keepdims=True)
    hn = h * lax.rsqrt(ms + EPS) * nmoe_ref[...]

    hi, lo = _split_bf16(hn)
    lg = (jnp.dot(hi, wrh_ref[...], preferred_element_type=F32)
          + jnp.dot(lo, wrh_ref[...], preferred_element_type=F32)
          + jnp.dot(hi, wrl_ref[...], preferred_element_type=F32)) + br_ref[...]
    lane = lax.broadcasted_iota(jnp.int32, (tm, LANES), 1)
    vals, idxs = [], []
    onehot = jnp.zeros((tm, LANES), F32)
    for _ in range(TOP_K):
        m = jnp.max(lg, axis=-1, keepdims=True)
        idx = jnp.min(jnp.where(lg == m, lane, LANES), axis=-1, keepdims=True)
        sel = lane == idx
        onehot = jnp.where(sel, 1.0, onehot)
        lg = jnp.where(sel, NEG * 2, lg)
        vals.append(m)
        idxs.append(idx)
    exps = [jnp.exp(v - vals[0]) for v in vals]
    inv = 1.0 / (exps[0] + exps[1] + exps[2] + exps[3])

    r_i = lax.broadcasted_iota(jnp.int32, (tm, tm), 0)
    c_i = lax.broadcasted_iota(jnp.int32, (tm, tm), 1)
    tri = jnp.where(r_i > c_i, 1.0, 0.0).astype(BF16)
    local = jnp.dot(tri, onehot.astype(BF16), preferred_element_type=F32)
    cnt_tile = jnp.sum(onehot, axis=0, keepdims=True)
    e_r = lax.broadcasted_iota(jnp.int32, (LANES, LANES), 0)
    e_c = lax.broadcasted_iota(jnp.int32, (LANES, LANES), 1)
    before_e = jnp.where(e_r < e_c, 1.0, 0.0).astype(BF16)
    off = jnp.dot(jnp.broadcast_to(cnt_tile, (8, LANES)).astype(BF16), before_e,
                  preferred_element_type=F32)[0:1]
    slot = off + local
    tw = jnp.zeros((tm, LANES), F32)
    ps = jnp.full((tm, LANES), -1.0, F32)
    for k in range(TOP_K):
        slot_k = jnp.sum(jnp.where(lane == idxs[k], slot, 0.0), axis=-1, keepdims=True)
        tw = jnp.where(lane == k, exps[k] * inv, tw)
        ps = jnp.where(lane == k, slot_k, ps)
    tw_ref[...] = tw
    ps_ref[...] = ps.astype(jnp.int32)

    ps_t = ps.T
    row_id = lax.broadcasted_iota(jnp.int32, (tm * TOP_K, tm), 0).astype(F32)
    perm = jnp.zeros((tm * TOP_K, tm), F32)
    for k in range(TOP_K):
        perm = jnp.where(row_id == ps_t[k:k + 1, :], 1.0, perm)
    perm = perm.astype(BF16)
    rows = jnp.dot(perm, hi, preferred_element_type=F32)
    for c in range(D_MODEL // LANES):
        xt_ref[pl.ds(c, tm * TOP_K, stride=ROW_SUB), :] = rows[:, c * LANES:(c + 1) * LANES]

    ct_ref[0] = cnt_tile
    rt_ref[0] = run_ref[...]
    run_new = run_ref[...] + cnt_tile
    run_ref[...] = run_new
    cnt_out_ref[...] = run_new


def _mix(x2, s, o1, o2, o3, l1, l2, l3, of, ob, og, gt, wa, wb, wo, eh, ogain, nmoe, wrh, wrl, br, cnt):
    t = x2.shape[0]
    tm = TM_MIX
    tps = s // tm
    row = lambda i: (i, 0)
    const = lambda i: (0, 0)
    seq = lambda i: (i // tps, 0, i % tps, 0)
    rb = lambda w: pl.BlockSpec((tm, w), row)
    cb = lambda a: pl.BlockSpec(a.shape, const)
    gb = lambda a: pl.BlockSpec((None, a.shape[1], tm // a.shape[1], a.shape[3]), seq)
    return pl.pallas_call(
        functools.partial(_mix_kernel, tm=tm),
        out_shape=(jax.ShapeDtypeStruct((t, D_MODEL), F32),
                   jax.ShapeDtypeStruct((t * TOP_K * ROW_SUB, LANES), F32),
                   jax.ShapeDtypeStruct((t, LANES), F32),
                   jax.ShapeDtypeStruct((t, LANES), jnp.int32),
                   jax.ShapeDtypeStruct((t // tm, 1, LANES), F32),
                   jax.ShapeDtypeStruct((t // tm, 1, LANES), F32),
                   jax.ShapeDtypeStruct((1, LANES), F32)),
        grid=(t // tm,),
        in_specs=[rb(D_MODEL), gb(o1), gb(o2), gb(o3), gb(l1), gb(l2), gb(l3),
                  rb(D_B), rb(D_B), rb(D_B), rb(2 * D_MODEL),
                  cb(wa), cb(wb), cb(wo), cb(eh), cb(ogain), cb(nmoe), cb(wrh), cb(wrl), cb(br), cb(cnt)],
        out_specs=(rb(D_MODEL), pl.BlockSpec((tm * TOP_K * ROW_SUB, LANES), row), rb(LANES), rb(LANES),
                   pl.BlockSpec((1, 1, LANES), lambda i: (i, 0, 0)),
                   pl.BlockSpec((1, 1, LANES), lambda i: (i, 0, 0)),
                   pl.BlockSpec((1, LANES), const)),
        scratch_shapes=[pltpu.VMEM((1, LANES), F32), pltpu.VMEM((GROUP_W // LANES, tm, LANES), F32),
                        pltpu.VMEM((1, tm, LANES), F32)],
        compiler_params=_cparams(("arbitrary",)),
        name="mix",
    )(x2, o1, o2, o3, l1, l2, l3, of, ob, og, gt, wa, wb, wo, eh, ogain, nmoe, wrh, wrl, br, cnt)


def _start_tile_segments(tile, off_tbl, row_tbl, len_tbl, make_piece):
    def segment(e, carry):
        sidx = tile * N_EXPERTS + e
        t0, r0, n = off_tbl[sidx], row_tbl[sidx], len_tbl[sidx]
        n_bulk = jnp.right_shift(n, SEG_CHUNK.bit_length() - 1)

        def bulk(j, c):
            make_piece(t0 + j * SEG_CHUNK, r0 + j * SEG_CHUNK, SEG_CHUNK).start()
            return c

        lax.fori_loop(0, n_bulk, bulk, 0)
        done = n_bulk * SEG_CHUNK
        bit = SEG_CHUNK // 2
        while bit >= 1:
            has = (n & bit) != 0

            @pl.when(has)
            def _(done=done, bit=bit):
                make_piece(t0 + done, r0 + done, bit).start()

            done = done + jnp.where(has, bit, 0)
            bit //= 2
        return carry

    lax.fori_loop(0, N_EXPERTS, segment, 0)


def _row_slice(row, n_rows):
    return pl.ds(pl.multiple_of(row * ROW_SUB, ROW_SUB), n_rows * ROW_SUB)


def _dispatch_kernel(off_tbl, row_tbl, len_tbl, xt_ref, xs_in_ref, xs_ref, sem):
    del xs_in_ref

    def piece(tile_row, expert_row, n_rows):
        return pltpu.make_async_copy(xt_ref.at[_row_slice(tile_row, n_rows)],
                                     xs_ref.at[_row_slice(expert_row, n_rows)], sem)

    _start_tile_segments(pl.program_id(0), off_tbl, row_tbl, len_tbl, piece)
    pltpu.make_async_copy(xt_ref, xs_ref.at[pl.ds(0, xt_ref.shape[0])], sem).wait()


def _dispatch(off_tbl, row_tbl, len_tbl, xt, xs):
    blk = TM_MIX * TOP_K * ROW_SUB
    return pl.pallas_call(
        _dispatch_kernel,
        out_shape=jax.ShapeDtypeStruct(xs.shape, xs.dtype),
        grid_spec=pltpu.PrefetchScalarGridSpec(
            num_scalar_prefetch=3,
            grid=(xt.shape[0] // blk,),
            in_specs=[pl.BlockSpec((blk, LANES), lambda i, *_: (i, 0)),
                      pl.BlockSpec(memory_space=pl.ANY)],
            out_specs=pl.BlockSpec(memory_space=pl.ANY),
            scratch_shapes=[pltpu.SemaphoreType.DMA],
        ),
        input_output_aliases={4: 0},
        compiler_params=_cparams(("arbitrary",)),
        name="dispatch",
    )(off_tbl, row_tbl, len_tbl, xt, xs)


def _combine_kernel(off_tbl, row_tbl, len_tbl, ps_ref, tw_ref, h_ref, ys_ref, y_ref, buf_ref, sems, *, tm):
    i = pl.program_id(0)
    n_rows = tm * TOP_K

    def start(tile):
        slot = tile % 2

        def piece(tile_row, expert_row, n):
            return pltpu.make_async_copy(ys_ref.at[_row_slice(expert_row, n)],
                                         buf_ref.at[slot, _row_slice(tile_row, n)], sems.at[slot])

        _start_tile_segments(tile, off_tbl, row_tbl, len_tbl, piece)

    @pl.when(i == 0)
    def _():
        start(i)

    @pl.when(i + 1 < pl.num_programs(0))
    def _():
        start(i + 1)

    slot = i % 2
    yt_ref = buf_ref.at[slot]
    pltpu.make_async_copy(ys_ref.at[pl.ds(0, n_rows * ROW_SUB)], yt_ref, sems.at[slot]).wait()
    ysorted = jnp.concatenate(
        [yt_ref[pl.ds(c, n_rows, stride=ROW_SUB), :] for c in range(ROW_SUB)], axis=-1).astype(BF16)
    col = lax.broadcasted_iota(jnp.int32, (tm, n_rows), 1)
    ps = ps_ref[...]
    tw = tw_ref[...]
    pw = jnp.zeros((tm, n_rows), F32)
    for k in range(TOP_K):
        pw = jnp.where(col == ps[:, k:k + 1], tw[:, k:k + 1], pw)
    hi, lo = _split_bf16(pw)
    y_ref[...] = (h_ref[...] + jnp.dot(hi, ysorted, preferred_element_type=F32)
                  + jnp.dot(lo, ysorted, preferred_element_type=F32))


def _combine(off_tbl, row_tbl, len_tbl, ps, tw, h, ys):
    t = h.shape[0]
    tm = TM_MIX
    row = lambda i, *_: (i, 0)
    return pl.pallas_call(
        functools.partial(_combine_kernel, tm=tm),
        out_shape=jax.ShapeDtypeStruct((t, D_MODEL), F32),
        grid_spec=pltpu.PrefetchScalarGridSpec(
            num_scalar_prefetch=3,
            grid=(t // tm,),
            in_specs=[pl.BlockSpec((tm, LANES), row),
                      pl.BlockSpec((tm, LANES), row),
                      pl.BlockSpec((tm, D_MODEL), row),
                      pl.BlockSpec(memory_space=pl.ANY)],
            out_specs=pl.BlockSpec((tm, D_MODEL), row),
            scratch_shapes=[pltpu.VMEM((2, tm * TOP_K * ROW_SUB, LANES), F32),
                            pltpu.SemaphoreType.DMA((2,))],
        ),
        compiler_params=_cparams(("arbitrary",)),
        name="combine",
    )(off_tbl, row_tbl, len_tbl, ps, tw, h, ys)


def _experts_kernel(be_ref, nused_ref, xs_ref, wgu_ref, bgu_ref, wd_ref, bd_ref, ys_ref):
    del be_ref
    i = pl.program_id(0)

    @pl.when(i < nused_ref[0])
    def _():
        x = jnp.concatenate(
            [xs_ref[pl.ds(c, BM, stride=ROW_SUB), :] for c in range(ROW_SUB)], axis=-1).astype(BF16)
        hh = jnp.dot(x, wgu_ref[0], preferred_element_type=F32) + bgu_ref[0]
        gate = jnp.minimum(hh[:, :D_FF], SWIGLU_LIMIT)
        up = jnp.clip(hh[:, D_FF:], -SWIGLU_LIMIT, SWIGLU_LIMIT)
        glu = gate * jax.nn.sigmoid(SWIGLU_ALPHA * gate)
        act = ((up + 1.0) * glu).astype(BF16)
        y = jnp.dot(act, wd_ref[0], preferred_element_type=F32) + bd_ref[0]
        for c in range(ROW_SUB):
            ys_ref[pl.ds(c, BM, stride=ROW_SUB), :] = y[:, c * LANES:(c + 1) * LANES]

    @pl.when(i >= nused_ref[0])
    def _():
        ys_ref[...] = jnp.zeros_like(ys_ref)


def _experts(block_e, nused, xs, wgu, bgu, wd, bd):
    nb = xs.shape[0] // (BM * ROW_SUB)
    emap3 = lambda i, be, nu: (be[i], 0, 0)
    rows = pl.BlockSpec((BM * ROW_SUB, LANES), lambda i, be, nu: (i, 0))
    return pl.pallas_call(
        _experts_kernel,
        out_shape=jax.ShapeDtypeStruct(xs.shape, F32),
        grid_spec=pltpu.PrefetchScalarGridSpec(
            num_scalar_prefetch=2,
            grid=(nb,),
            in_specs=[rows,
                      pl.BlockSpec((1, D_MODEL, 2 * D_FF), emap3),
                      pl.BlockSpec((1, 1, 2 * D_FF), emap3),
                      pl.BlockSpec((1, D_FF, D_MODEL), emap3),
                      pl.BlockSpec((1, 1, D_MODEL), emap3)],
            out_specs=rows,
        ),
        compiler_params=_cparams(("arbitrary",)),
        name="experts",
    )(block_e, nused, xs, wgu, bgu, wd, bd)


def _head_indicator(n_cols, head_dim):
    e = np.zeros((n_cols, LANES), np.float32)
    e[np.arange(n_cols), np.arange(n_cols) // head_dim] = 1.0
    return e


def _mixer(x, prm):
    b, s, d = x.shape
    t = b * s
    x2 = x.reshape(t, d)
    res = _inproj(x2, b, s, prm["norm_mix"], prm["w_all"], prm["q_gain"], prm["k_gain"], prm["e_in"],
                  prm["et_in"], prm["lb_f"], prm["lb_b"])
    n_grp = len(ATTN_GROUPS)
    qs, ks, vs = res[:n_grp], res[n_grp:2 * n_grp], res[2 * n_grp:3 * n_grp]
    qb, lff, lfb, ib, og, gt = res[3 * n_grp:]
    r3 = lambda a: a.reshape(b, s, a.shape[-1])
    outs, lses = [], []
    for g in range(n_grp):
        o, lse = _attention_group(qs[g], ks[g], vs[g], g)
        outs.append(o)
        lses.append(lse)
    of, ob = _hgrn(r3(qb), r3(lff), r3(lfb), r3(ib))
    return x2, outs, lses, of.reshape(t, D_B), ob.reshape(t, D_B), og, gt


def kernel(x_prompt, x_sample, norm_mix, w_in, q_gain, k_gain, hgrn_lb, hgrn_o_gain, w_gate, w_proj_a,
           w_proj_b, w_out, norm_moe, w_router, b_router, w_gu, b_gu, w_down, b_down):
    l = 0
    lb = jnp.cumsum(jax.nn.softmax(hgrn_lb.astype(F32), axis=1), axis=1)
    wr = jnp.zeros((D_MODEL, LANES), F32).at[:, :N_EXPERTS].set(w_router[l])
    wr_hi = wr.astype(BF16)
    prm = {
        "norm_mix": norm_mix[l].reshape(1, D_MODEL),
        "w_all": jnp.concatenate([w_in[l], w_gate[l]], axis=1).astype(BF16),
        "q_gain": q_gain[l].reshape(1, W_A) * (HEAD_DIM_A ** -0.5),
        "k_gain": k_gain[l].reshape(1, W_A),
        "e_in": jnp.asarray(_head_indicator(PIECE, HEAD_DIM_A), BF16),
        "et_in": jnp.asarray(_head_indicator(PIECE, HEAD_DIM_A).T, BF16),
        "lb_f": lb[0, l].reshape(1, D_B),
        "lb_b": lb[1, l].reshape(1, D_B),
    }
    wa, wb, wo = w_proj_a[l].astype(BF16), w_proj_b[l].astype(BF16), w_out[l].astype(BF16)
    eh = jnp.asarray(_head_indicator(GROUP_W, HEAD_DIM_A).T, BF16)
    ogain = hgrn_o_gain[l].reshape(1, D_B)
    nmoe = norm_moe[l].reshape(1, D_MODEL)
    wr_lo = (wr - wr_hi.astype(F32)).astype(BF16)
    br = jnp.full((1, LANES), NEG, F32).at[0, :N_EXPERTS].set(b_router[l])

    cnt = jnp.zeros((1, LANES), F32)
    per_batch = []
    for x in (x_prompt, x_sample):
        x2, outs, lses, of, ob, og, gt = _mixer(x, prm)
        h, xt, tw, ps, ct, rt, cnt = _mix(x2, x.shape[1], outs[0], outs[1], outs[2], lses[0], lses[1], lses[2],
                                          of, ob, og, gt, wa, wb, wo, eh, ogain, nmoe, wr_hi, wr_lo, br, cnt)
        per_batch.append((x.shape, h, xt, tw, ps, ct, rt))

    n_tok = sum(pb[1].shape[0] for pb in per_batch)
    sizes = cnt[0, :N_EXPERTS].astype(jnp.int32)
    pad_sizes = (sizes + BM - 1) // BM * BM
    pad_ends = jnp.cumsum(pad_sizes)
    pad_starts = pad_ends - pad_sizes
    nb = (n_tok * TOP_K) // BM + N_EXPERTS
    block_start = jnp.arange(nb, dtype=jnp.int32) * BM
    block_e = jnp.minimum(jnp.sum(pad_ends[None, :] <= block_start[:, None], axis=1),
                          N_EXPERTS - 1).astype(jnp.int32)
    nused = (pad_ends[-1:] // BM).astype(jnp.int32)

    tables = []
    for _, _, _, _, _, ct, rt in per_batch:
        cnt_te = ct[:, 0, :N_EXPERTS].astype(jnp.int32)
        tile_off = jnp.cumsum(cnt_te, axis=1) - cnt_te
        expert_row = pad_starts[None, :] + rt[:, 0, :N_EXPERTS].astype(jnp.int32)
        tables.append((tile_off.reshape(-1), expert_row.reshape(-1), cnt_te.reshape(-1)))

    xs = jnp.zeros((nb * BM * ROW_SUB, LANES), F32)
    for (_, _, xt, _, _, _, _), tbl in zip(per_batch, tables):
        xs = _dispatch(*tbl, xt, xs)
    ys = _experts(block_e, nused, xs, w_gu[l].astype(BF16), b_gu[l].reshape(N_EXPERTS, 1, 2 * D_FF),
                  w_down[l].astype(BF16), b_down[l].reshape(N_EXPERTS, 1, D_MODEL))
    results = []
    for (shape, h, _, tw, ps, _, _), tbl in zip(per_batch, tables):
        results.append(_combine(*tbl, ps, tw, h, ys).reshape(shape))
    return tuple(results)
```

```python
import functools
import math

import jax
import jax.numpy as jnp
import numpy as np
from jax import lax
from jax.experimental import pallas as pl
from jax.experimental.pallas import tpu as pltpu

F32 = jnp.float32
BF16 = jnp.bfloat16

D_MODEL = 1024
ATTN_GROUPS = ((128, 1), (512, 4), (2048, 16))
HEADS_PER_GROUP = 8
N_HEADS_A = 24
HEAD_DIM_A = 64
W_A = N_HEADS_A * HEAD_DIM_A
GROUP_W = HEADS_PER_GROUP * HEAD_DIM_A
N_SIDE = 64
N_HEADS_B = 4
DK_B = 128
HGRN_CHUNK = 64
HGRN_SUB = 16
HGRN_EXP_CLAMP = 80.0
HGRN_CHUNKS_PER_TRIP = 2
D_B = N_HEADS_B * DK_B
D_IN = 3 * W_A + 5 * D_B
N_EXPERTS = 32
TOP_K = 4
D_FF = 1024
SWIGLU_LIMIT = 7.0
SWIGLU_ALPHA = 1.702
EPS = 1e-6
NEG = -1e30

LANES = 128
VMEM_LIMIT = 56 * 1024 * 1024

TM_IN = 256
PIECE = 512
TQ = 512
SQ = 128
ATTN_PAIRS_IN_FLIGHT = 1
TS_HGRN = 512
TM_MIX = 256
BM = 512
SUBLANES = 8
ROW_SUB = D_MODEL // LANES
assert ROW_SUB == SUBLANES
SEG_CHUNK = 16


def _cparams(sem):
    return pltpu.CompilerParams(dimension_semantics=sem, vmem_limit_bytes=VMEM_LIMIT)


def _lockstep(*gens):
    out = [None] * len(gens)
    live = list(range(len(gens)))
    while live:
        for g in list(live):
            try:
                next(gens[g])
            except StopIteration as done:
                out[g] = done.value
                live.remove(g)
    return out


def _split_bf16(x):
    hi = x.astype(BF16)
    lo = (x - hi.astype(F32)).astype(BF16)
    return hi, lo


def _inproj_kernel(x_ref, nw_ref, w_ref, qg_ref, kg_ref, e_ref, et_ref, lbf_ref, lbb_ref,
                   q0_ref, q1_ref, q2_ref, k0_ref, k1_ref, k2_ref, v0_ref, v1_ref, v2_ref,
                   qb_ref, lff_ref, lfb_ref, ib_ref, og_ref, gt_ref, scr_ref, *, tm):
    q_refs, k_refs, v_refs = (q0_ref, q1_ref, q2_ref), (k0_ref, k1_ref, k2_ref), (v0_ref, v1_ref, v2_ref)
    x = x_ref[...]
    ms = jnp.mean(x * x, axis=-1, keepdims=True)
    xn = (x * lax.rsqrt(ms + EPS) * nw_ref[...]).astype(BF16)

    def proj(col):
        return jnp.dot(xn, w_ref[:, col:col + PIECE], preferred_element_type=F32)

    def normed(dst_ref, col, gain, dil):
        y = proj(col)
        yield
        ss = jnp.dot((y * y).astype(BF16), e_ref[...], preferred_element_type=F32)
        yield
        inv = lax.rsqrt(ss * (1.0 / HEAD_DIM_A) + EPS)
        hi, lo = _split_bf16(inv)
        inv_full = (jnp.dot(hi, et_ref[...], preferred_element_type=F32)
                    + jnp.dot(lo, et_ref[...], preferred_element_type=F32))
        yield
        store_group(dst_ref, y * inv_full * gain, dil)

    def plain(dst_ref, col, dil):
        y = proj(col)
        yield
        yield
        yield
        store_group(dst_ref, y, dil)

    def store_group(dst_ref, y, dil):
        if dil == 1:
            dst_ref[0, 0] = y.astype(BF16)
            return
        for c in range(GROUP_W // LANES):
            scr_ref[c] = y[:, c * LANES:(c + 1) * LANES]
        for r in range(dil):
            rows = pl.ds(r, tm // dil, stride=dil)
            dst_ref[0, r] = jnp.concatenate(
                [scr_ref[c, rows, :] for c in range(GROUP_W // LANES)], axis=-1).astype(BF16)

    for g, (_, dil) in enumerate(ATTN_GROUPS):
        c = g * GROUP_W
        _lockstep(normed(q_refs[g], c, qg_ref[:, c:c + GROUP_W], dil),
                  normed(k_refs[g], W_A + c, kg_ref[:, c:c + GROUP_W], dil),
                  plain(v_refs[g], 2 * W_A + c, dil))
    base = 3 * W_A
    qb = proj(base)
    qb_ref[...] = (qb * jax.nn.sigmoid(qb) * (DK_B ** -0.5)).astype(BF16)
    for dst, lb_ref, off in ((lff_ref, lbf_ref, D_B), (lfb_ref, lbb_ref, 2 * D_B)):
        lb = lb_ref[...]
        f = lb + (1.0 - lb) * jax.nn.sigmoid(proj(base + off))
        dst[...] = jnp.log(f)
    ib_ref[...] = proj(base + 3 * D_B).astype(BF16)
    og = proj(base + 4 * D_B)
    og_ref[...] = (og * jax.nn.sigmoid(og)).astype(BF16)
    for p in range(2 * D_MODEL // PIECE):
        c = p * PIECE
        gt_ref[:, c:c + PIECE] = jax.nn.sigmoid(proj(D_IN + c)).astype(BF16)


def _inproj(x2, b, s, nw, w_all, qg, kg, e_mat, et_mat, lbf, lbb):
    t = x2.shape[0]
    tm = TM_IN
    tps = s // tm
    n_all = w_all.shape[1]
    row = lambda i: (i, 0)
    const = lambda i: (0, 0)
    seq = lambda i: (i // tps, 0, i % tps, 0)
    widths = (D_B, D_B, D_B, D_B, D_B, 2 * D_MODEL)
    dtypes = (BF16, F32, F32, BF16, BF16, BF16)
    grp_shapes = [jax.ShapeDtypeStruct((b, dil, s // dil, GROUP_W), BF16) for _, dil in ATTN_GROUPS] * 3
    grp_specs = [pl.BlockSpec((1, dil, tm // dil, GROUP_W), seq) for _, dil in ATTN_GROUPS] * 3
    return pl.pallas_call(
        functools.partial(_inproj_kernel, tm=tm),
        out_shape=tuple(grp_shapes) + tuple(jax.ShapeDtypeStruct((t, w), dt) for w, dt in zip(widths, dtypes)),
        grid=(t // tm,),
        in_specs=[
            pl.BlockSpec((tm, D_MODEL), row),
            pl.BlockSpec((1, D_MODEL), const),
            pl.BlockSpec((D_MODEL, n_all), const),
            pl.BlockSpec((1, W_A), const),
            pl.BlockSpec((1, W_A), const),
            pl.BlockSpec((PIECE, LANES), const),
            pl.BlockSpec((LANES, PIECE), const),
            pl.BlockSpec((1, D_B), const),
            pl.BlockSpec((1, D_B), const),
        ],
        out_specs=tuple(grp_specs) + tuple(pl.BlockSpec((tm, w), row) for w in widths),
        scratch_shapes=[pltpu.VMEM((GROUP_W // LANES, tm, LANES), F32)],
        compiler_params=_cparams(("parallel",)),
        name="inproj",
    )(x2, nw, w_all, qg, kg, e_mat, et_mat, lbf, lbb)


def _attn_kernel(q_ref, kp_ref, kc_ref, kn_ref, vp_ref, vc_ref, vn_ref, bias_ref,
                 o_ref, lse_ref, *, tq, sub_len):
    i = pl.program_id(2)
    sq = bias_ref.shape[1] // 2
    nk = sq + 2 * N_SIDE
    kk = jnp.concatenate([kp_ref[...], kc_ref[...], kn_ref[...]], axis=0)
    vv = jnp.concatenate([vp_ref[...], vc_ref[...], vn_ref[...]], axis=0)
    lane = lax.broadcasted_iota(jnp.int32, (sq, LANES), 1)
    low = lane < HEAD_DIM_A
    ones = jnp.ones((nk, LANES), BF16)
    zero = jnp.zeros((sq, LANES), BF16)
    def pair(j, pr, colbias):
        cols = slice(pr * LANES, (pr + 1) * LANES)
        q2 = q_ref[j * sq:(j + 1) * sq, cols]
        k2 = kk[j * sq:j * sq + nk, cols]
        v2 = jnp.concatenate([vv[j * sq:j * sq + nk, cols], ones], axis=1)
        q_st = jnp.concatenate([jnp.where(low, q2, zero), jnp.where(low, zero, q2)], axis=0)
        s = lax.dot_general(q_st, k2, (((1,), (1,)), ((), ())), preferred_element_type=F32)
        yield
        s = s + bias_ref[pr] + colbias
        m = jnp.max(s, axis=-1, keepdims=True)
        p = jnp.exp(s - m).astype(BF16)
        r = jnp.dot(p, v2, preferred_element_type=F32)
        yield
        o2 = jnp.where(low, r[:sq, :LANES] / r[:sq, LANES:], r[sq:, :LANES] / r[sq:, LANES:])
        o_ref[j * sq:(j + 1) * sq, cols] = o2.astype(o_ref.dtype)
        lse = m + jnp.log(r[:, LANES:LANES + 1])
        lse_tile[0] = jnp.where(lane == 2 * pr, lse[:sq], lse_tile[0])
        lse_tile[0] = jnp.where(lane == 2 * pr + 1, lse[sq:], lse_tile[0])

    n_pairs = HEADS_PER_GROUP // 2
    for j in range(tq // sq):
        kpos = i * tq + j * sq - N_SIDE + lax.broadcasted_iota(jnp.int32, (1, nk), 1)
        colbias = jnp.where((kpos >= 0) & (kpos < sub_len), 0.0, NEG).astype(F32)
        lse_tile = [jnp.zeros((sq, LANES), F32)]
        for pr in range(0, n_pairs, ATTN_PAIRS_IN_FLIGHT):
            _lockstep(*[pair(j, pr + d, colbias) for d in range(ATTN_PAIRS_IN_FLIGHT)])
        lse_ref[j * sq:(j + 1) * sq, :] = lse_tile[0]


def _attn_bias(sq, dil, slopes):
    nk = sq + 2 * N_SIDE
    rel = np.arange(nk)[None, :] - N_SIDE - np.arange(sq)[:, None]
    band = np.abs(rel) <= N_SIDE
    alibi = -slopes[:, None, None] * (dil * np.abs(rel)).astype(np.float32)[None]
    bias = np.where(band[None], alibi, NEG).astype(np.float32)
    return jnp.asarray(bias.reshape(HEADS_PER_GROUP // 2, 2 * sq, nk))


def _attention_group(q, k, v, g):
    b, dil, sub_len, _ = q.shape
    tq = min(TQ, sub_len)
    sq = min(SQ, sub_len)
    hb = tq // N_SIDE
    n_halo = sub_len // N_SIDE
    slopes = (2.0 ** (-8.0 * (np.arange(N_HEADS_A) + 1) / N_HEADS_A)).astype(np.float32)
    bias = _attn_bias(sq, dil, slopes[g * HEADS_PER_GROUP:(g + 1) * HEADS_PER_GROUP])
    cur = lambda bi, r, i: (bi, r, i, 0)
    prev = lambda bi, r, i: (bi, r, jnp.maximum(i * hb - 1, 0), 0)
    nxt = lambda bi, r, i: (bi, r, jnp.minimum((i + 1) * hb, n_halo - 1), 0)
    blk_q = (None, None, tq, GROUP_W)
    blk_h = (None, None, N_SIDE, GROUP_W)
    return pl.pallas_call(
        functools.partial(_attn_kernel, tq=tq, sub_len=sub_len),
        out_shape=(jax.ShapeDtypeStruct((b, dil, sub_len, GROUP_W), BF16),
                   jax.ShapeDtypeStruct((b, dil, sub_len, LANES), F32)),
        grid=(b, dil, sub_len // tq),
        in_specs=[
            pl.BlockSpec(blk_q, cur),
            pl.BlockSpec(blk_h, prev), pl.BlockSpec(blk_q, cur), pl.BlockSpec(blk_h, nxt),
            pl.BlockSpec(blk_h, prev), pl.BlockSpec(blk_q, cur), pl.BlockSpec(blk_h, nxt),
            pl.BlockSpec(bias.shape, lambda bi, r, i: (0, 0, 0)),
        ],
        out_specs=(pl.BlockSpec((None, None, tq, GROUP_W), cur),
                   pl.BlockSpec((None, None, tq, LANES), cur)),
        compiler_params=_cparams(("parallel", "parallel", "parallel")),
        name=f"attn_d{dil}",
    )(q, k, k, k, v, v, v, bias)


def _hgrn_kernel(qf_ref, lf_ref, vf_ref, qr_ref, lr_ref, vr_ref, of_ref, or_ref,
                 sf_ref, sr_ref, qif_ref, qir_ref, uf_ref, ur_ref, df_ref, dr_ref, *, ts):
    c_len = HGRN_CHUNK
    nc = ts // c_len

    @pl.when(pl.program_id(1) == 0)
    def _():
        sf_ref[...] = jnp.zeros_like(sf_ref)
        sr_ref[...] = jnp.zeros_like(sr_ref)

    r_i = lax.broadcasted_iota(jnp.int32, (c_len, c_len), 0)
    c_i = lax.broadcasted_iota(jnp.int32, (c_len, c_len), 1)
    lower = r_i >= c_i
    upper = r_i <= c_i
    tri_f = jnp.where(lower, 1.0, 0.0).astype(BF16)
    tri_r = jnp.where(upper, 1.0, 0.0).astype(BF16)

    def cumsum(tri, x):
        hi = x.astype(BF16)
        r1 = x - hi.astype(F32)
        mid = r1.astype(BF16)
        lo = (r1 - mid.astype(F32)).astype(BF16)
        d = lambda a: jnp.dot(tri, a, preferred_element_type=F32)
        return d(hi) + d(mid) + d(lo)

    n_sub = c_len // HGRN_SUB
    shift = HGRN_SUB.bit_length() - 1
    sub_f = jnp.right_shift(lax.broadcasted_iota(jnp.int32, (c_len, D_B), 0), shift)
    sub_tf = jnp.right_shift(lax.broadcasted_iota(jnp.int32, (D_B, c_len), 1), shift)
    t_i = lax.broadcasted_iota(jnp.int32, (2 * c_len, c_len), 0) & (c_len - 1)
    s_i = lax.broadcasted_iota(jnp.int32, (2 * c_len, c_len), 1)
    same_sub2 = jnp.right_shift(t_i, shift) == jnp.right_shift(s_i, shift)
    lower2, upper2 = t_i >= s_i, t_i <= s_i
    lower_sub2, upper_sub2 = same_sub2 & lower2, same_sub2 & upper2
    row_head = lax.broadcasted_iota(jnp.int32, (2 * c_len, 2 * DK_B), 0) // c_len
    col_head = lax.broadcasted_iota(jnp.int32, (2 * c_len, 2 * DK_B), 1) // DK_B
    pair_cols = jnp.where(row_head == col_head, 1.0, 0.0).astype(BF16)
    pair_off = jnp.concatenate([pair_cols] * (n_sub - 1), axis=1)

    def intra(q_ref, l_ref, v_ref, o_ref, qi_ref, u_ref, d_ref, c, rev):
        rows = pl.ds(pl.multiple_of(c * c_len, c_len), c_len)
        lf = l_ref[0, rows, :]
        q = q_ref[0, rows, :].astype(F32)
        v = v_ref[0, rows, :]
        k = 1.0 - jnp.exp(lf)
        b = cumsum(tri_r if rev else tri_f, lf)
        yield
        sub = (n_sub - 1 - sub_f) if rev else sub_f

        def b_at(pos):
            r = c_len - 1 - pos if rev else pos
            return b[r:r + 1, :]

        def per_sub(vals):
            vals = vals[::-1] if rev else vals
            return jnp.concatenate([jnp.broadcast_to(x, (HGRN_SUB, D_B)) for x in vals], axis=0)

        a_end = [b_at(HGRN_SUB * j + HGRN_SUB - 1) for j in range(n_sub)]
        a_start = [jnp.zeros((1, D_B), F32)] + a_end[:-1]
        btot = a_end[-1]
        end_full, start_full = per_sub(a_end), per_sub(a_start)
        stores = [(qi_ref, (rows, slice(None)), (q * jnp.exp(b)).astype(BF16))]
        q_off = [jnp.where(sub > j, q * jnp.exp(jnp.minimum(b - a_end[j], 0.0)), 0.0).astype(BF16)
                 for j in range(n_sub - 1)]
        k_end = k * jnp.exp(end_full - b)
        sub_t = (n_sub - 1 - sub_tf) if rev else sub_tf
        q_dia = (q * jnp.exp(b - start_full)).astype(BF16)
        k_upd = (k * jnp.exp(btot - b)).astype(BF16)
        k_end_t = k_end.T
        k_dia_t = (k * jnp.exp(jnp.minimum(start_full - b, HGRN_EXP_CLAMP))).T.astype(BF16)
        k_off_t = [jnp.where(sub_t == j, k_end_t, 0.0).astype(BF16) for j in range(n_sub - 1)]
        dec8_t = jnp.broadcast_to(jnp.exp(btot), (SUBLANES, D_B)).T
        keep, keep_d = (upper2, upper_sub2) if rev else (lower2, lower_sub2)
        two = lambda a: jnp.concatenate([a, a], axis=0)
        for p in range(N_HEADS_B // 2):
            ps = slice(2 * p * DK_B, (2 * p + 2) * DK_B)
            qc = two(jnp.concatenate([q_off[j][:, ps] for j in range(n_sub - 1)], axis=1)) * pair_off
            kc = jnp.concatenate([k_off_t[j][ps, :] for j in range(n_sub - 1)], axis=0)
            yield
            sc = jnp.dot(qc, kc, preferred_element_type=F32)
            sc_d = jnp.dot(two(q_dia[:, ps]) * pair_cols, k_dia_t[ps, :], preferred_element_type=F32)
            u2 = lax.dot_general(k_upd[:, ps], v[:, ps], (((0,), (0,)), ((), ())),
                                 preferred_element_type=F32)
            yield
            sc = (jnp.where(keep, sc, 0.0) + jnp.where(keep_d, sc_d, 0.0)).astype(BF16)
            o2 = jnp.dot(sc, v[:, ps], preferred_element_type=F32)
            for i in range(2):
                h = 2 * p + i
                hs = slice(h * DK_B, (h + 1) * DK_B)
                blk = slice(i * DK_B, (i + 1) * DK_B)
                stores.append((o_ref, (0, rows, hs), o2[i * c_len:(i + 1) * c_len, blk]))
                stores.append((u_ref, (c, h), u2[blk, blk]))
                stores.append((d_ref, (c, h), jnp.broadcast_to(dec8_t[hs, 0:1], (DK_B, DK_B))))
        return stores

    def carry_state(o_ref, qi_ref, u_ref, d_ref, s_ref, c):
        rows = pl.ds(pl.multiple_of(c * c_len, c_len), c_len)
        zero = jnp.zeros((DK_B, DK_B), BF16)
        stores = []
        for p in range(N_HEADS_B // 2):
            ps = slice(2 * p * DK_B, (2 * p + 2) * DK_B)
            sa, sb = s_ref[2 * p], s_ref[2 * p + 1]
            s_bd = jnp.concatenate([jnp.concatenate([sa.astype(BF16), zero], axis=1),
                                    jnp.concatenate([zero, sb.astype(BF16)], axis=1)], axis=0)
            o_new = o_ref[0, rows, ps] + jnp.dot(qi_ref[rows, ps], s_bd, preferred_element_type=F32)
            stores.append((o_ref, (0, rows, ps), o_new))
            stores.append((s_ref, (2 * p,), sa * d_ref[c, 2 * p] + u_ref[c, 2 * p]))
            stores.append((s_ref, (2 * p + 1,), sb * d_ref[c, 2 * p + 1] + u_ref[c, 2 * p + 1]))
        return stores

    def commit(stores):
        for ref, idx, val in stores:
            ref[idx] = val

    def intra_body(trip, carry):
        gens = []
        for i in range(HGRN_CHUNKS_PER_TRIP):
            c = trip * HGRN_CHUNKS_PER_TRIP + i
            gens.append(intra(qf_ref, lf_ref, vf_ref, of_ref, qif_ref, uf_ref, df_ref, c, False))
            gens.append(intra(qr_ref, lr_ref, vr_ref, or_ref, qir_ref, ur_ref, dr_ref, c, True))
        commit(sum(_lockstep(*gens), []))
        return carry

    def state_body(c, carry):
        commit(carry_state(of_ref, qif_ref, uf_ref, df_ref, sf_ref, c)
               + carry_state(or_ref, qir_ref, ur_ref, dr_ref, sr_ref, nc - 1 - c))
        return carry

    lax.fori_loop(0, nc // HGRN_CHUNKS_PER_TRIP, intra_body, 0)
    lax.fori_loop(0, nc, state_body, 0)


def _hgrn(qb, lff, lfb, ib):
    b, s, _ = qb.shape
    ts = min(TS_HGRN, s)
    nt = s // ts
    nc = ts // HGRN_CHUNK
    fwd = lambda bi, j: (bi, j, 0)
    rev = lambda bi, j: (bi, nt - 1 - j, 0)
    blk = (1, ts, D_B)
    return pl.pallas_call(
        functools.partial(_hgrn_kernel, ts=ts),
        out_shape=(jax.ShapeDtypeStruct((b, s, D_B), F32), jax.ShapeDtypeStruct((b, s, D_B), F32)),
        grid=(b, nt),
        in_specs=[pl.BlockSpec(blk, fwd), pl.BlockSpec(blk, fwd), pl.BlockSpec(blk, fwd),
                  pl.BlockSpec(blk, rev), pl.BlockSpec(blk, rev), pl.BlockSpec(blk, rev)],
        out_specs=(pl.BlockSpec(blk, fwd), pl.BlockSpec(blk, rev)),
        scratch_shapes=[pltpu.VMEM((N_HEADS_B, DK_B, DK_B), F32),
                        pltpu.VMEM((N_HEADS_B, DK_B, DK_B), F32),
                        pltpu.VMEM((ts, D_B), BF16),
                        pltpu.VMEM((ts, D_B), BF16),
                        pltpu.VMEM((nc, N_HEADS_B, DK_B, DK_B), F32),
                        pltpu.VMEM((nc, N_HEADS_B, DK_B, DK_B), F32),
                        pltpu.VMEM((nc, N_HEADS_B, DK_B, DK_B), F32),
                        pltpu.VMEM((nc, N_HEADS_B, DK_B, DK_B), F32)],
        compiler_params=_cparams(("parallel", "arbitrary")),
        name="hgrn",
    )(qb, lff, ib, qb, lfb, ib)


def _mix_kernel(x_ref, o1_ref, o2_ref, o3_ref, l1_ref, l2_ref, l3_ref, of_ref, ob_ref, og_ref,
                gt_ref, wa_ref, wb_ref, wo_ref, eh_ref, ogain_ref, nmoe_ref, wrh_ref, wrl_ref,
                br_ref, cnt_ref,
                h_ref, xt_ref, tw_ref, ps_ref, ct_ref, rt_ref, cnt_out_ref, run_ref, so_ref, sl_ref, *, tm):
    i = pl.program_id(0)

    @pl.when(i == 0)
    def _():
        run_ref[...] = cnt_ref[...]

    def token_major(src_ref, scr_ref, dil):
        if dil == 1:
            return src_ref[0].astype(F32)
        n_chunk = scr_ref.shape[0]
        for r in range(dil):
            blk = src_ref[r].astype(F32)
            for c in range(n_chunk):
                scr_ref[c, pl.ds(r, tm // dil, stride=dil), :] = blk[:, c * LANES:(c + 1) * LANES]
        return jnp.concatenate([scr_ref[c] for c in range(n_chunk)], axis=-1)

    dils = [dil for _, dil in ATTN_GROUPS]
    l1, l2, l3 = (token_major(r, sl_ref, d) for r, d in zip((l1_ref, l2_ref, l3_ref), dils))
    mx = jnp.maximum(jnp.maximum(l1, l2), l3)
    e1, e2, e3 = jnp.exp(l1 - mx), jnp.exp(l2 - mx), jnp.exp(l3 - mx)
    inv_den = 1.0 / (e1 + e2 + e3)

    def expand(w):
        hi, lo = _split_bf16(w)
        return (jnp.dot(hi, eh_ref[...], preferred_element_type=F32)
                + jnp.dot(lo, eh_ref[...], preferred_element_type=F32))

    attn = expand(e1 * inv_den) * token_major(o1_ref, so_ref, dils[0])
    attn = attn + expand(e2 * inv_den) * token_major(o2_ref, so_ref, dils[1])
    attn = attn + expand(e3 * inv_den) * token_major(o3_ref, so_ref, dils[2])

    o = of_ref[...] + ob_ref[...]
    parts = []
    for h in range(N_HEADS_B):
        oh = o[:, h * DK_B:(h + 1) * DK_B]
        ms = jnp.mean(oh * oh, axis=-1, keepdims=True)
        parts.append(oh * lax.rsqrt(ms + EPS))
    hg = jnp.concatenate(parts, axis=-1) * ogain_ref[...] * og_ref[...].astype(F32)

    pa = jnp.dot(attn.astype(BF16), wa_ref[...], preferred_element_type=F32)
    pb = jnp.dot(hg.astype(BF16), wb_ref[...], preferred_element_type=F32)
    mixed = (gt_ref[:, :D_MODEL].astype(F32) * pa + gt_ref[:, D_MODEL:].astype(F32) * pb)
    h = x_ref[...] + jnp.dot(mixed.astype(BF16), wo_ref[...], preferred_element_type=F32)
    h_ref[...] = h
    ms = jnp.mean(h * h, axis=-1, keepdims=True)
    hn = h * lax.rsqrt(ms + EPS) * nmoe_ref[...]

    hi, lo = _split_bf16(hn)
    lg = (jnp.dot(hi, wrh_ref[...], preferred_element_type=F32)
          + jnp.dot(lo, wrh_ref[...], preferred_element_type=F32)
          + jnp.dot(hi, wrl_ref[...], preferred_element_type=F32)) + br_ref[...]
    lane = lax.broadcasted_iota(jnp.int32, (tm, LANES), 1)
    vals, idxs = [], []
    onehot = jnp.zeros((tm, LANES), F32)
    for _ in range(TOP_K):
        m = jnp.max(lg, axis=-1, keepdims=True)
        idx = jnp.min(jnp.where(lg == m, lane, LANES), axis=-1, keepdims=True)
        sel = lane == idx
        onehot = jnp.where(sel, 1.0, onehot)
        lg = jnp.where(sel, NEG * 2, lg)
        vals.append(m)
        idxs.append(idx)
    exps = [jnp.exp(v - vals[0]) for v in vals]
    inv = 1.0 / (exps[0] + exps[1] + exps[2] + exps[3])

    r_i = lax.broadcasted_iota(jnp.int32, (tm, tm), 0)
    c_i = lax.broadcasted_iota(jnp.int32, (tm, tm), 1)
    tri = jnp.where(r_i > c_i, 1.0, 0.0).astype(BF16)
    local = jnp.dot(tri, onehot.astype(BF16), preferred_element_type=F32)
    cnt_tile = jnp.sum(onehot, axis=0, keepdims=True)
    e_r = lax.broadcasted_iota(jnp.int32, (LANES, LANES), 0)
    e_c = lax.broadcasted_iota(jnp.int32, (LANES, LANES), 1)
    before_e = jnp.where(e_r < e_c, 1.0, 0.0).astype(BF16)
    off = jnp.dot(jnp.broadcast_to(cnt_tile, (8, LANES)).astype(BF16), before_e,
                  preferred_element_type=F32)[0:1]
    slot = off + local
    tw = jnp.zeros((tm, LANES), F32)
    ps = jnp.full((tm, LANES), -1.0, F32)
    for k in range(TOP_K):
        slot_k = jnp.sum(jnp.where(lane == idxs[k], slot, 0.0), axis=-1, keepdims=True)
        tw = jnp.where(lane == k, exps[k] * inv, tw)
        ps = jnp.where(lane == k, slot_k, ps)
    tw_ref[...] = tw
    ps_ref[...] = ps.astype(jnp.int32)

    ps_t = ps.T
    row_id = lax.broadcasted_iota(jnp.int32, (tm * TOP_K, tm), 0).astype(F32)
    perm = jnp.zeros((tm * TOP_K, tm), F32)
    for k in range(TOP_K):
        perm = jnp.where(row_id == ps_t[k:k + 1, :], 1.0, perm)
    perm = perm.astype(BF16)
    rows = jnp.dot(perm, hi, preferred_element_type=F32)
    for c in range(D_MODEL // LANES):
        xt_ref[pl.ds(c, tm * TOP_K, stride=ROW_SUB), :] = rows[:, c * LANES:(c + 1) * LANES]

    ct_ref[0] = cnt_tile
    rt_ref[0] = run_ref[...]
    run_new = run_ref[...] + cnt_tile
    run_ref[...] = run_new
    cnt_out_ref[...] = run_new


def _mix(x2, s, o1, o2, o3, l1, l2, l3, of, ob, og, gt, wa, wb, wo, eh, ogain, nmoe, wrh, wrl, br, cnt):
    t = x2.shape[0]
    tm = TM_MIX
    tps = s // tm
    row = lambda i: (i, 0)
    const = lambda i: (0, 0)
    seq = lambda i: (i // tps, 0, i % tps, 0)
    rb = lambda w: pl.BlockSpec((tm, w), row)
    cb = lambda a: pl.BlockSpec(a.shape, const)
    gb = lambda a: pl.BlockSpec((None, a.shape[1], tm // a.shape[1], a.shape[3]), seq)
    return pl.pallas_call(
        functools.partial(_mix_kernel, tm=tm),
        out_shape=(jax.ShapeDtypeStruct((t, D_MODEL), F32),
                   jax.ShapeDtypeStruct((t * TOP_K * ROW_SUB, LANES), F32),
                   jax.ShapeDtypeStruct((t, LANES), F32),
                   jax.ShapeDtypeStruct((t, LANES), jnp.int32),
                   jax.ShapeDtypeStruct((t // tm, 1, LANES), F32),
                   jax.ShapeDtypeStruct((t // tm, 1, LANES), F32),
                   jax.ShapeDtypeStruct((1, LANES), F32)),
        grid=(t // tm,),
        in_specs=[rb(D_MODEL), gb(o1), gb(o2), gb(o3), gb(l1), gb(l2), gb(l3),
                  rb(D_B), rb(D_B), rb(D_B), rb(2 * D_MODEL),
                  cb(wa), cb(wb), cb(wo), cb(eh), cb(ogain), cb(nmoe), cb(wrh), cb(wrl), cb(br), cb(cnt)],
        out_specs=(rb(D_MODEL), pl.BlockSpec((tm * TOP_K * ROW_SUB, LANES), row), rb(LANES), rb(LANES),
                   pl.BlockSpec((1, 1, LANES), lambda i: (i, 0, 0)),
                   pl.BlockSpec((1, 1, LANES), lambda i: (i, 0, 0)),
                   pl.BlockSpec((1, LANES), const)),
        scratch_shapes=[pltpu.VMEM((1, LANES), F32), pltpu.VMEM((GROUP_W // LANES, tm, LANES), F32),
                        pltpu.VMEM((1, tm, LANES), F32)],
        compiler_params=_cparams(("arbitrary",)),
        name="mix",
    )(x2, o1, o2, o3, l1, l2, l3, of, ob, og, gt, wa, wb, wo, eh, ogain, nmoe, wrh, wrl, br, cnt)


def _start_tile_segments(tile, off_tbl, row_tbl, len_tbl, make_piece):
    def segment(e, carry):
        sidx = tile * N_EXPERTS + e
        t0, r0, n = off_tbl[sidx], row_tbl[sidx], len_tbl[sidx]
        n_bulk = jnp.right_shift(n, SEG_CHUNK.bit_length() - 1)

        def bulk(j, c):
            make_piece(t0 + j * SEG_CHUNK, r0 + j * SEG_CHUNK, SEG_CHUNK).start()
            return c

        lax.fori_loop(0, n_bulk, bulk, 0)
        done = n_bulk * SEG_CHUNK
        bit = SEG_CHUNK // 2
        while bit >= 1:
            has = (n & bit) != 0

            @pl.when(has)
            def _(done=done, bit=bit):
                make_piece(t0 + done, r0 + done, bit).start()

            done = done + jnp.where(has, bit, 0)
            bit //= 2
        return carry

    lax.fori_loop(0, N_EXPERTS, segment, 0)


def _row_slice(row, n_rows):
    return pl.ds(pl.multiple_of(row * ROW_SUB, ROW_SUB), n_rows * ROW_SUB)


def _dispatch_kernel(off_tbl, row_tbl, len_tbl, xt_ref, xs_in_ref, xs_ref, sem):
    del xs_in_ref

    def piece(tile_row, expert_row, n_rows):
        return pltpu.make_async_copy(xt_ref.at[_row_slice(tile_row, n_rows)],
                                     xs_ref.at[_row_slice(expert_row, n_rows)], sem)

    _start_tile_segments(pl.program_id(0), off_tbl, row_tbl, len_tbl, piece)
    pltpu.make_async_copy(xt_ref, xs_ref.at[pl.ds(0, xt_ref.shape[0])], sem).wait()


def _dispatch(off_tbl, row_tbl, len_tbl, xt, xs):
    blk = TM_MIX * TOP_K * ROW_SUB
    return pl.pallas_call(
        _dispatch_kernel,
        out_shape=jax.ShapeDtypeStruct(xs.shape, xs.dtype),
        grid_spec=pltpu.PrefetchScalarGridSpec(
            num_scalar_prefetch=3,
            grid=(xt.shape[0] // blk,),
            in_specs=[pl.BlockSpec((blk, LANES), lambda i, *_: (i, 0)),
                      pl.BlockSpec(memory_space=pl.ANY)],
            out_specs=pl.BlockSpec(memory_space=pl.ANY),
            scratch_shapes=[pltpu.SemaphoreType.DMA],
        ),
        input_output_aliases={4: 0},
        compiler_params=_cparams(("arbitrary",)),
        name="dispatch",
    )(off_tbl, row_tbl, len_tbl, xt, xs)


def _combine_kernel(off_tbl, row_tbl, len_tbl, ps_ref, tw_ref, h_ref, ys_ref, y_ref, buf_ref, sems, *, tm):
    i = pl.program_id(0)
    n_rows = tm * TOP_K

    def start(tile):
        slot = tile % 2

        def piece(tile_row, expert_row, n):
            return pltpu.make_async_copy(ys_ref.at[_row_slice(expert_row, n)],
                                         buf_ref.at[slot, _row_slice(tile_row, n)], sems.at[slot])

        _start_tile_segments(tile, off_tbl, row_tbl, len_tbl, piece)

    @pl.when(i == 0)
    def _():
        start(i)

    @pl.when(i + 1 < pl.num_programs(0))
    def _():
        start(i + 1)

    slot = i % 2
    yt_ref = buf_ref.at[slot]
    pltpu.make_async_copy(ys_ref.at[pl.ds(0, n_rows * ROW_SUB)], yt_ref, sems.at[slot]).wait()
    ysorted = jnp.concatenate(
        [yt_ref[pl.ds(c, n_rows, stride=ROW_SUB), :] for c in range(ROW_SUB)], axis=-1).astype(BF16)
    col = lax.broadcasted_iota(jnp.int32, (tm, n_rows), 1)
    ps = ps_ref[...]
    tw = tw_ref[...]
    pw = jnp.zeros((tm, n_rows), F32)
    for k in range(TOP_K):
        pw = jnp.where(col == ps[:, k:k + 1], tw[:, k:k + 1], pw)
    hi, lo = _split_bf16(pw)
    y_ref[...] = (h_ref[...] + jnp.dot(hi, ysorted, preferred_element_type=F32)
                  + jnp.dot(lo, ysorted, preferred_element_type=F32))


def _combine(off_tbl, row_tbl, len_tbl, ps, tw, h, ys):
    t = h.shape[0]
    tm = TM_MIX
    row = lambda i, *_: (i, 0)
    return pl.pallas_call(
        functools.partial(_combine_kernel, tm=tm),
        out_shape=jax.ShapeDtypeStruct((t, D_MODEL), F32),
        grid_spec=pltpu.PrefetchScalarGridSpec(
            num_scalar_prefetch=3,
            grid=(t // tm,),
            in_specs=[pl.BlockSpec((tm, LANES), row),
                      pl.BlockSpec((tm, LANES), row),
                      pl.BlockSpec((tm, D_MODEL), row),
                      pl.BlockSpec(memory_space=pl.ANY)],
            out_specs=pl.BlockSpec((tm, D_MODEL), row),
            scratch_shapes=[pltpu.VMEM((2, tm * TOP_K * ROW_SUB, LANES), F32),
                            pltpu.SemaphoreType.DMA((2,))],
        ),
        compiler_params=_cparams(("arbitrary",)),
        name="combine",
    )(off_tbl, row_tbl, len_tbl, ps, tw, h, ys)


def _experts_kernel(be_ref, nused_ref, xs_ref, wgu_ref, bgu_ref, wd_ref, bd_ref, ys_ref):
    del be_ref
    i = pl.program_id(0)

    @pl.when(i < nused_ref[0])
    def _():
        x = jnp.concatenate(
            [xs_ref[pl.ds(c, BM, stride=ROW_SUB), :] for c in range(ROW_SUB)], axis=-1).astype(BF16)
        hh = jnp.dot(x, wgu_ref[0], preferred_element_type=F32) + bgu_ref[0]
        gate = jnp.minimum(hh[:, :D_FF], SWIGLU_LIMIT)
        up = jnp.clip(hh[:, D_FF:], -SWIGLU_LIMIT, SWIGLU_LIMIT)
        glu = gate * jax.nn.sigmoid(SWIGLU_ALPHA * gate)
        act = ((up + 1.0) * glu).astype(BF16)
        y = jnp.dot(act, wd_ref[0], preferred_element_type=F32) + bd_ref[0]
        for c in range(ROW_SUB):
            ys_ref[pl.ds(c, BM, stride=ROW_SUB), :] = y[:, c * LANES:(c + 1) * LANES]

    @pl.when(i >= nused_ref[0])
    def _():
        ys_ref[...] = jnp.zeros_like(ys_ref)


def _experts(block_e, nused, xs, wgu, bgu, wd, bd):
    nb = xs.shape[0] // (BM * ROW_SUB)
    emap3 = lambda i, be, nu: (be[i], 0, 0)
    rows = pl.BlockSpec((BM * ROW_SUB, LANES), lambda i, be, nu: (i, 0))
    return pl.pallas_call(
        _experts_kernel,
        out_shape=jax.ShapeDtypeStruct(xs.shape, F32),
        grid_spec=pltpu.PrefetchScalarGridSpec(
            num_scalar_prefetch=2,
            grid=(nb,),
            in_specs=[rows,
                      pl.BlockSpec((1, D_MODEL, 2 * D_FF), emap3),
                      pl.BlockSpec((1, 1, 2 * D_FF), emap3),
                      pl.BlockSpec((1, D_FF, D_MODEL), emap3),
                      pl.BlockSpec((1, 1, D_MODEL), emap3)],
            out_specs=rows,
        ),
        compiler_params=_cparams(("arbitrary",)),
        name="experts",
    )(block_e, nused, xs, wgu, bgu, wd, bd)


def _head_indicator(n_cols, head_dim):
    e = np.zeros((n_cols, LANES), np.float32)
    e[np.arange(n_cols), np.arange(n_cols) // head_dim] = 1.0
    return e


def _mixer(x, prm):
    b, s, d = x.shape
    t = b * s
    x2 = x.reshape(t, d)
    res = _inproj(x2, b, s, prm["norm_mix"], prm["w_all"], prm["q_gain"], prm["k_gain"], prm["e_in"],
                  prm["et_in"], prm["lb_f"], prm["lb_b"])
    n_grp = len(ATTN_GROUPS)
    qs, ks, vs = res[:n_grp], res[n_grp:2 * n_grp], res[2 * n_grp:3 * n_grp]
    qb, lff, lfb, ib, og, gt = res[3 * n_grp:]
    r3 = lambda a: a.reshape(b, s, a.shape[-1])
    outs, lses = [], []
    for g in range(n_grp):
        o, lse = _attention_group(qs[g], ks[g], vs[g], g)
        outs.append(o)
        lses.append(lse)
    of, ob = _hgrn(r3(qb), r3(lff), r3(lfb), r3(ib))
    return x2, outs, lses, of.reshape(t, D_B), ob.reshape(t, D_B), og, gt


def kernel(x_prompt, x_sample, norm_mix, w_in, q_gain, k_gain, hgrn_lb, hgrn_o_gain, w_gate, w_proj_a,
           w_proj_b, w_out, norm_moe, w_router, b_router, w_gu, b_gu, w_down, b_down):
    l = 0
    lb = jnp.cumsum(jax.nn.softmax(hgrn_lb.astype(F32), axis=1), axis=1)
    wr = jnp.zeros((D_MODEL, LANES), F32).at[:, :N_EXPERTS].set(w_router[l])
    wr_hi = wr.astype(BF16)
    prm = {
        "norm_mix": norm_mix[l].reshape(1, D_MODEL),
        "w_all": jnp.concatenate([w_in[l], w_gate[l]], axis=1).astype(BF16),
        "q_gain": q_gain[l].reshape(1, W_A) * (HEAD_DIM_A ** -0.5),
        "k_gain": k_gain[l].reshape(1, W_A),
        "e_in": jnp.asarray(_head_indicator(PIECE, HEAD_DIM_A), BF16),
        "et_in": jnp.asarray(_head_indicator(PIECE, HEAD_DIM_A).T, BF16),
        "lb_f": lb[0, l].reshape(1, D_B),
        "lb_b": lb[1, l].reshape(1, D_B),
    }
    wa, wb, wo = w_proj_a[l].astype(BF16), w_proj_b[l].astype(BF16), w_out[l].astype(BF16)
    eh = jnp.asarray(_head_indicator(GROUP_W, HEAD_DIM_A).T, BF16)
    ogain = hgrn_o_gain[l].reshape(1, D_B)
    nmoe = norm_moe[l].reshape(1, D_MODEL)
    wr_lo = (wr - wr_hi.astype(F32)).astype(BF16)
    br = jnp.full((1, LANES), NEG, F32).at[0, :N_EXPERTS].set(b_router[l])

    cnt = jnp.zeros((1, LANES), F32)
    per_batch = []
    for x in (x_prompt, x_sample):
        x2, outs, lses, of, ob, og, gt = _mixer(x, prm)
        h, xt, tw, ps, ct, rt, cnt = _mix(x2, x.shape[1], outs[0], outs[1], outs[2], lses[0], lses[1], lses[2],
                                          of, ob, og, gt, wa, wb, wo, eh, ogain, nmoe, wr_hi, wr_lo, br, cnt)
        per_batch.append((x.shape, h, xt, tw, ps, ct, rt))

    n_tok = sum(pb[1].shape[0] for pb in per_batch)
    sizes = cnt[0, :N_EXPERTS].astype(jnp.int32)
    pad_sizes = (sizes + BM - 1) // BM * BM
    pad_ends = jnp.cumsum(pad_sizes)
    pad_starts = pad_ends - pad_sizes
    nb = (n_tok * TOP_K) // BM + N_EXPERTS
    block_start = jnp.arange(nb, dtype=jnp.int32) * BM
    block_e = jnp.minimum(jnp.sum(pad_ends[None, :] <= block_start[:, None], axis=1),
                          N_EXPERTS - 1).astype(jnp.int32)
    nused = (pad_ends[-1:] // BM).astype(jnp.int32)

    tables = []
    for _, _, _, _, _, ct, rt in per_batch:
        cnt_te = ct[:, 0, :N_EXPERTS].astype(jnp.int32)
        tile_off = jnp.cumsum(cnt_te, axis=1) - cnt_te
        expert_row = pad_starts[None, :] + rt[:, 0, :N_EXPERTS].astype(jnp.int32)
        tables.append((tile_off.reshape(-1), expert_row.reshape(-1), cnt_te.reshape(-1)))

    xs = jnp.zeros((nb * BM * ROW_SUB, LANES), F32)
    for (_, _, xt, _, _, _, _), tbl in zip(per_batch, tables):
        xs = _dispatch(*tbl, xt, xs)
    ys = _experts(block_e, nused, xs, w_gu[l].astype(BF16), b_gu[l].reshape(N_EXPERTS, 1, 2 * D_FF),
                  w_down[l].astype(BF16), b_down[l].reshape(N_EXPERTS, 1, D_MODEL))
    results = []
    for (shape, h, _, tw, ps, _, _), tbl in zip(per_batch, tables):
        results.append(_combine(*tbl, ps, tw, h, ys).reshape(shape))
    return tuple(results)
```

```python
import functools
import math

import jax
import jax.numpy as jnp
import numpy as np
from jax import lax
from jax.experimental import pallas as pl
from jax.experimental.pallas import tpu as pltpu

F32 = jnp.float32
BF16 = jnp.bfloat16

D_MODEL = 1024
ATTN_GROUPS = ((128, 1), (512, 4), (2048, 16))
HEADS_PER_GROUP = 8
N_HEADS_A = 24
HEAD_DIM_A = 64
W_A = N_HEADS_A * HEAD_DIM_A
GROUP_W = HEADS_PER_GROUP * HEAD_DIM_A
N_SIDE = 64
N_HEADS_B = 4
DK_B = 128
HGRN_CHUNK = 64
HGRN_SUB = 16
HGRN_EXP_CLAMP = 80.0
HGRN_CHUNKS_PER_TRIP = 2
D_B = N_HEADS_B * DK_B
D_IN = 3 * W_A + 5 * D_B
N_EXPERTS = 32
TOP_K = 4
D_FF = 1024
SWIGLU_LIMIT = 7.0
SWIGLU_ALPHA = 1.702
EPS = 1e-6
NEG = -1e30

LANES = 128
VMEM_LIMIT = 56 * 1024 * 1024

TM_IN = 256
PIECE = 512
TQ = 512
SQ = 128
ATTN_PAIRS_IN_FLIGHT = 1
TS_HGRN = 512
TM_MIX = 256
MIX_ROW_PARTS = 1
BM = 512
SUBLANES = 8
ROW_SUB = D_MODEL // LANES
assert ROW_SUB == SUBLANES
SEG_CHUNK = 16


def _cparams(sem):
    return pltpu.CompilerParams(dimension_semantics=sem, vmem_limit_bytes=VMEM_LIMIT)


def _lockstep(*gens):
    out = [None] * len(gens)
    live = list(range(len(gens)))
    while live:
        for g in list(live):
            try:
                next(gens[g])
            except StopIteration as done:
                out[g] = done.value
                live.remove(g)
    return out


def _split_bf16(x):
    hi = x.astype(BF16)
    lo = (x - hi.astype(F32)).astype(BF16)
    return hi, lo


def _inproj_kernel(x_ref, nw_ref, w_ref, qg_ref, kg_ref, e_ref, lbf_ref, lbb_ref,
                   q0_ref, q1_ref, q2_ref, k0_ref, k1_ref, k2_ref, v0_ref, v1_ref, v2_ref,
                   qb_ref, lff_ref, lfb_ref, ib_ref, og_ref, gt_ref, scr_ref, *, tm):
    q_refs, k_refs, v_refs = (q0_ref, q1_ref, q2_ref), (k0_ref, k1_ref, k2_ref), (v0_ref, v1_ref, v2_ref)
    x = x_ref[...]
    ms = jnp.mean(x * x, axis=-1, keepdims=True)
    xn = (x * lax.rsqrt(ms + EPS) * nw_ref[...]).astype(BF16)

    def proj(col):
        return jnp.dot(xn, w_ref[:, col:col + PIECE], preferred_element_type=F32)

    def normed(dst_ref, col, gain, dil):
        y = proj(col)
        yield
        ss = jnp.dot((y * y).astype(BF16), e_ref[...], preferred_element_type=F32)
        yield
        store_group(dst_ref, y * lax.rsqrt(ss * (1.0 / HEAD_DIM_A) + EPS) * gain, dil)

    def plain(dst_ref, col, dil):
        y = proj(col)
        yield
        yield
        store_group(dst_ref, y, dil)

    def store_group(dst_ref, y, dil):
        if dil == 1:
            dst_ref[0, 0] = y.astype(BF16)
            return
        for c in range(GROUP_W // LANES):
            scr_ref[c] = y[:, c * LANES:(c + 1) * LANES]
        for r in range(dil):
            rows = pl.ds(r, tm // dil, stride=dil)
            dst_ref[0, r] = jnp.concatenate(
                [scr_ref[c, rows, :] for c in range(GROUP_W // LANES)], axis=-1).astype(BF16)

    for g, (_, dil) in enumerate(ATTN_GROUPS):
        c = g * GROUP_W
        _lockstep(normed(q_refs[g], c, qg_ref[:, c:c + GROUP_W], dil),
                  normed(k_refs[g], W_A + c, kg_ref[:, c:c + GROUP_W], dil),
                  plain(v_refs[g], 2 * W_A + c, dil))
    base = 3 * W_A
    qb = proj(base)
    qb_ref[...] = (qb * jax.nn.sigmoid(qb) * (DK_B ** -0.5)).astype(BF16)
    for dst, lb_ref, off in ((lff_ref, lbf_ref, D_B), (lfb_ref, lbb_ref, 2 * D_B)):
        lb = lb_ref[...]
        f = lb + (1.0 - lb) * jax.nn.sigmoid(proj(base + off))
        dst[...] = jnp.log(f)
    ib_ref[...] = proj(base + 3 * D_B).astype(BF16)
    og = proj(base + 4 * D_B)
    og_ref[...] = (og * jax.nn.sigmoid(og)).astype(BF16)
    for p in range(2 * D_MODEL // PIECE):
        c = p * PIECE
        gt_ref[:, c:c + PIECE] = jax.nn.sigmoid(proj(D_IN + c)).astype(BF16)


def _inproj(x2, b, s, nw, w_all, qg, kg, e_mat, lbf, lbb):
    t = x2.shape[0]
    tm = TM_IN
    tps = s // tm
    n_all = w_all.shape[1]
    row = lambda i: (i, 0)
    const = lambda i: (0, 0)
    seq = lambda i: (i // tps, 0, i % tps, 0)
    widths = (D_B, D_B, D_B, D_B, D_B, 2 * D_MODEL)
    dtypes = (BF16, F32, F32, BF16, BF16, BF16)
    grp_shapes = [jax.ShapeDtypeStruct((b, dil, s // dil, GROUP_W), BF16) for _, dil in ATTN_GROUPS] * 3
    grp_specs = [pl.BlockSpec((1, dil, tm // dil, GROUP_W), seq) for _, dil in ATTN_GROUPS] * 3
    return pl.pallas_call(
        functools.partial(_inproj_kernel, tm=tm),
        out_shape=tuple(grp_shapes) + tuple(jax.ShapeDtypeStruct((t, w), dt) for w, dt in zip(widths, dtypes)),
        grid=(t // tm,),
        in_specs=[
            pl.BlockSpec((tm, D_MODEL), row),
            pl.BlockSpec((1, D_MODEL), const),
            pl.BlockSpec((D_MODEL, n_all), const),
            pl.BlockSpec((1, W_A), const),
            pl.BlockSpec((1, W_A), const),
            pl.BlockSpec((PIECE, PIECE), const),
            pl.BlockSpec((1, D_B), const),
            pl.BlockSpec((1, D_B), const),
        ],
        out_specs=tuple(grp_specs) + tuple(pl.BlockSpec((tm, w), row) for w in widths),
        scratch_shapes=[pltpu.VMEM((GROUP_W // LANES, tm, LANES), F32)],
        compiler_params=_cparams(("parallel",)),
        name="inproj",
    )(x2, nw, w_all, qg, kg, e_mat, lbf, lbb)


def _attn_kernel(q_ref, kp_ref, kc_ref, kn_ref, vp_ref, vc_ref, vn_ref, bias_ref,
                 o_ref, lse_ref, *, tq, sub_len):
    i = pl.program_id(2)
    sq = bias_ref.shape[1] // 2
    nk = sq + 2 * N_SIDE
    kk = jnp.concatenate([kp_ref[...], kc_ref[...], kn_ref[...]], axis=0)
    vv = jnp.concatenate([vp_ref[...], vc_ref[...], vn_ref[...]], axis=0)
    lane = lax.broadcasted_iota(jnp.int32, (sq, LANES), 1)
    low = lane < HEAD_DIM_A
    ones = jnp.ones((nk, LANES), BF16)
    zero = jnp.zeros((sq, LANES), BF16)
    def pair(j, pr, colbias):
        cols = slice(pr * LANES, (pr + 1) * LANES)
        q2 = q_ref[j * sq:(j + 1) * sq, cols]
        k2 = kk[j * sq:j * sq + nk, cols]
        v2 = jnp.concatenate([vv[j * sq:j * sq + nk, cols], ones], axis=1)
        q_st = jnp.concatenate([jnp.where(low, q2, zero), jnp.where(low, zero, q2)], axis=0)
        s = lax.dot_general(q_st, k2, (((1,), (1,)), ((), ())), preferred_element_type=F32)
        yield
        s = s + bias_ref[pr] + colbias
        m = jnp.max(s, axis=-1, keepdims=True)
        p = jnp.exp(s - m).astype(BF16)
        r = jnp.dot(p, v2, preferred_element_type=F32)
        yield
        o2 = jnp.where(low, r[:sq, :LANES] / r[:sq, LANES:], r[sq:, :LANES] / r[sq:, LANES:])
        o_ref[j * sq:(j + 1) * sq, cols] = o2.astype(o_ref.dtype)
        lse = m + jnp.log(r[:, LANES:LANES + 1])
        lse_ref[j * sq:(j + 1) * sq, cols] = jnp.where(low, lse[:sq], lse[sq:])

    n_pairs = HEADS_PER_GROUP // 2
    for j in range(tq // sq):
        kpos = i * tq + j * sq - N_SIDE + lax.broadcasted_iota(jnp.int32, (1, nk), 1)
        colbias = jnp.where((kpos >= 0) & (kpos < sub_len), 0.0, NEG).astype(F32)
        for pr in range(0, n_pairs, ATTN_PAIRS_IN_FLIGHT):
            _lockstep(*[pair(j, pr + d, colbias) for d in range(ATTN_PAIRS_IN_FLIGHT)])


def _attn_bias(sq, dil, slopes):
    nk = sq + 2 * N_SIDE
    rel = np.arange(nk)[None, :] - N_SIDE - np.arange(sq)[:, None]
    band = np.abs(rel) <= N_SIDE
    alibi = -slopes[:, None, None] * (dil * np.abs(rel)).astype(np.float32)[None]
    bias = np.where(band[None], alibi, NEG).astype(np.float32)
    return jnp.asarray(bias.reshape(HEADS_PER_GROUP // 2, 2 * sq, nk))


def _attention_group(q, k, v, g):
    b, dil, sub_len, _ = q.shape
    tq = min(TQ, sub_len)
    sq = min(SQ, sub_len)
    hb = tq // N_SIDE
    n_halo = sub_len // N_SIDE
    slopes = (2.0 ** (-8.0 * (np.arange(N_HEADS_A) + 1) / N_HEADS_A)).astype(np.float32)
    bias = _attn_bias(sq, dil, slopes[g * HEADS_PER_GROUP:(g + 1) * HEADS_PER_GROUP])
    cur = lambda bi, r, i: (bi, r, i, 0)
    prev = lambda bi, r, i: (bi, r, jnp.maximum(i * hb - 1, 0), 0)
    nxt = lambda bi, r, i: (bi, r, jnp.minimum((i + 1) * hb, n_halo - 1), 0)
    blk_q = (None, None, tq, GROUP_W)
    blk_h = (None, None, N_SIDE, GROUP_W)
    return pl.pallas_call(
        functools.partial(_attn_kernel, tq=tq, sub_len=sub_len),
        out_shape=(jax.ShapeDtypeStruct((b, dil, sub_len, GROUP_W), BF16),
                   jax.ShapeDtypeStruct((b, dil, sub_len, GROUP_W), F32)),
        grid=(b, dil, sub_len // tq),
        in_specs=[
            pl.BlockSpec(blk_q, cur),
            pl.BlockSpec(blk_h, prev), pl.BlockSpec(blk_q, cur), pl.BlockSpec(blk_h, nxt),
            pl.BlockSpec(blk_h, prev), pl.BlockSpec(blk_q, cur), pl.BlockSpec(blk_h, nxt),
            pl.BlockSpec(bias.shape, lambda bi, r, i: (0, 0, 0)),
        ],
        out_specs=(pl.BlockSpec((None, None, tq, GROUP_W), cur),
                   pl.BlockSpec((None, None, tq, GROUP_W), cur)),
        compiler_params=_cparams(("parallel", "parallel", "parallel")),
        name=f"attn_d{dil}",
    )(q, k, k, k, v, v, v, bias)


def _hgrn_kernel(qf_ref, lf_ref, vf_ref, qr_ref, lr_ref, vr_ref, of_ref, or_ref,
                 sf_ref, sr_ref, qif_ref, qir_ref, uf_ref, ur_ref, df_ref, dr_ref, *, ts):
    c_len = HGRN_CHUNK
    nc = ts // c_len

    @pl.when(pl.program_id(1) == 0)
    def _():
        sf_ref[...] = jnp.zeros_like(sf_ref)
        sr_ref[...] = jnp.zeros_like(sr_ref)

    r_i = lax.broadcasted_iota(jnp.int32, (c_len, c_len), 0)
    c_i = lax.broadcasted_iota(jnp.int32, (c_len, c_len), 1)
    lower = r_i >= c_i
    upper = r_i <= c_i
    tri_f = jnp.where(lower, 1.0, 0.0).astype(BF16)
    tri_r = jnp.where(upper, 1.0, 0.0).astype(BF16)

    def cumsum(tri, x):
        hi = x.astype(BF16)
        r1 = x - hi.astype(F32)
        mid = r1.astype(BF16)
        lo = (r1 - mid.astype(F32)).astype(BF16)
        d = lambda a: jnp.dot(tri, a, preferred_element_type=F32)
        return d(hi) + d(mid) + d(lo)

    n_sub = c_len // HGRN_SUB
    shift = HGRN_SUB.bit_length() - 1
    sub_f = jnp.right_shift(lax.broadcasted_iota(jnp.int32, (c_len, D_B), 0), shift)
    sub_tf = jnp.right_shift(lax.broadcasted_iota(jnp.int32, (D_B, c_len), 1), shift)
    t_i = lax.broadcasted_iota(jnp.int32, (2 * c_len, c_len), 0) & (c_len - 1)
    s_i = lax.broadcasted_iota(jnp.int32, (2 * c_len, c_len), 1)
    same_sub2 = jnp.right_shift(t_i, shift) == jnp.right_shift(s_i, shift)
    lower2, upper2 = t_i >= s_i, t_i <= s_i
    lower_sub2, upper_sub2 = same_sub2 & lower2, same_sub2 & upper2
    row_head = lax.broadcasted_iota(jnp.int32, (2 * c_len, 2 * DK_B), 0) // c_len
    col_head = lax.broadcasted_iota(jnp.int32, (2 * c_len, 2 * DK_B), 1) // DK_B
    pair_cols = jnp.where(row_head == col_head, 1.0, 0.0).astype(BF16)
    pair_off = jnp.concatenate([pair_cols] * (n_sub - 1), axis=1)

    def intra(q_ref, l_ref, v_ref, o_ref, qi_ref, u_ref, d_ref, c, rev):
        rows = pl.ds(pl.multiple_of(c * c_len, c_len), c_len)
        lf = l_ref[0, rows, :]
        q = q_ref[0, rows, :].astype(F32)
        v = v_ref[0, rows, :]
        k = 1.0 - jnp.exp(lf)
        b = cumsum(tri_r if rev else tri_f, lf)
        yield
        sub = (n_sub - 1 - sub_f) if rev else sub_f

        def b_at(pos):
            r = c_len - 1 - pos if rev else pos
            return b[r:r + 1, :]

        def per_sub(vals):
            vals = vals[::-1] if rev else vals
            return jnp.concatenate([jnp.broadcast_to(x, (HGRN_SUB, D_B)) for x in vals], axis=0)

        a_end = [b_at(HGRN_SUB * j + HGRN_SUB - 1) for j in range(n_sub)]
        a_start = [jnp.zeros((1, D_B), F32)] + a_end[:-1]
        btot = a_end[-1]
        end_full, start_full = per_sub(a_end), per_sub(a_start)
        stores = [(qi_ref, (rows, slice(None)), (q * jnp.exp(b)).astype(BF16))]
        q_off = [jnp.where(sub > j, q * jnp.exp(jnp.minimum(b - a_end[j], 0.0)), 0.0).astype(BF16)
                 for j in range(n_sub - 1)]
        k_end = k * jnp.exp(end_full - b)
        sub_t = (n_sub - 1 - sub_tf) if rev else sub_tf
        q_dia = (q * jnp.exp(b - start_full)).astype(BF16)
        k_upd = (k * jnp.exp(btot - b)).astype(BF16)
        k_end_t = k_end.T
        k_dia_t = (k * jnp.exp(jnp.minimum(start_full - b, HGRN_EXP_CLAMP))).T.astype(BF16)
        k_off_t = [jnp.where(sub_t == j, k_end_t, 0.0).astype(BF16) for j in range(n_sub - 1)]
        dec8_t = jnp.broadcast_to(jnp.exp(btot), (SUBLANES, D_B)).T
        keep, keep_d = (upper2, upper_sub2) if rev else (lower2, lower_sub2)
        two = lambda a: jnp.concatenate([a, a], axis=0)
        for p in range(N_HEADS_B // 2):
            ps = slice(2 * p * DK_B, (2 * p + 2) * DK_B)
            qc = two(jnp.concatenate([q_off[j][:, ps] for j in range(n_sub - 1)], axis=1)) * pair_off
            kc = jnp.concatenate([k_off_t[j][ps, :] for j in range(n_sub - 1)], axis=0)
            yield
            sc = jnp.dot(qc, kc, preferred_element_type=F32)
            sc_d = jnp.dot(two(q_dia[:, ps]) * pair_cols, k_dia_t[ps, :], preferred_element_type=F32)
            u2 = lax.dot_general(k_upd[:, ps], v[:, ps], (((0,), (0,)), ((), ())),
                                 preferred_element_type=F32)
            yield
            sc = (jnp.where(keep, sc, 0.0) + jnp.where(keep_d, sc_d, 0.0)).astype(BF16)
            o2 = jnp.dot(sc, v[:, ps], preferred_element_type=F32)
            for i in range(2):
                h = 2 * p + i
                hs = slice(h * DK_B, (h + 1) * DK_B)
                blk = slice(i * DK_B, (i + 1) * DK_B)
                stores.append((o_ref, (0, rows, hs), o2[i * c_len:(i + 1) * c_len, blk]))
                stores.append((u_ref, (c, h), u2[blk, blk]))
                stores.append((d_ref, (c, h), jnp.broadcast_to(dec8_t[hs, 0:1], (DK_B, DK_B))))
        return stores

    def carry_state(o_ref, qi_ref, u_ref, d_ref, s_ref, c):
        rows = pl.ds(pl.multiple_of(c * c_len, c_len), c_len)
        zero = jnp.zeros((DK_B, DK_B), BF16)
        stores = []
        for p in range(N_HEADS_B // 2):
            ps = slice(2 * p * DK_B, (2 * p + 2) * DK_B)
            sa, sb = s_ref[2 * p], s_ref[2 * p + 1]
            s_bd = jnp.concatenate([jnp.concatenate([sa.astype(BF16), zero], axis=1),
                                    jnp.concatenate([zero, sb.astype(BF16)], axis=1)], axis=0)
            o_new = o_ref[0, rows, ps] + jnp.dot(qi_ref[rows, ps], s_bd, preferred_element_type=F32)
            stores.append((o_ref, (0, rows, ps), o_new))
            stores.append((s_ref, (2 * p,), sa * d_ref[c, 2 * p] + u_ref[c, 2 * p]))
            stores.append((s_ref, (2 * p + 1,), sb * d_ref[c, 2 * p + 1] + u_ref[c, 2 * p + 1]))
        return stores

    def commit(stores):
        for ref, idx, val in stores:
            ref[idx] = val

    def intra_body(trip, carry):
        gens = []
        for i in range(HGRN_CHUNKS_PER_TRIP):
            c = trip * HGRN_CHUNKS_PER_TRIP + i
            gens.append(intra(qf_ref, lf_ref, vf_ref, of_ref, qif_ref, uf_ref, df_ref, c, False))
            gens.append(intra(qr_ref, lr_ref, vr_ref, or_ref, qir_ref, ur_ref, dr_ref, c, True))
        commit(sum(_lockstep(*gens), []))
        return carry

    def state_body(c, carry):
        commit(carry_state(of_ref, qif_ref, uf_ref, df_ref, sf_ref, c)
               + carry_state(or_ref, qir_ref, ur_ref, dr_ref, sr_ref, nc - 1 - c))
        return carry

    lax.fori_loop(0, nc // HGRN_CHUNKS_PER_TRIP, intra_body, 0)
    lax.fori_loop(0, nc, state_body, 0)


def _hgrn(qb, lff, lfb, ib):
    b, s, _ = qb.shape
    ts = min(TS_HGRN, s)
    nt = s // ts
    nc = ts // HGRN_CHUNK
    fwd = lambda bi, j: (bi, j, 0)
    rev = lambda bi, j: (bi, nt - 1 - j, 0)
    blk = (1, ts, D_B)
    return pl.pallas_call(
        functools.partial(_hgrn_kernel, ts=ts),
        out_shape=(jax.ShapeDtypeStruct((b, s, D_B), F32), jax.ShapeDtypeStruct((b, s, D_B), F32)),
        grid=(b, nt),
        in_specs=[pl.BlockSpec(blk, fwd), pl.BlockSpec(blk, fwd), pl.BlockSpec(blk, fwd),
                  pl.BlockSpec(blk, rev), pl.BlockSpec(blk, rev), pl.BlockSpec(blk, rev)],
        out_specs=(pl.BlockSpec(blk, fwd), pl.BlockSpec(blk, rev)),
        scratch_shapes=[pltpu.VMEM((N_HEADS_B, DK_B, DK_B), F32),
                        pltpu.VMEM((N_HEADS_B, DK_B, DK_B), F32),
                        pltpu.VMEM((ts, D_B), BF16),
                        pltpu.VMEM((ts, D_B), BF16),
                        pltpu.VMEM((nc, N_HEADS_B, DK_B, DK_B), F32),
                        pltpu.VMEM((nc, N_HEADS_B, DK_B, DK_B), F32),
                        pltpu.VMEM((nc, N_HEADS_B, DK_B, DK_B), F32),
                        pltpu.VMEM((nc, N_HEADS_B, DK_B, DK_B), F32)],
        compiler_params=_cparams(("parallel", "arbitrary")),
        name="hgrn",
    )(qb, lff, ib, qb, lfb, ib)


def _mix_kernel(x_ref, o1_ref, o2_ref, o3_ref, l1_ref, l2_ref, l3_ref, of_ref, ob_ref, og_ref,
                gt_ref, wa_ref, wb_ref, wo_ref, ogain_ref, nmoe_ref, wrh_ref, wrl_ref,
                br_ref, cnt_ref,
                h_ref, xt_ref, tw_ref, ps_ref, ct_ref, rt_ref, cnt_out_ref, run_ref, so_ref, *, tm):
    i = pl.program_id(0)

    @pl.when(i == 0)
    def _():
        run_ref[...] = cnt_ref[...]

    def token_major(src_ref, scr_ref, dil):
        if dil == 1:
            return src_ref[0].astype(F32)
        n_chunk = scr_ref.shape[0]
        for r in range(dil):
            blk = src_ref[r].astype(F32)
            for c in range(n_chunk):
                scr_ref[c, pl.ds(r, tm // dil, stride=dil), :] = blk[:, c * LANES:(c + 1) * LANES]
        return jnp.concatenate([scr_ref[c] for c in range(n_chunk)], axis=-1)

    dils = [dil for _, dil in ATTN_GROUPS]
    l1, l2, l3 = [token_major(r, so_ref, d) for r, d in zip((l1_ref, l2_ref, l3_ref), dils)]
    mx = jnp.maximum(jnp.maximum(l1, l2), l3)
    e1, e2, e3 = jnp.exp(l1 - mx), jnp.exp(l2 - mx), jnp.exp(l3 - mx)
    attn = (e1 * token_major(o1_ref, so_ref, dils[0]) + e2 * token_major(o2_ref, so_ref, dils[1])
            + e3 * token_major(o3_ref, so_ref, dils[2])) / (e1 + e2 + e3)

    o = of_ref[...] + ob_ref[...]
    parts = []
    for h in range(N_HEADS_B):
        oh = o[:, h * DK_B:(h + 1) * DK_B]
        ms = jnp.mean(oh * oh, axis=-1, keepdims=True)
        parts.append(oh * lax.rsqrt(ms + EPS))
    hg = jnp.concatenate(parts, axis=-1) * ogain_ref[...] * og_ref[...].astype(F32)

    attn_bf, hg_bf = attn.astype(BF16), hg.astype(BF16)
    n_part = MIX_ROW_PARTS
    tp = tm // n_part

    def token_chain(part):
        rs = slice(part * tp, (part + 1) * tp)
        pa = jnp.dot(attn_bf[rs], wa_ref[...], preferred_element_type=F32)
        pb = jnp.dot(hg_bf[rs], wb_ref[...], preferred_element_type=F32)
        yield
        mixed = (gt_ref[rs, :D_MODEL].astype(F32) * pa + gt_ref[rs, D_MODEL:].astype(F32) * pb)
        h = x_ref[rs, :] + jnp.dot(mixed.astype(BF16), wo_ref[...], preferred_element_type=F32)
        yield
        h_ref[rs, :] = h
        ms = jnp.mean(h * h, axis=-1, keepdims=True)
        hn = h * lax.rsqrt(ms + EPS) * nmoe_ref[...]
        hi, lo = _split_bf16(hn)
        lg = (jnp.dot(hi, wrh_ref[...], preferred_element_type=F32)
              + jnp.dot(lo, wrh_ref[...], preferred_element_type=F32)
              + jnp.dot(hi, wrl_ref[...], preferred_element_type=F32)) + br_ref[...]
        yield
        lane_p = lax.broadcasted_iota(jnp.int32, (tp, LANES), 1)
        vals, idxs = [], []
        onehot = jnp.zeros((tp, LANES), F32)
        for _ in range(TOP_K):
            m = jnp.max(lg, axis=-1, keepdims=True)
            idx = jnp.min(jnp.where(lg == m, lane_p, LANES), axis=-1, keepdims=True)
            sel = lane_p == idx
            onehot = jnp.where(sel, 1.0, onehot)
            lg = jnp.where(sel, NEG * 2, lg)
            vals.append(m)
            idxs.append(idx)
        exps = [jnp.exp(v - vals[0]) for v in vals]
        inv = 1.0 / (exps[0] + exps[1] + exps[2] + exps[3])
        return hi, onehot, idxs, [e * inv for e in exps]

    parts = _lockstep(*[token_chain(p) for p in range(n_part)])
    rows_of = lambda pick: jnp.concatenate([pick(p) for p in parts], axis=0)
    hi = rows_of(lambda p: p[0])
    onehot = rows_of(lambda p: p[1])
    idxs = [rows_of(lambda p, k=k: p[2][k]) for k in range(TOP_K)]
    gates = [rows_of(lambda p, k=k: p[3][k]) for k in range(TOP_K)]
    lane = lax.broadcasted_iota(jnp.int32, (tm, LANES), 1)

    r_i = lax.broadcasted_iota(jnp.int32, (tm, tm), 0)
    c_i = lax.broadcasted_iota(jnp.int32, (tm, tm), 1)
    tri = jnp.where(r_i > c_i, 1.0, 0.0).astype(BF16)
    local = jnp.dot(tri, onehot.astype(BF16), preferred_element_type=F32)
    cnt_tile = jnp.sum(onehot, axis=0, keepdims=True)
    e_r = lax.broadcasted_iota(jnp.int32, (LANES, LANES), 0)
    e_c = lax.broadcasted_iota(jnp.int32, (LANES, LANES), 1)
    before_e = jnp.where(e_r < e_c, 1.0, 0.0).astype(BF16)
    off = jnp.dot(jnp.broadcast_to(cnt_tile, (8, LANES)).astype(BF16), before_e,
                  preferred_element_type=F32)[0:1]
    slot = off + local
    tw = jnp.zeros((tm, LANES), F32)
    ps = jnp.full((tm, LANES), -1.0, F32)
    for k in range(TOP_K):
        slot_k = jnp.sum(jnp.where(lane == idxs[k], slot, 0.0), axis=-1, keepdims=True)
        tw = jnp.where(lane == k, gates[k], tw)
        ps = jnp.where(lane == k, slot_k, ps)
    tw_ref[...] = tw
    ps_ref[...] = ps.astype(jnp.int32)

    ps_t = ps.T
    n_out = tm * TOP_K // n_part
    for part in range(n_part):
        row_id = (lax.broadcasted_iota(jnp.int32, (n_out, tm), 0) + part * n_out).astype(F32)
        perm = jnp.zeros((n_out, tm), F32)
        for k in range(TOP_K):
            perm = jnp.where(row_id == ps_t[k:k + 1, :], 1.0, perm)
        rows = jnp.dot(perm.astype(BF16), hi, preferred_element_type=F32)
        for c in range(D_MODEL // LANES):
            xt_ref[pl.ds(part * n_out * ROW_SUB + c, n_out, stride=ROW_SUB), :] = (
                rows[:, c * LANES:(c + 1) * LANES])

    ct_ref[0] = cnt_tile
    rt_ref[0] = run_ref[...]
    run_new = run_ref[...] + cnt_tile
    run_ref[...] = run_new
    cnt_out_ref[...] = run_new


def _mix(x2, s, o1, o2, o3, l1, l2, l3, of, ob, og, gt, wa, wb, wo, ogain, nmoe, wrh, wrl, br, cnt):
    t = x2.shape[0]
    tm = TM_MIX
    tps = s // tm
    row = lambda i: (i, 0)
    const = lambda i: (0, 0)
    seq = lambda i: (i // tps, 0, i % tps, 0)
    rb = lambda w: pl.BlockSpec((tm, w), row)
    cb = lambda a: pl.BlockSpec(a.shape, const)
    gb = lambda a: pl.BlockSpec((None, a.shape[1], tm // a.shape[1], a.shape[3]), seq)
    return pl.pallas_call(
        functools.partial(_mix_kernel, tm=tm),
        out_shape=(jax.ShapeDtypeStruct((t, D_MODEL), F32),
                   jax.ShapeDtypeStruct((t * TOP_K * ROW_SUB, LANES), F32),
                   jax.ShapeDtypeStruct((t, LANES), F32),
                   jax.ShapeDtypeStruct((t, LANES), jnp.int32),
                   jax.ShapeDtypeStruct((t // tm, 1, LANES), F32),
                   jax.ShapeDtypeStruct((t // tm, 1, LANES), F32),
                   jax.ShapeDtypeStruct((1, LANES), F32)),
        grid=(t // tm,),
        in_specs=[rb(D_MODEL), gb(o1), gb(o2), gb(o3), gb(l1), gb(l2), gb(l3),
                  rb(D_B), rb(D_B), rb(D_B), rb(2 * D_MODEL),
                  cb(wa), cb(wb), cb(wo), cb(ogain), cb(nmoe), cb(wrh), cb(wrl), cb(br), cb(cnt)],
        out_specs=(rb(D_MODEL), pl.BlockSpec((tm * TOP_K * ROW_SUB, LANES), row), rb(LANES), rb(LANES),
                   pl.BlockSpec((1, 1, LANES), lambda i: (i, 0, 0)),
                   pl.BlockSpec((1, 1, LANES), lambda i: (i, 0, 0)),
                   pl.BlockSpec((1, LANES), const)),
        scratch_shapes=[pltpu.VMEM((1, LANES), F32), pltpu.VMEM((GROUP_W // LANES, tm, LANES), F32)],
        compiler_params=_cparams(("arbitrary",)),
        name="mix",
    )(x2, o1, o2, o3, l1, l2, l3, of, ob, og, gt, wa, wb, wo, ogain, nmoe, wrh, wrl, br, cnt)


def _start_tile_segments(tile, off_tbl, row_tbl, len_tbl, make_piece):
    def segment(e, carry):
        sidx = tile * N_EXPERTS + e
        t0, r0, n = off_tbl[sidx], row_tbl[sidx], len_tbl[sidx]
        n_bulk = jnp.right_shift(n, SEG_CHUNK.bit_length() - 1)

        def bulk(j, c):
            make_piece(t0 + j * SEG_CHUNK, r0 + j * SEG_CHUNK, SEG_CHUNK).start()
            return c

        lax.fori_loop(0, n_bulk, bulk, 0)
        done = n_bulk * SEG_CHUNK
        bit = SEG_CHUNK // 2
        while bit >= 1:
            has = (n & bit) != 0

            @pl.when(has)
            def _(done=done, bit=bit):
                make_piece(t0 + done, r0 + done, bit).start()

            done = done + jnp.where(has, bit, 0)
            bit //= 2
        return carry

    lax.fori_loop(0, N_EXPERTS, segment, 0)


def _row_slice(row, n_rows):
    return pl.ds(pl.multiple_of(row * ROW_SUB, ROW_SUB), n_rows * ROW_SUB)


def _dispatch_kernel(off_tbl, row_tbl, len_tbl, xt_ref, xs_in_ref, xs_ref, sem):
    del xs_in_ref

    def piece(tile_row, expert_row, n_rows):
        return pltpu.make_async_copy(xt_ref.at[_row_slice(tile_row, n_rows)],
                                     xs_ref.at[_row_slice(expert_row, n_rows)], sem)

    _start_tile_segments(pl.program_id(0), off_tbl, row_tbl, len_tbl, piece)
    pltpu.make_async_copy(xt_ref, xs_ref.at[pl.ds(0, xt_ref.shape[0])], sem).wait()


def _dispatch(off_tbl, row_tbl, len_tbl, xt, xs):
    blk = TM_MIX * TOP_K * ROW_SUB
    return pl.pallas_call(
        _dispatch_kernel,
        out_shape=jax.ShapeDtypeStruct(xs.shape, xs.dtype),
        grid_spec=pltpu.PrefetchScalarGridSpec(
            num_scalar_prefetch=3,
            grid=(xt.shape[0] // blk,),
            in_specs=[pl.BlockSpec((blk, LANES), lambda i, *_: (i, 0)),
                      pl.BlockSpec(memory_space=pl.ANY)],
            out_specs=pl.BlockSpec(memory_space=pl.ANY),
            scratch_shapes=[pltpu.SemaphoreType.DMA],
        ),
        input_output_aliases={4: 0},
        compiler_params=_cparams(("arbitrary",)),
        name="dispatch",
    )(off_tbl, row_tbl, len_tbl, xt, xs)


def _combine_kernel(off_tbl, row_tbl, len_tbl, ps_ref, tw_ref, h_ref, ys_ref, y_ref, buf_ref, sems, *, tm):
    i = pl.program_id(0)
    n_rows = tm * TOP_K

    def start(tile):
        slot = tile % 2

        def piece(tile_row, expert_row, n):
            return pltpu.make_async_copy(ys_ref.at[_row_slice(expert_row, n)],
                                         buf_ref.at[slot, _row_slice(tile_row, n)], sems.at[slot])

        _start_tile_segments(tile, off_tbl, row_tbl, len_tbl, piece)

    @pl.when(i == 0)
    def _():
        start(i)

    @pl.when(i + 1 < pl.num_programs(0))
    def _():
        start(i + 1)

    slot = i % 2
    yt_ref = buf_ref.at[slot]
    pltpu.make_async_copy(ys_ref.at[pl.ds(0, n_rows * ROW_SUB)], yt_ref, sems.at[slot]).wait()
    ysorted = jnp.concatenate(
        [yt_ref[pl.ds(c, n_rows, stride=ROW_SUB), :] for c in range(ROW_SUB)], axis=-1).astype(BF16)
    col = lax.broadcasted_iota(jnp.int32, (tm, n_rows), 1)
    ps = ps_ref[...]
    tw = tw_ref[...]
    pw = jnp.zeros((tm, n_rows), F32)
    for k in range(TOP_K):
        pw = jnp.where(col == ps[:, k:k + 1], tw[:, k:k + 1], pw)
    hi, lo = _split_bf16(pw)
    y_ref[...] = (h_ref[...] + jnp.dot(hi, ysorted, preferred_element_type=F32)
                  + jnp.dot(lo, ysorted, preferred_element_type=F32))


def _combine(off_tbl, row_tbl, len_tbl, ps, tw, h, ys):
    t = h.shape[0]
    tm = TM_MIX
    row = lambda i, *_: (i, 0)
    return pl.pallas_call(
        functools.partial(_combine_kernel, tm=tm),
        out_shape=jax.ShapeDtypeStruct((t, D_MODEL), F32),
        grid_spec=pltpu.PrefetchScalarGridSpec(
            num_scalar_prefetch=3,
            grid=(t // tm,),
            in_specs=[pl.BlockSpec((tm, LANES), row),
                      pl.BlockSpec((tm, LANES), row),
                      pl.BlockSpec((tm, D_MODEL), row),
                      pl.BlockSpec(memory_space=pl.ANY)],
            out_specs=pl.BlockSpec((tm, D_MODEL), row),
            scratch_shapes=[pltpu.VMEM((2, tm * TOP_K * ROW_SUB, LANES), F32),
                            pltpu.SemaphoreType.DMA((2,))],
        ),
        compiler_params=_cparams(("arbitrary",)),
        name="combine",
    )(off_tbl, row_tbl, len_tbl, ps, tw, h, ys)


def _experts_kernel(be_ref, nused_ref, xs_ref, wgu_ref, bgu_ref, wd_ref, bd_ref, ys_ref, wgu_bf, wd_bf):
    i = pl.program_id(0)
    used = i < nused_ref[0]
    new_expert = (i == 0) | (be_ref[i] != be_ref[jnp.maximum(i - 1, 0)])

    @pl.when(used & new_expert)
    def _():
        wgu_bf[...] = wgu_ref[0].astype(BF16)
        wd_bf[...] = wd_ref[0].astype(BF16)

    @pl.when(used)
    def _():
        x = jnp.concatenate(
            [xs_ref[pl.ds(c, BM, stride=ROW_SUB), :] for c in range(ROW_SUB)], axis=-1).astype(BF16)
        hh = jnp.dot(x, wgu_bf[...], preferred_element_type=F32) + bgu_ref[0]
        gate = jnp.minimum(hh[:, :D_FF], SWIGLU_LIMIT)
        up = jnp.clip(hh[:, D_FF:], -SWIGLU_LIMIT, SWIGLU_LIMIT)
        glu = gate * jax.nn.sigmoid(SWIGLU_ALPHA * gate)
        act = ((up + 1.0) * glu).astype(BF16)
        y = jnp.dot(act, wd_bf[...], preferred_element_type=F32) + bd_ref[0]
        for c in range(ROW_SUB):
            ys_ref[pl.ds(c, BM, stride=ROW_SUB), :] = y[:, c * LANES:(c + 1) * LANES]

    @pl.when(i >= nused_ref[0])
    def _():
        ys_ref[...] = jnp.zeros_like(ys_ref)


def _experts(block_e, nused, xs, wgu, bgu, wd, bd):
    nb = xs.shape[0] // (BM * ROW_SUB)
    emap3 = lambda i, be, nu: (be[i], 0, 0)
    rows = pl.BlockSpec((BM * ROW_SUB, LANES), lambda i, be, nu: (i, 0))
    return pl.pallas_call(
        _experts_kernel,
        out_shape=jax.ShapeDtypeStruct(xs.shape, F32),
        grid_spec=pltpu.PrefetchScalarGridSpec(
            num_scalar_prefetch=2,
            grid=(nb,),
            in_specs=[rows,
                      pl.BlockSpec((1, D_MODEL, 2 * D_FF), emap3),
                      pl.BlockSpec((1, 1, 2 * D_FF), emap3),
                      pl.BlockSpec((1, D_FF, D_MODEL), emap3),
                      pl.BlockSpec((1, 1, D_MODEL), emap3)],
            out_specs=rows,
            scratch_shapes=[pltpu.VMEM((D_MODEL, 2 * D_FF), BF16), pltpu.VMEM((D_FF, D_MODEL), BF16)],
        ),
        compiler_params=_cparams(("arbitrary",)),
        name="experts",
    )(block_e, nused, xs, wgu, bgu, wd, bd)


def _head_indicator(n_cols, head_dim):
    e = np.zeros((n_cols, LANES), np.float32)
    e[np.arange(n_cols), np.arange(n_cols) // head_dim] = 1.0
    return e


def _mixer(x, prm):
    b, s, d = x.shape
    t = b * s
    x2 = x.reshape(t, d)
    res = _inproj(x2, b, s, prm["norm_mix"], prm["w_all"], prm["q_gain"], prm["k_gain"], prm["e_in"],
                  prm["lb_f"], prm["lb_b"])
    n_grp = len(ATTN_GROUPS)
    qs, ks, vs = res[:n_grp], res[n_grp:2 * n_grp], res[2 * n_grp:3 * n_grp]
    qb, lff, lfb, ib, og, gt = res[3 * n_grp:]
    r3 = lambda a: a.reshape(b, s, a.shape[-1])
    outs, lses = [], []
    for g in range(n_grp):
        o, lse = _attention_group(qs[g], ks[g], vs[g], g)
        outs.append(o)
        lses.append(lse)
    of, ob = _hgrn(r3(qb), r3(lff), r3(lfb), r3(ib))
    return x2, outs, lses, of.reshape(t, D_B), ob.reshape(t, D_B), og, gt


def kernel(x_prompt, x_sample, norm_mix, w_in, q_gain, k_gain, hgrn_lb, hgrn_o_gain, w_gate, w_proj_a,
           w_proj_b, w_out, norm_moe, w_router, b_router, w_gu, b_gu, w_down, b_down):
    l = 0
    lb = jnp.cumsum(jax.nn.softmax(hgrn_lb.astype(F32), axis=1), axis=1)
    wr = jnp.zeros((D_MODEL, LANES), F32).at[:, :N_EXPERTS].set(w_router[l])
    wr_hi = wr.astype(BF16)
    prm = {
        "norm_mix": norm_mix[l].reshape(1, D_MODEL),
        "w_all": jnp.concatenate([w_in[l], w_gate[l]], axis=1).astype(BF16),
        "q_gain": q_gain[l].reshape(1, W_A) * (HEAD_DIM_A ** -0.5),
        "k_gain": k_gain[l].reshape(1, W_A),
        "e_in": jnp.asarray(_head_indicator(PIECE, HEAD_DIM_A) @ _head_indicator(PIECE, HEAD_DIM_A).T, BF16),
        "lb_f": lb[0, l].reshape(1, D_B),
        "lb_b": lb[1, l].reshape(1, D_B),
    }
    wa, wb, wo = w_proj_a[l].astype(BF16), w_proj_b[l].astype(BF16), w_out[l].astype(BF16)
    ogain = hgrn_o_gain[l].reshape(1, D_B)
    nmoe = norm_moe[l].reshape(1, D_MODEL)
    wr_lo = (wr - wr_hi.astype(F32)).astype(BF16)
    br = jnp.full((1, LANES), NEG, F32).at[0, :N_EXPERTS].set(b_router[l])

    cnt = jnp.zeros((1, LANES), F32)
    per_batch = []
    for x in (x_prompt, x_sample):
        x2, outs, lses, of, ob, og, gt = _mixer(x, prm)
        h, xt, tw, ps, ct, rt, cnt = _mix(x2, x.shape[1], outs[0], outs[1], outs[2], lses[0], lses[1], lses[2],
                                          of, ob, og, gt, wa, wb, wo, ogain, nmoe, wr_hi, wr_lo, br, cnt)
        per_batch.append((x.shape, h, xt, tw, ps, ct, rt))

    n_tok = sum(pb[1].shape[0] for pb in per_batch)
    sizes = cnt[0, :N_EXPERTS].astype(jnp.int32)
    pad_sizes = (sizes + BM - 1) // BM * BM
    pad_ends = jnp.cumsum(pad_sizes)
    pad_starts = pad_ends - pad_sizes
    nb = (n_tok * TOP_K) // BM + N_EXPERTS
    block_start = jnp.arange(nb, dtype=jnp.int32) * BM
    block_e = jnp.minimum(jnp.sum(pad_ends[None, :] <= block_start[:, None], axis=1),
                          N_EXPERTS - 1).astype(jnp.int32)
    nused = (pad_ends[-1:] // BM).astype(jnp.int32)

    tables = []
    for _, _, _, _, _, ct, rt in per_batch:
        cnt_te = ct[:, 0, :N_EXPERTS].astype(jnp.int32)
        tile_off = jnp.cumsum(cnt_te, axis=1) - cnt_te
        expert_row = pad_starts[None, :] + rt[:, 0, :N_EXPERTS].astype(jnp.int32)
        tables.append((tile_off.reshape(-1), expert_row.reshape(-1), cnt_te.reshape(-1)))

    xs = jnp.zeros((nb * BM * ROW_SUB, LANES), F32)
    for (_, _, xt, _, _, _, _), tbl in zip(per_batch, tables):
        xs = _dispatch(*tbl, xt, xs)
    ys = _experts(block_e, nused, xs, w_gu[l], b_gu[l].reshape(N_EXPERTS, 1, 2 * D_FF),
                  w_down[l], b_down[l].reshape(N_EXPERTS, 1, D_MODEL))
    results = []
    for (shape, h, _, tw, ps, _, _), tbl in zip(per_batch, tables):
        results.append(_combine(*tbl, ps, tw, h, ys).reshape(shape))
    return tuple(results)
```

```python
import functools
import math

import jax
import jax.numpy as jnp
import numpy as np
from jax import lax
from jax.experimental import pallas as pl
from jax.experimental.pallas import tpu as pltpu

F32 = jnp.float32
BF16 = jnp.bfloat16

D_MODEL = 1024
ATTN_GROUPS = ((128, 1), (512, 4), (2048, 16))
HEADS_PER_GROUP = 8
N_HEADS_A = 24
HEAD_DIM_A = 64
W_A = N_HEADS_A * HEAD_DIM_A
GROUP_W = HEADS_PER_GROUP * HEAD_DIM_A
N_SIDE = 64
N_HEADS_B = 4
DK_B = 128
HGRN_CHUNK = 64
HGRN_SUB = 16
HGRN_EXP_CLAMP = 80.0
HGRN_CHUNKS_PER_TRIP = 2
D_B = N_HEADS_B * DK_B
D_IN = 3 * W_A + 5 * D_B
N_EXPERTS = 32
TOP_K = 4
D_FF = 1024
SWIGLU_LIMIT = 7.0
SWIGLU_ALPHA = 1.702
EPS = 1e-6
NEG = -1e30

LANES = 128
VMEM_LIMIT = 56 * 1024 * 1024

TM_IN = 256
PIECE = 512
TQ = 2048
SQ = 128
ATTN_PAIRS_IN_FLIGHT = 1
TS_HGRN = 512
TM_MIX = 256
MIX_ROW_PARTS = 1
BM = 512
SUBLANES = 8
ROW_SUB = D_MODEL // LANES
assert ROW_SUB == SUBLANES
SEG_CHUNK = 16


def _cparams(sem):
    return pltpu.CompilerParams(dimension_semantics=sem, vmem_limit_bytes=VMEM_LIMIT)


def _lockstep(*gens):
    out = [None] * len(gens)
    live = list(range(len(gens)))
    while live:
        for g in list(live):
            try:
                next(gens[g])
            except StopIteration as done:
                out[g] = done.value
                live.remove(g)
    return out


def _split_bf16(x):
    hi = x.astype(BF16)
    lo = (x - hi.astype(F32)).astype(BF16)
    return hi, lo


def _inproj_kernel(x_ref, nw_ref, w_ref, qg_ref, kg_ref, e_ref, lbf_ref, lbb_ref,
                   q0_ref, q1_ref, q2_ref, k0_ref, k1_ref, k2_ref, v0_ref, v1_ref, v2_ref,
                   qb_ref, lff_ref, lfb_ref, ib_ref, og_ref, gt_ref, scr_ref, *, tm):
    q_refs, k_refs, v_refs = (q0_ref, q1_ref, q2_ref), (k0_ref, k1_ref, k2_ref), (v0_ref, v1_ref, v2_ref)
    x = x_ref[...]
    ms = jnp.mean(x * x, axis=-1, keepdims=True)
    xn = (x * lax.rsqrt(ms + EPS) * nw_ref[...]).astype(BF16)

    def proj(col):
        return jnp.dot(xn, w_ref[:, col:col + PIECE], preferred_element_type=F32)

    def normed(dst_ref, col, gain, dil):
        y = proj(col)
        yield
        ss = jnp.dot((y * y).astype(BF16), e_ref[...], preferred_element_type=F32)
        yield
        store_group(dst_ref, y * lax.rsqrt(ss * (1.0 / HEAD_DIM_A) + EPS) * gain, dil)

    def plain(dst_ref, col, dil):
        y = proj(col)
        yield
        yield
        store_group(dst_ref, y, dil)

    def store_group(dst_ref, y, dil):
        if dil == 1:
            dst_ref[0, 0] = y.astype(BF16)
            return
        for c in range(GROUP_W // LANES):
            scr_ref[c] = y[:, c * LANES:(c + 1) * LANES]
        for r in range(dil):
            rows = pl.ds(r, tm // dil, stride=dil)
            dst_ref[0, r] = jnp.concatenate(
                [scr_ref[c, rows, :] for c in range(GROUP_W // LANES)], axis=-1).astype(BF16)

    for g, (_, dil) in enumerate(ATTN_GROUPS):
        c = g * GROUP_W
        _lockstep(normed(q_refs[g], c, qg_ref[:, c:c + GROUP_W], dil),
                  normed(k_refs[g], W_A + c, kg_ref[:, c:c + GROUP_W], dil),
                  plain(v_refs[g], 2 * W_A + c, dil))
    base = 3 * W_A
    qb = proj(base)
    qb_ref[...] = (qb * jax.nn.sigmoid(qb) * (DK_B ** -0.5)).astype(BF16)
    for dst, lb_ref, off in ((lff_ref, lbf_ref, D_B), (lfb_ref, lbb_ref, 2 * D_B)):
        lb = lb_ref[...]
        f = lb + (1.0 - lb) * jax.nn.sigmoid(proj(base + off))
        dst[...] = jnp.log(f)
    ib_ref[...] = proj(base + 3 * D_B).astype(BF16)
    og = proj(base + 4 * D_B)
    og_ref[...] = (og * jax.nn.sigmoid(og)).astype(BF16)
    for p in range(2 * D_MODEL // PIECE):
        c = p * PIECE
        gt_ref[:, c:c + PIECE] = jax.nn.sigmoid(proj(D_IN + c)).astype(BF16)


def _inproj(x2, b, s, nw, w_all, qg, kg, e_mat, lbf, lbb):
    t = x2.shape[0]
    tm = TM_IN
    tps = s // tm
    n_all = w_all.shape[1]
    row = lambda i: (i, 0)
    const = lambda i: (0, 0)
    seq = lambda i: (i // tps, 0, i % tps, 0)
    widths = (D_B, D_B, D_B, D_B, D_B, 2 * D_MODEL)
    dtypes = (BF16, F32, F32, BF16, BF16, BF16)
    grp_shapes = [jax.ShapeDtypeStruct((b, dil, s // dil, GROUP_W), BF16) for _, dil in ATTN_GROUPS] * 3
    grp_specs = [pl.BlockSpec((1, dil, tm // dil, GROUP_W), seq) for _, dil in ATTN_GROUPS] * 3
    return pl.pallas_call(
        functools.partial(_inproj_kernel, tm=tm),
        out_shape=tuple(grp_shapes) + tuple(jax.ShapeDtypeStruct((t, w), dt) for w, dt in zip(widths, dtypes)),
        grid=(t // tm,),
        in_specs=[
            pl.BlockSpec((tm, D_MODEL), row),
            pl.BlockSpec((1, D_MODEL), const),
            pl.BlockSpec((D_MODEL, n_all), const),
            pl.BlockSpec((1, W_A), const),
            pl.BlockSpec((1, W_A), const),
            pl.BlockSpec((PIECE, PIECE), const),
            pl.BlockSpec((1, D_B), const),
            pl.BlockSpec((1, D_B), const),
        ],
        out_specs=tuple(grp_specs) + tuple(pl.BlockSpec((tm, w), row) for w in widths),
        scratch_shapes=[pltpu.VMEM((GROUP_W // LANES, tm, LANES), F32)],
        compiler_params=_cparams(("parallel",)),
        name="inproj",
    )(x2, nw, w_all, qg, kg, e_mat, lbf, lbb)


def _attn_kernel(q_ref, kp_ref, kc_ref, kn_ref, vp_ref, vc_ref, vn_ref, bias_ref,
                 o_ref, lse_ref, *, tq, sub_len):
    i = pl.program_id(2)
    sq = bias_ref.shape[1] // 2
    nk = sq + 2 * N_SIDE
    kk = jnp.concatenate([kp_ref[...], kc_ref[...], kn_ref[...]], axis=0)
    vv = jnp.concatenate([vp_ref[...], vc_ref[...], vn_ref[...]], axis=0)
    lane = lax.broadcasted_iota(jnp.int32, (sq, LANES), 1)
    low = lane < HEAD_DIM_A
    ones = jnp.ones((nk, LANES), BF16)
    zero = jnp.zeros((sq, LANES), BF16)
    def pair(j, pr, colbias):
        cols = slice(pr * LANES, (pr + 1) * LANES)
        q2 = q_ref[j * sq:(j + 1) * sq, cols]
        k2 = kk[j * sq:j * sq + nk, cols]
        v2 = jnp.concatenate([vv[j * sq:j * sq + nk, cols], ones], axis=1)
        q_st = jnp.concatenate([jnp.where(low, q2, zero), jnp.where(low, zero, q2)], axis=0)
        s = lax.dot_general(q_st, k2, (((1,), (1,)), ((), ())), preferred_element_type=F32)
        yield
        s = s + bias_ref[pr] + colbias
        m = jnp.max(s, axis=-1, keepdims=True)
        p = jnp.exp(s - m).astype(BF16)
        r = jnp.dot(p, v2, preferred_element_type=F32)
        yield
        o2 = jnp.where(low, r[:sq, :LANES] / r[:sq, LANES:], r[sq:, :LANES] / r[sq:, LANES:])
        o_ref[j * sq:(j + 1) * sq, cols] = o2.astype(o_ref.dtype)
        lse = m + jnp.log(r[:, LANES:LANES + 1])
        lse_ref[j * sq:(j + 1) * sq, cols] = jnp.where(low, lse[:sq], lse[sq:])

    n_pairs = HEADS_PER_GROUP // 2
    for j in range(tq // sq):
        kpos = i * tq + j * sq - N_SIDE + lax.broadcasted_iota(jnp.int32, (1, nk), 1)
        colbias = jnp.where((kpos >= 0) & (kpos < sub_len), 0.0, NEG).astype(F32)
        for pr in range(0, n_pairs, ATTN_PAIRS_IN_FLIGHT):
            _lockstep(*[pair(j, pr + d, colbias) for d in range(ATTN_PAIRS_IN_FLIGHT)])


def _attn_bias(sq, dil, slopes):
    nk = sq + 2 * N_SIDE
    rel = np.arange(nk)[None, :] - N_SIDE - np.arange(sq)[:, None]
    band = np.abs(rel) <= N_SIDE
    alibi = -slopes[:, None, None] * (dil * np.abs(rel)).astype(np.float32)[None]
    bias = np.where(band[None], alibi, NEG).astype(np.float32)
    return jnp.asarray(bias.reshape(HEADS_PER_GROUP // 2, 2 * sq, nk))


def _attention_group(q, k, v, g):
    b, dil, sub_len, _ = q.shape
    tq = min(TQ, sub_len)
    sq = min(SQ, sub_len)
    hb = tq // N_SIDE
    n_halo = sub_len // N_SIDE
    slopes = (2.0 ** (-8.0 * (np.arange(N_HEADS_A) + 1) / N_HEADS_A)).astype(np.float32)
    bias = _attn_bias(sq, dil, slopes[g * HEADS_PER_GROUP:(g + 1) * HEADS_PER_GROUP])
    cur = lambda bi, r, i: (bi, r, i, 0)
    prev = lambda bi, r, i: (bi, r, jnp.maximum(i * hb - 1, 0), 0)
    nxt = lambda bi, r, i: (bi, r, jnp.minimum((i + 1) * hb, n_halo - 1), 0)
    blk_q = (None, None, tq, GROUP_W)
    blk_h = (None, None, N_SIDE, GROUP_W)
    return pl.pallas_call(
        functools.partial(_attn_kernel, tq=tq, sub_len=sub_len),
        out_shape=(jax.ShapeDtypeStruct((b, dil, sub_len, GROUP_W), BF16),
                   jax.ShapeDtypeStruct((b, dil, sub_len, GROUP_W), F32)),
        grid=(b, dil, sub_len // tq),
        in_specs=[
            pl.BlockSpec(blk_q, cur),
            pl.BlockSpec(blk_h, prev), pl.BlockSpec(blk_q, cur), pl.BlockSpec(blk_h, nxt),
            pl.BlockSpec(blk_h, prev), pl.BlockSpec(blk_q, cur), pl.BlockSpec(blk_h, nxt),
            pl.BlockSpec(bias.shape, lambda bi, r, i: (0, 0, 0)),
        ],
        out_specs=(pl.BlockSpec((None, None, tq, GROUP_W), cur),
                   pl.BlockSpec((None, None, tq, GROUP_W), cur)),
        compiler_params=_cparams(("parallel", "parallel", "parallel")),
        name=f"attn_d{dil}",
    )(q, k, k, k, v, v, v, bias)


def _hgrn_kernel(qf_ref, lf_ref, vf_ref, qr_ref, lr_ref, vr_ref, of_ref, or_ref,
                 sf_ref, sr_ref, qif_ref, qir_ref, uf_ref, ur_ref, df_ref, dr_ref, *, ts):
    c_len = HGRN_CHUNK
    nc = ts // c_len

    @pl.when(pl.program_id(1) == 0)
    def _():
        sf_ref[...] = jnp.zeros_like(sf_ref)
        sr_ref[...] = jnp.zeros_like(sr_ref)

    r_i = lax.broadcasted_iota(jnp.int32, (c_len, c_len), 0)
    c_i = lax.broadcasted_iota(jnp.int32, (c_len, c_len), 1)
    lower = r_i >= c_i
    upper = r_i <= c_i
    tri_f = jnp.where(lower, 1.0, 0.0).astype(BF16)
    tri_r = jnp.where(upper, 1.0, 0.0).astype(BF16)

    def cumsum(tri, x):
        hi = x.astype(BF16)
        r1 = x - hi.astype(F32)
        mid = r1.astype(BF16)
        lo = (r1 - mid.astype(F32)).astype(BF16)
        d = lambda a: jnp.dot(tri, a, preferred_element_type=F32)
        return d(hi) + d(mid) + d(lo)

    n_sub = c_len // HGRN_SUB
    shift = HGRN_SUB.bit_length() - 1
    sub_f = jnp.right_shift(lax.broadcasted_iota(jnp.int32, (c_len, D_B), 0), shift)
    sub_tf = jnp.right_shift(lax.broadcasted_iota(jnp.int32, (D_B, c_len), 1), shift)
    t_i = lax.broadcasted_iota(jnp.int32, (2 * c_len, c_len), 0) & (c_len - 1)
    s_i = lax.broadcasted_iota(jnp.int32, (2 * c_len, c_len), 1)
    same_sub2 = jnp.right_shift(t_i, shift) == jnp.right_shift(s_i, shift)
    lower2, upper2 = t_i >= s_i, t_i <= s_i
    lower_sub2, upper_sub2 = same_sub2 & lower2, same_sub2 & upper2
    row_head = lax.broadcasted_iota(jnp.int32, (2 * c_len, 2 * DK_B), 0) // c_len
    col_head = lax.broadcasted_iota(jnp.int32, (2 * c_len, 2 * DK_B), 1) // DK_B
    pair_cols = jnp.where(row_head == col_head, 1.0, 0.0).astype(BF16)
    pair_off = jnp.concatenate([pair_cols] * (n_sub - 1), axis=1)

    def intra(q_ref, l_ref, v_ref, o_ref, qi_ref, u_ref, d_ref, c, rev):
        rows = pl.ds(pl.multiple_of(c * c_len, c_len), c_len)
        lf = l_ref[0, rows, :]
        q = q_ref[0, rows, :].astype(F32)
        v = v_ref[0, rows, :]
        k = 1.0 - jnp.exp(lf)
        b = cumsum(tri_r if rev else tri_f, lf)
        yield
        sub = (n_sub - 1 - sub_f) if rev else sub_f

        def b_at(pos):
            r = c_len - 1 - pos if rev else pos
            return b[r:r + 1, :]

        def per_sub(vals):
            vals = vals[::-1] if rev else vals
            return jnp.concatenate([jnp.broadcast_to(x, (HGRN_SUB, D_B)) for x in vals], axis=0)

        a_end = [b_at(HGRN_SUB * j + HGRN_SUB - 1) for j in range(n_sub)]
        a_start = [jnp.zeros((1, D_B), F32)] + a_end[:-1]
        btot = a_end[-1]
        end_full, start_full = per_sub(a_end), per_sub(a_start)
        stores = [(qi_ref, (rows, slice(None)), (q * jnp.exp(b)).astype(BF16))]
        q_off = [jnp.where(sub > j, q * jnp.exp(jnp.minimum(b - a_end[j], 0.0)), 0.0).astype(BF16)
                 for j in range(n_sub - 1)]
        k_end = k * jnp.exp(end_full - b)
        sub_t = (n_sub - 1 - sub_tf) if rev else sub_tf
        q_dia = (q * jnp.exp(b - start_full)).astype(BF16)
        k_upd = (k * jnp.exp(btot - b)).astype(BF16)
        k_end_t = k_end.T
        k_dia_t = (k * jnp.exp(jnp.minimum(start_full - b, HGRN_EXP_CLAMP))).T.astype(BF16)
        k_off_t = [jnp.where(sub_t == j, k_end_t, 0.0).astype(BF16) for j in range(n_sub - 1)]
        dec8_t = jnp.broadcast_to(jnp.exp(btot), (SUBLANES, D_B)).T
        keep, keep_d = (upper2, upper_sub2) if rev else (lower2, lower_sub2)
        two = lambda a: jnp.concatenate([a, a], axis=0)
        for p in range(N_HEADS_B // 2):
            ps = slice(2 * p * DK_B, (2 * p + 2) * DK_B)
            qc = two(jnp.concatenate([q_off[j][:, ps] for j in range(n_sub - 1)], axis=1)) * pair_off
            kc = jnp.concatenate([k_off_t[j][ps, :] for j in range(n_sub - 1)], axis=0)
            yield
            sc = jnp.dot(qc, kc, preferred_element_type=F32)
            sc_d = jnp.dot(two(q_dia[:, ps]) * pair_cols, k_dia_t[ps, :], preferred_element_type=F32)
            u2 = lax.dot_general(k_upd[:, ps], v[:, ps], (((0,), (0,)), ((), ())),
                                 preferred_element_type=F32)
            yield
            sc = (jnp.where(keep, sc, 0.0) + jnp.where(keep_d, sc_d, 0.0)).astype(BF16)
            o2 = jnp.dot(sc, v[:, ps], preferred_element_type=F32)
            for i in range(2):
                h = 2 * p + i
                hs = slice(h * DK_B, (h + 1) * DK_B)
                blk = slice(i * DK_B, (i + 1) * DK_B)
                stores.append((o_ref, (0, rows, hs), o2[i * c_len:(i + 1) * c_len, blk]))
                stores.append((u_ref, (c, h), u2[blk, blk]))
                stores.append((d_ref, (c, h), jnp.broadcast_to(dec8_t[hs, 0:1], (DK_B, DK_B))))
        return stores

    def carry_state(o_ref, qi_ref, u_ref, d_ref, s_ref, c):
        rows = pl.ds(pl.multiple_of(c * c_len, c_len), c_len)
        zero = jnp.zeros((DK_B, DK_B), BF16)
        stores = []
        for p in range(N_HEADS_B // 2):
            ps = slice(2 * p * DK_B, (2 * p + 2) * DK_B)
            sa, sb = s_ref[2 * p], s_ref[2 * p + 1]
            s_bd = jnp.concatenate([jnp.concatenate([sa.astype(BF16), zero], axis=1),
                                    jnp.concatenate([zero, sb.astype(BF16)], axis=1)], axis=0)
            o_new = o_ref[0, rows, ps] + jnp.dot(qi_ref[rows, ps], s_bd, preferred_element_type=F32)
            stores.append((o_ref, (0, rows, ps), o_new))
            stores.append((s_ref, (2 * p,), sa * d_ref[c, 2 * p] + u_ref[c, 2 * p]))
            stores.append((s_ref, (2 * p + 1,), sb * d_ref[c, 2 * p + 1] + u_ref[c, 2 * p + 1]))
        return stores

    def commit(stores):
        for ref, idx, val in stores:
            ref[idx] = val

    def intra_body(trip, carry):
        gens = []
        for i in range(HGRN_CHUNKS_PER_TRIP):
            c = trip * HGRN_CHUNKS_PER_TRIP + i
            gens.append(intra(qf_ref, lf_ref, vf_ref, of_ref, qif_ref, uf_ref, df_ref, c, False))
            gens.append(intra(qr_ref, lr_ref, vr_ref, or_ref, qir_ref, ur_ref, dr_ref, c, True))
        commit(sum(_lockstep(*gens), []))
        return carry

    def state_body(c, carry):
        commit(carry_state(of_ref, qif_ref, uf_ref, df_ref, sf_ref, c)
               + carry_state(or_ref, qir_ref, ur_ref, dr_ref, sr_ref, nc - 1 - c))
        return carry

    lax.fori_loop(0, nc // HGRN_CHUNKS_PER_TRIP, intra_body, 0)
    lax.fori_loop(0, nc, state_body, 0)


def _hgrn(qb, lff, lfb, ib):
    b, s, _ = qb.shape
    ts = min(TS_HGRN, s)
    nt = s // ts
    nc = ts // HGRN_CHUNK
    fwd = lambda bi, j: (bi, j, 0)
    rev = lambda bi, j: (bi, nt - 1 - j, 0)
    blk = (1, ts, D_B)
    return pl.pallas_call(
        functools.partial(_hgrn_kernel, ts=ts),
        out_shape=(jax.ShapeDtypeStruct((b, s, D_B), F32), jax.ShapeDtypeStruct((b, s, D_B), F32)),
        grid=(b, nt),
        in_specs=[pl.BlockSpec(blk, fwd), pl.BlockSpec(blk, fwd), pl.BlockSpec(blk, fwd),
                  pl.BlockSpec(blk, rev), pl.BlockSpec(blk, rev), pl.BlockSpec(blk, rev)],
        out_specs=(pl.BlockSpec(blk, fwd), pl.BlockSpec(blk, rev)),
        scratch_shapes=[pltpu.VMEM((N_HEADS_B, DK_B, DK_B), F32),
                        pltpu.VMEM((N_HEADS_B, DK_B, DK_B), F32),
                        pltpu.VMEM((ts, D_B), BF16),
                        pltpu.VMEM((ts, D_B), BF16),
                        pltpu.VMEM((nc, N_HEADS_B, DK_B, DK_B), F32),
                        pltpu.VMEM((nc, N_HEADS_B, DK_B, DK_B), F32),
                        pltpu.VMEM((nc, N_HEADS_B, DK_B, DK_B), F32),
                        pltpu.VMEM((nc, N_HEADS_B, DK_B, DK_B), F32)],
        compiler_params=_cparams(("parallel", "arbitrary")),
        name="hgrn",
    )(qb, lff, ib, qb, lfb, ib)


def _mix_kernel(x_ref, o1_ref, o2_ref, o3_ref, l1_ref, l2_ref, l3_ref, of_ref, ob_ref, og_ref,
                gt_ref, wa_ref, wb_ref, wo_ref, ogain_ref, nmoe_ref, wrh_ref, wrl_ref,
                br_ref, cnt_ref,
                h_ref, xt_ref, tw_ref, ps_ref, ct_ref, rt_ref, cnt_out_ref, run_ref, so_ref, *, tm):
    i = pl.program_id(0)

    @pl.when(i == 0)
    def _():
        run_ref[...] = cnt_ref[...]

    def token_major(src_ref, scr_ref, dil):
        if dil == 1:
            return src_ref[0].astype(F32)
        n_chunk = scr_ref.shape[0]
        for r in range(dil):
            blk = src_ref[r].astype(F32)
            for c in range(n_chunk):
                scr_ref[c, pl.ds(r, tm // dil, stride=dil), :] = blk[:, c * LANES:(c + 1) * LANES]
        return jnp.concatenate([scr_ref[c] for c in range(n_chunk)], axis=-1)

    dils = [dil for _, dil in ATTN_GROUPS]
    l1, l2, l3 = [token_major(r, so_ref, d) for r, d in zip((l1_ref, l2_ref, l3_ref), dils)]
    mx = jnp.maximum(jnp.maximum(l1, l2), l3)
    e1, e2, e3 = jnp.exp(l1 - mx), jnp.exp(l2 - mx), jnp.exp(l3 - mx)
    attn = (e1 * token_major(o1_ref, so_ref, dils[0]) + e2 * token_major(o2_ref, so_ref, dils[1])
            + e3 * token_major(o3_ref, so_ref, dils[2])) / (e1 + e2 + e3)

    o = of_ref[...] + ob_ref[...]
    parts = []
    for h in range(N_HEADS_B):
        oh = o[:, h * DK_B:(h + 1) * DK_B]
        ms = jnp.mean(oh * oh, axis=-1, keepdims=True)
        parts.append(oh * lax.rsqrt(ms + EPS))
    hg = jnp.concatenate(parts, axis=-1) * ogain_ref[...] * og_ref[...].astype(F32)

    attn_bf, hg_bf = attn.astype(BF16), hg.astype(BF16)
    n_part = MIX_ROW_PARTS
    tp = tm // n_part

    def token_chain(part):
        rs = slice(part * tp, (part + 1) * tp)
        pa = jnp.dot(attn_bf[rs], wa_ref[...], preferred_element_type=F32)
        pb = jnp.dot(hg_bf[rs], wb_ref[...], preferred_element_type=F32)
        yield
        mixed = (gt_ref[rs, :D_MODEL].astype(F32) * pa + gt_ref[rs, D_MODEL:].astype(F32) * pb)
        h = x_ref[rs, :] + jnp.dot(mixed.astype(BF16), wo_ref[...], preferred_element_type=F32)
        yield
        h_ref[rs, :] = h
        ms = jnp.mean(h * h, axis=-1, keepdims=True)
        hn = h * lax.rsqrt(ms + EPS) * nmoe_ref[...]
        hi, lo = _split_bf16(hn)
        lg = (jnp.dot(hi, wrh_ref[...], preferred_element_type=F32)
              + jnp.dot(lo, wrh_ref[...], preferred_element_type=F32)
              + jnp.dot(hi, wrl_ref[...], preferred_element_type=F32)) + br_ref[...]
        yield
        lane_p = lax.broadcasted_iota(jnp.int32, (tp, LANES), 1)
        vals, idxs = [], []
        onehot = jnp.zeros((tp, LANES), F32)
        for _ in range(TOP_K):
            m = jnp.max(lg, axis=-1, keepdims=True)
            idx = jnp.min(jnp.where(lg == m, lane_p, LANES), axis=-1, keepdims=True)
            sel = lane_p == idx
            onehot = jnp.where(sel, 1.0, onehot)
            lg = jnp.where(sel, NEG * 2, lg)
            vals.append(m)
            idxs.append(idx)
        exps = [jnp.exp(v - vals[0]) for v in vals]
        inv = 1.0 / (exps[0] + exps[1] + exps[2] + exps[3])
        return hi, onehot, idxs, [e * inv for e in exps]

    parts = _lockstep(*[token_chain(p) for p in range(n_part)])
    rows_of = lambda pick: jnp.concatenate([pick(p) for p in parts], axis=0)
    hi = rows_of(lambda p: p[0])
    onehot = rows_of(lambda p: p[1])
    idxs = [rows_of(lambda p, k=k: p[2][k]) for k in range(TOP_K)]
    gates = [rows_of(lambda p, k=k: p[3][k]) for k in range(TOP_K)]
    lane = lax.broadcasted_iota(jnp.int32, (tm, LANES), 1)

    r_i = lax.broadcasted_iota(jnp.int32, (tm, tm), 0)
    c_i = lax.broadcasted_iota(jnp.int32, (tm, tm), 1)
    tri = jnp.where(r_i > c_i, 1.0, 0.0).astype(BF16)
    local = jnp.dot(tri, onehot.astype(BF16), preferred_element_type=F32)
    cnt_tile = jnp.sum(onehot, axis=0, keepdims=True)
    e_r = lax.broadcasted_iota(jnp.int32, (LANES, LANES), 0)
    e_c = lax.broadcasted_iota(jnp.int32, (LANES, LANES), 1)
    before_e = jnp.where(e_r < e_c, 1.0, 0.0).astype(BF16)
    off = jnp.dot(jnp.broadcast_to(cnt_tile, (8, LANES)).astype(BF16), before_e,
                  preferred_element_type=F32)[0:1]
    slot = off + local
    tw = jnp.zeros((tm, LANES), F32)
    ps = jnp.full((tm, LANES), -1.0, F32)
    for k in range(TOP_K):
        slot_k = jnp.sum(jnp.where(lane == idxs[k], slot, 0.0), axis=-1, keepdims=True)
        tw = jnp.where(lane == k, gates[k], tw)
        ps = jnp.where(lane == k, slot_k, ps)
    tw_ref[...] = tw
    ps_ref[...] = ps.astype(jnp.int32)

    ps_t = ps.T
    n_out = tm * TOP_K // n_part
    for part in range(n_part):
        row_id = (lax.broadcasted_iota(jnp.int32, (n_out, tm), 0) + part * n_out).astype(F32)
        perm = jnp.zeros((n_out, tm), F32)
        for k in range(TOP_K):
            perm = jnp.where(row_id == ps_t[k:k + 1, :], 1.0, perm)
        rows = jnp.dot(perm.astype(BF16), hi, preferred_element_type=F32)
        for c in range(D_MODEL // LANES):
            xt_ref[pl.ds(part * n_out * ROW_SUB + c, n_out, stride=ROW_SUB), :] = (
                rows[:, c * LANES:(c + 1) * LANES])

    ct_ref[0] = cnt_tile
    rt_ref[0] = run_ref[...]
    run_new = run_ref[...] + cnt_tile
    run_ref[...] = run_new
    cnt_out_ref[...] = run_new


def _mix(x2, s, o1, o2, o3, l1, l2, l3, of, ob, og, gt, wa, wb, wo, ogain, nmoe, wrh, wrl, br, cnt):
    t = x2.shape[0]
    tm = TM_MIX
    tps = s // tm
    row = lambda i: (i, 0)
    const = lambda i: (0, 0)
    seq = lambda i: (i // tps, 0, i % tps, 0)
    rb = lambda w: pl.BlockSpec((tm, w), row)
    cb = lambda a: pl.BlockSpec(a.shape, const)
    gb = lambda a: pl.BlockSpec((None, a.shape[1], tm // a.shape[1], a.shape[3]), seq)
    return pl.pallas_call(
        functools.partial(_mix_kernel, tm=tm),
        out_shape=(jax.ShapeDtypeStruct((t, D_MODEL), F32),
                   jax.ShapeDtypeStruct((t * TOP_K * ROW_SUB, LANES), F32),
                   jax.ShapeDtypeStruct((t, LANES), F32),
                   jax.ShapeDtypeStruct((t, LANES), jnp.int32),
                   jax.ShapeDtypeStruct((t // tm, 1, LANES), F32),
                   jax.ShapeDtypeStruct((t // tm, 1, LANES), F32),
                   jax.ShapeDtypeStruct((1, LANES), F32)),
        grid=(t // tm,),
        in_specs=[rb(D_MODEL), gb(o1), gb(o2), gb(o3), gb(l1), gb(l2), gb(l3),
                  rb(D_B), rb(D_B), rb(D_B), rb(2 * D_MODEL),
                  cb(wa), cb(wb), cb(wo), cb(ogain), cb(nmoe), cb(wrh), cb(wrl), cb(br), cb(cnt)],
        out_specs=(rb(D_MODEL), pl.BlockSpec((tm * TOP_K * ROW_SUB, LANES), row), rb(LANES), rb(LANES),
                   pl.BlockSpec((1, 1, LANES), lambda i: (i, 0, 0)),
                   pl.BlockSpec((1, 1, LANES), lambda i: (i, 0, 0)),
                   pl.BlockSpec((1, LANES), const)),
        scratch_shapes=[pltpu.VMEM((1, LANES), F32), pltpu.VMEM((GROUP_W // LANES, tm, LANES), F32)],
        compiler_params=_cparams(("arbitrary",)),
        name="mix",
    )(x2, o1, o2, o3, l1, l2, l3, of, ob, og, gt, wa, wb, wo, ogain, nmoe, wrh, wrl, br, cnt)


def _start_tile_segments(tile, off_tbl, row_tbl, len_tbl, make_piece):
    def segment(e, carry):
        sidx = tile * N_EXPERTS + e
        t0, r0, n = off_tbl[sidx], row_tbl[sidx], len_tbl[sidx]
        n_bulk = jnp.right_shift(n, SEG_CHUNK.bit_length() - 1)

        def bulk(j, c):
            make_piece(t0 + j * SEG_CHUNK, r0 + j * SEG_CHUNK, SEG_CHUNK).start()
            return c

        lax.fori_loop(0, n_bulk, bulk, 0)
        done = n_bulk * SEG_CHUNK
        bit = SEG_CHUNK // 2
        while bit >= 1:
            has = (n & bit) != 0

            @pl.when(has)
            def _(done=done, bit=bit):
                make_piece(t0 + done, r0 + done, bit).start()

            done = done + jnp.where(has, bit, 0)
            bit //= 2
        return carry

    lax.fori_loop(0, N_EXPERTS, segment, 0)


def _row_slice(row, n_rows):
    return pl.ds(pl.multiple_of(row * ROW_SUB, ROW_SUB), n_rows * ROW_SUB)


def _dispatch_kernel(off_tbl, row_tbl, len_tbl, fill_row, fill_len, xa_ref, xb_ref, xs_ref, zero_ref, sem,
                     *, tiles_a):
    i = pl.program_id(0)
    blk_rows = xa_ref.shape[0]

    def from_tile(xt_ref):
        def piece(tile_row, expert_row, n_rows):
            return pltpu.make_async_copy(xt_ref.at[_row_slice(tile_row, n_rows)],
                                         xs_ref.at[_row_slice(expert_row, n_rows)], sem)

        _start_tile_segments(i, off_tbl, row_tbl, len_tbl, piece)
        pltpu.make_async_copy(xt_ref, xs_ref.at[pl.ds(0, blk_rows)], sem).wait()

    def zeros_to(row, n_rows):
        return pltpu.make_async_copy(zero_ref.at[pl.ds(0, n_rows * ROW_SUB)],
                                     xs_ref.at[_row_slice(row, n_rows)], sem)

    @pl.when(i == 0)
    def _():
        zero_ref[...] = jnp.zeros_like(zero_ref)

        def fill(e, total):
            r0, n = fill_row[e], fill_len[e]
            n_blk = jnp.right_shift(n, BM.bit_length() - 1)

            def blocks(j, c):
                zeros_to(r0 + j * BM, BM).start()
                return c

            lax.fori_loop(0, n_blk, blocks, 0)
            done = n_blk * BM
            bit = BM // 2
            while bit >= 1:
                has = (n & bit) != 0

                @pl.when(has)
                def _(done=done, bit=bit):
                    zeros_to(r0 + done, bit).start()

                done = done + jnp.where(has, bit, 0)
                bit //= 2
            return total + n

        total = lax.fori_loop(0, N_EXPERTS + 1, fill, 0)
        bit = 1
        while bit * ROW_SUB <= xs_ref.shape[0]:
            @pl.when((total & bit) != 0)
            def _(bit=bit):
                n = bit * ROW_SUB
                pltpu.make_async_copy(xs_ref.at[pl.ds(0, n)], xs_ref.at[pl.ds(0, n)], sem).wait()

            bit *= 2

    @pl.when(i < tiles_a)
    def _():
        from_tile(xa_ref)

    @pl.when(i >= tiles_a)
    def _():
        from_tile(xb_ref)


def _dispatch(off_tbl, row_tbl, len_tbl, fill_row, fill_len, xt_a, xt_b, n_rows):
    blk = TM_MIX * TOP_K * ROW_SUB
    tiles_a, tiles_b = xt_a.shape[0] // blk, xt_b.shape[0] // blk
    return pl.pallas_call(
        functools.partial(_dispatch_kernel, tiles_a=tiles_a),
        out_shape=jax.ShapeDtypeStruct((n_rows * ROW_SUB, LANES), F32),
        grid_spec=pltpu.PrefetchScalarGridSpec(
            num_scalar_prefetch=5,
            grid=(tiles_a + tiles_b,),
            in_specs=[pl.BlockSpec((blk, LANES), lambda i, *_: (jnp.minimum(i, tiles_a - 1), 0)),
                      pl.BlockSpec((blk, LANES), lambda i, *_: (jnp.maximum(i - tiles_a, 0), 0))],
            out_specs=pl.BlockSpec(memory_space=pl.ANY),
            scratch_shapes=[pltpu.VMEM((BM * ROW_SUB, LANES), F32), pltpu.SemaphoreType.DMA],
        ),
        compiler_params=_cparams(("arbitrary",)),
        name="dispatch",
    )(off_tbl, row_tbl, len_tbl, fill_row, fill_len, xt_a, xt_b)


def _combine_kernel(off_tbl, row_tbl, len_tbl, ps_ref, tw_ref, h_ref, ys_ref, y_ref, buf_ref, sems, *, tm):
    i = pl.program_id(0)
    n_rows = tm * TOP_K

    def start(tile):
        slot = tile % 2

        def piece(tile_row, expert_row, n):
            return pltpu.make_async_copy(ys_ref.at[_row_slice(expert_row, n)],
                                         buf_ref.at[slot, _row_slice(tile_row, n)], sems.at[slot])

        _start_tile_segments(tile, off_tbl, row_tbl, len_tbl, piece)

    @pl.when(i == 0)
    def _():
        start(i)

    @pl.when(i + 1 < pl.num_programs(0))
    def _():
        start(i + 1)

    slot = i % 2
    yt_ref = buf_ref.at[slot]
    pltpu.make_async_copy(ys_ref.at[pl.ds(0, n_rows * ROW_SUB)], yt_ref, sems.at[slot]).wait()
    ysorted = jnp.concatenate(
        [yt_ref[pl.ds(c, n_rows, stride=ROW_SUB), :] for c in range(ROW_SUB)], axis=-1).astype(BF16)
    col = lax.broadcasted_iota(jnp.int32, (tm, n_rows), 1)
    ps = ps_ref[...]
    tw = tw_ref[...]
    pw = jnp.zeros((tm, n_rows), F32)
    for k in range(TOP_K):
        pw = jnp.where(col == ps[:, k:k + 1], tw[:, k:k + 1], pw)
    hi, lo = _split_bf16(pw)
    y_ref[...] = (h_ref[...] + jnp.dot(hi, ysorted, preferred_element_type=F32)
                  + jnp.dot(lo, ysorted, preferred_element_type=F32))


def _combine(off_tbl, row_tbl, len_tbl, ps, tw, h, ys):
    t = h.shape[0]
    tm = TM_MIX
    row = lambda i, *_: (i, 0)
    return pl.pallas_call(
        functools.partial(_combine_kernel, tm=tm),
        out_shape=jax.ShapeDtypeStruct((t, D_MODEL), F32),
        grid_spec=pltpu.PrefetchScalarGridSpec(
            num_scalar_prefetch=3,
            grid=(t // tm,),
            in_specs=[pl.BlockSpec((tm, LANES), row),
                      pl.BlockSpec((tm, LANES), row),
                      pl.BlockSpec((tm, D_MODEL), row),
                      pl.BlockSpec(memory_space=pl.ANY)],
            out_specs=pl.BlockSpec((tm, D_MODEL), row),
            scratch_shapes=[pltpu.VMEM((2, tm * TOP_K * ROW_SUB, LANES), F32),
                            pltpu.SemaphoreType.DMA((2,))],
        ),
        compiler_params=_cparams(("arbitrary",)),
        name="combine",
    )(off_tbl, row_tbl, len_tbl, ps, tw, h, ys)


def _experts_kernel(be_ref, nused_ref, xs_ref, wgu_ref, bgu_ref, wd_ref, bd_ref, ys_ref, wgu_bf, wd_bf):
    i = pl.program_id(0)
    used = i < nused_ref[0]
    new_expert = (i == 0) | (be_ref[i] != be_ref[jnp.maximum(i - 1, 0)])

    @pl.when(used & new_expert)
    def _():
        wgu_bf[...] = wgu_ref[0].astype(BF16)
        wd_bf[...] = wd_ref[0].astype(BF16)

    @pl.when(used)
    def _():
        x = jnp.concatenate(
            [xs_ref[pl.ds(c, BM, stride=ROW_SUB), :] for c in range(ROW_SUB)], axis=-1).astype(BF16)
        hh = jnp.dot(x, wgu_bf[...], preferred_element_type=F32) + bgu_ref[0]
        gate = jnp.minimum(hh[:, :D_FF], SWIGLU_LIMIT)
        up = jnp.clip(hh[:, D_FF:], -SWIGLU_LIMIT, SWIGLU_LIMIT)
        glu = gate * jax.nn.sigmoid(SWIGLU_ALPHA * gate)
        act = ((up + 1.0) * glu).astype(BF16)
        y = jnp.dot(act, wd_bf[...], preferred_element_type=F32) + bd_ref[0]
        for c in range(ROW_SUB):
            ys_ref[pl.ds(c, BM, stride=ROW_SUB), :] = y[:, c * LANES:(c + 1) * LANES]

    @pl.when(i >= nused_ref[0])
    def _():
        ys_ref[...] = jnp.zeros_like(ys_ref)


def _experts(block_e, nused, xs, wgu, bgu, wd, bd):
    nb = xs.shape[0] // (BM * ROW_SUB)
    emap3 = lambda i, be, nu: (be[i], 0, 0)
    rows = pl.BlockSpec((BM * ROW_SUB, LANES), lambda i, be, nu: (i, 0))
    return pl.pallas_call(
        _experts_kernel,
        out_shape=jax.ShapeDtypeStruct(xs.shape, F32),
        grid_spec=pltpu.PrefetchScalarGridSpec(
            num_scalar_prefetch=2,
            grid=(nb,),
            in_specs=[rows,
                      pl.BlockSpec((1, D_MODEL, 2 * D_FF), emap3),
                      pl.BlockSpec((1, 1, 2 * D_FF), emap3),
                      pl.BlockSpec((1, D_FF, D_MODEL), emap3),
                      pl.BlockSpec((1, 1, D_MODEL), emap3)],
            out_specs=rows,
            scratch_shapes=[pltpu.VMEM((D_MODEL, 2 * D_FF), BF16), pltpu.VMEM((D_FF, D_MODEL), BF16)],
        ),
        compiler_params=_cparams(("arbitrary",)),
        name="experts",
    )(block_e, nused, xs, wgu, bgu, wd, bd)


def _head_indicator(n_cols, head_dim):
    e = np.zeros((n_cols, LANES), np.float32)
    e[np.arange(n_cols), np.arange(n_cols) // head_dim] = 1.0
    return e


def _mixer(x, prm):
    b, s, d = x.shape
    t = b * s
    x2 = x.reshape(t, d)
    res = _inproj(x2, b, s, prm["norm_mix"], prm["w_all"], prm["q_gain"], prm["k_gain"], prm["e_in"],
                  prm["lb_f"], prm["lb_b"])
    n_grp = len(ATTN_GROUPS)
    qs, ks, vs = res[:n_grp], res[n_grp:2 * n_grp], res[2 * n_grp:3 * n_grp]
    qb, lff, lfb, ib, og, gt = res[3 * n_grp:]
    r3 = lambda a: a.reshape(b, s, a.shape[-1])
    outs, lses = [], []
    for g in range(n_grp):
        o, lse = _attention_group(qs[g], ks[g], vs[g], g)
        outs.append(o)
        lses.append(lse)
    of, ob = _hgrn(r3(qb), r3(lff), r3(lfb), r3(ib))
    return x2, outs, lses, of.reshape(t, D_B), ob.reshape(t, D_B), og, gt


def kernel(x_prompt, x_sample, norm_mix, w_in, q_gain, k_gain, hgrn_lb, hgrn_o_gain, w_gate, w_proj_a,
           w_proj_b, w_out, norm_moe, w_router, b_router, w_gu, b_gu, w_down, b_down):
    l = 0
    lb = jnp.cumsum(jax.nn.softmax(hgrn_lb.astype(F32), axis=1), axis=1)
    wr = jnp.zeros((D_MODEL, LANES), F32).at[:, :N_EXPERTS].set(w_router[l])
    wr_hi = wr.astype(BF16)
    prm = {
        "norm_mix": norm_mix[l].reshape(1, D_MODEL),
        "w_all": jnp.concatenate([w_in[l], w_gate[l]], axis=1).astype(BF16),
        "q_gain": q_gain[l].reshape(1, W_A) * (HEAD_DIM_A ** -0.5),
        "k_gain": k_gain[l].reshape(1, W_A),
        "e_in": jnp.asarray(_head_indicator(PIECE, HEAD_DIM_A) @ _head_indicator(PIECE, HEAD_DIM_A).T, BF16),
        "lb_f": lb[0, l].reshape(1, D_B),
        "lb_b": lb[1, l].reshape(1, D_B),
    }
    wa, wb, wo = w_proj_a[l].astype(BF16), w_proj_b[l].astype(BF16), w_out[l].astype(BF16)
    ogain = hgrn_o_gain[l].reshape(1, D_B)
    nmoe = norm_moe[l].reshape(1, D_MODEL)
    wr_lo = (wr - wr_hi.astype(F32)).astype(BF16)
    br = jnp.full((1, LANES), NEG, F32).at[0, :N_EXPERTS].set(b_router[l])

    cnt = jnp.zeros((1, LANES), F32)
    per_batch = []
    for x in (x_prompt, x_sample):
        x2, outs, lses, of, ob, og, gt = _mixer(x, prm)
        h, xt, tw, ps, ct, rt, cnt = _mix(x2, x.shape[1], outs[0], outs[1], outs[2], lses[0], lses[1], lses[2],
                                          of, ob, og, gt, wa, wb, wo, ogain, nmoe, wr_hi, wr_lo, br, cnt)
        per_batch.append((x.shape, h, xt, tw, ps, ct, rt))

    n_tok = sum(pb[1].shape[0] for pb in per_batch)
    sizes = cnt[0, :N_EXPERTS].astype(jnp.int32)
    pad_sizes = (sizes + BM - 1) // BM * BM
    pad_ends = jnp.cumsum(pad_sizes)
    pad_starts = pad_ends - pad_sizes
    nb = (n_tok * TOP_K) // BM + N_EXPERTS
    block_start = jnp.arange(nb, dtype=jnp.int32) * BM
    block_e = jnp.minimum(jnp.sum(pad_ends[None, :] <= block_start[:, None], axis=1),
                          N_EXPERTS - 1).astype(jnp.int32)
    nused = (pad_ends[-1:] // BM).astype(jnp.int32)

    tables = []
    for _, _, _, _, _, ct, rt in per_batch:
        cnt_te = ct[:, 0, :N_EXPERTS].astype(jnp.int32)
        tile_off = jnp.cumsum(cnt_te, axis=1) - cnt_te
        expert_row = pad_starts[None, :] + rt[:, 0, :N_EXPERTS].astype(jnp.int32)
        tables.append((tile_off.reshape(-1), expert_row.reshape(-1), cnt_te.reshape(-1)))

    fill_row = jnp.concatenate([pad_starts + sizes, pad_ends[-1:]]).astype(jnp.int32)
    fill_len = jnp.concatenate([pad_sizes - sizes, nb * BM - pad_ends[-1:]]).astype(jnp.int32)
    both = [jnp.concatenate(cols) for cols in zip(*tables)]
    xs = _dispatch(*both, fill_row, fill_len, per_batch[0][2], per_batch[1][2], nb * BM)
    ys = _experts(block_e, nused, xs, w_gu[l], b_gu[l].reshape(N_EXPERTS, 1, 2 * D_FF),
                  w_down[l], b_down[l].reshape(N_EXPERTS, 1, D_MODEL))
    results = []
    for (shape, h, _, tw, ps, _, _), tbl in zip(per_batch, tables):
        results.append(_combine(*tbl, ps, tw, h, ys).reshape(shape))
    return tuple(results)
```

```python
import functools
import math

import jax
import jax.numpy as jnp
import numpy as np
from jax import lax
from jax.experimental import pallas as pl
from jax.experimental.pallas import tpu as pltpu

F32 = jnp.float32
BF16 = jnp.bfloat16

D_MODEL = 1024
ATTN_GROUPS = ((128, 1), (512, 4), (2048, 16))
HEADS_PER_GROUP = 8
N_HEADS_A = 24
HEAD_DIM_A = 64
W_A = N_HEADS_A * HEAD_DIM_A
GROUP_W = HEADS_PER_GROUP * HEAD_DIM_A
N_SIDE = 64
N_HEADS_B = 4
DK_B = 128
HGRN_CHUNK = 64
HGRN_SUB = 16
HGRN_EXP_CLAMP = 80.0
HGRN_CHUNKS_PER_TRIP = 2
D_B = N_HEADS_B * DK_B
D_IN = 3 * W_A + 5 * D_B
N_EXPERTS = 32
TOP_K = 4
D_FF = 1024
SWIGLU_LIMIT = 7.0
SWIGLU_ALPHA = 1.702
EPS = 1e-6
NEG = -1e30

LANES = 128
VMEM_LIMIT = 56 * 1024 * 1024

TM_IN = 256
PIECE = 512
TQ = 2048
SQ = 128
ATTN_PAIRS_IN_FLIGHT = 1
TS_HGRN = 512
TM_MIX = 256
MIX_ROW_PARTS = 1
BM = 512
SUBLANES = 8
ROW_SUB = D_MODEL // LANES
assert ROW_SUB == SUBLANES
SEG_CHUNK = 16


def _cparams(sem):
    return pltpu.CompilerParams(dimension_semantics=sem, vmem_limit_bytes=VMEM_LIMIT)


def _lockstep(*gens):
    out = [None] * len(gens)
    live = list(range(len(gens)))
    while live:
        for g in list(live):
            try:
                next(gens[g])
            except StopIteration as done:
                out[g] = done.value
                live.remove(g)
    return out


def _split_bf16(x):
    hi = x.astype(BF16)
    lo = (x - hi.astype(F32)).astype(BF16)
    return hi, lo


def _inproj_kernel(x_ref, nw_ref, w_ref, qg_ref, kg_ref, e_ref, lbf_ref, lbb_ref,
                   q0_ref, q1_ref, q2_ref, k0_ref, k1_ref, k2_ref, v0_ref, v1_ref, v2_ref,
                   qb_ref, lff_ref, lfb_ref, ib_ref, og_ref, gt_ref, scr_ref, *, tm):
    q_refs, k_refs, v_refs = (q0_ref, q1_ref, q2_ref), (k0_ref, k1_ref, k2_ref), (v0_ref, v1_ref, v2_ref)
    x = x_ref[...]
    ms = jnp.mean(x * x, axis=-1, keepdims=True)
    xn = (x * lax.rsqrt(ms + EPS) * nw_ref[...]).astype(BF16)

    def proj(col):
        return jnp.dot(xn, w_ref[:, col:col + PIECE], preferred_element_type=F32)

    def normed(dst_ref, col, gain, dil):
        y = proj(col)
        yield
        ss = jnp.dot((y * y).astype(BF16), e_ref[...], preferred_element_type=F32)
        yield
        store_group(dst_ref, y * lax.rsqrt(ss * (1.0 / HEAD_DIM_A) + EPS) * gain, dil)

    def plain(dst_ref, col, dil):
        y = proj(col)
        yield
        yield
        store_group(dst_ref, y, dil)

    def store_group(dst_ref, y, dil):
        if dil == 1:
            dst_ref[0, 0] = y.astype(BF16)
            return
        for c in range(GROUP_W // LANES):
            scr_ref[c] = y[:, c * LANES:(c + 1) * LANES]
        for r in range(dil):
            rows = pl.ds(r, tm // dil, stride=dil)
            dst_ref[0, r] = jnp.concatenate(
                [scr_ref[c, rows, :] for c in range(GROUP_W // LANES)], axis=-1).astype(BF16)

    for g, (_, dil) in enumerate(ATTN_GROUPS):
        c = g * GROUP_W
        _lockstep(normed(q_refs[g], c, qg_ref[:, c:c + GROUP_W], dil),
                  normed(k_refs[g], W_A + c, kg_ref[:, c:c + GROUP_W], dil),
                  plain(v_refs[g], 2 * W_A + c, dil))
    base = 3 * W_A
    qb = proj(base)
    qb_ref[...] = (qb * jax.nn.sigmoid(qb) * (DK_B ** -0.5)).astype(BF16)
    for dst, lb_ref, off in ((lff_ref, lbf_ref, D_B), (lfb_ref, lbb_ref, 2 * D_B)):
        lb = lb_ref[...]
        f = lb + (1.0 - lb) * jax.nn.sigmoid(proj(base + off))
        dst[...] = jnp.log(f)
    ib_ref[...] = proj(base + 3 * D_B).astype(BF16)
    og = proj(base + 4 * D_B)
    og_ref[...] = (og * jax.nn.sigmoid(og)).astype(BF16)
    for p in range(2 * D_MODEL // PIECE):
        c = p * PIECE
        gt_ref[:, c:c + PIECE] = jax.nn.sigmoid(proj(D_IN + c)).astype(BF16)


def _inproj(x2, b, s, nw, w_all, qg, kg, e_mat, lbf, lbb):
    t = x2.shape[0]
    tm = TM_IN
    tps = s // tm
    n_all = w_all.shape[1]
    row = lambda i: (i, 0)
    const = lambda i: (0, 0)
    seq = lambda i: (i // tps, 0, i % tps, 0)
    widths = (D_B, D_B, D_B, D_B, D_B, 2 * D_MODEL)
    dtypes = (BF16, F32, F32, BF16, BF16, BF16)
    grp_shapes = [jax.ShapeDtypeStruct((b, dil, s // dil, GROUP_W), BF16) for _, dil in ATTN_GROUPS] * 3
    grp_specs = [pl.BlockSpec((1, dil, tm // dil, GROUP_W), seq) for _, dil in ATTN_GROUPS] * 3
    return pl.pallas_call(
        functools.partial(_inproj_kernel, tm=tm),
        out_shape=tuple(grp_shapes) + tuple(jax.ShapeDtypeStruct((t, w), dt) for w, dt in zip(widths, dtypes)),
        grid=(t // tm,),
        in_specs=[
            pl.BlockSpec((tm, D_MODEL), row),
            pl.BlockSpec((1, D_MODEL), const),
            pl.BlockSpec((D_MODEL, n_all), const),
            pl.BlockSpec((1, W_A), const),
            pl.BlockSpec((1, W_A), const),
            pl.BlockSpec((PIECE, PIECE), const),
            pl.BlockSpec((1, D_B), const),
            pl.BlockSpec((1, D_B), const),
        ],
        out_specs=tuple(grp_specs) + tuple(pl.BlockSpec((tm, w), row) for w in widths),
        scratch_shapes=[pltpu.VMEM((GROUP_W // LANES, tm, LANES), F32)],
        compiler_params=_cparams(("parallel",)),
        name="inproj",
    )(x2, nw, w_all, qg, kg, e_mat, lbf, lbb)


def _attn_kernel(q_ref, kp_ref, kc_ref, kn_ref, vp_ref, vc_ref, vn_ref, bias_ref,
                 o_ref, lse_ref, *, tq, sub_len):
    i = pl.program_id(2)
    sq = bias_ref.shape[1] // 2
    nk = sq + 2 * N_SIDE
    lane = lax.broadcasted_iota(jnp.int32, (sq, LANES), 1)
    low = lane < HEAD_DIM_A
    ones = jnp.ones((nk, LANES), BF16)
    zero = jnp.zeros((sq, LANES), BF16)

    def pair(res, kk, vv, j, pr, colbias):
        cols = slice(pr * LANES, (pr + 1) * LANES)
        q2 = q_ref[res, j * sq:(j + 1) * sq, cols]
        k2 = kk[j * sq:j * sq + nk, cols]
        v2 = jnp.concatenate([vv[j * sq:j * sq + nk, cols], ones], axis=1)
        q_st = jnp.concatenate([jnp.where(low, q2, zero), jnp.where(low, zero, q2)], axis=0)
        s = lax.dot_general(q_st, k2, (((1,), (1,)), ((), ())), preferred_element_type=F32)
        yield
        s = s + bias_ref[pr] + colbias
        m = jnp.max(s, axis=-1, keepdims=True)
        p = jnp.exp(s - m).astype(BF16)
        r = jnp.dot(p, v2, preferred_element_type=F32)
        yield
        o2 = jnp.where(low, r[:sq, :LANES] / r[:sq, LANES:], r[sq:, :LANES] / r[sq:, LANES:])
        o_ref[res, j * sq:(j + 1) * sq, cols] = o2.astype(o_ref.dtype)
        lse = m + jnp.log(r[:, LANES:LANES + 1])
        lse_ref[res, j * sq:(j + 1) * sq, cols] = jnp.where(low, lse[:sq], lse[sq:])

    n_pairs = HEADS_PER_GROUP // 2
    for res in range(q_ref.shape[0]):
        kk = jnp.concatenate([kp_ref[res], kc_ref[res], kn_ref[res]], axis=0)
        vv = jnp.concatenate([vp_ref[res], vc_ref[res], vn_ref[res]], axis=0)
        for j in range(tq // sq):
            kpos = i * tq + j * sq - N_SIDE + lax.broadcasted_iota(jnp.int32, (1, nk), 1)
            colbias = jnp.where((kpos >= 0) & (kpos < sub_len), 0.0, NEG).astype(F32)
            for pr in range(0, n_pairs, ATTN_PAIRS_IN_FLIGHT):
                _lockstep(*[pair(res, kk, vv, j, pr + d, colbias) for d in range(ATTN_PAIRS_IN_FLIGHT)])


def _attn_bias(sq, dil, slopes):
    nk = sq + 2 * N_SIDE
    rel = np.arange(nk)[None, :] - N_SIDE - np.arange(sq)[:, None]
    band = np.abs(rel) <= N_SIDE
    alibi = -slopes[:, None, None] * (dil * np.abs(rel)).astype(np.float32)[None]
    bias = np.where(band[None], alibi, NEG).astype(np.float32)
    return jnp.asarray(bias.reshape(HEADS_PER_GROUP // 2, 2 * sq, nk))


def _attention_group(q, k, v, g):
    b, dil, sub_len, _ = q.shape
    tq = min(TQ, sub_len)
    sq = min(SQ, sub_len)
    hb = tq // N_SIDE
    n_halo = sub_len // N_SIDE
    slopes = (2.0 ** (-8.0 * (np.arange(N_HEADS_A) + 1) / N_HEADS_A)).astype(np.float32)
    bias = _attn_bias(sq, dil, slopes[g * HEADS_PER_GROUP:(g + 1) * HEADS_PER_GROUP])
    cur = lambda bi, r, i: (bi, r, i, 0)
    prev = lambda bi, r, i: (bi, r, jnp.maximum(i * hb - 1, 0), 0)
    nxt = lambda bi, r, i: (bi, r, jnp.minimum((i + 1) * hb, n_halo - 1), 0)
    n_res = max(1, min(dil, TQ // tq))
    blk_q = (None, n_res, tq, GROUP_W)
    blk_h = (None, n_res, N_SIDE, GROUP_W)
    return pl.pallas_call(
        functools.partial(_attn_kernel, tq=tq, sub_len=sub_len),
        out_shape=(jax.ShapeDtypeStruct((b, dil, sub_len, GROUP_W), BF16),
                   jax.ShapeDtypeStruct((b, dil, sub_len, GROUP_W), F32)),
        grid=(b, dil // n_res, sub_len // tq),
        in_specs=[
            pl.BlockSpec(blk_q, cur),
            pl.BlockSpec(blk_h, prev), pl.BlockSpec(blk_q, cur), pl.BlockSpec(blk_h, nxt),
            pl.BlockSpec(blk_h, prev), pl.BlockSpec(blk_q, cur), pl.BlockSpec(blk_h, nxt),
            pl.BlockSpec(bias.shape, lambda bi, r, i: (0, 0, 0)),
        ],
        out_specs=(pl.BlockSpec(blk_q, cur), pl.BlockSpec(blk_q, cur)),
        compiler_params=_cparams(("parallel", "parallel", "parallel")),
        name=f"attn_d{dil}",
    )(q, k, k, k, v, v, v, bias)


def _hgrn_kernel(qf_ref, lf_ref, vf_ref, qr_ref, lr_ref, vr_ref, of_ref, or_ref,
                 sf_ref, sr_ref, qif_ref, qir_ref, uf_ref, ur_ref, df_ref, dr_ref, *, ts):
    c_len = HGRN_CHUNK
    nc = ts // c_len

    @pl.when(pl.program_id(1) == 0)
    def _():
        sf_ref[...] = jnp.zeros_like(sf_ref)
        sr_ref[...] = jnp.zeros_like(sr_ref)

    r_i = lax.broadcasted_iota(jnp.int32, (c_len, c_len), 0)
    c_i = lax.broadcasted_iota(jnp.int32, (c_len, c_len), 1)
    lower = r_i >= c_i
    upper = r_i <= c_i
    tri_f = jnp.where(lower, 1.0, 0.0).astype(BF16)
    tri_r = jnp.where(upper, 1.0, 0.0).astype(BF16)

    def cumsum(tri, x):
        hi = x.astype(BF16)
        r1 = x - hi.astype(F32)
        mid = r1.astype(BF16)
        lo = (r1 - mid.astype(F32)).astype(BF16)
        d = lambda a: jnp.dot(tri, a, preferred_element_type=F32)
        return d(hi) + d(mid) + d(lo)

    n_sub = c_len // HGRN_SUB
    shift = HGRN_SUB.bit_length() - 1
    sub_f = jnp.right_shift(lax.broadcasted_iota(jnp.int32, (c_len, D_B), 0), shift)
    sub_tf = jnp.right_shift(lax.broadcasted_iota(jnp.int32, (D_B, c_len), 1), shift)
    t_i = lax.broadcasted_iota(jnp.int32, (2 * c_len, c_len), 0) & (c_len - 1)
    s_i = lax.broadcasted_iota(jnp.int32, (2 * c_len, c_len), 1)
    same_sub2 = jnp.right_shift(t_i, shift) == jnp.right_shift(s_i, shift)
    lower2, upper2 = t_i >= s_i, t_i <= s_i
    lower_sub2, upper_sub2 = same_sub2 & lower2, same_sub2 & upper2
    row_head = lax.broadcasted_iota(jnp.int32, (2 * c_len, 2 * DK_B), 0) // c_len
    col_head = lax.broadcasted_iota(jnp.int32, (2 * c_len, 2 * DK_B), 1) // DK_B
    pair_cols = jnp.where(row_head == col_head, 1.0, 0.0).astype(BF16)
    pair_off = jnp.concatenate([pair_cols] * (n_sub - 1), axis=1)

    def intra(q_ref, l_ref, v_ref, o_ref, qi_ref, u_ref, d_ref, c, rev):
        rows = pl.ds(pl.multiple_of(c * c_len, c_len), c_len)
        lf = l_ref[0, rows, :]
        q = q_ref[0, rows, :].astype(F32)
        v = v_ref[0, rows, :]
        k = 1.0 - jnp.exp(lf)
        b = cumsum(tri_r if rev else tri_f, lf)
        yield
        sub = (n_sub - 1 - sub_f) if rev else sub_f

        def b_at(pos):
            r = c_len - 1 - pos if rev else pos
            return b[r:r + 1, :]

        def per_sub(vals):
            vals = vals[::-1] if rev else vals
            return jnp.concatenate([jnp.broadcast_to(x, (HGRN_SUB, D_B)) for x in vals], axis=0)

        a_end = [b_at(HGRN_SUB * j + HGRN_SUB - 1) for j in range(n_sub)]
        a_start = [jnp.zeros((1, D_B), F32)] + a_end[:-1]
        btot = a_end[-1]
        end_full, start_full = per_sub(a_end), per_sub(a_start)
        stores = [(qi_ref, (rows, slice(None)), (q * jnp.exp(b)).astype(BF16))]
        q_off = [jnp.where(sub > j, q * jnp.exp(jnp.minimum(b - a_end[j], 0.0)), 0.0).astype(BF16)
                 for j in range(n_sub - 1)]
        k_end = k * jnp.exp(end_full - b)
        sub_t = (n_sub - 1 - sub_tf) if rev else sub_tf
        q_dia = (q * jnp.exp(b - start_full)).astype(BF16)
        k_upd = (k * jnp.exp(btot - b)).astype(BF16)
        k_end_t = k_end.T
        k_dia_t = (k * jnp.exp(jnp.minimum(start_full - b, HGRN_EXP_CLAMP))).T.astype(BF16)
        k_off_t = [jnp.where(sub_t == j, k_end_t, 0.0).astype(BF16) for j in range(n_sub - 1)]
        dec8_t = jnp.broadcast_to(jnp.exp(btot), (SUBLANES, D_B)).T
        keep, keep_d = (upper2, upper_sub2) if rev else (lower2, lower_sub2)
        two = lambda a: jnp.concatenate([a, a], axis=0)
        for p in range(N_HEADS_B // 2):
            ps = slice(2 * p * DK_B, (2 * p + 2) * DK_B)
            qc = two(jnp.concatenate([q_off[j][:, ps] for j in range(n_sub - 1)], axis=1)) * pair_off
            kc = jnp.concatenate([k_off_t[j][ps, :] for j in range(n_sub - 1)], axis=0)
            yield
            sc = jnp.dot(qc, kc, preferred_element_type=F32)
            sc_d = jnp.dot(two(q_dia[:, ps]) * pair_cols, k_dia_t[ps, :], preferred_element_type=F32)
            u2 = lax.dot_general(k_upd[:, ps], v[:, ps], (((0,), (0,)), ((), ())),
                                 preferred_element_type=F32)
            yield
            sc = (jnp.where(keep, sc, 0.0) + jnp.where(keep_d, sc_d, 0.0)).astype(BF16)
            o2 = jnp.dot(sc, v[:, ps], preferred_element_type=F32)
            for i in range(2):
                h = 2 * p + i
                hs = slice(h * DK_B, (h + 1) * DK_B)
                blk = slice(i * DK_B, (i + 1) * DK_B)
                stores.append((o_ref, (0, rows, hs), o2[i * c_len:(i + 1) * c_len, blk]))
                stores.append((u_ref, (c, h), u2[blk, blk]))
                stores.append((d_ref, (c, h), jnp.broadcast_to(dec8_t[hs, 0:1], (DK_B, DK_B))))
        return stores

    def carry_state(o_ref, qi_ref, u_ref, d_ref, s_ref, c):
        rows = pl.ds(pl.multiple_of(c * c_len, c_len), c_len)
        zero = jnp.zeros((DK_B, DK_B), BF16)
        stores = []
        for p in range(N_HEADS_B // 2):
            ps = slice(2 * p * DK_B, (2 * p + 2) * DK_B)
            sa, sb = s_ref[2 * p], s_ref[2 * p + 1]
            s_bd = jnp.concatenate([jnp.concatenate([sa.astype(BF16), zero], axis=1),
                                    jnp.concatenate([zero, sb.astype(BF16)], axis=1)], axis=0)
            o_new = o_ref[0, rows, ps] + jnp.dot(qi_ref[rows, ps], s_bd, preferred_element_type=F32)
            stores.append((o_ref, (0, rows, ps), o_new))
            stores.append((s_ref, (2 * p,), sa * d_ref[c, 2 * p] + u_ref[c, 2 * p]))
            stores.append((s_ref, (2 * p + 1,), sb * d_ref[c, 2 * p + 1] + u_ref[c, 2 * p + 1]))
        return stores

    def commit(stores):
        for ref, idx, val in stores:
            ref[idx] = val

    def intra_body(trip, carry):
        gens = []
        for i in range(HGRN_CHUNKS_PER_TRIP):
            c = trip * HGRN_CHUNKS_PER_TRIP + i
            gens.append(intra(qf_ref, lf_ref, vf_ref, of_ref, qif_ref, uf_ref, df_ref, c, False))
            gens.append(intra(qr_ref, lr_ref, vr_ref, or_ref, qir_ref, ur_ref, dr_ref, c, True))
        commit(sum(_lockstep(*gens), []))
        return carry

    def state_body(c, carry):
        commit(carry_state(of_ref, qif_ref, uf_ref, df_ref, sf_ref, c)
               + carry_state(or_ref, qir_ref, ur_ref, dr_ref, sr_ref, nc - 1 - c))
        return carry

    lax.fori_loop(0, nc // HGRN_CHUNKS_PER_TRIP, intra_body, 0)
    lax.fori_loop(0, nc, state_body, 0)


def _hgrn(qb, lff, lfb, ib):
    b, s, _ = qb.shape
    ts = min(TS_HGRN, s)
    nt = s // ts
    nc = ts // HGRN_CHUNK
    fwd = lambda bi, j: (bi, j, 0)
    rev = lambda bi, j: (bi, nt - 1 - j, 0)
    blk = (1, ts, D_B)
    return pl.pallas_call(
        functools.partial(_hgrn_kernel, ts=ts),
        out_shape=(jax.ShapeDtypeStruct((b, s, D_B), F32), jax.ShapeDtypeStruct((b, s, D_B), F32)),
        grid=(b, nt),
        in_specs=[pl.BlockSpec(blk, fwd), pl.BlockSpec(blk, fwd), pl.BlockSpec(blk, fwd),
                  pl.BlockSpec(blk, rev), pl.BlockSpec(blk, rev), pl.BlockSpec(blk, rev)],
        out_specs=(pl.BlockSpec(blk, fwd), pl.BlockSpec(blk, rev)),
        scratch_shapes=[pltpu.VMEM((N_HEADS_B, DK_B, DK_B), F32),
                        pltpu.VMEM((N_HEADS_B, DK_B, DK_B), F32),
                        pltpu.VMEM((ts, D_B), BF16),
                        pltpu.VMEM((ts, D_B), BF16),
                        pltpu.VMEM((nc, N_HEADS_B, DK_B, DK_B), F32),
                        pltpu.VMEM((nc, N_HEADS_B, DK_B, DK_B), F32),
                        pltpu.VMEM((nc, N_HEADS_B, DK_B, DK_B), F32),
                        pltpu.VMEM((nc, N_HEADS_B, DK_B, DK_B), F32)],
        compiler_params=_cparams(("parallel", "arbitrary")),
        name="hgrn",
    )(qb, lff, ib, qb, lfb, ib)


def _mix_kernel(x_ref, o1_ref, o2_ref, o3_ref, l1_ref, l2_ref, l3_ref, of_ref, ob_ref, og_ref,
                gt_ref, wa_ref, wb_ref, wo_ref, ogain_ref, nmoe_ref, wrh_ref, wrl_ref,
                br_ref, cnt_ref,
                h_ref, xt_ref, tw_ref, ps_ref, ct_ref, rt_ref, cnt_out_ref, run_ref, so_ref, *, tm):
    i = pl.program_id(0)

    @pl.when(i == 0)
    def _():
        run_ref[...] = cnt_ref[...]

    def token_major(src_ref, scr_ref, dil):
        if dil == 1:
            return src_ref[0].astype(F32)
        n_chunk = scr_ref.shape[0]
        for r in range(dil):
            blk = src_ref[r].astype(F32)
            for c in range(n_chunk):
                scr_ref[c, pl.ds(r, tm // dil, stride=dil), :] = blk[:, c * LANES:(c + 1) * LANES]
        return jnp.concatenate([scr_ref[c] for c in range(n_chunk)], axis=-1)

    dils = [dil for _, dil in ATTN_GROUPS]
    l1, l2, l3 = [token_major(r, so_ref, d) for r, d in zip((l1_ref, l2_ref, l3_ref), dils)]
    mx = jnp.maximum(jnp.maximum(l1, l2), l3)
    e1, e2, e3 = jnp.exp(l1 - mx), jnp.exp(l2 - mx), jnp.exp(l3 - mx)
    attn = (e1 * token_major(o1_ref, so_ref, dils[0]) + e2 * token_major(o2_ref, so_ref, dils[1])
            + e3 * token_major(o3_ref, so_ref, dils[2])) / (e1 + e2 + e3)

    o = of_ref[...] + ob_ref[...]
    parts = []
    for h in range(N_HEADS_B):
        oh = o[:, h * DK_B:(h + 1) * DK_B]
        ms = jnp.mean(oh * oh, axis=-1, keepdims=True)
        parts.append(oh * lax.rsqrt(ms + EPS))
    hg = jnp.concatenate(parts, axis=-1) * ogain_ref[...] * og_ref[...].astype(F32)

    attn_bf, hg_bf = attn.astype(BF16), hg.astype(BF16)
    n_part = MIX_ROW_PARTS
    tp = tm // n_part

    def token_chain(part):
        rs = slice(part * tp, (part + 1) * tp)
        pa = jnp.dot(attn_bf[rs], wa_ref[...], preferred_element_type=F32)
        pb = jnp.dot(hg_bf[rs], wb_ref[...], preferred_element_type=F32)
        yield
        mixed = (gt_ref[rs, :D_MODEL].astype(F32) * pa + gt_ref[rs, D_MODEL:].astype(F32) * pb)
        h = x_ref[rs, :] + jnp.dot(mixed.astype(BF16), wo_ref[...], preferred_element_type=F32)
        yield
        h_ref[rs, :] = h
        ms = jnp.mean(h * h, axis=-1, keepdims=True)
        hn = h * lax.rsqrt(ms + EPS) * nmoe_ref[...]
        hi, lo = _split_bf16(hn)
        lg = (jnp.dot(hi, wrh_ref[...], preferred_element_type=F32)
              + jnp.dot(lo, wrh_ref[...], preferred_element_type=F32)
              + jnp.dot(hi, wrl_ref[...], preferred_element_type=F32)) + br_ref[...]
        yield
        lane_p = lax.broadcasted_iota(jnp.int32, (tp, LANES), 1)
        vals, idxs = [], []
        onehot = jnp.zeros((tp, LANES), F32)
        for _ in range(TOP_K):
            m = jnp.max(lg, axis=-1, keepdims=True)
            idx = jnp.min(jnp.where(lg == m, lane_p, LANES), axis=-1, keepdims=True)
            sel = lane_p == idx
            onehot = jnp.where(sel, 1.0, onehot)
            lg = jnp.where(sel, NEG * 2, lg)
            vals.append(m)
            idxs.append(idx)
        exps = [jnp.exp(v - vals[0]) for v in vals]
        inv = 1.0 / (exps[0] + exps[1] + exps[2] + exps[3])
        return hi, onehot, idxs, [e * inv for e in exps]

    parts = _lockstep(*[token_chain(p) for p in range(n_part)])
    rows_of = lambda pick: jnp.concatenate([pick(p) for p in parts], axis=0)
    hi = rows_of(lambda p: p[0])
    onehot = rows_of(lambda p: p[1])
    idxs = [rows_of(lambda p, k=k: p[2][k]) for k in range(TOP_K)]
    gates = [rows_of(lambda p, k=k: p[3][k]) for k in range(TOP_K)]
    lane = lax.broadcasted_iota(jnp.int32, (tm, LANES), 1)

    r_i = lax.broadcasted_iota(jnp.int32, (tm, tm), 0)
    c_i = lax.broadcasted_iota(jnp.int32, (tm, tm), 1)
    tri = jnp.where(r_i > c_i, 1.0, 0.0).astype(BF16)
    local = jnp.dot(tri, onehot.astype(BF16), preferred_element_type=F32)
    cnt_tile = jnp.sum(onehot, axis=0, keepdims=True)
    e_r = lax.broadcasted_iota(jnp.int32, (LANES, LANES), 0)
    e_c = lax.broadcasted_iota(jnp.int32, (LANES, LANES), 1)
    before_e = jnp.where(e_r < e_c, 1.0, 0.0).astype(BF16)
    off = jnp.dot(jnp.broadcast_to(cnt_tile, (8, LANES)).astype(BF16), before_e,
                  preferred_element_type=F32)[0:1]
    slot = off + local
    tw = jnp.zeros((tm, LANES), F32)
    ps = jnp.full((tm, LANES), -1.0, F32)
    for k in range(TOP_K):
        slot_k = jnp.sum(jnp.where(lane == idxs[k], slot, 0.0), axis=-1, keepdims=True)
        tw = jnp.where(lane == k, gates[k], tw)
        ps = jnp.where(lane == k, slot_k, ps)
    tw_ref[...] = tw
    ps_ref[...] = ps.astype(jnp.int32)

    ps_t = ps.T
    n_out = tm * TOP_K // n_part
    for part in range(n_part):
        row_id = (lax.broadcasted_iota(jnp.int32, (n_out, tm), 0) + part * n_out).astype(F32)
        perm = jnp.zeros((n_out, tm), F32)
        for k in range(TOP_K):
            perm = jnp.where(row_id == ps_t[k:k + 1, :], 1.0, perm)
        rows = jnp.dot(perm.astype(BF16), hi, preferred_element_type=F32)
        for c in range(D_MODEL // LANES):
            xt_ref[pl.ds(part * n_out * ROW_SUB + c, n_out, stride=ROW_SUB), :] = (
                rows[:, c * LANES:(c + 1) * LANES])

    ct_ref[0] = cnt_tile
    rt_ref[0] = run_ref[...]
    run_new = run_ref[...] + cnt_tile
    run_ref[...] = run_new
    cnt_out_ref[...] = run_new


def _mix(x2, s, o1, o2, o3, l1, l2, l3, of, ob, og, gt, wa, wb, wo, ogain, nmoe, wrh, wrl, br, cnt):
    t = x2.shape[0]
    tm = TM_MIX
    tps = s // tm
    row = lambda i: (i, 0)
    const = lambda i: (0, 0)
    seq = lambda i: (i // tps, 0, i % tps, 0)
    rb = lambda w: pl.BlockSpec((tm, w), row)
    cb = lambda a: pl.BlockSpec(a.shape, const)
    gb = lambda a: pl.BlockSpec((None, a.shape[1], tm // a.shape[1], a.shape[3]), seq)
    return pl.pallas_call(
        functools.partial(_mix_kernel, tm=tm),
        out_shape=(jax.ShapeDtypeStruct((t, D_MODEL), F32),
                   jax.ShapeDtypeStruct((t * TOP_K * ROW_SUB, LANES), F32),
                   jax.ShapeDtypeStruct((t, LANES), F32),
                   jax.ShapeDtypeStruct((t, LANES), jnp.int32),
                   jax.ShapeDtypeStruct((t // tm, 1, LANES), F32),
                   jax.ShapeDtypeStruct((t // tm, 1, LANES), F32),
                   jax.ShapeDtypeStruct((1, LANES), F32)),
        grid=(t // tm,),
        in_specs=[rb(D_MODEL), gb(o1), gb(o2), gb(o3), gb(l1), gb(l2), gb(l3),
                  rb(D_B), rb(D_B), rb(D_B), rb(2 * D_MODEL),
                  cb(wa), cb(wb), cb(wo), cb(ogain), cb(nmoe), cb(wrh), cb(wrl), cb(br), cb(cnt)],
        out_specs=(rb(D_MODEL), pl.BlockSpec((tm * TOP_K * ROW_SUB, LANES), row), rb(LANES), rb(LANES),
                   pl.BlockSpec((1, 1, LANES), lambda i: (i, 0, 0)),
                   pl.BlockSpec((1, 1, LANES), lambda i: (i, 0, 0)),
                   pl.BlockSpec((1, LANES), const)),
        scratch_shapes=[pltpu.VMEM((1, LANES), F32), pltpu.VMEM((GROUP_W // LANES, tm, LANES), F32)],
        compiler_params=_cparams(("arbitrary",)),
        name="mix",
    )(x2, o1, o2, o3, l1, l2, l3, of, ob, og, gt, wa, wb, wo, ogain, nmoe, wrh, wrl, br, cnt)


def _start_tile_segments(tile, off_tbl, row_tbl, len_tbl, make_piece):
    def segment(e, carry):
        sidx = tile * N_EXPERTS + e
        t0, r0, n = off_tbl[sidx], row_tbl[sidx], len_tbl[sidx]
        n_bulk = jnp.right_shift(n, SEG_CHUNK.bit_length() - 1)

        def bulk(j, c):
            make_piece(t0 + j * SEG_CHUNK, r0 + j * SEG_CHUNK, SEG_CHUNK).start()
            return c

        lax.fori_loop(0, n_bulk, bulk, 0)
        done = n_bulk * SEG_CHUNK
        bit = SEG_CHUNK // 2
        while bit >= 1:
            has = (n & bit) != 0

            @pl.when(has)
            def _(done=done, bit=bit):
                make_piece(t0 + done, r0 + done, bit).start()

            done = done + jnp.where(has, bit, 0)
            bit //= 2
        return carry

    lax.fori_loop(0, N_EXPERTS, segment, 0)


def _row_slice(row, n_rows):
    return pl.ds(pl.multiple_of(row * ROW_SUB, ROW_SUB), n_rows * ROW_SUB)


def _dispatch_kernel(off_tbl, row_tbl, len_tbl, fill_row, fill_len, xa_ref, xb_ref, xs_ref, zero_ref, sem,
                     *, tiles_a):
    i = pl.program_id(0)
    blk_rows = xa_ref.shape[0]

    def from_tile(xt_ref):
        def piece(tile_row, expert_row, n_rows):
            return pltpu.make_async_copy(xt_ref.at[_row_slice(tile_row, n_rows)],
                                         xs_ref.at[_row_slice(expert_row, n_rows)], sem)

        _start_tile_segments(i, off_tbl, row_tbl, len_tbl, piece)
        pltpu.make_async_copy(xt_ref, xs_ref.at[pl.ds(0, blk_rows)], sem).wait()

    def zeros_to(row, n_rows):
        return pltpu.make_async_copy(zero_ref.at[pl.ds(0, n_rows * ROW_SUB)],
                                     xs_ref.at[_row_slice(row, n_rows)], sem)

    @pl.when(i == 0)
    def _():
        zero_ref[...] = jnp.zeros_like(zero_ref)

        def fill(e, total):
            r0, n = fill_row[e], fill_len[e]
            n_blk = jnp.right_shift(n, BM.bit_length() - 1)

            def blocks(j, c):
                zeros_to(r0 + j * BM, BM).start()
                return c

            lax.fori_loop(0, n_blk, blocks, 0)
            done = n_blk * BM
            bit = BM // 2
            while bit >= 1:
                has = (n & bit) != 0

                @pl.when(has)
                def _(done=done, bit=bit):
                    zeros_to(r0 + done, bit).start()

                done = done + jnp.where(has, bit, 0)
                bit //= 2
            return total + n

        total = lax.fori_loop(0, N_EXPERTS + 1, fill, 0)
        bit = 1
        while bit * ROW_SUB <= xs_ref.shape[0]:
            @pl.when((total & bit) != 0)
            def _(bit=bit):
                n = bit * ROW_SUB
                pltpu.make_async_copy(xs_ref.at[pl.ds(0, n)], xs_ref.at[pl.ds(0, n)], sem).wait()

            bit *= 2

    @pl.when(i < tiles_a)
    def _():
        from_tile(xa_ref)

    @pl.when(i >= tiles_a)
    def _():
        from_tile(xb_ref)


def _dispatch(off_tbl, row_tbl, len_tbl, fill_row, fill_len, xt_a, xt_b, n_rows):
    blk = TM_MIX * TOP_K * ROW_SUB
    tiles_a, tiles_b = xt_a.shape[0] // blk, xt_b.shape[0] // blk
    return pl.pallas_call(
        functools.partial(_dispatch_kernel, tiles_a=tiles_a),
        out_shape=jax.ShapeDtypeStruct((n_rows * ROW_SUB, LANES), F32),
        grid_spec=pltpu.PrefetchScalarGridSpec(
            num_scalar_prefetch=5,
            grid=(tiles_a + tiles_b,),
            in_specs=[pl.BlockSpec((blk, LANES), lambda i, *_: (jnp.minimum(i, tiles_a - 1), 0)),
                      pl.BlockSpec((blk, LANES), lambda i, *_: (jnp.maximum(i - tiles_a, 0), 0))],
            out_specs=pl.BlockSpec(memory_space=pl.ANY),
            scratch_shapes=[pltpu.VMEM((BM * ROW_SUB, LANES), F32), pltpu.SemaphoreType.DMA],
        ),
        compiler_params=_cparams(("arbitrary",)),
        name="dispatch",
    )(off_tbl, row_tbl, len_tbl, fill_row, fill_len, xt_a, xt_b)


def _combine_kernel(off_tbl, row_tbl, len_tbl, ps_ref, tw_ref, h_ref, ys_ref, y_ref, buf_ref, sems, *, tm):
    i = pl.program_id(0)
    n_rows = tm * TOP_K

    def start(tile):
        slot = tile % 2

        def piece(tile_row, expert_row, n):
            return pltpu.make_async_copy(ys_ref.at[_row_slice(expert_row, n)],
                                         buf_ref.at[slot, _row_slice(tile_row, n)], sems.at[slot])

        _start_tile_segments(tile, off_tbl, row_tbl, len_tbl, piece)

    @pl.when(i == 0)
    def _():
        start(i)

    @pl.when(i + 1 < pl.num_programs(0))
    def _():
        start(i + 1)

    slot = i % 2
    yt_ref = buf_ref.at[slot]
    pltpu.make_async_copy(ys_ref.at[pl.ds(0, n_rows * ROW_SUB)], yt_ref, sems.at[slot]).wait()
    ysorted = jnp.concatenate(
        [yt_ref[pl.ds(c, n_rows, stride=ROW_SUB), :] for c in range(ROW_SUB)], axis=-1).astype(BF16)
    col = lax.broadcasted_iota(jnp.int32, (tm, n_rows), 1)
    ps = ps_ref[...]
    tw = tw_ref[...]
    pw = jnp.zeros((tm, n_rows), F32)
    for k in range(TOP_K):
        pw = jnp.where(col == ps[:, k:k + 1], tw[:, k:k + 1], pw)
    hi, lo = _split_bf16(pw)
    y_ref[...] = (h_ref[...] + jnp.dot(hi, ysorted, preferred_element_type=F32)
                  + jnp.dot(lo, ysorted, preferred_element_type=F32))


def _combine(off_tbl, row_tbl, len_tbl, ps, tw, h, ys):
    t = h.shape[0]
    tm = TM_MIX
    row = lambda i, *_: (i, 0)
    return pl.pallas_call(
        functools.partial(_combine_kernel, tm=tm),
        out_shape=jax.ShapeDtypeStruct((t, D_MODEL), F32),
        grid_spec=pltpu.PrefetchScalarGridSpec(
            num_scalar_prefetch=3,
            grid=(t // tm,),
            in_specs=[pl.BlockSpec((tm, LANES), row),
                      pl.BlockSpec((tm, LANES), row),
                      pl.BlockSpec((tm, D_MODEL), row),
                      pl.BlockSpec(memory_space=pl.ANY)],
            out_specs=pl.BlockSpec((tm, D_MODEL), row),
            scratch_shapes=[pltpu.VMEM((2, tm * TOP_K * ROW_SUB, LANES), F32),
                            pltpu.SemaphoreType.DMA((2,))],
        ),
        compiler_params=_cparams(("arbitrary",)),
        name="combine",
    )(off_tbl, row_tbl, len_tbl, ps, tw, h, ys)


def _experts_kernel(be_ref, nused_ref, xs_ref, wgu_ref, bgu_ref, wd_ref, bd_ref, ys_ref, wgu_bf, wd_bf):
    i = pl.program_id(0)
    used = i < nused_ref[0]
    new_expert = (i == 0) | (be_ref[i] != be_ref[jnp.maximum(i - 1, 0)])

    @pl.when(used & new_expert)
    def _():
        wgu_bf[...] = wgu_ref[0].astype(BF16)
        wd_bf[...] = wd_ref[0].astype(BF16)

    @pl.when(used)
    def _():
        x = jnp.concatenate(
            [xs_ref[pl.ds(c, BM, stride=ROW_SUB), :] for c in range(ROW_SUB)], axis=-1).astype(BF16)
        hh = jnp.dot(x, wgu_bf[...], preferred_element_type=F32) + bgu_ref[0]
        gate = jnp.minimum(hh[:, :D_FF], SWIGLU_LIMIT)
        up = jnp.clip(hh[:, D_FF:], -SWIGLU_LIMIT, SWIGLU_LIMIT)
        glu = gate * jax.nn.sigmoid(SWIGLU_ALPHA * gate)
        act = ((up + 1.0) * glu).astype(BF16)
        y = jnp.dot(act, wd_bf[...], preferred_element_type=F32) + bd_ref[0]
        for c in range(ROW_SUB):
            ys_ref[pl.ds(c, BM, stride=ROW_SUB), :] = y[:, c * LANES:(c + 1) * LANES]

    @pl.when(i >= nused_ref[0])
    def _():
        ys_ref[...] = jnp.zeros_like(ys_ref)


def _experts(block_e, nused, xs, wgu, bgu, wd, bd):
    nb = xs.shape[0] // (BM * ROW_SUB)
    emap3 = lambda i, be, nu: (be[i], 0, 0)
    rows = pl.BlockSpec((BM * ROW_SUB, LANES), lambda i, be, nu: (i, 0))
    return pl.pallas_call(
        _experts_kernel,
        out_shape=jax.ShapeDtypeStruct(xs.shape, F32),
        grid_spec=pltpu.PrefetchScalarGridSpec(
            num_scalar_prefetch=2,
            grid=(nb,),
            in_specs=[rows,
                      pl.BlockSpec((1, D_MODEL, 2 * D_FF), emap3),
                      pl.BlockSpec((1, 1, 2 * D_FF), emap3),
                      pl.BlockSpec((1, D_FF, D_MODEL), emap3),
                      pl.BlockSpec((1, 1, D_MODEL), emap3)],
            out_specs=rows,
            scratch_shapes=[pltpu.VMEM((D_MODEL, 2 * D_FF), BF16), pltpu.VMEM((D_FF, D_MODEL), BF16)],
        ),
        compiler_params=_cparams(("arbitrary",)),
        name="experts",
    )(block_e, nused, xs, wgu, bgu, wd, bd)


def _head_indicator(n_cols, head_dim):
    e = np.zeros((n_cols, LANES), np.float32)
    e[np.arange(n_cols), np.arange(n_cols) // head_dim] = 1.0
    return e


def _mixer(x, prm):
    b, s, d = x.shape
    t = b * s
    x2 = x.reshape(t, d)
    res = _inproj(x2, b, s, prm["norm_mix"], prm["w_all"], prm["q_gain"], prm["k_gain"], prm["e_in"],
                  prm["lb_f"], prm["lb_b"])
    n_grp = len(ATTN_GROUPS)
    qs, ks, vs = res[:n_grp], res[n_grp:2 * n_grp], res[2 * n_grp:3 * n_grp]
    qb, lff, lfb, ib, og, gt = res[3 * n_grp:]
    r3 = lambda a: a.reshape(b, s, a.shape[-1])
    outs, lses = [], []
    for g in range(n_grp):
        o, lse = _attention_group(qs[g], ks[g], vs[g], g)
        outs.append(o)
        lses.append(lse)
    of, ob = _hgrn(r3(qb), r3(lff), r3(lfb), r3(ib))
    return x2, outs, lses, of.reshape(t, D_B), ob.reshape(t, D_B), og, gt


def kernel(x_prompt, x_sample, norm_mix, w_in, q_gain, k_gain, hgrn_lb, hgrn_o_gain, w_gate, w_proj_a,
           w_proj_b, w_out, norm_moe, w_router, b_router, w_gu, b_gu, w_down, b_down):
    l = 0
    lb = jnp.cumsum(jax.nn.softmax(hgrn_lb.astype(F32), axis=1), axis=1)
    wr = jnp.zeros((D_MODEL, LANES), F32).at[:, :N_EXPERTS].set(w_router[l])
    wr_hi = wr.astype(BF16)
    prm = {
        "norm_mix": norm_mix[l].reshape(1, D_MODEL),
        "w_all": jnp.concatenate([w_in[l], w_gate[l]], axis=1).astype(BF16),
        "q_gain": q_gain[l].reshape(1, W_A) * (HEAD_DIM_A ** -0.5),
        "k_gain": k_gain[l].reshape(1, W_A),
        "e_in": jnp.asarray(_head_indicator(PIECE, HEAD_DIM_A) @ _head_indicator(PIECE, HEAD_DIM_A).T, BF16),
        "lb_f": lb[0, l].reshape(1, D_B),
        "lb_b": lb[1, l].reshape(1, D_B),
    }
    wa, wb, wo = w_proj_a[l].astype(BF16), w_proj_b[l].astype(BF16), w_out[l].astype(BF16)
    ogain = hgrn_o_gain[l].reshape(1, D_B)
    nmoe = norm_moe[l].reshape(1, D_MODEL)
    wr_lo = (wr - wr_hi.astype(F32)).astype(BF16)
    br = jnp.full((1, LANES), NEG, F32).at[0, :N_EXPERTS].set(b_router[l])

    cnt = jnp.zeros((1, LANES), F32)
    per_batch = []
    for x in (x_prompt, x_sample):
        x2, outs, lses, of, ob, og, gt = _mixer(x, prm)
        h, xt, tw, ps, ct, rt, cnt = _mix(x2, x.shape[1], outs[0], outs[1], outs[2], lses[0], lses[1], lses[2],
                                          of, ob, og, gt, wa, wb, wo, ogain, nmoe, wr_hi, wr_lo, br, cnt)
        per_batch.append((x.shape, h, xt, tw, ps, ct, rt))

    n_tok = sum(pb[1].shape[0] for pb in per_batch)
    sizes = cnt[0, :N_EXPERTS].astype(jnp.int32)
    pad_sizes = (sizes + BM - 1) // BM * BM
    pad_ends = jnp.cumsum(pad_sizes)
    pad_starts = pad_ends - pad_sizes
    nb = (n_tok * TOP_K) // BM + N_EXPERTS
    block_start = jnp.arange(nb, dtype=jnp.int32) * BM
    block_e = jnp.minimum(jnp.sum(pad_ends[None, :] <= block_start[:, None], axis=1),
                          N_EXPERTS - 1).astype(jnp.int32)
    nused = (pad_ends[-1:] // BM).astype(jnp.int32)

    tables = []
    for _, _, _, _, _, ct, rt in per_batch:
        cnt_te = ct[:, 0, :N_EXPERTS].astype(jnp.int32)
        tile_off = jnp.cumsum(cnt_te, axis=1) - cnt_te
        expert_row = pad_starts[None, :] + rt[:, 0, :N_EXPERTS].astype(jnp.int32)
        tables.append((tile_off.reshape(-1), expert_row.reshape(-1), cnt_te.reshape(-1)))

    fill_row = jnp.concatenate([pad_starts + sizes, pad_ends[-1:]]).astype(jnp.int32)
    fill_len = jnp.concatenate([pad_sizes - sizes, nb * BM - pad_ends[-1:]]).astype(jnp.int32)
    both = [jnp.concatenate(cols) for cols in zip(*tables)]
    xs = _dispatch(*both, fill_row, fill_len, per_batch[0][2], per_batch[1][2], nb * BM)
    ys = _experts(block_e, nused, xs, w_gu[l], b_gu[l].reshape(N_EXPERTS, 1, 2 * D_FF),
                  w_down[l], b_down[l].reshape(N_EXPERTS, 1, D_MODEL))
    results = []
    for (shape, h, _, tw, ps, _, _), tbl in zip(per_batch, tables):
        results.append(_combine(*tbl, ps, tw, h, ys).reshape(shape))
    return tuple(results)
```

```python
import functools
import math

import jax
import jax.numpy as jnp
import numpy as np
from jax import lax
from jax.experimental import pallas as pl
from jax.experimental.pallas import tpu as pltpu

F32 = jnp.float32
BF16 = jnp.bfloat16

D_MODEL = 1024
ATTN_GROUPS = ((128, 1), (512, 4), (2048, 16))
HEADS_PER_GROUP = 8
N_HEADS_A = 24
HEAD_DIM_A = 64
W_A = N_HEADS_A * HEAD_DIM_A
GROUP_W = HEADS_PER_GROUP * HEAD_DIM_A
N_SIDE = 64
N_HEADS_B = 4
DK_B = 128
HGRN_CHUNK = 64
HGRN_SUB = 16
HGRN_EXP_CLAMP = 80.0
HGRN_CHUNKS_PER_TRIP = 2
D_B = N_HEADS_B * DK_B
D_IN = 3 * W_A + 5 * D_B
N_EXPERTS = 32
TOP_K = 4
D_FF = 1024
SWIGLU_LIMIT = 7.0
SWIGLU_ALPHA = 1.702
EPS = 1e-6
NEG = -1e30

LANES = 128
VMEM_LIMIT = 56 * 1024 * 1024

TM_IN = 256
PIECE = 512
TQ = 2048
SQ = 128
ATTN_PAIRS_IN_FLIGHT = 1
TS_HGRN = 512
TM_MIX = 256
MIX_ROW_PARTS = 1
BM = 512
SUBLANES = 8
ROW_SUB = D_MODEL // LANES
assert ROW_SUB == SUBLANES
DISPATCH_SLOTS = 3
SEG_CHUNK = 16


def _cparams(sem):
    return pltpu.CompilerParams(dimension_semantics=sem, vmem_limit_bytes=VMEM_LIMIT)


def _lockstep(*gens):
    out = [None] * len(gens)
    live = list(range(len(gens)))
    while live:
        for g in list(live):
            try:
                next(gens[g])
            except StopIteration as done:
                out[g] = done.value
                live.remove(g)
    return out


def _split_bf16(x):
    hi = x.astype(BF16)
    lo = (x - hi.astype(F32)).astype(BF16)
    return hi, lo


def _inproj_kernel(x_ref, nw_ref, w_ref, qg_ref, kg_ref, e_ref, lbf_ref, lbb_ref,
                   q0_ref, q1_ref, q2_ref, k0_ref, k1_ref, k2_ref, v0_ref, v1_ref, v2_ref,
                   qb_ref, lff_ref, lfb_ref, ib_ref, og_ref, gt_ref, scr_ref, *, tm):
    q_refs, k_refs, v_refs = (q0_ref, q1_ref, q2_ref), (k0_ref, k1_ref, k2_ref), (v0_ref, v1_ref, v2_ref)
    x = x_ref[...]
    ms = jnp.mean(x * x, axis=-1, keepdims=True)
    xn = (x * lax.rsqrt(ms + EPS) * nw_ref[...]).astype(BF16)

    def proj(col):
        return jnp.dot(xn, w_ref[:, col:col + PIECE], preferred_element_type=F32)

    def normed(dst_ref, col, gain, dil):
        y = proj(col)
        yield
        ss = jnp.dot((y * y).astype(BF16), e_ref[...], preferred_element_type=F32)
        yield
        store_group(dst_ref, y * lax.rsqrt(ss * (1.0 / HEAD_DIM_A) + EPS) * gain, dil)

    def plain(dst_ref, col, dil):
        y = proj(col)
        yield
        yield
        store_group(dst_ref, y, dil)

    def store_group(dst_ref, y, dil):
        if dil == 1:
            dst_ref[0, 0] = y.astype(BF16)
            return
        for c in range(GROUP_W // LANES):
            scr_ref[c] = y[:, c * LANES:(c + 1) * LANES]
        for r in range(dil):
            rows = pl.ds(r, tm // dil, stride=dil)
            dst_ref[0, r] = jnp.concatenate(
                [scr_ref[c, rows, :] for c in range(GROUP_W // LANES)], axis=-1).astype(BF16)

    for g, (_, dil) in enumerate(ATTN_GROUPS):
        c = g * GROUP_W
        _lockstep(normed(q_refs[g], c, qg_ref[:, c:c + GROUP_W], dil),
                  normed(k_refs[g], W_A + c, kg_ref[:, c:c + GROUP_W], dil),
                  plain(v_refs[g], 2 * W_A + c, dil))
    base = 3 * W_A
    qb = proj(base)
    qb_ref[...] = (qb * jax.nn.sigmoid(qb) * (DK_B ** -0.5)).astype(BF16)
    for dst, lb_ref, off in ((lff_ref, lbf_ref, D_B), (lfb_ref, lbb_ref, 2 * D_B)):
        lb = lb_ref[...]
        f = lb + (1.0 - lb) * jax.nn.sigmoid(proj(base + off))
        dst[...] = jnp.log(f)
    ib_ref[...] = proj(base + 3 * D_B).astype(BF16)
    og = proj(base + 4 * D_B)
    og_ref[...] = (og * jax.nn.sigmoid(og)).astype(BF16)
    for p in range(2 * D_MODEL // PIECE):
        c = p * PIECE
        gt_ref[:, c:c + PIECE] = jax.nn.sigmoid(proj(D_IN + c)).astype(BF16)


def _inproj(x2, b, s, nw, w_all, qg, kg, e_mat, lbf, lbb):
    t = x2.shape[0]
    tm = TM_IN
    tps = s // tm
    n_all = w_all.shape[1]
    row = lambda i: (i, 0)
    const = lambda i: (0, 0)
    seq = lambda i: (i // tps, 0, i % tps, 0)
    widths = (D_B, D_B, D_B, D_B, D_B, 2 * D_MODEL)
    dtypes = (BF16, F32, F32, BF16, BF16, BF16)
    grp_shapes = [jax.ShapeDtypeStruct((b, dil, s // dil, GROUP_W), BF16) for _, dil in ATTN_GROUPS] * 3
    grp_specs = [pl.BlockSpec((1, dil, tm // dil, GROUP_W), seq) for _, dil in ATTN_GROUPS] * 3
    return pl.pallas_call(
        functools.partial(_inproj_kernel, tm=tm),
        out_shape=tuple(grp_shapes) + tuple(jax.ShapeDtypeStruct((t, w), dt) for w, dt in zip(widths, dtypes)),
        grid=(t // tm,),
        in_specs=[
            pl.BlockSpec((tm, D_MODEL), row),
            pl.BlockSpec((1, D_MODEL), const),
            pl.BlockSpec((D_MODEL, n_all), const),
            pl.BlockSpec((1, W_A), const),
            pl.BlockSpec((1, W_A), const),
            pl.BlockSpec((PIECE, PIECE), const),
            pl.BlockSpec((1, D_B), const),
            pl.BlockSpec((1, D_B), const),
        ],
        out_specs=tuple(grp_specs) + tuple(pl.BlockSpec((tm, w), row) for w in widths),
        scratch_shapes=[pltpu.VMEM((GROUP_W // LANES, tm, LANES), F32)],
        compiler_params=_cparams(("parallel",)),
        name="inproj",
    )(x2, nw, w_all, qg, kg, e_mat, lbf, lbb)


def _attn_kernel(q_ref, kp_ref, kc_ref, kn_ref, vp_ref, vc_ref, vn_ref, bias_ref,
                 o_ref, lse_ref, *, tq, sub_len):
    i = pl.program_id(2)
    sq = bias_ref.shape[1] // 2
    nk = sq + 2 * N_SIDE
    lane = lax.broadcasted_iota(jnp.int32, (sq, LANES), 1)
    low = lane < HEAD_DIM_A
    ones = jnp.ones((nk, LANES), BF16)
    zero = jnp.zeros((sq, LANES), BF16)

    def pair(res, kk, vv, j, pr, colbias):
        cols = slice(pr * LANES, (pr + 1) * LANES)
        q2 = q_ref[res, j * sq:(j + 1) * sq, cols]
        k2 = kk[j * sq:j * sq + nk, cols]
        v2 = jnp.concatenate([vv[j * sq:j * sq + nk, cols], ones], axis=1)
        q_st = jnp.concatenate([jnp.where(low, q2, zero), jnp.where(low, zero, q2)], axis=0)
        s = lax.dot_general(q_st, k2, (((1,), (1,)), ((), ())), preferred_element_type=F32)
        yield
        s = s + bias_ref[pr] + colbias
        m = jnp.max(s, axis=-1, keepdims=True)
        p = jnp.exp(s - m).astype(BF16)
        r = jnp.dot(p, v2, preferred_element_type=F32)
        yield
        o2 = jnp.where(low, r[:sq, :LANES] / r[:sq, LANES:], r[sq:, :LANES] / r[sq:, LANES:])
        o_ref[res, j * sq:(j + 1) * sq, cols] = o2.astype(o_ref.dtype)
        lse = m + jnp.log(r[:, LANES:LANES + 1])
        lse_ref[res, j * sq:(j + 1) * sq, cols] = jnp.where(low, lse[:sq], lse[sq:])

    n_pairs = HEADS_PER_GROUP // 2
    for res in range(q_ref.shape[0]):
        kk = jnp.concatenate([kp_ref[res], kc_ref[res], kn_ref[res]], axis=0)
        vv = jnp.concatenate([vp_ref[res], vc_ref[res], vn_ref[res]], axis=0)
        for j in range(tq // sq):
            kpos = i * tq + j * sq - N_SIDE + lax.broadcasted_iota(jnp.int32, (1, nk), 1)
            colbias = jnp.where((kpos >= 0) & (kpos < sub_len), 0.0, NEG).astype(F32)
            for pr in range(0, n_pairs, ATTN_PAIRS_IN_FLIGHT):
                _lockstep(*[pair(res, kk, vv, j, pr + d, colbias) for d in range(ATTN_PAIRS_IN_FLIGHT)])


def _attn_bias(sq, dil, slopes):
    nk = sq + 2 * N_SIDE
    rel = np.arange(nk)[None, :] - N_SIDE - np.arange(sq)[:, None]
    band = np.abs(rel) <= N_SIDE
    alibi = -slopes[:, None, None] * (dil * np.abs(rel)).astype(np.float32)[None]
    bias = np.where(band[None], alibi, NEG).astype(np.float32)
    return jnp.asarray(bias.reshape(HEADS_PER_GROUP // 2, 2 * sq, nk))


def _attention_group(q, k, v, g):
    b, dil, sub_len, _ = q.shape
    tq = min(TQ, sub_len)
    sq = min(SQ, sub_len)
    hb = tq // N_SIDE
    n_halo = sub_len // N_SIDE
    slopes = (2.0 ** (-8.0 * (np.arange(N_HEADS_A) + 1) / N_HEADS_A)).astype(np.float32)
    bias = _attn_bias(sq, dil, slopes[g * HEADS_PER_GROUP:(g + 1) * HEADS_PER_GROUP])
    cur = lambda bi, r, i: (bi, r, i, 0)
    prev = lambda bi, r, i: (bi, r, jnp.maximum(i * hb - 1, 0), 0)
    nxt = lambda bi, r, i: (bi, r, jnp.minimum((i + 1) * hb, n_halo - 1), 0)
    n_res = max(1, min(dil, TQ // tq))
    blk_q = (None, n_res, tq, GROUP_W)
    blk_h = (None, n_res, N_SIDE, GROUP_W)
    return pl.pallas_call(
        functools.partial(_attn_kernel, tq=tq, sub_len=sub_len),
        out_shape=(jax.ShapeDtypeStruct((b, dil, sub_len, GROUP_W), BF16),
                   jax.ShapeDtypeStruct((b, dil, sub_len, GROUP_W), F32)),
        grid=(b, dil // n_res, sub_len // tq),
        in_specs=[
            pl.BlockSpec(blk_q, cur),
            pl.BlockSpec(blk_h, prev), pl.BlockSpec(blk_q, cur), pl.BlockSpec(blk_h, nxt),
            pl.BlockSpec(blk_h, prev), pl.BlockSpec(blk_q, cur), pl.BlockSpec(blk_h, nxt),
            pl.BlockSpec(bias.shape, lambda bi, r, i: (0, 0, 0)),
        ],
        out_specs=(pl.BlockSpec(blk_q, cur), pl.BlockSpec(blk_q, cur)),
        compiler_params=_cparams(("parallel", "parallel", "parallel")),
        name=f"attn_d{dil}",
    )(q, k, k, k, v, v, v, bias)


def _hgrn_kernel(qf_ref, lf_ref, vf_ref, qr_ref, lr_ref, vr_ref, of_ref, or_ref,
                 sf_ref, sr_ref, qif_ref, qir_ref, uf_ref, ur_ref, df_ref, dr_ref, *, ts):
    c_len = HGRN_CHUNK
    nc = ts // c_len

    @pl.when(pl.program_id(1) == 0)
    def _():
        sf_ref[...] = jnp.zeros_like(sf_ref)
        sr_ref[...] = jnp.zeros_like(sr_ref)

    r_i = lax.broadcasted_iota(jnp.int32, (c_len, c_len), 0)
    c_i = lax.broadcasted_iota(jnp.int32, (c_len, c_len), 1)
    lower = r_i >= c_i
    upper = r_i <= c_i
    tri_f = jnp.where(lower, 1.0, 0.0).astype(BF16)
    tri_r = jnp.where(upper, 1.0, 0.0).astype(BF16)

    def cumsum(tri, x):
        hi = x.astype(BF16)
        r1 = x - hi.astype(F32)
        mid = r1.astype(BF16)
        lo = (r1 - mid.astype(F32)).astype(BF16)
        d = lambda a: jnp.dot(tri, a, preferred_element_type=F32)
        return d(hi) + d(mid) + d(lo)

    n_sub = c_len // HGRN_SUB
    shift = HGRN_SUB.bit_length() - 1
    sub_f = jnp.right_shift(lax.broadcasted_iota(jnp.int32, (c_len, D_B), 0), shift)
    sub_tf = jnp.right_shift(lax.broadcasted_iota(jnp.int32, (D_B, c_len), 1), shift)
    t_i = lax.broadcasted_iota(jnp.int32, (2 * c_len, c_len), 0) & (c_len - 1)
    s_i = lax.broadcasted_iota(jnp.int32, (2 * c_len, c_len), 1)
    same_sub2 = jnp.right_shift(t_i, shift) == jnp.right_shift(s_i, shift)
    lower2, upper2 = t_i >= s_i, t_i <= s_i
    lower_sub2, upper_sub2 = same_sub2 & lower2, same_sub2 & upper2
    row_head = lax.broadcasted_iota(jnp.int32, (2 * c_len, 2 * DK_B), 0) // c_len
    col_head = lax.broadcasted_iota(jnp.int32, (2 * c_len, 2 * DK_B), 1) // DK_B
    pair_cols = jnp.where(row_head == col_head, 1.0, 0.0).astype(BF16)
    pair_off = jnp.concatenate([pair_cols] * (n_sub - 1), axis=1)

    def intra(q_ref, l_ref, v_ref, o_ref, qi_ref, u_ref, d_ref, c, rev):
        rows = pl.ds(pl.multiple_of(c * c_len, c_len), c_len)
        lf = l_ref[0, rows, :]
        q = q_ref[0, rows, :].astype(F32)
        v = v_ref[0, rows, :]
        k = 1.0 - jnp.exp(lf)
        b = cumsum(tri_r if rev else tri_f, lf)
        yield
        sub = (n_sub - 1 - sub_f) if rev else sub_f

        def b_at(pos):
            r = c_len - 1 - pos if rev else pos
            return b[r:r + 1, :]

        def per_sub(vals):
            vals = vals[::-1] if rev else vals
            return jnp.concatenate([jnp.broadcast_to(x, (HGRN_SUB, D_B)) for x in vals], axis=0)

        a_end = [b_at(HGRN_SUB * j + HGRN_SUB - 1) for j in range(n_sub)]
        a_start = [jnp.zeros((1, D_B), F32)] + a_end[:-1]
        btot = a_end[-1]
        end_full, start_full = per_sub(a_end), per_sub(a_start)
        stores = [(qi_ref, (rows, slice(None)), (q * jnp.exp(b)).astype(BF16))]
        q_off = [jnp.where(sub > j, q * jnp.exp(jnp.minimum(b - a_end[j], 0.0)), 0.0).astype(BF16)
                 for j in range(n_sub - 1)]
        k_end = k * jnp.exp(end_full - b)
        sub_t = (n_sub - 1 - sub_tf) if rev else sub_tf
        q_dia = (q * jnp.exp(b - start_full)).astype(BF16)
        k_upd = (k * jnp.exp(btot - b)).astype(BF16)
        k_end_t = k_end.T
        k_dia_t = (k * jnp.exp(jnp.minimum(start_full - b, HGRN_EXP_CLAMP))).T.astype(BF16)
        k_off_t = [jnp.where(sub_t == j, k_end_t, 0.0).astype(BF16) for j in range(n_sub - 1)]
        dec8_t = jnp.broadcast_to(jnp.exp(btot), (SUBLANES, D_B)).T
        keep, keep_d = (upper2, upper_sub2) if rev else (lower2, lower_sub2)
        two = lambda a: jnp.concatenate([a, a], axis=0)
        for p in range(N_HEADS_B // 2):
            ps = slice(2 * p * DK_B, (2 * p + 2) * DK_B)
            qc = two(jnp.concatenate([q_off[j][:, ps] for j in range(n_sub - 1)], axis=1)) * pair_off
            kc = jnp.concatenate([k_off_t[j][ps, :] for j in range(n_sub - 1)], axis=0)
            yield
            sc = jnp.dot(qc, kc, preferred_element_type=F32)
            sc_d = jnp.dot(two(q_dia[:, ps]) * pair_cols, k_dia_t[ps, :], preferred_element_type=F32)
            u2 = lax.dot_general(k_upd[:, ps], v[:, ps], (((0,), (0,)), ((), ())),
                                 preferred_element_type=F32)
            yield
            sc = (jnp.where(keep, sc, 0.0) + jnp.where(keep_d, sc_d, 0.0)).astype(BF16)
            o2 = jnp.dot(sc, v[:, ps], preferred_element_type=F32)
            for i in range(2):
                h = 2 * p + i
                hs = slice(h * DK_B, (h + 1) * DK_B)
                blk = slice(i * DK_B, (i + 1) * DK_B)
                stores.append((o_ref, (0, rows, hs), o2[i * c_len:(i + 1) * c_len, blk]))
                stores.append((u_ref, (c, h), u2[blk, blk]))
                stores.append((d_ref, (c, h), jnp.broadcast_to(dec8_t[hs, 0:1], (DK_B, DK_B))))
        return stores

    def carry_state(o_ref, qi_ref, u_ref, d_ref, s_ref, c):
        rows = pl.ds(pl.multiple_of(c * c_len, c_len), c_len)
        zero = jnp.zeros((DK_B, DK_B), BF16)
        stores = []
        for p in range(N_HEADS_B // 2):
            ps = slice(2 * p * DK_B, (2 * p + 2) * DK_B)
            sa, sb = s_ref[2 * p], s_ref[2 * p + 1]
            s_bd = jnp.concatenate([jnp.concatenate([sa.astype(BF16), zero], axis=1),
                                    jnp.concatenate([zero, sb.astype(BF16)], axis=1)], axis=0)
            o_new = o_ref[0, rows, ps] + jnp.dot(qi_ref[rows, ps], s_bd, preferred_element_type=F32)
            stores.append((o_ref, (0, rows, ps), o_new))
            stores.append((s_ref, (2 * p,), sa * d_ref[c, 2 * p] + u_ref[c, 2 * p]))
            stores.append((s_ref, (2 * p + 1,), sb * d_ref[c, 2 * p + 1] + u_ref[c, 2 * p + 1]))
        return stores

    def commit(stores):
        for ref, idx, val in stores:
            ref[idx] = val

    def intra_body(trip, carry):
        gens = []
        for i in range(HGRN_CHUNKS_PER_TRIP):
            c = trip * HGRN_CHUNKS_PER_TRIP + i
            gens.append(intra(qf_ref, lf_ref, vf_ref, of_ref, qif_ref, uf_ref, df_ref, c, False))
            gens.append(intra(qr_ref, lr_ref, vr_ref, or_ref, qir_ref, ur_ref, dr_ref, c, True))
        commit(sum(_lockstep(*gens), []))
        return carry

    def state_body(c, carry):
        commit(carry_state(of_ref, qif_ref, uf_ref, df_ref, sf_ref, c)
               + carry_state(or_ref, qir_ref, ur_ref, dr_ref, sr_ref, nc - 1 - c))
        return carry

    lax.fori_loop(0, nc // HGRN_CHUNKS_PER_TRIP, intra_body, 0)
    lax.fori_loop(0, nc, state_body, 0)


def _hgrn(qb, lff, lfb, ib):
    b, s, _ = qb.shape
    ts = min(TS_HGRN, s)
    nt = s // ts
    nc = ts // HGRN_CHUNK
    fwd = lambda bi, j: (bi, j, 0)
    rev = lambda bi, j: (bi, nt - 1 - j, 0)
    blk = (1, ts, D_B)
    return pl.pallas_call(
        functools.partial(_hgrn_kernel, ts=ts),
        out_shape=(jax.ShapeDtypeStruct((b, s, D_B), F32), jax.ShapeDtypeStruct((b, s, D_B), F32)),
        grid=(b, nt),
        in_specs=[pl.BlockSpec(blk, fwd), pl.BlockSpec(blk, fwd), pl.BlockSpec(blk, fwd),
                  pl.BlockSpec(blk, rev), pl.BlockSpec(blk, rev), pl.BlockSpec(blk, rev)],
        out_specs=(pl.BlockSpec(blk, fwd), pl.BlockSpec(blk, rev)),
        scratch_shapes=[pltpu.VMEM((N_HEADS_B, DK_B, DK_B), F32),
                        pltpu.VMEM((N_HEADS_B, DK_B, DK_B), F32),
                        pltpu.VMEM((ts, D_B), BF16),
                        pltpu.VMEM((ts, D_B), BF16),
                        pltpu.VMEM((nc, N_HEADS_B, DK_B, DK_B), F32),
                        pltpu.VMEM((nc, N_HEADS_B, DK_B, DK_B), F32),
                        pltpu.VMEM((nc, N_HEADS_B, DK_B, DK_B), F32),
                        pltpu.VMEM((nc, N_HEADS_B, DK_B, DK_B), F32)],
        compiler_params=_cparams(("parallel", "arbitrary")),
        name="hgrn",
    )(qb, lff, ib, qb, lfb, ib)


def _mix_kernel(x_ref, o1_ref, o2_ref, o3_ref, l1_ref, l2_ref, l3_ref, of_ref, ob_ref, og_ref,
                gt_ref, wa_ref, wb_ref, wo_ref, ogain_ref, nmoe_ref, wrh_ref, wrl_ref,
                br_ref, cnt_ref,
                h_ref, xt_ref, tw_ref, ps_ref, ct_ref, rt_ref, cnt_out_ref, run_ref, so_ref, *, tm):
    i = pl.program_id(0)

    @pl.when(i == 0)
    def _():
        run_ref[...] = cnt_ref[...]

    def token_major(src_ref, scr_ref, dil):
        if dil == 1:
            return src_ref[0].astype(F32)
        n_chunk = scr_ref.shape[0]
        for r in range(dil):
            blk = src_ref[r].astype(F32)
            for c in range(n_chunk):
                scr_ref[c, pl.ds(r, tm // dil, stride=dil), :] = blk[:, c * LANES:(c + 1) * LANES]
        return jnp.concatenate([scr_ref[c] for c in range(n_chunk)], axis=-1)

    dils = [dil for _, dil in ATTN_GROUPS]
    l1, l2, l3 = [token_major(r, so_ref, d) for r, d in zip((l1_ref, l2_ref, l3_ref), dils)]
    mx = jnp.maximum(jnp.maximum(l1, l2), l3)
    e1, e2, e3 = jnp.exp(l1 - mx), jnp.exp(l2 - mx), jnp.exp(l3 - mx)
    attn = (e1 * token_major(o1_ref, so_ref, dils[0]) + e2 * token_major(o2_ref, so_ref, dils[1])
            + e3 * token_major(o3_ref, so_ref, dils[2])) / (e1 + e2 + e3)

    o = of_ref[...] + ob_ref[...]
    parts = []
    for h in range(N_HEADS_B):
        oh = o[:, h * DK_B:(h + 1) * DK_B]
        ms = jnp.mean(oh * oh, axis=-1, keepdims=True)
        parts.append(oh * lax.rsqrt(ms + EPS))
    hg = jnp.concatenate(parts, axis=-1) * ogain_ref[...] * og_ref[...].astype(F32)

    attn_bf, hg_bf = attn.astype(BF16), hg.astype(BF16)
    n_part = MIX_ROW_PARTS
    tp = tm // n_part

    def token_chain(part):
        rs = slice(part * tp, (part + 1) * tp)
        pa = jnp.dot(attn_bf[rs], wa_ref[...], preferred_element_type=F32)
        pb = jnp.dot(hg_bf[rs], wb_ref[...], preferred_element_type=F32)
        yield
        mixed = (gt_ref[rs, :D_MODEL].astype(F32) * pa + gt_ref[rs, D_MODEL:].astype(F32) * pb)
        h = x_ref[rs, :] + jnp.dot(mixed.astype(BF16), wo_ref[...], preferred_element_type=F32)
        yield
        h_ref[rs, :] = h
        ms = jnp.mean(h * h, axis=-1, keepdims=True)
        hn = h * lax.rsqrt(ms + EPS) * nmoe_ref[...]
        hi, lo = _split_bf16(hn)
        lg = (jnp.dot(hi, wrh_ref[...], preferred_element_type=F32)
              + jnp.dot(lo, wrh_ref[...], preferred_element_type=F32)
              + jnp.dot(hi, wrl_ref[...], preferred_element_type=F32)) + br_ref[...]
        yield
        lane_p = lax.broadcasted_iota(jnp.int32, (tp, LANES), 1)
        vals, idxs = [], []
        onehot = jnp.zeros((tp, LANES), F32)
        for _ in range(TOP_K):
            m = jnp.max(lg, axis=-1, keepdims=True)
            idx = jnp.min(jnp.where(lg == m, lane_p, LANES), axis=-1, keepdims=True)
            sel = lane_p == idx
            onehot = jnp.where(sel, 1.0, onehot)
            lg = jnp.where(sel, NEG * 2, lg)
            vals.append(m)
            idxs.append(idx)
        exps = [jnp.exp(v - vals[0]) for v in vals]
        inv = 1.0 / (exps[0] + exps[1] + exps[2] + exps[3])
        return hi, onehot, idxs, [e * inv for e in exps]

    parts = _lockstep(*[token_chain(p) for p in range(n_part)])
    rows_of = lambda pick: jnp.concatenate([pick(p) for p in parts], axis=0)
    hi = rows_of(lambda p: p[0])
    onehot = rows_of(lambda p: p[1])
    idxs = [rows_of(lambda p, k=k: p[2][k]) for k in range(TOP_K)]
    gates = [rows_of(lambda p, k=k: p[3][k]) for k in range(TOP_K)]
    lane = lax.broadcasted_iota(jnp.int32, (tm, LANES), 1)

    r_i = lax.broadcasted_iota(jnp.int32, (tm, tm), 0)
    c_i = lax.broadcasted_iota(jnp.int32, (tm, tm), 1)
    tri = jnp.where(r_i > c_i, 1.0, 0.0).astype(BF16)
    local = jnp.dot(tri, onehot.astype(BF16), preferred_element_type=F32)
    cnt_tile = jnp.sum(onehot, axis=0, keepdims=True)
    e_r = lax.broadcasted_iota(jnp.int32, (LANES, LANES), 0)
    e_c = lax.broadcasted_iota(jnp.int32, (LANES, LANES), 1)
    before_e = jnp.where(e_r < e_c, 1.0, 0.0).astype(BF16)
    off = jnp.dot(jnp.broadcast_to(cnt_tile, (8, LANES)).astype(BF16), before_e,
                  preferred_element_type=F32)[0:1]
    slot = off + local
    tw = jnp.zeros((tm, LANES), F32)
    ps = jnp.full((tm, LANES), -1.0, F32)
    for k in range(TOP_K):
        slot_k = jnp.sum(jnp.where(lane == idxs[k], slot, 0.0), axis=-1, keepdims=True)
        tw = jnp.where(lane == k, gates[k], tw)
        ps = jnp.where(lane == k, slot_k, ps)
    tw_ref[...] = tw
    ps_ref[...] = ps.astype(jnp.int32)

    ps_t = ps.T
    n_out = tm * TOP_K // n_part
    for part in range(n_part):
        row_id = (lax.broadcasted_iota(jnp.int32, (n_out, tm), 0) + part * n_out).astype(F32)
        perm = jnp.zeros((n_out, tm), F32)
        for k in range(TOP_K):
            perm = jnp.where(row_id == ps_t[k:k + 1, :], 1.0, perm)
        rows = jnp.dot(perm.astype(BF16), hi, preferred_element_type=F32)
        for c in range(D_MODEL // LANES):
            xt_ref[pl.ds(part * n_out * ROW_SUB + c, n_out, stride=ROW_SUB), :] = (
                rows[:, c * LANES:(c + 1) * LANES])

    ct_ref[0] = cnt_tile
    rt_ref[0] = run_ref[...]
    run_new = run_ref[...] + cnt_tile
    run_ref[...] = run_new
    cnt_out_ref[...] = run_new


def _mix(x2, s, o1, o2, o3, l1, l2, l3, of, ob, og, gt, wa, wb, wo, ogain, nmoe, wrh, wrl, br, cnt):
    t = x2.shape[0]
    tm = TM_MIX
    tps = s // tm
    row = lambda i: (i, 0)
    const = lambda i: (0, 0)
    seq = lambda i: (i // tps, 0, i % tps, 0)
    rb = lambda w: pl.BlockSpec((tm, w), row)
    cb = lambda a: pl.BlockSpec(a.shape, const)
    gb = lambda a: pl.BlockSpec((None, a.shape[1], tm // a.shape[1], a.shape[3]), seq)
    return pl.pallas_call(
        functools.partial(_mix_kernel, tm=tm),
        out_shape=(jax.ShapeDtypeStruct((t, D_MODEL), F32),
                   jax.ShapeDtypeStruct((t * TOP_K * ROW_SUB, LANES), F32),
                   jax.ShapeDtypeStruct((t, LANES), F32),
                   jax.ShapeDtypeStruct((t, LANES), jnp.int32),
                   jax.ShapeDtypeStruct((t // tm, 1, LANES), F32),
                   jax.ShapeDtypeStruct((t // tm, 1, LANES), F32),
                   jax.ShapeDtypeStruct((1, LANES), F32)),
        grid=(t // tm,),
        in_specs=[rb(D_MODEL), gb(o1), gb(o2), gb(o3), gb(l1), gb(l2), gb(l3),
                  rb(D_B), rb(D_B), rb(D_B), rb(2 * D_MODEL),
                  cb(wa), cb(wb), cb(wo), cb(ogain), cb(nmoe), cb(wrh), cb(wrl), cb(br), cb(cnt)],
        out_specs=(rb(D_MODEL), pl.BlockSpec((tm * TOP_K * ROW_SUB, LANES), row), rb(LANES), rb(LANES),
                   pl.BlockSpec((1, 1, LANES), lambda i: (i, 0, 0)),
                   pl.BlockSpec((1, 1, LANES), lambda i: (i, 0, 0)),
                   pl.BlockSpec((1, LANES), const)),
        scratch_shapes=[pltpu.VMEM((1, LANES), F32), pltpu.VMEM((GROUP_W // LANES, tm, LANES), F32)],
        compiler_params=_cparams(("arbitrary",)),
        name="mix",
    )(x2, o1, o2, o3, l1, l2, l3, of, ob, og, gt, wa, wb, wo, ogain, nmoe, wrh, wrl, br, cnt)


def _start_tile_segments(tile, off_tbl, row_tbl, len_tbl, make_piece):
    def segment(e, carry):
        sidx = tile * N_EXPERTS + e
        t0, r0, n = off_tbl[sidx], row_tbl[sidx], len_tbl[sidx]
        n_bulk = jnp.right_shift(n, SEG_CHUNK.bit_length() - 1)

        def bulk(j, c):
            make_piece(t0 + j * SEG_CHUNK, r0 + j * SEG_CHUNK, SEG_CHUNK).start()
            return c

        lax.fori_loop(0, n_bulk, bulk, 0)
        done = n_bulk * SEG_CHUNK
        bit = SEG_CHUNK // 2
        while bit >= 1:
            has = (n & bit) != 0

            @pl.when(has)
            def _(done=done, bit=bit):
                make_piece(t0 + done, r0 + done, bit).start()

            done = done + jnp.where(has, bit, 0)
            bit //= 2
        return carry

    lax.fori_loop(0, N_EXPERTS, segment, 0)


def _row_slice(row, n_rows):
    return pl.ds(pl.multiple_of(row * ROW_SUB, ROW_SUB), n_rows * ROW_SUB)


def _dispatch_kernel(off_tbl, row_tbl, len_tbl, fill_row, fill_len, xa_ref, xb_ref, xs_ref, buf_ref,
                     zero_ref, sems_in, sems_out, sem, *, tiles_a):
    i = pl.program_id(0)
    n_tiles = pl.num_programs(0)
    blk_rows = buf_ref.shape[1]

    def fetch(tile):
        slot = tile % DISPATCH_SLOTS

        def copy(src_ref, src_tile):
            start = pl.multiple_of(src_tile * blk_rows, blk_rows)
            pltpu.make_async_copy(src_ref.at[pl.ds(start, blk_rows)], buf_ref.at[slot], sems_in.at[slot]).start()

        @pl.when(tile < tiles_a)
        def _():
            copy(xa_ref, tile)

        @pl.when(tile >= tiles_a)
        def _():
            copy(xb_ref, tile - tiles_a)

    def scatter_done(tile):
        slot = tile % DISPATCH_SLOTS
        pltpu.make_async_copy(buf_ref.at[slot], xs_ref.at[pl.ds(0, blk_rows)], sems_out.at[slot]).wait()

    def zeros_to(row, n_rows):
        return pltpu.make_async_copy(zero_ref.at[pl.ds(0, n_rows * ROW_SUB)],
                                     xs_ref.at[_row_slice(row, n_rows)], sem)

    @pl.when(i == 0)
    def _():
        fetch(i)
        zero_ref[...] = jnp.zeros_like(zero_ref)

        def fill(e, total):
            r0, n = fill_row[e], fill_len[e]
            n_blk = jnp.right_shift(n, BM.bit_length() - 1)

            def blocks(j, c):
                zeros_to(r0 + j * BM, BM).start()
                return c

            lax.fori_loop(0, n_blk, blocks, 0)
            done = n_blk * BM
            bit = BM // 2
            while bit >= 1:
                has = (n & bit) != 0

                @pl.when(has)
                def _(done=done, bit=bit):
                    zeros_to(r0 + done, bit).start()

                done = done + jnp.where(has, bit, 0)
                bit //= 2
            return total + n

        total = lax.fori_loop(0, N_EXPERTS + 1, fill, 0)
        bit = 1
        while bit * ROW_SUB <= xs_ref.shape[0]:
            @pl.when((total & bit) != 0)
            def _(bit=bit):
                n = bit * ROW_SUB
                pltpu.make_async_copy(xs_ref.at[pl.ds(0, n)], xs_ref.at[pl.ds(0, n)], sem).wait()

            bit *= 2

    @pl.when(i + 1 < n_tiles)
    def _():
        fetch(i + 1)

    slot = i % DISPATCH_SLOTS
    pltpu.make_async_copy(xa_ref.at[pl.ds(0, blk_rows)], buf_ref.at[slot], sems_in.at[slot]).wait()

    def piece(tile_row, expert_row, n_rows):
        return pltpu.make_async_copy(buf_ref.at[slot, _row_slice(tile_row, n_rows)],
                                     xs_ref.at[_row_slice(expert_row, n_rows)], sems_out.at[slot])

    _start_tile_segments(i, off_tbl, row_tbl, len_tbl, piece)

    @pl.when(i >= 1)
    def _():
        scatter_done(i - 1)

    @pl.when(i == n_tiles - 1)
    def _():
        scatter_done(i)


def _dispatch(off_tbl, row_tbl, len_tbl, fill_row, fill_len, xt_a, xt_b, n_rows):
    blk = TM_MIX * TOP_K * ROW_SUB
    tiles_a, tiles_b = xt_a.shape[0] // blk, xt_b.shape[0] // blk
    return pl.pallas_call(
        functools.partial(_dispatch_kernel, tiles_a=tiles_a),
        out_shape=jax.ShapeDtypeStruct((n_rows * ROW_SUB, LANES), F32),
        grid_spec=pltpu.PrefetchScalarGridSpec(
            num_scalar_prefetch=5,
            grid=(tiles_a + tiles_b,),
            in_specs=[pl.BlockSpec(memory_space=pl.ANY), pl.BlockSpec(memory_space=pl.ANY)],
            out_specs=pl.BlockSpec(memory_space=pl.ANY),
            scratch_shapes=[pltpu.VMEM((DISPATCH_SLOTS, blk, LANES), F32),
                            pltpu.VMEM((BM * ROW_SUB, LANES), F32),
                            pltpu.SemaphoreType.DMA((DISPATCH_SLOTS,)),
                            pltpu.SemaphoreType.DMA((DISPATCH_SLOTS,)),
                            pltpu.SemaphoreType.DMA],
        ),
        compiler_params=_cparams(("arbitrary",)),
        name="dispatch",
    )(off_tbl, row_tbl, len_tbl, fill_row, fill_len, xt_a, xt_b)


def _combine_kernel(off_tbl, row_tbl, len_tbl, ps_ref, tw_ref, h_ref, ys_ref, y_ref, buf_ref, sems, *, tm):
    i = pl.program_id(0)
    n_rows = tm * TOP_K

    def start(tile):
        slot = tile % 2

        def piece(tile_row, expert_row, n):
            return pltpu.make_async_copy(ys_ref.at[_row_slice(expert_row, n)],
                                         buf_ref.at[slot, _row_slice(tile_row, n)], sems.at[slot])

        _start_tile_segments(tile, off_tbl, row_tbl, len_tbl, piece)

    @pl.when(i == 0)
    def _():
        start(i)

    @pl.when(i + 1 < pl.num_programs(0))
    def _():
        start(i + 1)

    slot = i % 2
    yt_ref = buf_ref.at[slot]
    pltpu.make_async_copy(ys_ref.at[pl.ds(0, n_rows * ROW_SUB)], yt_ref, sems.at[slot]).wait()
    ysorted = jnp.concatenate(
        [yt_ref[pl.ds(c, n_rows, stride=ROW_SUB), :] for c in range(ROW_SUB)], axis=-1).astype(BF16)
    col = lax.broadcasted_iota(jnp.int32, (tm, n_rows), 1)
    ps = ps_ref[...]
    tw = tw_ref[...]
    pw = jnp.zeros((tm, n_rows), F32)
    for k in range(TOP_K):
        pw = jnp.where(col == ps[:, k:k + 1], tw[:, k:k + 1], pw)
    hi, lo = _split_bf16(pw)
    y_ref[...] = (h_ref[...] + jnp.dot(hi, ysorted, preferred_element_type=F32)
                  + jnp.dot(lo, ysorted, preferred_element_type=F32))


def _combine(off_tbl, row_tbl, len_tbl, ps, tw, h, ys):
    t = h.shape[0]
    tm = TM_MIX
    row = lambda i, *_: (i, 0)
    return pl.pallas_call(
        functools.partial(_combine_kernel, tm=tm),
        out_shape=jax.ShapeDtypeStruct((t, D_MODEL), F32),
        grid_spec=pltpu.PrefetchScalarGridSpec(
            num_scalar_prefetch=3,
            grid=(t // tm,),
            in_specs=[pl.BlockSpec((tm, LANES), row),
                      pl.BlockSpec((tm, LANES), row),
                      pl.BlockSpec((tm, D_MODEL), row),
                      pl.BlockSpec(memory_space=pl.ANY)],
            out_specs=pl.BlockSpec((tm, D_MODEL), row),
            scratch_shapes=[pltpu.VMEM((2, tm * TOP_K * ROW_SUB, LANES), F32),
                            pltpu.SemaphoreType.DMA((2,))],
        ),
        compiler_params=_cparams(("arbitrary",)),
        name="combine",
    )(off_tbl, row_tbl, len_tbl, ps, tw, h, ys)


def _experts_kernel(be_ref, nused_ref, xs_ref, wgu_ref, bgu_ref, wd_ref, bd_ref, ys_ref, wgu_bf, wd_bf):
    i = pl.program_id(0)
    used = i < nused_ref[0]
    new_expert = (i == 0) | (be_ref[i] != be_ref[jnp.maximum(i - 1, 0)])

    @pl.when(used & new_expert)
    def _():
        wgu_bf[...] = wgu_ref[0].astype(BF16)
        wd_bf[...] = wd_ref[0].astype(BF16)

    @pl.when(used)
    def _():
        x = jnp.concatenate(
            [xs_ref[pl.ds(c, BM, stride=ROW_SUB), :] for c in range(ROW_SUB)], axis=-1).astype(BF16)
        hh = jnp.dot(x, wgu_bf[...], preferred_element_type=F32) + bgu_ref[0]
        gate = jnp.minimum(hh[:, :D_FF], SWIGLU_LIMIT)
        up = jnp.clip(hh[:, D_FF:], -SWIGLU_LIMIT, SWIGLU_LIMIT)
        glu = gate * jax.nn.sigmoid(SWIGLU_ALPHA * gate)
        act = ((up + 1.0) * glu).astype(BF16)
        y = jnp.dot(act, wd_bf[...], preferred_element_type=F32) + bd_ref[0]
        for c in range(ROW_SUB):
            ys_ref[pl.ds(c, BM, stride=ROW_SUB), :] = y[:, c * LANES:(c + 1) * LANES]

    @pl.when(i >= nused_ref[0])
    def _():
        ys_ref[...] = jnp.zeros_like(ys_ref)


def _experts(block_e, nused, xs, wgu, bgu, wd, bd):
    nb = xs.shape[0] // (BM * ROW_SUB)
    emap3 = lambda i, be, nu: (be[i], 0, 0)
    rows = pl.BlockSpec((BM * ROW_SUB, LANES), lambda i, be, nu: (i, 0))
    return pl.pallas_call(
        _experts_kernel,
        out_shape=jax.ShapeDtypeStruct(xs.shape, F32),
        grid_spec=pltpu.PrefetchScalarGridSpec(
            num_scalar_prefetch=2,
            grid=(nb,),
            in_specs=[rows,
                      pl.BlockSpec((1, D_MODEL, 2 * D_FF), emap3),
                      pl.BlockSpec((1, 1, 2 * D_FF), emap3),
                      pl.BlockSpec((1, D_FF, D_MODEL), emap3),
                      pl.BlockSpec((1, 1, D_MODEL), emap3)],
            out_specs=rows,
            scratch_shapes=[pltpu.VMEM((D_MODEL, 2 * D_FF), BF16), pltpu.VMEM((D_FF, D_MODEL), BF16)],
        ),
        compiler_params=_cparams(("arbitrary",)),
        name="experts",
    )(block_e, nused, xs, wgu, bgu, wd, bd)


def _head_indicator(n_cols, head_dim):
    e = np.zeros((n_cols, LANES), np.float32)
    e[np.arange(n_cols), np.arange(n_cols) // head_dim] = 1.0
    return e


def _mixer(x, prm):
    b, s, d = x.shape
    t = b * s
    x2 = x.reshape(t, d)
    res = _inproj(x2, b, s, prm["norm_mix"], prm["w_all"], prm["q_gain"], prm["k_gain"], prm["e_in"],
                  prm["lb_f"], prm["lb_b"])
    n_grp = len(ATTN_GROUPS)
    qs, ks, vs = res[:n_grp], res[n_grp:2 * n_grp], res[2 * n_grp:3 * n_grp]
    qb, lff, lfb, ib, og, gt = res[3 * n_grp:]
    r3 = lambda a: a.reshape(b, s, a.shape[-1])
    outs, lses = [], []
    for g in range(n_grp):
        o, lse = _attention_group(qs[g], ks[g], vs[g], g)
        outs.append(o)
        lses.append(lse)
    of, ob = _hgrn(r3(qb), r3(lff), r3(lfb), r3(ib))
    return x2, outs, lses, of.reshape(t, D_B), ob.reshape(t, D_B), og, gt


def kernel(x_prompt, x_sample, norm_mix, w_in, q_gain, k_gain, hgrn_lb, hgrn_o_gain, w_gate, w_proj_a,
           w_proj_b, w_out, norm_moe, w_router, b_router, w_gu, b_gu, w_down, b_down):
    l = 0
    lb = jnp.cumsum(jax.nn.softmax(hgrn_lb.astype(F32), axis=1), axis=1)
    wr = jnp.zeros((D_MODEL, LANES), F32).at[:, :N_EXPERTS].set(w_router[l])
    wr_hi = wr.astype(BF16)
    prm = {
        "norm_mix": norm_mix[l].reshape(1, D_MODEL),
        "w_all": jnp.concatenate([w_in[l], w_gate[l]], axis=1).astype(BF16),
        "q_gain": q_gain[l].reshape(1, W_A) * (HEAD_DIM_A ** -0.5),
        "k_gain": k_gain[l].reshape(1, W_A),
        "e_in": jnp.asarray(_head_indicator(PIECE, HEAD_DIM_A) @ _head_indicator(PIECE, HEAD_DIM_A).T, BF16),
        "lb_f": lb[0, l].reshape(1, D_B),
        "lb_b": lb[1, l].reshape(1, D_B),
    }
    wa, wb, wo = w_proj_a[l].astype(BF16), w_proj_b[l].astype(BF16), w_out[l].astype(BF16)
    ogain = hgrn_o_gain[l].reshape(1, D_B)
    nmoe = norm_moe[l].reshape(1, D_MODEL)
    wr_lo = (wr - wr_hi.astype(F32)).astype(BF16)
    br = jnp.full((1, LANES), NEG, F32).at[0, :N_EXPERTS].set(b_router[l])

    cnt = jnp.zeros((1, LANES), F32)
    per_batch = []
    for x in (x_prompt, x_sample):
        x2, outs, lses, of, ob, og, gt = _mixer(x, prm)
        h, xt, tw, ps, ct, rt, cnt = _mix(x2, x.shape[1], outs[0], outs[1], outs[2], lses[0], lses[1], lses[2],
                                          of, ob, og, gt, wa, wb, wo, ogain, nmoe, wr_hi, wr_lo, br, cnt)
        per_batch.append((x.shape, h, xt, tw, ps, ct, rt))

    n_tok = sum(pb[1].shape[0] for pb in per_batch)
    sizes = cnt[0, :N_EXPERTS].astype(jnp.int32)
    pad_sizes = (sizes + BM - 1) // BM * BM
    pad_ends = jnp.cumsum(pad_sizes)
    pad_starts = pad_ends - pad_sizes
    nb = (n_tok * TOP_K) // BM + N_EXPERTS
    block_start = jnp.arange(nb, dtype=jnp.int32) * BM
    block_e = jnp.minimum(jnp.sum(pad_ends[None, :] <= block_start[:, None], axis=1),
                          N_EXPERTS - 1).astype(jnp.int32)
    nused = (pad_ends[-1:] // BM).astype(jnp.int32)

    tables = []
    for _, _, _, _, _, ct, rt in per_batch:
        cnt_te = ct[:, 0, :N_EXPERTS].astype(jnp.int32)
        tile_off = jnp.cumsum(cnt_te, axis=1) - cnt_te
        expert_row = pad_starts[None, :] + rt[:, 0, :N_EXPERTS].astype(jnp.int32)
        tables.append((tile_off.reshape(-1), expert_row.reshape(-1), cnt_te.reshape(-1)))

    fill_row = jnp.concatenate([pad_starts + sizes, pad_ends[-1:]]).astype(jnp.int32)
    fill_len = jnp.concatenate([pad_sizes - sizes, nb * BM - pad_ends[-1:]]).astype(jnp.int32)
    both = [jnp.concatenate(cols) for cols in zip(*tables)]
    xs = _dispatch(*both, fill_row, fill_len, per_batch[0][2], per_batch[1][2], nb * BM)
    ys = _experts(block_e, nused, xs, w_gu[l], b_gu[l].reshape(N_EXPERTS, 1, 2 * D_FF),
                  w_down[l], b_down[l].reshape(N_EXPERTS, 1, D_MODEL))
    results = []
    for (shape, h, _, tw, ps, _, _), tbl in zip(per_batch, tables):
        results.append(_combine(*tbl, ps, tw, h, ys).reshape(shape))
    return tuple(results)
```

```python
import functools
import math

import jax
import jax.numpy as jnp
import numpy as np
from jax import lax
from jax.experimental import pallas as pl
from jax.experimental.pallas import tpu as pltpu

F32 = jnp.float32
BF16 = jnp.bfloat16

D_MODEL = 1024
ATTN_GROUPS = ((128, 1), (512, 4), (2048, 16))
HEADS_PER_GROUP = 8
N_HEADS_A = 24
HEAD_DIM_A = 64
W_A = N_HEADS_A * HEAD_DIM_A
GROUP_W = HEADS_PER_GROUP * HEAD_DIM_A
N_SIDE = 64
N_HEADS_B = 4
DK_B = 128
HGRN_CHUNK = 64
HGRN_SUB = 16
HGRN_EXP_CLAMP = 80.0
HGRN_CHUNKS_PER_TRIP = 2
D_B = N_HEADS_B * DK_B
D_IN = 3 * W_A + 5 * D_B
N_EXPERTS = 32
TOP_K = 4
D_FF = 1024
SWIGLU_LIMIT = 7.0
SWIGLU_ALPHA = 1.702
EPS = 1e-6
NEG = -1e30

LANES = 128
VMEM_LIMIT = 56 * 1024 * 1024

TM_IN = 256
PIECE = 512
TQ = 2048
SQ = 128
ATTN_PAIRS_IN_FLIGHT = 1
TS_HGRN = 512
TM_MIX = 256
MIX_TILES_PER_STEP = 2
BM = 512
SUBLANES = 8
ROW_SUB = D_MODEL // LANES
assert ROW_SUB == SUBLANES
DISPATCH_SLOTS = 3
SEG_CHUNK = 16


def _cparams(sem):
    return pltpu.CompilerParams(dimension_semantics=sem, vmem_limit_bytes=VMEM_LIMIT)


def _lockstep(*gens):
    out = [None] * len(gens)
    live = list(range(len(gens)))
    while live:
        for g in list(live):
            try:
                next(gens[g])
            except StopIteration as done:
                out[g] = done.value
                live.remove(g)
    return out


def _split_bf16(x):
    hi = x.astype(BF16)
    lo = (x - hi.astype(F32)).astype(BF16)
    return hi, lo


def _inproj_kernel(x_ref, nw_ref, w_ref, qg_ref, kg_ref, e_ref, lbf_ref, lbb_ref,
                   q0_ref, q1_ref, q2_ref, k0_ref, k1_ref, k2_ref, v0_ref, v1_ref, v2_ref,
                   qb_ref, lff_ref, lfb_ref, ib_ref, og_ref, gt_ref, scr_ref, *, tm):
    q_refs, k_refs, v_refs = (q0_ref, q1_ref, q2_ref), (k0_ref, k1_ref, k2_ref), (v0_ref, v1_ref, v2_ref)
    x = x_ref[...]
    ms = jnp.mean(x * x, axis=-1, keepdims=True)
    xn = (x * lax.rsqrt(ms + EPS) * nw_ref[...]).astype(BF16)

    def proj(col):
        return jnp.dot(xn, w_ref[:, col:col + PIECE], preferred_element_type=F32)

    def normed(dst_ref, col, gain, dil):
        y = proj(col)
        yield
        ss = jnp.dot((y * y).astype(BF16), e_ref[...], preferred_element_type=F32)
        yield
        store_group(dst_ref, y * lax.rsqrt(ss * (1.0 / HEAD_DIM_A) + EPS) * gain, dil)

    def plain(dst_ref, col, dil):
        y = proj(col)
        yield
        yield
        store_group(dst_ref, y, dil)

    def store_group(dst_ref, y, dil):
        if dil == 1:
            dst_ref[0, 0] = y.astype(BF16)
            return
        for c in range(GROUP_W // LANES):
            scr_ref[c] = y[:, c * LANES:(c + 1) * LANES]
        for r in range(dil):
            rows = pl.ds(r, tm // dil, stride=dil)
            dst_ref[0, r] = jnp.concatenate(
                [scr_ref[c, rows, :] for c in range(GROUP_W // LANES)], axis=-1).astype(BF16)

    for g, (_, dil) in enumerate(ATTN_GROUPS):
        c = g * GROUP_W
        _lockstep(normed(q_refs[g], c, qg_ref[:, c:c + GROUP_W], dil),
                  normed(k_refs[g], W_A + c, kg_ref[:, c:c + GROUP_W], dil),
                  plain(v_refs[g], 2 * W_A + c, dil))
    base = 3 * W_A
    qb = proj(base)
    qb_ref[...] = (qb * jax.nn.sigmoid(qb) * (DK_B ** -0.5)).astype(BF16)
    for dst, lb_ref, off in ((lff_ref, lbf_ref, D_B), (lfb_ref, lbb_ref, 2 * D_B)):
        lb = lb_ref[...]
        f = lb + (1.0 - lb) * jax.nn.sigmoid(proj(base + off))
        dst[...] = jnp.log(f)
    ib_ref[...] = proj(base + 3 * D_B).astype(BF16)
    og = proj(base + 4 * D_B)
    og_ref[...] = (og * jax.nn.sigmoid(og)).astype(BF16)
    for p in range(2 * D_MODEL // PIECE):
        c = p * PIECE
        gt_ref[:, c:c + PIECE] = jax.nn.sigmoid(proj(D_IN + c)).astype(BF16)


def _inproj(x2, b, s, nw, w_all, qg, kg, e_mat, lbf, lbb):
    t = x2.shape[0]
    tm = TM_IN
    tps = s // tm
    n_all = w_all.shape[1]
    row = lambda i: (i, 0)
    const = lambda i: (0, 0)
    seq = lambda i: (i // tps, 0, i % tps, 0)
    widths = (D_B, D_B, D_B, D_B, D_B, 2 * D_MODEL)
    dtypes = (BF16, F32, F32, BF16, BF16, BF16)
    grp_shapes = [jax.ShapeDtypeStruct((b, dil, s // dil, GROUP_W), BF16) for _, dil in ATTN_GROUPS] * 3
    grp_specs = [pl.BlockSpec((1, dil, tm // dil, GROUP_W), seq) for _, dil in ATTN_GROUPS] * 3
    return pl.pallas_call(
        functools.partial(_inproj_kernel, tm=tm),
        out_shape=tuple(grp_shapes) + tuple(jax.ShapeDtypeStruct((t, w), dt) for w, dt in zip(widths, dtypes)),
        grid=(t // tm,),
        in_specs=[
            pl.BlockSpec((tm, D_MODEL), row),
            pl.BlockSpec((1, D_MODEL), const),
            pl.BlockSpec((D_MODEL, n_all), const),
            pl.BlockSpec((1, W_A), const),
            pl.BlockSpec((1, W_A), const),
            pl.BlockSpec((PIECE, PIECE), const),
            pl.BlockSpec((1, D_B), const),
            pl.BlockSpec((1, D_B), const),
        ],
        out_specs=tuple(grp_specs) + tuple(pl.BlockSpec((tm, w), row) for w in widths),
        scratch_shapes=[pltpu.VMEM((GROUP_W // LANES, tm, LANES), F32)],
        compiler_params=_cparams(("parallel",)),
        name="inproj",
    )(x2, nw, w_all, qg, kg, e_mat, lbf, lbb)


def _attn_kernel(q_ref, kp_ref, kc_ref, kn_ref, vp_ref, vc_ref, vn_ref, bias_ref,
                 o_ref, lse_ref, *, tq, sub_len):
    i = pl.program_id(2)
    sq = bias_ref.shape[1] // 2
    nk = sq + 2 * N_SIDE
    lane = lax.broadcasted_iota(jnp.int32, (sq, LANES), 1)
    low = lane < HEAD_DIM_A
    ones = jnp.ones((nk, LANES), BF16)
    zero = jnp.zeros((sq, LANES), BF16)

    def pair(res, kk, vv, j, pr, colbias):
        cols = slice(pr * LANES, (pr + 1) * LANES)
        q2 = q_ref[res, j * sq:(j + 1) * sq, cols]
        k2 = kk[j * sq:j * sq + nk, cols]
        v2 = jnp.concatenate([vv[j * sq:j * sq + nk, cols], ones], axis=1)
        q_st = jnp.concatenate([jnp.where(low, q2, zero), jnp.where(low, zero, q2)], axis=0)
        s = lax.dot_general(q_st, k2, (((1,), (1,)), ((), ())), preferred_element_type=F32)
        yield
        s = s + bias_ref[pr] + colbias
        m = jnp.max(s, axis=-1, keepdims=True)
        p = jnp.exp(s - m).astype(BF16)
        r = jnp.dot(p, v2, preferred_element_type=F32)
        yield
        o2 = jnp.where(low, r[:sq, :LANES] / r[:sq, LANES:], r[sq:, :LANES] / r[sq:, LANES:])
        o_ref[res, j * sq:(j + 1) * sq, cols] = o2.astype(o_ref.dtype)
        lse = m + jnp.log(r[:, LANES:LANES + 1])
        lse_ref[res, j * sq:(j + 1) * sq, cols] = jnp.where(low, lse[:sq], lse[sq:])

    n_pairs = HEADS_PER_GROUP // 2
    for res in range(q_ref.shape[0]):
        kk = jnp.concatenate([kp_ref[res], kc_ref[res], kn_ref[res]], axis=0)
        vv = jnp.concatenate([vp_ref[res], vc_ref[res], vn_ref[res]], axis=0)
        for j in range(tq // sq):
            kpos = i * tq + j * sq - N_SIDE + lax.broadcasted_iota(jnp.int32, (1, nk), 1)
            colbias = jnp.where((kpos >= 0) & (kpos < sub_len), 0.0, NEG).astype(F32)
            for pr in range(0, n_pairs, ATTN_PAIRS_IN_FLIGHT):
                _lockstep(*[pair(res, kk, vv, j, pr + d, colbias) for d in range(ATTN_PAIRS_IN_FLIGHT)])


def _attn_bias(sq, dil, slopes):
    nk = sq + 2 * N_SIDE
    rel = np.arange(nk)[None, :] - N_SIDE - np.arange(sq)[:, None]
    band = np.abs(rel) <= N_SIDE
    alibi = -slopes[:, None, None] * (dil * np.abs(rel)).astype(np.float32)[None]
    bias = np.where(band[None], alibi, NEG).astype(np.float32)
    return jnp.asarray(bias.reshape(HEADS_PER_GROUP // 2, 2 * sq, nk))


def _attention_group(q, k, v, g):
    b, dil, sub_len, _ = q.shape
    tq = min(TQ, sub_len)
    sq = min(SQ, sub_len)
    hb = tq // N_SIDE
    n_halo = sub_len // N_SIDE
    slopes = (2.0 ** (-8.0 * (np.arange(N_HEADS_A) + 1) / N_HEADS_A)).astype(np.float32)
    bias = _attn_bias(sq, dil, slopes[g * HEADS_PER_GROUP:(g + 1) * HEADS_PER_GROUP])
    cur = lambda bi, r, i: (bi, r, i, 0)
    prev = lambda bi, r, i: (bi, r, jnp.maximum(i * hb - 1, 0), 0)
    nxt = lambda bi, r, i: (bi, r, jnp.minimum((i + 1) * hb, n_halo - 1), 0)
    n_res = max(1, min(dil, TQ // tq))
    blk_q = (None, n_res, tq, GROUP_W)
    blk_h = (None, n_res, N_SIDE, GROUP_W)
    return pl.pallas_call(
        functools.partial(_attn_kernel, tq=tq, sub_len=sub_len),
        out_shape=(jax.ShapeDtypeStruct((b, dil, sub_len, GROUP_W), BF16),
                   jax.ShapeDtypeStruct((b, dil, sub_len, GROUP_W), F32)),
        grid=(b, dil // n_res, sub_len // tq),
        in_specs=[
            pl.BlockSpec(blk_q, cur),
            pl.BlockSpec(blk_h, prev), pl.BlockSpec(blk_q, cur), pl.BlockSpec(blk_h, nxt),
            pl.BlockSpec(blk_h, prev), pl.BlockSpec(blk_q, cur), pl.BlockSpec(blk_h, nxt),
            pl.BlockSpec(bias.shape, lambda bi, r, i: (0, 0, 0)),
        ],
        out_specs=(pl.BlockSpec(blk_q, cur), pl.BlockSpec(blk_q, cur)),
        compiler_params=_cparams(("parallel", "parallel", "parallel")),
        name=f"attn_d{dil}",
    )(q, k, k, k, v, v, v, bias)


def _hgrn_kernel(qf_ref, lf_ref, vf_ref, qr_ref, lr_ref, vr_ref, of_ref, or_ref,
                 sf_ref, sr_ref, qif_ref, qir_ref, uf_ref, ur_ref, df_ref, dr_ref, *, ts):
    c_len = HGRN_CHUNK
    nc = ts // c_len

    @pl.when(pl.program_id(1) == 0)
    def _():
        sf_ref[...] = jnp.zeros_like(sf_ref)
        sr_ref[...] = jnp.zeros_like(sr_ref)

    r_i = lax.broadcasted_iota(jnp.int32, (c_len, c_len), 0)
    c_i = lax.broadcasted_iota(jnp.int32, (c_len, c_len), 1)
    lower = r_i >= c_i
    upper = r_i <= c_i
    tri_f = jnp.where(lower, 1.0, 0.0).astype(BF16)
    tri_r = jnp.where(upper, 1.0, 0.0).astype(BF16)

    def cumsum(tri, x):
        hi = x.astype(BF16)
        r1 = x - hi.astype(F32)
        mid = r1.astype(BF16)
        lo = (r1 - mid.astype(F32)).astype(BF16)
        d = lambda a: jnp.dot(tri, a, preferred_element_type=F32)
        return d(hi) + d(mid) + d(lo)

    n_sub = c_len // HGRN_SUB
    shift = HGRN_SUB.bit_length() - 1
    sub_f = jnp.right_shift(lax.broadcasted_iota(jnp.int32, (c_len, D_B), 0), shift)
    sub_tf = jnp.right_shift(lax.broadcasted_iota(jnp.int32, (D_B, c_len), 1), shift)
    t_i = lax.broadcasted_iota(jnp.int32, (2 * c_len, c_len), 0) & (c_len - 1)
    s_i = lax.broadcasted_iota(jnp.int32, (2 * c_len, c_len), 1)
    same_sub2 = jnp.right_shift(t_i, shift) == jnp.right_shift(s_i, shift)
    lower2, upper2 = t_i >= s_i, t_i <= s_i
    lower_sub2, upper_sub2 = same_sub2 & lower2, same_sub2 & upper2
    row_head = lax.broadcasted_iota(jnp.int32, (2 * c_len, 2 * DK_B), 0) // c_len
    col_head = lax.broadcasted_iota(jnp.int32, (2 * c_len, 2 * DK_B), 1) // DK_B
    pair_cols = jnp.where(row_head == col_head, 1.0, 0.0).astype(BF16)
    pair_off = jnp.concatenate([pair_cols] * (n_sub - 1), axis=1)

    def intra(q_ref, l_ref, v_ref, o_ref, qi_ref, u_ref, d_ref, c, rev):
        rows = pl.ds(pl.multiple_of(c * c_len, c_len), c_len)
        lf = l_ref[0, rows, :]
        q = q_ref[0, rows, :].astype(F32)
        v = v_ref[0, rows, :]
        k = 1.0 - jnp.exp(lf)
        b = cumsum(tri_r if rev else tri_f, lf)
        yield
        sub = (n_sub - 1 - sub_f) if rev else sub_f

        def b_at(pos):
            r = c_len - 1 - pos if rev else pos
            return b[r:r + 1, :]

        def per_sub(vals):
            vals = vals[::-1] if rev else vals
            return jnp.concatenate([jnp.broadcast_to(x, (HGRN_SUB, D_B)) for x in vals], axis=0)

        a_end = [b_at(HGRN_SUB * j + HGRN_SUB - 1) for j in range(n_sub)]
        a_start = [jnp.zeros((1, D_B), F32)] + a_end[:-1]
        btot = a_end[-1]
        end_full, start_full = per_sub(a_end), per_sub(a_start)
        stores = [(qi_ref, (rows, slice(None)), (q * jnp.exp(b)).astype(BF16))]
        q_off = [jnp.where(sub > j, q * jnp.exp(jnp.minimum(b - a_end[j], 0.0)), 0.0).astype(BF16)
                 for j in range(n_sub - 1)]
        k_end = k * jnp.exp(end_full - b)
        sub_t = (n_sub - 1 - sub_tf) if rev else sub_tf
        q_dia = (q * jnp.exp(b - start_full)).astype(BF16)
        k_upd = (k * jnp.exp(btot - b)).astype(BF16)
        k_end_t = k_end.T
        k_dia_t = (k * jnp.exp(jnp.minimum(start_full - b, HGRN_EXP_CLAMP))).T.astype(BF16)
        k_off_t = [jnp.where(sub_t == j, k_end_t, 0.0).astype(BF16) for j in range(n_sub - 1)]
        dec8_t = jnp.broadcast_to(jnp.exp(btot), (SUBLANES, D_B)).T
        keep, keep_d = (upper2, upper_sub2) if rev else (lower2, lower_sub2)
        two = lambda a: jnp.concatenate([a, a], axis=0)
        for p in range(N_HEADS_B // 2):
            ps = slice(2 * p * DK_B, (2 * p + 2) * DK_B)
            qc = two(jnp.concatenate([q_off[j][:, ps] for j in range(n_sub - 1)], axis=1)) * pair_off
            kc = jnp.concatenate([k_off_t[j][ps, :] for j in range(n_sub - 1)], axis=0)
            yield
            sc = jnp.dot(qc, kc, preferred_element_type=F32)
            sc_d = jnp.dot(two(q_dia[:, ps]) * pair_cols, k_dia_t[ps, :], preferred_element_type=F32)
            u2 = lax.dot_general(k_upd[:, ps], v[:, ps], (((0,), (0,)), ((), ())),
                                 preferred_element_type=F32)
            yield
            sc = (jnp.where(keep, sc, 0.0) + jnp.where(keep_d, sc_d, 0.0)).astype(BF16)
            o2 = jnp.dot(sc, v[:, ps], preferred_element_type=F32)
            for i in range(2):
                h = 2 * p + i
                hs = slice(h * DK_B, (h + 1) * DK_B)
                blk = slice(i * DK_B, (i + 1) * DK_B)
                stores.append((o_ref, (0, rows, hs), o2[i * c_len:(i + 1) * c_len, blk]))
                stores.append((u_ref, (c, h), u2[blk, blk]))
                stores.append((d_ref, (c, h), jnp.broadcast_to(dec8_t[hs, 0:1], (DK_B, DK_B))))
        return stores

    def carry_state(o_ref, qi_ref, u_ref, d_ref, s_ref, c):
        rows = pl.ds(pl.multiple_of(c * c_len, c_len), c_len)
        zero = jnp.zeros((DK_B, DK_B), BF16)
        stores = []
        for p in range(N_HEADS_B // 2):
            ps = slice(2 * p * DK_B, (2 * p + 2) * DK_B)
            sa, sb = s_ref[2 * p], s_ref[2 * p + 1]
            s_bd = jnp.concatenate([jnp.concatenate([sa.astype(BF16), zero], axis=1),
                                    jnp.concatenate([zero, sb.astype(BF16)], axis=1)], axis=0)
            o_new = o_ref[0, rows, ps] + jnp.dot(qi_ref[rows, ps], s_bd, preferred_element_type=F32)
            stores.append((o_ref, (0, rows, ps), o_new))
            stores.append((s_ref, (2 * p,), sa * d_ref[c, 2 * p] + u_ref[c, 2 * p]))
            stores.append((s_ref, (2 * p + 1,), sb * d_ref[c, 2 * p + 1] + u_ref[c, 2 * p + 1]))
        return stores

    def commit(stores):
        for ref, idx, val in stores:
            ref[idx] = val

    def intra_body(trip, carry):
        gens = []
        for i in range(HGRN_CHUNKS_PER_TRIP):
            c = trip * HGRN_CHUNKS_PER_TRIP + i
            gens.append(intra(qf_ref, lf_ref, vf_ref, of_ref, qif_ref, uf_ref, df_ref, c, False))
            gens.append(intra(qr_ref, lr_ref, vr_ref, or_ref, qir_ref, ur_ref, dr_ref, c, True))
        commit(sum(_lockstep(*gens), []))
        return carry

    def state_body(c, carry):
        commit(carry_state(of_ref, qif_ref, uf_ref, df_ref, sf_ref, c)
               + carry_state(or_ref, qir_ref, ur_ref, dr_ref, sr_ref, nc - 1 - c))
        return carry

    lax.fori_loop(0, nc // HGRN_CHUNKS_PER_TRIP, intra_body, 0)
    lax.fori_loop(0, nc, state_body, 0)


def _hgrn(qb, lff, lfb, ib):
    b, s, _ = qb.shape
    ts = min(TS_HGRN, s)
    nt = s // ts
    nc = ts // HGRN_CHUNK
    fwd = lambda bi, j: (bi, j, 0)
    rev = lambda bi, j: (bi, nt - 1 - j, 0)
    blk = (1, ts, D_B)
    return pl.pallas_call(
        functools.partial(_hgrn_kernel, ts=ts),
        out_shape=(jax.ShapeDtypeStruct((b, s, D_B), F32), jax.ShapeDtypeStruct((b, s, D_B), F32)),
        grid=(b, nt),
        in_specs=[pl.BlockSpec(blk, fwd), pl.BlockSpec(blk, fwd), pl.BlockSpec(blk, fwd),
                  pl.BlockSpec(blk, rev), pl.BlockSpec(blk, rev), pl.BlockSpec(blk, rev)],
        out_specs=(pl.BlockSpec(blk, fwd), pl.BlockSpec(blk, rev)),
        scratch_shapes=[pltpu.VMEM((N_HEADS_B, DK_B, DK_B), F32),
                        pltpu.VMEM((N_HEADS_B, DK_B, DK_B), F32),
                        pltpu.VMEM((ts, D_B), BF16),
                        pltpu.VMEM((ts, D_B), BF16),
                        pltpu.VMEM((nc, N_HEADS_B, DK_B, DK_B), F32),
                        pltpu.VMEM((nc, N_HEADS_B, DK_B, DK_B), F32),
                        pltpu.VMEM((nc, N_HEADS_B, DK_B, DK_B), F32),
                        pltpu.VMEM((nc, N_HEADS_B, DK_B, DK_B), F32)],
        compiler_params=_cparams(("parallel", "arbitrary")),
        name="hgrn",
    )(qb, lff, ib, qb, lfb, ib)


def _mix_kernel(x_ref, o1_ref, o2_ref, o3_ref, l1_ref, l2_ref, l3_ref, of_ref, ob_ref, og_ref,
                gt_ref, wa_ref, wb_ref, wo_ref, ogain_ref, nmoe_ref, wrh_ref, wrl_ref,
                br_ref, cnt_ref,
                h_ref, xt_ref, tw_ref, ps_ref, ct_ref, rt_ref, cnt_out_ref, run_ref, so_ref, *, tm):
    i = pl.program_id(0)

    @pl.when(i == 0)
    def _():
        run_ref[...] = cnt_ref[...]

    def token_major(src_ref, scr_ref, dil):
        if dil == 1:
            return src_ref[0].astype(F32)
        n_chunk = scr_ref.shape[0]
        for r in range(dil):
            blk = src_ref[r].astype(F32)
            for c in range(n_chunk):
                scr_ref[c, pl.ds(r, tm // dil, stride=dil), :] = blk[:, c * LANES:(c + 1) * LANES]
        return jnp.concatenate([scr_ref[c] for c in range(n_chunk)], axis=-1)

    dils = [dil for _, dil in ATTN_GROUPS]
    l1, l2, l3 = [token_major(r, so_ref, d) for r, d in zip((l1_ref, l2_ref, l3_ref), dils)]
    mx = jnp.maximum(jnp.maximum(l1, l2), l3)
    e1, e2, e3 = jnp.exp(l1 - mx), jnp.exp(l2 - mx), jnp.exp(l3 - mx)
    attn = (e1 * token_major(o1_ref, so_ref, dils[0]) + e2 * token_major(o2_ref, so_ref, dils[1])
            + e3 * token_major(o3_ref, so_ref, dils[2])) / (e1 + e2 + e3)

    o = of_ref[...] + ob_ref[...]
    parts = []
    for h in range(N_HEADS_B):
        oh = o[:, h * DK_B:(h + 1) * DK_B]
        ms = jnp.mean(oh * oh, axis=-1, keepdims=True)
        parts.append(oh * lax.rsqrt(ms + EPS))
    hg = jnp.concatenate(parts, axis=-1) * ogain_ref[...] * og_ref[...].astype(F32)

    attn_bf, hg_bf = attn.astype(BF16), hg.astype(BF16)
    tp = TM_MIX
    lane = lax.broadcasted_iota(jnp.int32, (tp, LANES), 1)
    r_i = lax.broadcasted_iota(jnp.int32, (tp, tp), 0)
    c_i = lax.broadcasted_iota(jnp.int32, (tp, tp), 1)
    tri = jnp.where(r_i > c_i, 1.0, 0.0).astype(BF16)
    e_r = lax.broadcasted_iota(jnp.int32, (LANES, LANES), 0)
    e_c = lax.broadcasted_iota(jnp.int32, (LANES, LANES), 1)
    before_e = jnp.where(e_r < e_c, 1.0, 0.0).astype(BF16)
    row_id = lax.broadcasted_iota(jnp.int32, (tp * TOP_K, tp), 0).astype(F32)

    def token_chain(part):
        rs = slice(part * tp, (part + 1) * tp)
        pa = jnp.dot(attn_bf[rs], wa_ref[...], preferred_element_type=F32)
        pb = jnp.dot(hg_bf[rs], wb_ref[...], preferred_element_type=F32)
        yield
        mixed = (gt_ref[rs, :D_MODEL].astype(F32) * pa + gt_ref[rs, D_MODEL:].astype(F32) * pb)
        h = x_ref[rs, :] + jnp.dot(mixed.astype(BF16), wo_ref[...], preferred_element_type=F32)
        yield
        h_ref[rs, :] = h
        ms = jnp.mean(h * h, axis=-1, keepdims=True)
        hn = h * lax.rsqrt(ms + EPS) * nmoe_ref[...]
        hi, lo = _split_bf16(hn)
        lg = (jnp.dot(hi, wrh_ref[...], preferred_element_type=F32)
              + jnp.dot(lo, wrh_ref[...], preferred_element_type=F32)
              + jnp.dot(hi, wrl_ref[...], preferred_element_type=F32)) + br_ref[...]
        yield
        vals, idxs = [], []
        onehot = jnp.zeros((tp, LANES), F32)
        for _ in range(TOP_K):
            m = jnp.max(lg, axis=-1, keepdims=True)
            idx = jnp.min(jnp.where(lg == m, lane, LANES), axis=-1, keepdims=True)
            sel = lane == idx
            onehot = jnp.where(sel, 1.0, onehot)
            lg = jnp.where(sel, NEG * 2, lg)
            vals.append(m)
            idxs.append(idx)
        exps = [jnp.exp(v - vals[0]) for v in vals]
        inv = 1.0 / (exps[0] + exps[1] + exps[2] + exps[3])
        local = jnp.dot(tri, onehot.astype(BF16), preferred_element_type=F32)
        cnt_tile = jnp.sum(onehot, axis=0, keepdims=True)
        off = jnp.dot(jnp.broadcast_to(cnt_tile, (8, LANES)).astype(BF16), before_e,
                      preferred_element_type=F32)[0:1]
        yield
        slot = off + local
        tw = jnp.zeros((tp, LANES), F32)
        ps = jnp.full((tp, LANES), -1.0, F32)
        for k in range(TOP_K):
            slot_k = jnp.sum(jnp.where(lane == idxs[k], slot, 0.0), axis=-1, keepdims=True)
            tw = jnp.where(lane == k, exps[k] * inv, tw)
            ps = jnp.where(lane == k, slot_k, ps)
        tw_ref[rs, :] = tw
        ps_ref[rs, :] = ps.astype(jnp.int32)
        ps_t = ps.T
        perm = jnp.zeros((tp * TOP_K, tp), F32)
        for k in range(TOP_K):
            perm = jnp.where(row_id == ps_t[k:k + 1, :], 1.0, perm)
        rows = jnp.dot(perm.astype(BF16), hi, preferred_element_type=F32)
        yield
        for c in range(D_MODEL // LANES):
            xt_ref[pl.ds(part * tp * TOP_K * ROW_SUB + c, tp * TOP_K, stride=ROW_SUB), :] = (
                rows[:, c * LANES:(c + 1) * LANES])
        return cnt_tile

    counts = _lockstep(*[token_chain(p) for p in range(tm // tp)])
    run = run_ref[...]
    for part, cnt_tile in enumerate(counts):
        ct_ref[part] = cnt_tile
        rt_ref[part] = run
        run = run + cnt_tile
    run_ref[...] = run
    cnt_out_ref[...] = run


def _mix(x2, s, o1, o2, o3, l1, l2, l3, of, ob, og, gt, wa, wb, wo, ogain, nmoe, wrh, wrl, br, cnt):
    t = x2.shape[0]
    tm = TM_MIX * MIX_TILES_PER_STEP
    tps = s // tm
    row = lambda i: (i, 0)
    const = lambda i: (0, 0)
    seq = lambda i: (i // tps, 0, i % tps, 0)
    rb = lambda w: pl.BlockSpec((tm, w), row)
    cb = lambda a: pl.BlockSpec(a.shape, const)
    gb = lambda a: pl.BlockSpec((None, a.shape[1], tm // a.shape[1], a.shape[3]), seq)
    per_tile = pl.BlockSpec((MIX_TILES_PER_STEP, 1, LANES), lambda i: (i, 0, 0))
    return pl.pallas_call(
        functools.partial(_mix_kernel, tm=tm),
        out_shape=(jax.ShapeDtypeStruct((t, D_MODEL), F32),
                   jax.ShapeDtypeStruct((t * TOP_K * ROW_SUB, LANES), F32),
                   jax.ShapeDtypeStruct((t, LANES), F32),
                   jax.ShapeDtypeStruct((t, LANES), jnp.int32),
                   jax.ShapeDtypeStruct((t // TM_MIX, 1, LANES), F32),
                   jax.ShapeDtypeStruct((t // TM_MIX, 1, LANES), F32),
                   jax.ShapeDtypeStruct((1, LANES), F32)),
        grid=(t // tm,),
        in_specs=[rb(D_MODEL), gb(o1), gb(o2), gb(o3), gb(l1), gb(l2), gb(l3),
                  rb(D_B), rb(D_B), rb(D_B), rb(2 * D_MODEL),
                  cb(wa), cb(wb), cb(wo), cb(ogain), cb(nmoe), cb(wrh), cb(wrl), cb(br), cb(cnt)],
        out_specs=(rb(D_MODEL), pl.BlockSpec((tm * TOP_K * ROW_SUB, LANES), row), rb(LANES), rb(LANES),
                   per_tile, per_tile, pl.BlockSpec((1, LANES), const)),
        scratch_shapes=[pltpu.VMEM((1, LANES), F32), pltpu.VMEM((GROUP_W // LANES, tm, LANES), F32)],
        compiler_params=_cparams(("arbitrary",)),
        name="mix",
    )(x2, o1, o2, o3, l1, l2, l3, of, ob, og, gt, wa, wb, wo, ogain, nmoe, wrh, wrl, br, cnt)


def _start_tile_segments(tile, off_tbl, row_tbl, len_tbl, make_piece):
    def segment(e, carry):
        sidx = tile * N_EXPERTS + e
        t0, r0, n = off_tbl[sidx], row_tbl[sidx], len_tbl[sidx]
        n_bulk = jnp.right_shift(n, SEG_CHUNK.bit_length() - 1)

        def bulk(j, c):
            make_piece(t0 + j * SEG_CHUNK, r0 + j * SEG_CHUNK, SEG_CHUNK).start()
            return c

        lax.fori_loop(0, n_bulk, bulk, 0)
        done = n_bulk * SEG_CHUNK
        bit = SEG_CHUNK // 2
        while bit >= 1:
            has = (n & bit) != 0

            @pl.when(has)
            def _(done=done, bit=bit):
                make_piece(t0 + done, r0 + done, bit).start()

            done = done + jnp.where(has, bit, 0)
            bit //= 2
        return carry

    lax.fori_loop(0, N_EXPERTS, segment, 0)


def _row_slice(row, n_rows):
    return pl.ds(pl.multiple_of(row * ROW_SUB, ROW_SUB), n_rows * ROW_SUB)


def _dispatch_kernel(off_tbl, row_tbl, len_tbl, fill_row, fill_len, xa_ref, xb_ref, xs_ref, buf_ref,
                     zero_ref, sems_in, sems_out, sem, *, tiles_a):
    i = pl.program_id(0)
    n_tiles = pl.num_programs(0)
    blk_rows = buf_ref.shape[1]

    def fetch(tile):
        slot = tile % DISPATCH_SLOTS

        def copy(src_ref, src_tile):
            start = pl.multiple_of(src_tile * blk_rows, blk_rows)
            pltpu.make_async_copy(src_ref.at[pl.ds(start, blk_rows)], buf_ref.at[slot], sems_in.at[slot]).start()

        @pl.when(tile < tiles_a)
        def _():
            copy(xa_ref, tile)

        @pl.when(tile >= tiles_a)
        def _():
            copy(xb_ref, tile - tiles_a)

    def scatter_done(tile):
        slot = tile % DISPATCH_SLOTS
        pltpu.make_async_copy(buf_ref.at[slot], xs_ref.at[pl.ds(0, blk_rows)], sems_out.at[slot]).wait()

    def zeros_to(row, n_rows):
        return pltpu.make_async_copy(zero_ref.at[pl.ds(0, n_rows * ROW_SUB)],
                                     xs_ref.at[_row_slice(row, n_rows)], sem)

    @pl.when(i == 0)
    def _():
        fetch(i)
        zero_ref[...] = jnp.zeros_like(zero_ref)

        def fill(e, total):
            r0, n = fill_row[e], fill_len[e]
            n_blk = jnp.right_shift(n, BM.bit_length() - 1)

            def blocks(j, c):
                zeros_to(r0 + j * BM, BM).start()
                return c

            lax.fori_loop(0, n_blk, blocks, 0)
            done = n_blk * BM
            bit = BM // 2
            while bit >= 1:
                has = (n & bit) != 0

                @pl.when(has)
                def _(done=done, bit=bit):
                    zeros_to(r0 + done, bit).start()

                done = done + jnp.where(has, bit, 0)
                bit //= 2
            return total + n

        total = lax.fori_loop(0, N_EXPERTS + 1, fill, 0)
        bit = 1
        while bit * ROW_SUB <= xs_ref.shape[0]:
            @pl.when((total & bit) != 0)
            def _(bit=bit):
                n = bit * ROW_SUB
                pltpu.make_async_copy(xs_ref.at[pl.ds(0, n)], xs_ref.at[pl.ds(0, n)], sem).wait()

            bit *= 2

    @pl.when(i + 1 < n_tiles)
    def _():
        fetch(i + 1)

    slot = i % DISPATCH_SLOTS
    pltpu.make_async_copy(xa_ref.at[pl.ds(0, blk_rows)], buf_ref.at[slot], sems_in.at[slot]).wait()

    def piece(tile_row, expert_row, n_rows):
        return pltpu.make_async_copy(buf_ref.at[slot, _row_slice(tile_row, n_rows)],
                                     xs_ref.at[_row_slice(expert_row, n_rows)], sems_out.at[slot])

    _start_tile_segments(i, off_tbl, row_tbl, len_tbl, piece)

    @pl.when(i >= 1)
    def _():
        scatter_done(i - 1)

    @pl.when(i == n_tiles - 1)
    def _():
        scatter_done(i)


def _dispatch(off_tbl, row_tbl, len_tbl, fill_row, fill_len, xt_a, xt_b, n_rows):
    blk = TM_MIX * TOP_K * ROW_SUB
    tiles_a, tiles_b = xt_a.shape[0] // blk, xt_b.shape[0] // blk
    return pl.pallas_call(
        functools.partial(_dispatch_kernel, tiles_a=tiles_a),
        out_shape=jax.ShapeDtypeStruct((n_rows * ROW_SUB, LANES), F32),
        grid_spec=pltpu.PrefetchScalarGridSpec(
            num_scalar_prefetch=5,
            grid=(tiles_a + tiles_b,),
            in_specs=[pl.BlockSpec(memory_space=pl.ANY), pl.BlockSpec(memory_space=pl.ANY)],
            out_specs=pl.BlockSpec(memory_space=pl.ANY),
            scratch_shapes=[pltpu.VMEM((DISPATCH_SLOTS, blk, LANES), F32),
                            pltpu.VMEM((BM * ROW_SUB, LANES), F32),
                            pltpu.SemaphoreType.DMA((DISPATCH_SLOTS,)),
                            pltpu.SemaphoreType.DMA((DISPATCH_SLOTS,)),
                            pltpu.SemaphoreType.DMA],
        ),
        compiler_params=_cparams(("arbitrary",)),
        name="dispatch",
    )(off_tbl, row_tbl, len_tbl, fill_row, fill_len, xt_a, xt_b)


def _combine_kernel(off_tbl, row_tbl, len_tbl, ps_ref, tw_ref, h_ref, ys_ref, y_ref, buf_ref, sems, *, tm):
    i = pl.program_id(0)
    n_rows = tm * TOP_K

    def start(tile):
        slot = tile % 2

        def piece(tile_row, expert_row, n):
            return pltpu.make_async_copy(ys_ref.at[_row_slice(expert_row, n)],
                                         buf_ref.at[slot, _row_slice(tile_row, n)], sems.at[slot])

        _start_tile_segments(tile, off_tbl, row_tbl, len_tbl, piece)

    @pl.when(i == 0)
    def _():
        start(i)

    @pl.when(i + 1 < pl.num_programs(0))
    def _():
        start(i + 1)

    slot = i % 2
    yt_ref = buf_ref.at[slot]
    pltpu.make_async_copy(ys_ref.at[pl.ds(0, n_rows * ROW_SUB)], yt_ref, sems.at[slot]).wait()
    ysorted = jnp.concatenate(
        [yt_ref[pl.ds(c, n_rows, stride=ROW_SUB), :] for c in range(ROW_SUB)], axis=-1).astype(BF16)
    col = lax.broadcasted_iota(jnp.int32, (tm, n_rows), 1)
    ps = ps_ref[...]
    tw = tw_ref[...]
    pw = jnp.zeros((tm, n_rows), F32)
    for k in range(TOP_K):
        pw = jnp.where(col == ps[:, k:k + 1], tw[:, k:k + 1], pw)
    hi, lo = _split_bf16(pw)
    y_ref[...] = (h_ref[...] + jnp.dot(hi, ysorted, preferred_element_type=F32)
                  + jnp.dot(lo, ysorted, preferred_element_type=F32))


def _combine(off_tbl, row_tbl, len_tbl, ps, tw, h, ys):
    t = h.shape[0]
    tm = TM_MIX
    row = lambda i, *_: (i, 0)
    return pl.pallas_call(
        functools.partial(_combine_kernel, tm=tm),
        out_shape=jax.ShapeDtypeStruct((t, D_MODEL), F32),
        grid_spec=pltpu.PrefetchScalarGridSpec(
            num_scalar_prefetch=3,
            grid=(t // tm,),
            in_specs=[pl.BlockSpec((tm, LANES), row),
                      pl.BlockSpec((tm, LANES), row),
                      pl.BlockSpec((tm, D_MODEL), row),
                      pl.BlockSpec(memory_space=pl.ANY)],
            out_specs=pl.BlockSpec((tm, D_MODEL), row),
            scratch_shapes=[pltpu.VMEM((2, tm * TOP_K * ROW_SUB, LANES), F32),
                            pltpu.SemaphoreType.DMA((2,))],
        ),
        compiler_params=_cparams(("arbitrary",)),
        name="combine",
    )(off_tbl, row_tbl, len_tbl, ps, tw, h, ys)


def _experts_kernel(be_ref, nused_ref, xs_ref, wgu_ref, bgu_ref, wd_ref, bd_ref, ys_ref, wgu_bf, wd_bf):
    i = pl.program_id(0)
    used = i < nused_ref[0]
    new_expert = (i == 0) | (be_ref[i] != be_ref[jnp.maximum(i - 1, 0)])

    @pl.when(used & new_expert)
    def _():
        wgu_bf[...] = wgu_ref[0].astype(BF16)
        wd_bf[...] = wd_ref[0].astype(BF16)

    @pl.when(used)
    def _():
        x = jnp.concatenate(
            [xs_ref[pl.ds(c, BM, stride=ROW_SUB), :] for c in range(ROW_SUB)], axis=-1).astype(BF16)
        hh = jnp.dot(x, wgu_bf[...], preferred_element_type=F32) + bgu_ref[0]
        gate = jnp.minimum(hh[:, :D_FF], SWIGLU_LIMIT)
        up = jnp.clip(hh[:, D_FF:], -SWIGLU_LIMIT, SWIGLU_LIMIT)
        glu = gate * jax.nn.sigmoid(SWIGLU_ALPHA * gate)
        act = ((up + 1.0) * glu).astype(BF16)
        y = jnp.dot(act, wd_bf[...], preferred_element_type=F32) + bd_ref[0]
        for c in range(ROW_SUB):
            ys_ref[pl.ds(c, BM, stride=ROW_SUB), :] = y[:, c * LANES:(c + 1) * LANES]

    @pl.when(i >= nused_ref[0])
    def _():
        ys_ref[...] = jnp.zeros_like(ys_ref)


def _experts(block_e, nused, xs, wgu, bgu, wd, bd):
    nb = xs.shape[0] // (BM * ROW_SUB)
    emap3 = lambda i, be, nu: (be[i], 0, 0)
    rows = pl.BlockSpec((BM * ROW_SUB, LANES), lambda i, be, nu: (i, 0))
    return pl.pallas_call(
        _experts_kernel,
        out_shape=jax.ShapeDtypeStruct(xs.shape, F32),
        grid_spec=pltpu.PrefetchScalarGridSpec(
            num_scalar_prefetch=2,
            grid=(nb,),
            in_specs=[rows,
                      pl.BlockSpec((1, D_MODEL, 2 * D_FF), emap3),
                      pl.BlockSpec((1, 1, 2 * D_FF), emap3),
                      pl.BlockSpec((1, D_FF, D_MODEL), emap3),
                      pl.BlockSpec((1, 1, D_MODEL), emap3)],
            out_specs=rows,
            scratch_shapes=[pltpu.VMEM((D_MODEL, 2 * D_FF), BF16), pltpu.VMEM((D_FF, D_MODEL), BF16)],
        ),
        compiler_params=_cparams(("arbitrary",)),
        name="experts",
    )(block_e, nused, xs, wgu, bgu, wd, bd)


def _head_indicator(n_cols, head_dim):
    e = np.zeros((n_cols, LANES), np.float32)
    e[np.arange(n_cols), np.arange(n_cols) // head_dim] = 1.0
    return e


def _mixer(x, prm):
    b, s, d = x.shape
    t = b * s
    x2 = x.reshape(t, d)
    res = _inproj(x2, b, s, prm["norm_mix"], prm["w_all"], prm["q_gain"], prm["k_gain"], prm["e_in"],
                  prm["lb_f"], prm["lb_b"])
    n_grp = len(ATTN_GROUPS)
    qs, ks, vs = res[:n_grp], res[n_grp:2 * n_grp], res[2 * n_grp:3 * n_grp]
    qb, lff, lfb, ib, og, gt = res[3 * n_grp:]
    r3 = lambda a: a.reshape(b, s, a.shape[-1])
    outs, lses = [], []
    for g in range(n_grp):
        o, lse = _attention_group(qs[g], ks[g], vs[g], g)
        outs.append(o)
        lses.append(lse)
    of, ob = _hgrn(r3(qb), r3(lff), r3(lfb), r3(ib))
    return x2, outs, lses, of.reshape(t, D_B), ob.reshape(t, D_B), og, gt


def kernel(x_prompt, x_sample, norm_mix, w_in, q_gain, k_gain, hgrn_lb, hgrn_o_gain, w_gate, w_proj_a,
           w_proj_b, w_out, norm_moe, w_router, b_router, w_gu, b_gu, w_down, b_down):
    l = 0
    lb = jnp.cumsum(jax.nn.softmax(hgrn_lb.astype(F32), axis=1), axis=1)
    wr = jnp.zeros((D_MODEL, LANES), F32).at[:, :N_EXPERTS].set(w_router[l])
    wr_hi = wr.astype(BF16)
    prm = {
        "norm_mix": norm_mix[l].reshape(1, D_MODEL),
        "w_all": jnp.concatenate([w_in[l], w_gate[l]], axis=1).astype(BF16),
        "q_gain": q_gain[l].reshape(1, W_A) * (HEAD_DIM_A ** -0.5),
        "k_gain": k_gain[l].reshape(1, W_A),
        "e_in": jnp.asarray(_head_indicator(PIECE, HEAD_DIM_A) @ _head_indicator(PIECE, HEAD_DIM_A).T, BF16),
        "lb_f": lb[0, l].reshape(1, D_B),
        "lb_b": lb[1, l].reshape(1, D_B),
    }
    wa, wb, wo = w_proj_a[l].astype(BF16), w_proj_b[l].astype(BF16), w_out[l].astype(BF16)
    ogain = hgrn_o_gain[l].reshape(1, D_B)
    nmoe = norm_moe[l].reshape(1, D_MODEL)
    wr_lo = (wr - wr_hi.astype(F32)).astype(BF16)
    br = jnp.full((1, LANES), NEG, F32).at[0, :N_EXPERTS].set(b_router[l])

    cnt = jnp.zeros((1, LANES), F32)
    per_batch = []
    for x in (x_prompt, x_sample):
        x2, outs, lses, of, ob, og, gt = _mixer(x, prm)
        h, xt, tw, ps, ct, rt, cnt = _mix(x2, x.shape[1], outs[0], outs[1], outs[2], lses[0], lses[1], lses[2],
                                          of, ob, og, gt, wa, wb, wo, ogain, nmoe, wr_hi, wr_lo, br, cnt)
        per_batch.append((x.shape, h, xt, tw, ps, ct, rt))

    n_tok = sum(pb[1].shape[0] for pb in per_batch)
    sizes = cnt[0, :N_EXPERTS].astype(jnp.int32)
    pad_sizes = (sizes + BM - 1) // BM * BM
    pad_ends = jnp.cumsum(pad_sizes)
    pad_starts = pad_ends - pad_sizes
    nb = (n_tok * TOP_K) // BM + N_EXPERTS
    block_start = jnp.arange(nb, dtype=jnp.int32) * BM
    block_e = jnp.minimum(jnp.sum(pad_ends[None, :] <= block_start[:, None], axis=1),
                          N_EXPERTS - 1).astype(jnp.int32)
    nused = (pad_ends[-1:] // BM).astype(jnp.int32)

    tables = []
    for _, _, _, _, _, ct, rt in per_batch:
        cnt_te = ct[:, 0, :N_EXPERTS].astype(jnp.int32)
        tile_off = jnp.cumsum(cnt_te, axis=1) - cnt_te
        expert_row = pad_starts[None, :] + rt[:, 0, :N_EXPERTS].astype(jnp.int32)
        tables.append((tile_off.reshape(-1), expert_row.reshape(-1), cnt_te.reshape(-1)))

    fill_row = jnp.concatenate([pad_starts + sizes, pad_ends[-1:]]).astype(jnp.int32)
    fill_len = jnp.concatenate([pad_sizes - sizes, nb * BM - pad_ends[-1:]]).astype(jnp.int32)
    both = [jnp.concatenate(cols) for cols in zip(*tables)]
    xs = _dispatch(*both, fill_row, fill_len, per_batch[0][2], per_batch[1][2], nb * BM)
    ys = _experts(block_e, nused, xs, w_gu[l], b_gu[l].reshape(N_EXPERTS, 1, 2 * D_FF),
                  w_down[l], b_down[l].reshape(N_EXPERTS, 1, D_MODEL))
    results = []
    for (shape, h, _, tw, ps, _, _), tbl in zip(per_batch, tables):
        results.append(_combine(*tbl, ps, tw, h, ys).reshape(shape))
    return tuple(results)
```

```python
import functools
import math

import jax
import jax.numpy as jnp
import numpy as np
from jax import lax
from jax.experimental import pallas as pl
from jax.experimental.pallas import tpu as pltpu

F32 = jnp.float32
BF16 = jnp.bfloat16

D_MODEL = 1024
ATTN_GROUPS = ((128, 1), (512, 4), (2048, 16))
HEADS_PER_GROUP = 8
N_HEADS_A = 24
HEAD_DIM_A = 64
W_A = N_HEADS_A * HEAD_DIM_A
GROUP_W = HEADS_PER_GROUP * HEAD_DIM_A
N_SIDE = 64
N_HEADS_B = 4
DK_B = 128
HGRN_CHUNK = 64
HGRN_SUB = 16
HGRN_EXP_CLAMP = 80.0
HGRN_CHUNKS_PER_TRIP = 2
D_B = N_HEADS_B * DK_B
D_IN = 3 * W_A + 5 * D_B
N_EXPERTS = 32
TOP_K = 4
D_FF = 1024
SWIGLU_LIMIT = 7.0
SWIGLU_ALPHA = 1.702
EPS = 1e-6
NEG = -1e30
LOG2E = math.log2(math.e)
LN2 = math.log(2.0)

LANES = 128
VMEM_LIMIT = 56 * 1024 * 1024

TM_IN = 256
PIECE = 512
TQ = 2048
SQ = 128
ATTN_PAIRS_IN_FLIGHT = 1
TS_HGRN = 512
TM_MIX = 256
MIX_TILES_PER_STEP = 2
BM = 512
EXPERT_FF_CHUNK = 256
SUBLANES = 8
ROW_SUB = D_MODEL // LANES
assert ROW_SUB == SUBLANES
DISPATCH_SLOTS = 3
SEG_CHUNK = 16


def _cparams(sem):
    return pltpu.CompilerParams(dimension_semantics=sem, vmem_limit_bytes=VMEM_LIMIT)


def _lockstep(*gens):
    out = [None] * len(gens)
    live = list(range(len(gens)))
    while live:
        for g in list(live):
            try:
                next(gens[g])
            except StopIteration as done:
                out[g] = done.value
                live.remove(g)
    return out


def _split_bf16(x):
    hi = x.astype(BF16)
    lo = (x - hi.astype(F32)).astype(BF16)
    return hi, lo


def _inproj_kernel(x_ref, nw_ref, w_ref, qg_ref, kg_ref, e_ref, lbf_ref, lbb_ref,
                   q0_ref, q1_ref, q2_ref, k0_ref, k1_ref, k2_ref, v0_ref, v1_ref, v2_ref,
                   qb_ref, lff_ref, lfb_ref, ib_ref, og_ref, gt_ref, scr_ref, *, tm):
    q_refs, k_refs, v_refs = (q0_ref, q1_ref, q2_ref), (k0_ref, k1_ref, k2_ref), (v0_ref, v1_ref, v2_ref)
    x = x_ref[...]
    ms = jnp.mean(x * x, axis=-1, keepdims=True)
    xn = (x * lax.rsqrt(ms + EPS) * nw_ref[...]).astype(BF16)

    def proj(col):
        return jnp.dot(xn, w_ref[:, col:col + PIECE], preferred_element_type=F32)

    def normed(dst_ref, col, gain, dil):
        y = proj(col)
        yield
        ss = jnp.dot((y * y).astype(BF16), e_ref[...], preferred_element_type=F32)
        yield
        store_group(dst_ref, y * lax.rsqrt(ss * (1.0 / HEAD_DIM_A) + EPS) * gain, dil)

    def plain(dst_ref, col, dil):
        y = proj(col)
        yield
        yield
        store_group(dst_ref, y, dil)

    def store_group(dst_ref, y, dil):
        if dil == 1:
            dst_ref[0, 0] = y.astype(BF16)
            return
        for c in range(GROUP_W // LANES):
            scr_ref[c] = y[:, c * LANES:(c + 1) * LANES]
        for r in range(dil):
            rows = pl.ds(r, tm // dil, stride=dil)
            dst_ref[0, r] = jnp.concatenate(
                [scr_ref[c, rows, :] for c in range(GROUP_W // LANES)], axis=-1).astype(BF16)

    for g, (_, dil) in enumerate(ATTN_GROUPS):
        c = g * GROUP_W
        _lockstep(normed(q_refs[g], c, qg_ref[:, c:c + GROUP_W], dil),
                  normed(k_refs[g], W_A + c, kg_ref[:, c:c + GROUP_W], dil),
                  plain(v_refs[g], 2 * W_A + c, dil))
    base = 3 * W_A
    qb = proj(base)
    qb_ref[...] = (qb * jax.nn.sigmoid(qb) * (DK_B ** -0.5)).astype(BF16)
    for dst, lb_ref, off in ((lff_ref, lbf_ref, D_B), (lfb_ref, lbb_ref, 2 * D_B)):
        lb = lb_ref[...]
        f = lb + (1.0 - lb) * jax.nn.sigmoid(proj(base + off))
        dst[...] = jnp.log(f)
    ib_ref[...] = proj(base + 3 * D_B).astype(BF16)
    og = proj(base + 4 * D_B)
    og_ref[...] = (og * jax.nn.sigmoid(og)).astype(BF16)
    for p in range(2 * D_MODEL // PIECE):
        c = p * PIECE
        gt_ref[:, c:c + PIECE] = jax.nn.sigmoid(proj(D_IN + c)).astype(BF16)


def _inproj(x2, b, s, nw, w_all, qg, kg, e_mat, lbf, lbb):
    t = x2.shape[0]
    tm = TM_IN
    tps = s // tm
    n_all = w_all.shape[1]
    row = lambda i: (i, 0)
    const = lambda i: (0, 0)
    seq = lambda i: (i // tps, 0, i % tps, 0)
    widths = (D_B, D_B, D_B, D_B, D_B, 2 * D_MODEL)
    dtypes = (BF16, F32, F32, BF16, BF16, BF16)
    grp_shapes = [jax.ShapeDtypeStruct((b, dil, s // dil, GROUP_W), BF16) for _, dil in ATTN_GROUPS] * 3
    grp_specs = [pl.BlockSpec((1, dil, tm // dil, GROUP_W), seq) for _, dil in ATTN_GROUPS] * 3
    return pl.pallas_call(
        functools.partial(_inproj_kernel, tm=tm),
        out_shape=tuple(grp_shapes) + tuple(jax.ShapeDtypeStruct((t, w), dt) for w, dt in zip(widths, dtypes)),
        grid=(t // tm,),
        in_specs=[
            pl.BlockSpec((tm, D_MODEL), row),
            pl.BlockSpec((1, D_MODEL), const),
            pl.BlockSpec((D_MODEL, n_all), const),
            pl.BlockSpec((1, W_A), const),
            pl.BlockSpec((1, W_A), const),
            pl.BlockSpec((PIECE, PIECE), const),
            pl.BlockSpec((1, D_B), const),
            pl.BlockSpec((1, D_B), const),
        ],
        out_specs=tuple(grp_specs) + tuple(pl.BlockSpec((tm, w), row) for w in widths),
        scratch_shapes=[pltpu.VMEM((GROUP_W // LANES, tm, LANES), F32)],
        compiler_params=_cparams(("parallel",)),
        name="inproj",
    )(x2, nw, w_all, qg, kg, e_mat, lbf, lbb)


def _attn_kernel(q_ref, kp_ref, kc_ref, kn_ref, vp_ref, vc_ref, vn_ref, bias_ref,
                 o_ref, lse_ref, *, tq, sub_len):
    i = pl.program_id(2)
    sq = bias_ref.shape[1] // 2
    nk = sq + 2 * N_SIDE
    lane = lax.broadcasted_iota(jnp.int32, (sq, LANES), 1)
    low = lane < HEAD_DIM_A
    ones = jnp.ones((nk, LANES), BF16)
    zero = jnp.zeros((sq, LANES), BF16)

    def pair(res, kk, vv, j, pr, colbias):
        cols = slice(pr * LANES, (pr + 1) * LANES)
        q2 = q_ref[res, j * sq:(j + 1) * sq, cols]
        k2 = kk[j * sq:j * sq + nk, cols]
        v2 = jnp.concatenate([vv[j * sq:j * sq + nk, cols], ones], axis=1)
        q_st = jnp.concatenate([jnp.where(low, q2, zero), jnp.where(low, zero, q2)], axis=0)
        s = lax.dot_general(q_st, k2, (((1,), (1,)), ((), ())), preferred_element_type=F32)
        yield
        s = s + bias_ref[pr]
        if colbias is not None:
            s = s + colbias
        m = jnp.max(s, axis=-1, keepdims=True)
        p = jnp.exp2(s - m).astype(BF16)
        r = jnp.dot(p, v2, preferred_element_type=F32)
        yield
        o2 = jnp.where(low, r[:sq, :LANES] / r[:sq, LANES:], r[sq:, :LANES] / r[sq:, LANES:])
        o_ref[res, j * sq:(j + 1) * sq, cols] = o2.astype(o_ref.dtype)
        lse = (m + jnp.log2(r[:, LANES:LANES + 1])) * LN2
        lse_ref[res, j * sq:(j + 1) * sq, cols] = jnp.where(low, lse[:sq], lse[sq:])

    n_pairs = HEADS_PER_GROUP // 2
    for res in range(q_ref.shape[0]):
        kk = jnp.concatenate([kp_ref[res], kc_ref[res], kn_ref[res]], axis=0)
        vv = jnp.concatenate([vp_ref[res], vc_ref[res], vn_ref[res]], axis=0)
        n_sub = tq // sq
        for j in range(n_sub):
            colbias = None
            if j in (0, n_sub - 1):
                kpos = i * tq + j * sq - N_SIDE + lax.broadcasted_iota(jnp.int32, (1, nk), 1)
                colbias = jnp.where((kpos >= 0) & (kpos < sub_len), 0.0, NEG).astype(F32)
            for pr in range(0, n_pairs, ATTN_PAIRS_IN_FLIGHT):
                _lockstep(*[pair(res, kk, vv, j, pr + d, colbias) for d in range(ATTN_PAIRS_IN_FLIGHT)])


def _attn_bias(sq, dil, slopes):
    nk = sq + 2 * N_SIDE
    rel = np.arange(nk)[None, :] - N_SIDE - np.arange(sq)[:, None]
    band = np.abs(rel) <= N_SIDE
    alibi = -slopes[:, None, None] * (dil * np.abs(rel)).astype(np.float32)[None]
    bias = np.where(band[None], alibi * LOG2E, NEG).astype(np.float32)
    return jnp.asarray(bias.reshape(HEADS_PER_GROUP // 2, 2 * sq, nk))


def _attention_group(q, k, v, g):
    b, dil, sub_len, _ = q.shape
    tq = min(TQ, sub_len)
    sq = min(SQ, sub_len)
    hb = tq // N_SIDE
    n_halo = sub_len // N_SIDE
    slopes = (2.0 ** (-8.0 * (np.arange(N_HEADS_A) + 1) / N_HEADS_A)).astype(np.float32)
    bias = _attn_bias(sq, dil, slopes[g * HEADS_PER_GROUP:(g + 1) * HEADS_PER_GROUP])
    cur = lambda bi, r, i: (bi, r, i, 0)
    prev = lambda bi, r, i: (bi, r, jnp.maximum(i * hb - 1, 0), 0)
    nxt = lambda bi, r, i: (bi, r, jnp.minimum((i + 1) * hb, n_halo - 1), 0)
    n_res = max(1, min(dil, TQ // tq))
    blk_q = (None, n_res, tq, GROUP_W)
    blk_h = (None, n_res, N_SIDE, GROUP_W)
    return pl.pallas_call(
        functools.partial(_attn_kernel, tq=tq, sub_len=sub_len),
        out_shape=(jax.ShapeDtypeStruct((b, dil, sub_len, GROUP_W), BF16),
                   jax.ShapeDtypeStruct((b, dil, sub_len, GROUP_W), F32)),
        grid=(b, dil // n_res, sub_len // tq),
        in_specs=[
            pl.BlockSpec(blk_q, cur),
            pl.BlockSpec(blk_h, prev), pl.BlockSpec(blk_q, cur), pl.BlockSpec(blk_h, nxt),
            pl.BlockSpec(blk_h, prev), pl.BlockSpec(blk_q, cur), pl.BlockSpec(blk_h, nxt),
            pl.BlockSpec(bias.shape, lambda bi, r, i: (0, 0, 0)),
        ],
        out_specs=(pl.BlockSpec(blk_q, cur), pl.BlockSpec(blk_q, cur)),
        compiler_params=_cparams(("parallel", "parallel", "parallel")),
        name=f"attn_d{dil}",
    )(q, k, k, k, v, v, v, bias)


def _hgrn_kernel(qf_ref, lf_ref, vf_ref, qr_ref, lr_ref, vr_ref, of_ref, or_ref,
                 sf_ref, sr_ref, qif_ref, qir_ref, uf_ref, ur_ref, df_ref, dr_ref, *, ts):
    c_len = HGRN_CHUNK
    nc = ts // c_len

    @pl.when(pl.program_id(1) == 0)
    def _():
        sf_ref[...] = jnp.zeros_like(sf_ref)
        sr_ref[...] = jnp.zeros_like(sr_ref)

    r_i = lax.broadcasted_iota(jnp.int32, (c_len, c_len), 0)
    c_i = lax.broadcasted_iota(jnp.int32, (c_len, c_len), 1)
    lower = r_i >= c_i
    upper = r_i <= c_i
    tri_f = jnp.where(lower, 1.0, 0.0).astype(BF16)
    tri_r = jnp.where(upper, 1.0, 0.0).astype(BF16)

    def cumsum(tri, x):
        hi = x.astype(BF16)
        r1 = x - hi.astype(F32)
        mid = r1.astype(BF16)
        lo = (r1 - mid.astype(F32)).astype(BF16)
        d = lambda a: jnp.dot(tri, a, preferred_element_type=F32)
        return d(hi) + d(mid) + d(lo)

    n_sub = c_len // HGRN_SUB
    shift = HGRN_SUB.bit_length() - 1
    sub_f = jnp.right_shift(lax.broadcasted_iota(jnp.int32, (c_len, D_B), 0), shift)
    sub_tf = jnp.right_shift(lax.broadcasted_iota(jnp.int32, (D_B, c_len), 1), shift)
    t_i = lax.broadcasted_iota(jnp.int32, (2 * c_len, c_len), 0) & (c_len - 1)
    s_i = lax.broadcasted_iota(jnp.int32, (2 * c_len, c_len), 1)
    same_sub2 = jnp.right_shift(t_i, shift) == jnp.right_shift(s_i, shift)
    lower2, upper2 = t_i >= s_i, t_i <= s_i
    lower_sub2, upper_sub2 = same_sub2 & lower2, same_sub2 & upper2
    row_head = lax.broadcasted_iota(jnp.int32, (2 * c_len, 2 * DK_B), 0) // c_len
    col_head = lax.broadcasted_iota(jnp.int32, (2 * c_len, 2 * DK_B), 1) // DK_B
    pair_cols = jnp.where(row_head == col_head, 1.0, 0.0).astype(BF16)
    pair_off = jnp.concatenate([pair_cols] * (n_sub - 1), axis=1)

    def intra(q_ref, l_ref, v_ref, o_ref, qi_ref, u_ref, d_ref, c, rev):
        rows = pl.ds(pl.multiple_of(c * c_len, c_len), c_len)
        lf = l_ref[0, rows, :]
        q = q_ref[0, rows, :].astype(F32)
        v = v_ref[0, rows, :]
        k = 1.0 - jnp.exp(lf)
        b = cumsum(tri_r if rev else tri_f, lf)
        yield
        sub = (n_sub - 1 - sub_f) if rev else sub_f

        def b_at(pos):
            r = c_len - 1 - pos if rev else pos
            return b[r:r + 1, :]

        def per_sub(vals):
            vals = vals[::-1] if rev else vals
            return jnp.concatenate([jnp.broadcast_to(x, (HGRN_SUB, D_B)) for x in vals], axis=0)

        a_end = [b_at(HGRN_SUB * j + HGRN_SUB - 1) for j in range(n_sub)]
        a_start = [jnp.zeros((1, D_B), F32)] + a_end[:-1]
        btot = a_end[-1]
        end_full, start_full = per_sub(a_end), per_sub(a_start)
        stores = [(qi_ref, (rows, slice(None)), (q * jnp.exp(b)).astype(BF16))]
        q_off = [jnp.where(sub > j, q * jnp.exp(jnp.minimum(b - a_end[j], 0.0)), 0.0).astype(BF16)
                 for j in range(n_sub - 1)]
        k_end = k * jnp.exp(end_full - b)
        sub_t = (n_sub - 1 - sub_tf) if rev else sub_tf
        q_dia = (q * jnp.exp(b - start_full)).astype(BF16)
        k_upd = (k * jnp.exp(btot - b)).astype(BF16)
        k_end_t = k_end.T
        k_dia_t = (k * jnp.exp(jnp.minimum(start_full - b, HGRN_EXP_CLAMP))).T.astype(BF16)
        k_off_t = [jnp.where(sub_t == j, k_end_t, 0.0).astype(BF16) for j in range(n_sub - 1)]
        dec8_t = jnp.broadcast_to(jnp.exp(btot), (SUBLANES, D_B)).T
        keep, keep_d = (upper2, upper_sub2) if rev else (lower2, lower_sub2)
        two = lambda a: jnp.concatenate([a, a], axis=0)
        for p in range(N_HEADS_B // 2):
            ps = slice(2 * p * DK_B, (2 * p + 2) * DK_B)
            qc = two(jnp.concatenate([q_off[j][:, ps] for j in range(n_sub - 1)], axis=1)) * pair_off
            kc = jnp.concatenate([k_off_t[j][ps, :] for j in range(n_sub - 1)], axis=0)
            yield
            sc = jnp.dot(qc, kc, preferred_element_type=F32)
            sc_d = jnp.dot(two(q_dia[:, ps]) * pair_cols, k_dia_t[ps, :], preferred_element_type=F32)
            u2 = lax.dot_general(k_upd[:, ps], v[:, ps], (((0,), (0,)), ((), ())),
                                 preferred_element_type=F32)
            yield
            sc = (jnp.where(keep, sc, 0.0) + jnp.where(keep_d, sc_d, 0.0)).astype(BF16)
            o2 = jnp.dot(sc, v[:, ps], preferred_element_type=F32)
            for i in range(2):
                h = 2 * p + i
                hs = slice(h * DK_B, (h + 1) * DK_B)
                blk = slice(i * DK_B, (i + 1) * DK_B)
                stores.append((o_ref, (0, rows, hs), o2[i * c_len:(i + 1) * c_len, blk]))
                stores.append((u_ref, (c, h), u2[blk, blk]))
                stores.append((d_ref, (c, h), jnp.broadcast_to(dec8_t[hs, 0:1], (DK_B, DK_B))))
        return stores

    def carry_state(o_ref, qi_ref, u_ref, d_ref, s_ref, c):
        rows = pl.ds(pl.multiple_of(c * c_len, c_len), c_len)
        zero = jnp.zeros((DK_B, DK_B), BF16)
        stores = []
        for p in range(N_HEADS_B // 2):
            ps = slice(2 * p * DK_B, (2 * p + 2) * DK_B)
            sa, sb = s_ref[2 * p], s_ref[2 * p + 1]
            s_bd = jnp.concatenate([jnp.concatenate([sa.astype(BF16), zero], axis=1),
                                    jnp.concatenate([zero, sb.astype(BF16)], axis=1)], axis=0)
            o_new = o_ref[0, rows, ps] + jnp.dot(qi_ref[rows, ps], s_bd, preferred_element_type=F32)
            stores.append((o_ref, (0, rows, ps), o_new))
            stores.append((s_ref, (2 * p,), sa * d_ref[c, 2 * p] + u_ref[c, 2 * p]))
            stores.append((s_ref, (2 * p + 1,), sb * d_ref[c, 2 * p + 1] + u_ref[c, 2 * p + 1]))
        return stores

    def commit(stores):
        for ref, idx, val in stores:
            ref[idx] = val

    def intra_body(trip, carry):
        gens = []
        for i in range(HGRN_CHUNKS_PER_TRIP):
            c = trip * HGRN_CHUNKS_PER_TRIP + i
            gens.append(intra(qf_ref, lf_ref, vf_ref, of_ref, qif_ref, uf_ref, df_ref, c, False))
            gens.append(intra(qr_ref, lr_ref, vr_ref, or_ref, qir_ref, ur_ref, dr_ref, c, True))
        commit(sum(_lockstep(*gens), []))
        return carry

    def state_body(c, carry):
        commit(carry_state(of_ref, qif_ref, uf_ref, df_ref, sf_ref, c)
               + carry_state(or_ref, qir_ref, ur_ref, dr_ref, sr_ref, nc - 1 - c))
        return carry

    lax.fori_loop(0, nc // HGRN_CHUNKS_PER_TRIP, intra_body, 0)
    lax.fori_loop(0, nc, state_body, 0)


def _hgrn(qb, lff, lfb, ib):
    b, s, _ = qb.shape
    ts = min(TS_HGRN, s)
    nt = s // ts
    nc = ts // HGRN_CHUNK
    fwd = lambda bi, j: (bi, j, 0)
    rev = lambda bi, j: (bi, nt - 1 - j, 0)
    blk = (1, ts, D_B)
    return pl.pallas_call(
        functools.partial(_hgrn_kernel, ts=ts),
        out_shape=(jax.ShapeDtypeStruct((b, s, D_B), F32), jax.ShapeDtypeStruct((b, s, D_B), F32)),
        grid=(b, nt),
        in_specs=[pl.BlockSpec(blk, fwd), pl.BlockSpec(blk, fwd), pl.BlockSpec(blk, fwd),
                  pl.BlockSpec(blk, rev), pl.BlockSpec(blk, rev), pl.BlockSpec(blk, rev)],
        out_specs=(pl.BlockSpec(blk, fwd), pl.BlockSpec(blk, rev)),
        scratch_shapes=[pltpu.VMEM((N_HEADS_B, DK_B, DK_B), F32),
                        pltpu.VMEM((N_HEADS_B, DK_B, DK_B), F32),
                        pltpu.VMEM((ts, D_B), BF16),
                        pltpu.VMEM((ts, D_B), BF16),
                        pltpu.VMEM((nc, N_HEADS_B, DK_B, DK_B), F32),
                        pltpu.VMEM((nc, N_HEADS_B, DK_B, DK_B), F32),
                        pltpu.VMEM((nc, N_HEADS_B, DK_B, DK_B), F32),
                        pltpu.VMEM((nc, N_HEADS_B, DK_B, DK_B), F32)],
        compiler_params=_cparams(("parallel", "arbitrary")),
        name="hgrn",
    )(qb, lff, ib, qb, lfb, ib)


def _mix_kernel(x_ref, o1_ref, o2_ref, o3_ref, l1_ref, l2_ref, l3_ref, of_ref, ob_ref, og_ref,
                gt_ref, wa_ref, wb_ref, wo_ref, ogain_ref, nmoe_ref, wrh_ref, wrl_ref,
                br_ref, cnt_ref,
                h_ref, xt_ref, tw_ref, ps_ref, ct_ref, rt_ref, cnt_out_ref, run_ref, so_ref, *, tm):
    i = pl.program_id(0)

    @pl.when(i == 0)
    def _():
        run_ref[...] = cnt_ref[...]

    def token_major(src_ref, scr_ref, dil):
        if dil == 1:
            return src_ref[0].astype(F32)
        n_chunk = scr_ref.shape[0]
        for r in range(dil):
            blk = src_ref[r].astype(F32)
            for c in range(n_chunk):
                scr_ref[c, pl.ds(r, tm // dil, stride=dil), :] = blk[:, c * LANES:(c + 1) * LANES]
        return jnp.concatenate([scr_ref[c] for c in range(n_chunk)], axis=-1)

    dils = [dil for _, dil in ATTN_GROUPS]
    l1, l2, l3 = [token_major(r, so_ref, d) for r, d in zip((l1_ref, l2_ref, l3_ref), dils)]
    mx = jnp.maximum(jnp.maximum(l1, l2), l3)
    e1, e2, e3 = jnp.exp(l1 - mx), jnp.exp(l2 - mx), jnp.exp(l3 - mx)
    attn = (e1 * token_major(o1_ref, so_ref, dils[0]) + e2 * token_major(o2_ref, so_ref, dils[1])
            + e3 * token_major(o3_ref, so_ref, dils[2])) / (e1 + e2 + e3)

    o = of_ref[...] + ob_ref[...]
    parts = []
    for h in range(N_HEADS_B):
        oh = o[:, h * DK_B:(h + 1) * DK_B]
        ms = jnp.mean(oh * oh, axis=-1, keepdims=True)
        parts.append(oh * lax.rsqrt(ms + EPS))
    hg = jnp.concatenate(parts, axis=-1) * ogain_ref[...] * og_ref[...].astype(F32)

    attn_bf, hg_bf = attn.astype(BF16), hg.astype(BF16)
    tp = TM_MIX
    lane = lax.broadcasted_iota(jnp.int32, (tp, LANES), 1)
    r_i = lax.broadcasted_iota(jnp.int32, (tp, tp), 0)
    c_i = lax.broadcasted_iota(jnp.int32, (tp, tp), 1)
    tri = jnp.where(r_i > c_i, 1.0, 0.0).astype(BF16)
    e_r = lax.broadcasted_iota(jnp.int32, (LANES, LANES), 0)
    e_c = lax.broadcasted_iota(jnp.int32, (LANES, LANES), 1)
    before_e = jnp.where(e_r < e_c, 1.0, 0.0).astype(BF16)
    row_id = lax.broadcasted_iota(jnp.int32, (tp * TOP_K, tp), 0).astype(F32)

    def token_chain(part):
        rs = slice(part * tp, (part + 1) * tp)
        pa = jnp.dot(attn_bf[rs], wa_ref[...], preferred_element_type=F32)
        pb = jnp.dot(hg_bf[rs], wb_ref[...], preferred_element_type=F32)
        yield
        mixed = (gt_ref[rs, :D_MODEL].astype(F32) * pa + gt_ref[rs, D_MODEL:].astype(F32) * pb)
        h = x_ref[rs, :] + jnp.dot(mixed.astype(BF16), wo_ref[...], preferred_element_type=F32)
        yield
        h_ref[rs, :] = h
        ms = jnp.mean(h * h, axis=-1, keepdims=True)
        hn = h * lax.rsqrt(ms + EPS) * nmoe_ref[...]
        hi, lo = _split_bf16(hn)
        lg = (jnp.dot(hi, wrh_ref[...], preferred_element_type=F32)
              + jnp.dot(lo, wrh_ref[...], preferred_element_type=F32)
              + jnp.dot(hi, wrl_ref[...], preferred_element_type=F32)) + br_ref[...]
        yield
        vals, idxs = [], []
        onehot = jnp.zeros((tp, LANES), F32)
        for _ in range(TOP_K):
            m = jnp.max(lg, axis=-1, keepdims=True)
            idx = jnp.min(jnp.where(lg == m, lane, LANES), axis=-1, keepdims=True)
            sel = lane == idx
            onehot = jnp.where(sel, 1.0, onehot)
            lg = jnp.where(sel, NEG * 2, lg)
            vals.append(m)
            idxs.append(idx)
        exps = [jnp.exp(v - vals[0]) for v in vals]
        inv = 1.0 / (exps[0] + exps[1] + exps[2] + exps[3])
        local = jnp.dot(tri, onehot.astype(BF16), preferred_element_type=F32)
        cnt_tile = jnp.sum(onehot, axis=0, keepdims=True)
        off = jnp.dot(jnp.broadcast_to(cnt_tile, (8, LANES)).astype(BF16), before_e,
                      preferred_element_type=F32)[0:1]
        yield
        slot = off + local
        tw = jnp.zeros((tp, LANES), F32)
        ps = jnp.full((tp, LANES), -1.0, F32)
        for k in range(TOP_K):
            slot_k = jnp.sum(jnp.where(lane == idxs[k], slot, 0.0), axis=-1, keepdims=True)
            tw = jnp.where(lane == k, exps[k] * inv, tw)
            ps = jnp.where(lane == k, slot_k, ps)
        tw_ref[rs, :] = tw
        ps_ref[rs, :] = ps.astype(jnp.int32)
        ps_t = ps.T
        perm = jnp.zeros((tp * TOP_K, tp), F32)
        for k in range(TOP_K):
            perm = jnp.where(row_id == ps_t[k:k + 1, :], 1.0, perm)
        rows = jnp.dot(perm.astype(BF16), hi, preferred_element_type=F32)
        yield
        for c in range(D_MODEL // LANES):
            xt_ref[pl.ds(part * tp * TOP_K * ROW_SUB + c, tp * TOP_K, stride=ROW_SUB), :] = (
                rows[:, c * LANES:(c + 1) * LANES])
        return cnt_tile

    counts = _lockstep(*[token_chain(p) for p in range(tm // tp)])
    run = run_ref[...]
    for part, cnt_tile in enumerate(counts):
        ct_ref[part] = cnt_tile
        rt_ref[part] = run
        run = run + cnt_tile
    run_ref[...] = run
    cnt_out_ref[...] = run


def _mix(x2, s, o1, o2, o3, l1, l2, l3, of, ob, og, gt, wa, wb, wo, ogain, nmoe, wrh, wrl, br, cnt):
    t = x2.shape[0]
    tm = TM_MIX * MIX_TILES_PER_STEP
    tps = s // tm
    row = lambda i: (i, 0)
    const = lambda i: (0, 0)
    seq = lambda i: (i // tps, 0, i % tps, 0)
    rb = lambda w: pl.BlockSpec((tm, w), row)
    cb = lambda a: pl.BlockSpec(a.shape, const)
    gb = lambda a: pl.BlockSpec((None, a.shape[1], tm // a.shape[1], a.shape[3]), seq)
    per_tile = pl.BlockSpec((MIX_TILES_PER_STEP, 1, LANES), lambda i: (i, 0, 0))
    return pl.pallas_call(
        functools.partial(_mix_kernel, tm=tm),
        out_shape=(jax.ShapeDtypeStruct((t, D_MODEL), F32),
                   jax.ShapeDtypeStruct((t * TOP_K * ROW_SUB, LANES), F32),
                   jax.ShapeDtypeStruct((t, LANES), F32),
                   jax.ShapeDtypeStruct((t, LANES), jnp.int32),
                   jax.ShapeDtypeStruct((t // TM_MIX, 1, LANES), F32),
                   jax.ShapeDtypeStruct((t // TM_MIX, 1, LANES), F32),
                   jax.ShapeDtypeStruct((1, LANES), F32)),
        grid=(t // tm,),
        in_specs=[rb(D_MODEL), gb(o1), gb(o2), gb(o3), gb(l1), gb(l2), gb(l3),
                  rb(D_B), rb(D_B), rb(D_B), rb(2 * D_MODEL),
                  cb(wa), cb(wb), cb(wo), cb(ogain), cb(nmoe), cb(wrh), cb(wrl), cb(br), cb(cnt)],
        out_specs=(rb(D_MODEL), pl.BlockSpec((tm * TOP_K * ROW_SUB, LANES), row), rb(LANES), rb(LANES),
                   per_tile, per_tile, pl.BlockSpec((1, LANES), const)),
        scratch_shapes=[pltpu.VMEM((1, LANES), F32), pltpu.VMEM((GROUP_W // LANES, tm, LANES), F32)],
        compiler_params=_cparams(("arbitrary",)),
        name="mix",
    )(x2, o1, o2, o3, l1, l2, l3, of, ob, og, gt, wa, wb, wo, ogain, nmoe, wrh, wrl, br, cnt)


def _start_tile_segments(tile, off_tbl, row_tbl, len_tbl, make_piece):
    def segment(e, carry):
        sidx = tile * N_EXPERTS + e
        t0, r0, n = off_tbl[sidx], row_tbl[sidx], len_tbl[sidx]
        n_bulk = jnp.right_shift(n, SEG_CHUNK.bit_length() - 1)

        def bulk(j, c):
            make_piece(t0 + j * SEG_CHUNK, r0 + j * SEG_CHUNK, SEG_CHUNK).start()
            return c

        lax.fori_loop(0, n_bulk, bulk, 0)
        done = n_bulk * SEG_CHUNK
        bit = SEG_CHUNK // 2
        while bit >= 1:
            has = (n & bit) != 0

            @pl.when(has)
            def _(done=done, bit=bit):
                make_piece(t0 + done, r0 + done, bit).start()

            done = done + jnp.where(has, bit, 0)
            bit //= 2
        return carry

    lax.fori_loop(0, N_EXPERTS, segment, 0)


def _row_slice(row, n_rows):
    return pl.ds(pl.multiple_of(row * ROW_SUB, ROW_SUB), n_rows * ROW_SUB)


def _dispatch_kernel(off_tbl, row_tbl, len_tbl, fill_row, fill_len, xa_ref, xb_ref, xs_ref, buf_ref,
                     zero_ref, sems_in, sems_out, sem, *, tiles_a):
    i = pl.program_id(0)
    n_tiles = pl.num_programs(0)
    blk_rows = buf_ref.shape[1]

    def fetch(tile):
        slot = tile % DISPATCH_SLOTS

        def copy(src_ref, src_tile):
            start = pl.multiple_of(src_tile * blk_rows, blk_rows)
            pltpu.make_async_copy(src_ref.at[pl.ds(start, blk_rows)], buf_ref.at[slot], sems_in.at[slot]).start()

        @pl.when(tile < tiles_a)
        def _():
            copy(xa_ref, tile)

        @pl.when(tile >= tiles_a)
        def _():
            copy(xb_ref, tile - tiles_a)

    def scatter_done(tile):
        slot = tile % DISPATCH_SLOTS
        pltpu.make_async_copy(buf_ref.at[slot], xs_ref.at[pl.ds(0, blk_rows)], sems_out.at[slot]).wait()

    def zeros_to(row, n_rows):
        return pltpu.make_async_copy(zero_ref.at[pl.ds(0, n_rows * ROW_SUB)],
                                     xs_ref.at[_row_slice(row, n_rows)], sem)

    @pl.when(i == 0)
    def _():
        fetch(i)
        zero_ref[...] = jnp.zeros_like(zero_ref)

        def fill(e, total):
            r0, n = fill_row[e], fill_len[e]
            n_blk = jnp.right_shift(n, BM.bit_length() - 1)

            def blocks(j, c):
                zeros_to(r0 + j * BM, BM).start()
                return c

            lax.fori_loop(0, n_blk, blocks, 0)
            done = n_blk * BM
            bit = BM // 2
            while bit >= 1:
                has = (n & bit) != 0

                @pl.when(has)
                def _(done=done, bit=bit):
                    zeros_to(r0 + done, bit).start()

                done = done + jnp.where(has, bit, 0)
                bit //= 2
            return total + n

        total = lax.fori_loop(0, N_EXPERTS + 1, fill, 0)
        bit = 1
        while bit * ROW_SUB <= xs_ref.shape[0]:
            @pl.when((total & bit) != 0)
            def _(bit=bit):
                n = bit * ROW_SUB
                pltpu.make_async_copy(xs_ref.at[pl.ds(0, n)], xs_ref.at[pl.ds(0, n)], sem).wait()

            bit *= 2

    @pl.when(i + 1 < n_tiles)
    def _():
        fetch(i + 1)

    slot = i % DISPATCH_SLOTS
    pltpu.make_async_copy(xa_ref.at[pl.ds(0, blk_rows)], buf_ref.at[slot], sems_in.at[slot]).wait()

    def piece(tile_row, expert_row, n_rows):
        return pltpu.make_async_copy(buf_ref.at[slot, _row_slice(tile_row, n_rows)],
                                     xs_ref.at[_row_slice(expert_row, n_rows)], sems_out.at[slot])

    _start_tile_segments(i, off_tbl, row_tbl, len_tbl, piece)

    @pl.when(i >= 1)
    def _():
        scatter_done(i - 1)

    @pl.when(i == n_tiles - 1)
    def _():
        scatter_done(i)


def _dispatch(off_tbl, row_tbl, len_tbl, fill_row, fill_len, xt_a, xt_b, n_rows):
    blk = TM_MIX * TOP_K * ROW_SUB
    tiles_a, tiles_b = xt_a.shape[0] // blk, xt_b.shape[0] // blk
    return pl.pallas_call(
        functools.partial(_dispatch_kernel, tiles_a=tiles_a),
        out_shape=jax.ShapeDtypeStruct((n_rows * ROW_SUB, LANES), F32),
        grid_spec=pltpu.PrefetchScalarGridSpec(
            num_scalar_prefetch=5,
            grid=(tiles_a + tiles_b,),
            in_specs=[pl.BlockSpec(memory_space=pl.ANY), pl.BlockSpec(memory_space=pl.ANY)],
            out_specs=pl.BlockSpec(memory_space=pl.ANY),
            scratch_shapes=[pltpu.VMEM((DISPATCH_SLOTS, blk, LANES), F32),
                            pltpu.VMEM((BM * ROW_SUB, LANES), F32),
                            pltpu.SemaphoreType.DMA((DISPATCH_SLOTS,)),
                            pltpu.SemaphoreType.DMA((DISPATCH_SLOTS,)),
                            pltpu.SemaphoreType.DMA],
        ),
        compiler_params=_cparams(("arbitrary",)),
        name="dispatch",
    )(off_tbl, row_tbl, len_tbl, fill_row, fill_len, xt_a, xt_b)


def _combine_kernel(off_tbl, row_tbl, len_tbl, ps_ref, tw_ref, h_ref, ys_ref, y_ref, buf_ref, sems, *, tm):
    i = pl.program_id(0)
    n_rows = tm * TOP_K

    def start(tile):
        slot = tile % 2

        def piece(tile_row, expert_row, n):
            return pltpu.make_async_copy(ys_ref.at[_row_slice(expert_row, n)],
                                         buf_ref.at[slot, _row_slice(tile_row, n)], sems.at[slot])

        _start_tile_segments(tile, off_tbl, row_tbl, len_tbl, piece)

    @pl.when(i == 0)
    def _():
        start(i)

    @pl.when(i + 1 < pl.num_programs(0))
    def _():
        start(i + 1)

    slot = i % 2
    yt_ref = buf_ref.at[slot]
    pltpu.make_async_copy(ys_ref.at[pl.ds(0, n_rows * ROW_SUB)], yt_ref, sems.at[slot]).wait()
    ysorted = jnp.concatenate(
        [yt_ref[pl.ds(c, n_rows, stride=ROW_SUB), :] for c in range(ROW_SUB)], axis=-1).astype(BF16)
    col = lax.broadcasted_iota(jnp.int32, (tm, n_rows), 1)
    ps = ps_ref[...]
    tw = tw_ref[...]
    pw = jnp.zeros((tm, n_rows), F32)
    for k in range(TOP_K):
        pw = jnp.where(col == ps[:, k:k + 1], tw[:, k:k + 1], pw)
    hi, lo = _split_bf16(pw)
    y_ref[...] = (h_ref[...] + jnp.dot(hi, ysorted, preferred_element_type=F32)
                  + jnp.dot(lo, ysorted, preferred_element_type=F32))


def _combine(off_tbl, row_tbl, len_tbl, ps, tw, h, ys):
    t = h.shape[0]
    tm = TM_MIX
    row = lambda i, *_: (i, 0)
    return pl.pallas_call(
        functools.partial(_combine_kernel, tm=tm),
        out_shape=jax.ShapeDtypeStruct((t, D_MODEL), F32),
        grid_spec=pltpu.PrefetchScalarGridSpec(
            num_scalar_prefetch=3,
            grid=(t // tm,),
            in_specs=[pl.BlockSpec((tm, LANES), row),
                      pl.BlockSpec((tm, LANES), row),
                      pl.BlockSpec((tm, D_MODEL), row),
                      pl.BlockSpec(memory_space=pl.ANY)],
            out_specs=pl.BlockSpec((tm, D_MODEL), row),
            scratch_shapes=[pltpu.VMEM((2, tm * TOP_K * ROW_SUB, LANES), F32),
                            pltpu.SemaphoreType.DMA((2,))],
        ),
        compiler_params=_cparams(("arbitrary",)),
        name="combine",
    )(off_tbl, row_tbl, len_tbl, ps, tw, h, ys)


def _experts_kernel(be_ref, nused_ref, xs_ref, wgu_ref, bgu_ref, wd_ref, bd_ref, ys_ref, wgu_bf, wd_bf):
    i = pl.program_id(0)
    used = i < nused_ref[0]
    new_expert = (i == 0) | (be_ref[i] != be_ref[jnp.maximum(i - 1, 0)])

    @pl.when(used & new_expert)
    def _():
        wgu_bf[...] = wgu_ref[0].astype(BF16)
        wd_bf[...] = wd_ref[0].astype(BF16)

    @pl.when(used)
    def _():
        x = jnp.concatenate(
            [xs_ref[pl.ds(c, BM, stride=ROW_SUB), :] for c in range(ROW_SUB)], axis=-1).astype(BF16)
        def gate_up(c):
            cs = slice(c * EXPERT_FF_CHUNK, (c + 1) * EXPERT_FF_CHUNK)
            us = slice(D_FF + c * EXPERT_FF_CHUNK, D_FF + (c + 1) * EXPERT_FF_CHUNK)
            return (jnp.dot(x, wgu_bf[:, cs], preferred_element_type=F32) + bgu_ref[0, :, cs],
                    jnp.dot(x, wgu_bf[:, us], preferred_element_type=F32) + bgu_ref[0, :, us])

        n_chunk = D_FF // EXPERT_FF_CHUNK
        y = bd_ref[0]
        ahead = gate_up(0)
        for c in range(n_chunk):
            g, u = ahead
            if c + 1 < n_chunk:
                ahead = gate_up(c + 1)
            gate = jnp.minimum(g, SWIGLU_LIMIT)
            up = jnp.clip(u, -SWIGLU_LIMIT, SWIGLU_LIMIT)
            glu = gate * jax.nn.sigmoid(SWIGLU_ALPHA * gate)
            act = ((up + 1.0) * glu).astype(BF16)
            y = y + jnp.dot(act, wd_bf[c * EXPERT_FF_CHUNK:(c + 1) * EXPERT_FF_CHUNK, :],
                            preferred_element_type=F32)
        for c in range(ROW_SUB):
            ys_ref[pl.ds(c, BM, stride=ROW_SUB), :] = y[:, c * LANES:(c + 1) * LANES]

    @pl.when(i >= nused_ref[0])
    def _():
        ys_ref[...] = jnp.zeros_like(ys_ref)


def _experts(block_e, nused, xs, wgu, bgu, wd, bd):
    nb = xs.shape[0] // (BM * ROW_SUB)
    emap3 = lambda i, be, nu: (be[i], 0, 0)
    rows = pl.BlockSpec((BM * ROW_SUB, LANES), lambda i, be, nu: (i, 0))
    return pl.pallas_call(
        _experts_kernel,
        out_shape=jax.ShapeDtypeStruct(xs.shape, F32),
        grid_spec=pltpu.PrefetchScalarGridSpec(
            num_scalar_prefetch=2,
            grid=(nb,),
            in_specs=[rows,
                      pl.BlockSpec((1, D_MODEL, 2 * D_FF), emap3),
                      pl.BlockSpec((1, 1, 2 * D_FF), emap3),
                      pl.BlockSpec((1, D_FF, D_MODEL), emap3),
                      pl.BlockSpec((1, 1, D_MODEL), emap3)],
            out_specs=rows,
            scratch_shapes=[pltpu.VMEM((D_MODEL, 2 * D_FF), BF16), pltpu.VMEM((D_FF, D_MODEL), BF16)],
        ),
        compiler_params=_cparams(("arbitrary",)),
        name="experts",
    )(block_e, nused, xs, wgu, bgu, wd, bd)


def _head_indicator(n_cols, head_dim):
    e = np.zeros((n_cols, LANES), np.float32)
    e[np.arange(n_cols), np.arange(n_cols) // head_dim] = 1.0
    return e


def _mixer(x, prm):
    b, s, d = x.shape
    t = b * s
    x2 = x.reshape(t, d)
    res = _inproj(x2, b, s, prm["norm_mix"], prm["w_all"], prm["q_gain"], prm["k_gain"], prm["e_in"],
                  prm["lb_f"], prm["lb_b"])
    n_grp = len(ATTN_GROUPS)
    qs, ks, vs = res[:n_grp], res[n_grp:2 * n_grp], res[2 * n_grp:3 * n_grp]
    qb, lff, lfb, ib, og, gt = res[3 * n_grp:]
    r3 = lambda a: a.reshape(b, s, a.shape[-1])
    outs, lses = [], []
    for g in range(n_grp):
        o, lse = _attention_group(qs[g], ks[g], vs[g], g)
        outs.append(o)
        lses.append(lse)
    of, ob = _hgrn(r3(qb), r3(lff), r3(lfb), r3(ib))
    return x2, outs, lses, of.reshape(t, D_B), ob.reshape(t, D_B), og, gt


def kernel(x_prompt, x_sample, norm_mix, w_in, q_gain, k_gain, hgrn_lb, hgrn_o_gain, w_gate, w_proj_a,
           w_proj_b, w_out, norm_moe, w_router, b_router, w_gu, b_gu, w_down, b_down):
    l = 0
    lb = jnp.cumsum(jax.nn.softmax(hgrn_lb.astype(F32), axis=1), axis=1)
    wr = jnp.zeros((D_MODEL, LANES), F32).at[:, :N_EXPERTS].set(w_router[l])
    wr_hi = wr.astype(BF16)
    prm = {
        "norm_mix": norm_mix[l].reshape(1, D_MODEL),
        "w_all": jnp.concatenate([w_in[l], w_gate[l]], axis=1).astype(BF16),
        "q_gain": q_gain[l].reshape(1, W_A) * (HEAD_DIM_A ** -0.5 * LOG2E),
        "k_gain": k_gain[l].reshape(1, W_A),
        "e_in": jnp.asarray(_head_indicator(PIECE, HEAD_DIM_A) @ _head_indicator(PIECE, HEAD_DIM_A).T, BF16),
        "lb_f": lb[0, l].reshape(1, D_B),
        "lb_b": lb[1, l].reshape(1, D_B),
    }
    wa, wb, wo = w_proj_a[l].astype(BF16), w_proj_b[l].astype(BF16), w_out[l].astype(BF16)
    ogain = hgrn_o_gain[l].reshape(1, D_B)
    nmoe = norm_moe[l].reshape(1, D_MODEL)
    wr_lo = (wr - wr_hi.astype(F32)).astype(BF16)
    br = jnp.full((1, LANES), NEG, F32).at[0, :N_EXPERTS].set(b_router[l])

    cnt = jnp.zeros((1, LANES), F32)
    per_batch = []
    for x in (x_prompt, x_sample):
        x2, outs, lses, of, ob, og, gt = _mixer(x, prm)
        h, xt, tw, ps, ct, rt, cnt = _mix(x2, x.shape[1], outs[0], outs[1], outs[2], lses[0], lses[1], lses[2],
                                          of, ob, og, gt, wa, wb, wo, ogain, nmoe, wr_hi, wr_lo, br, cnt)
        per_batch.append((x.shape, h, xt, tw, ps, ct, rt))

    n_tok = sum(pb[1].shape[0] for pb in per_batch)
    sizes = cnt[0, :N_EXPERTS].astype(jnp.int32)
    pad_sizes = (sizes + BM - 1) // BM * BM
    pad_ends = jnp.cumsum(pad_sizes)
    pad_starts = pad_ends - pad_sizes
    nb = (n_tok * TOP_K) // BM + N_EXPERTS
    block_start = jnp.arange(nb, dtype=jnp.int32) * BM
    block_e = jnp.minimum(jnp.sum(pad_ends[None, :] <= block_start[:, None], axis=1),
                          N_EXPERTS - 1).astype(jnp.int32)
    nused = (pad_ends[-1:] // BM).astype(jnp.int32)

    tables = []
    for _, _, _, _, _, ct, rt in per_batch:
        cnt_te = ct[:, 0, :N_EXPERTS].astype(jnp.int32)
        tile_off = jnp.cumsum(cnt_te, axis=1) - cnt_te
        expert_row = pad_starts[None, :] + rt[:, 0, :N_EXPERTS].astype(jnp.int32)
        tables.append((tile_off.reshape(-1), expert_row.reshape(-1), cnt_te.reshape(-1)))

    fill_row = jnp.concatenate([pad_starts + sizes, pad_ends[-1:]]).astype(jnp.int32)
    fill_len = jnp.concatenate([pad_sizes - sizes, nb * BM - pad_ends[-1:]]).astype(jnp.int32)
    both = [jnp.concatenate(cols) for cols in zip(*tables)]
    xs = _dispatch(*both, fill_row, fill_len, per_batch[0][2], per_batch[1][2], nb * BM)
    ys = _experts(block_e, nused, xs, w_gu[l], b_gu[l].reshape(N_EXPERTS, 1, 2 * D_FF),
                  w_down[l], b_down[l].reshape(N_EXPERTS, 1, D_MODEL))
    results = []
    for (shape, h, _, tw, ps, _, _), tbl in zip(per_batch, tables):
        results.append(_combine(*tbl, ps, tw, h, ys).reshape(shape))
    return tuple(results)
```

```python
import functools
import math

import jax
import jax.numpy as jnp
import numpy as np
from jax import lax
from jax.experimental import pallas as pl
from jax.experimental.pallas import tpu as pltpu

F32 = jnp.float32
BF16 = jnp.bfloat16

D_MODEL = 1024
ATTN_GROUPS = ((128, 1), (512, 4), (2048, 16))
HEADS_PER_GROUP = 8
N_HEADS_A = 24
HEAD_DIM_A = 64
W_A = N_HEADS_A * HEAD_DIM_A
GROUP_W = HEADS_PER_GROUP * HEAD_DIM_A
N_SIDE = 64
N_HEADS_B = 4
DK_B = 128
HGRN_CHUNK = 64
HGRN_SUB = 16
HGRN_EXP_CLAMP = 80.0
HGRN_CHUNKS_PER_TRIP = 2
D_B = N_HEADS_B * DK_B
D_IN = 3 * W_A + 5 * D_B
N_EXPERTS = 32
TOP_K = 4
D_FF = 1024
SWIGLU_LIMIT = 7.0
SWIGLU_ALPHA = 1.702
EPS = 1e-6
NEG = -1e30
LOG2E = math.log2(math.e)
LN2 = math.log(2.0)

LANES = 128
VMEM_LIMIT = 56 * 1024 * 1024

TM_IN = 256
PIECE = 512
TQ = 2048
SQ = 128
ATTN_PAIRS_IN_FLIGHT = 1
TS_HGRN = 512
TM_MIX = 256
MIX_TILES_PER_STEP = 2
BM = 512
SUBLANES = 8
ROW_SUB = D_MODEL // LANES
assert ROW_SUB == SUBLANES
DISPATCH_SLOTS = 3
SEG_CHUNK = 16


def _cparams(sem):
    return pltpu.CompilerParams(dimension_semantics=sem, vmem_limit_bytes=VMEM_LIMIT)


def _lockstep(*gens):
    out = [None] * len(gens)
    live = list(range(len(gens)))
    while live:
        for g in list(live):
            try:
                next(gens[g])
            except StopIteration as done:
                out[g] = done.value
                live.remove(g)
    return out


def _split_bf16(x):
    hi = x.astype(BF16)
    lo = (x - hi.astype(F32)).astype(BF16)
    return hi, lo


def _inproj_kernel(x_ref, nw_ref, w_ref, qg_ref, kg_ref, e_ref, lbf_ref, lbb_ref,
                   q0_ref, q1_ref, q2_ref, k0_ref, k1_ref, k2_ref, v0_ref, v1_ref, v2_ref,
                   qb_ref, lff_ref, lfb_ref, ib_ref, og_ref, gt_ref, scr_ref, *, tm):
    q_refs, k_refs, v_refs = (q0_ref, q1_ref, q2_ref), (k0_ref, k1_ref, k2_ref), (v0_ref, v1_ref, v2_ref)
    x = x_ref[...]
    ms = jnp.mean(x * x, axis=-1, keepdims=True)
    xn = (x * lax.rsqrt(ms + EPS) * nw_ref[...]).astype(BF16)

    def proj(col):
        return jnp.dot(xn, w_ref[:, col:col + PIECE], preferred_element_type=F32)

    def normed(dst_ref, col, gain, dil):
        y = proj(col)
        yield
        ss = jnp.dot((y * y).astype(BF16), e_ref[...], preferred_element_type=F32)
        yield
        store_group(dst_ref, y * lax.rsqrt(ss * (1.0 / HEAD_DIM_A) + EPS) * gain, dil)

    def plain(dst_ref, col, dil):
        y = proj(col)
        yield
        yield
        store_group(dst_ref, y, dil)

    def store_group(dst_ref, y, dil):
        if dil == 1:
            dst_ref[0, 0] = y.astype(BF16)
            return
        for c in range(GROUP_W // LANES):
            scr_ref[c] = y[:, c * LANES:(c + 1) * LANES]
        for r in range(dil):
            rows = pl.ds(r, tm // dil, stride=dil)
            dst_ref[0, r] = jnp.concatenate(
                [scr_ref[c, rows, :] for c in range(GROUP_W // LANES)], axis=-1).astype(BF16)

    for g, (_, dil) in enumerate(ATTN_GROUPS):
        c = g * GROUP_W
        _lockstep(normed(q_refs[g], c, qg_ref[:, c:c + GROUP_W], dil),
                  normed(k_refs[g], W_A + c, kg_ref[:, c:c + GROUP_W], dil),
                  plain(v_refs[g], 2 * W_A + c, dil))
    base = 3 * W_A
    qb = proj(base)
    qb_ref[...] = (qb * jax.nn.sigmoid(qb) * (DK_B ** -0.5)).astype(BF16)
    for dst, lb_ref, off in ((lff_ref, lbf_ref, D_B), (lfb_ref, lbb_ref, 2 * D_B)):
        lb = lb_ref[...]
        f = lb + (1.0 - lb) * jax.nn.sigmoid(proj(base + off))
        dst[...] = jnp.log(f)
    ib_ref[...] = proj(base + 3 * D_B).astype(BF16)
    og = proj(base + 4 * D_B)
    og_ref[...] = (og * jax.nn.sigmoid(og)).astype(BF16)
    for p in range(2 * D_MODEL // PIECE):
        c = p * PIECE
        gt_ref[:, c:c + PIECE] = jax.nn.sigmoid(proj(D_IN + c)).astype(BF16)


def _inproj(x2, b, s, nw, w_all, qg, kg, e_mat, lbf, lbb):
    t = x2.shape[0]
    tm = TM_IN
    tps = s // tm
    n_all = w_all.shape[1]
    row = lambda i: (i, 0)
    const = lambda i: (0, 0)
    seq = lambda i: (i // tps, 0, i % tps, 0)
    widths = (D_B, D_B, D_B, D_B, D_B, 2 * D_MODEL)
    dtypes = (BF16, F32, F32, BF16, BF16, BF16)
    grp_shapes = [jax.ShapeDtypeStruct((b, dil, s // dil, GROUP_W), BF16) for _, dil in ATTN_GROUPS] * 3
    grp_specs = [pl.BlockSpec((1, dil, tm // dil, GROUP_W), seq) for _, dil in ATTN_GROUPS] * 3
    return pl.pallas_call(
        functools.partial(_inproj_kernel, tm=tm),
        out_shape=tuple(grp_shapes) + tuple(jax.ShapeDtypeStruct((t, w), dt) for w, dt in zip(widths, dtypes)),
        grid=(t // tm,),
        in_specs=[
            pl.BlockSpec((tm, D_MODEL), row),
            pl.BlockSpec((1, D_MODEL), const),
            pl.BlockSpec((D_MODEL, n_all), const),
            pl.BlockSpec((1, W_A), const),
            pl.BlockSpec((1, W_A), const),
            pl.BlockSpec((PIECE, PIECE), const),
            pl.BlockSpec((1, D_B), const),
            pl.BlockSpec((1, D_B), const),
        ],
        out_specs=tuple(grp_specs) + tuple(pl.BlockSpec((tm, w), row) for w in widths),
        scratch_shapes=[pltpu.VMEM((GROUP_W // LANES, tm, LANES), F32)],
        compiler_params=_cparams(("parallel",)),
        name="inproj",
    )(x2, nw, w_all, qg, kg, e_mat, lbf, lbb)


def _attn_kernel(q_ref, kp_ref, kc_ref, kn_ref, vp_ref, vc_ref, vn_ref, bias_ref,
                 o_ref, lse_ref, *, tq, sub_len):
    i = pl.program_id(2)
    sq = bias_ref.shape[1] // 2
    nk = sq + 2 * N_SIDE
    lane = lax.broadcasted_iota(jnp.int32, (sq, LANES), 1)
    low = lane < HEAD_DIM_A
    ones = jnp.ones((nk, LANES), BF16)
    zero = jnp.zeros((sq, LANES), BF16)

    def pair(res, kk, vv, j, pr, colbias):
        cols = slice(pr * LANES, (pr + 1) * LANES)
        q2 = q_ref[res, j * sq:(j + 1) * sq, cols]
        k2 = kk[j * sq:j * sq + nk, cols]
        v2 = jnp.concatenate([vv[j * sq:j * sq + nk, cols], ones], axis=1)
        q_st = jnp.concatenate([jnp.where(low, q2, zero), jnp.where(low, zero, q2)], axis=0)
        s = lax.dot_general(q_st, k2, (((1,), (1,)), ((), ())), preferred_element_type=F32)
        yield
        s = s + bias_ref[pr]
        if colbias is not None:
            s = s + colbias
        m = jnp.max(s, axis=-1, keepdims=True)
        p = jnp.exp2(s - m).astype(BF16)
        r = jnp.dot(p, v2, preferred_element_type=F32)
        yield
        o2 = jnp.where(low, r[:sq, :LANES] / r[:sq, LANES:], r[sq:, :LANES] / r[sq:, LANES:])
        o_ref[res, j * sq:(j + 1) * sq, cols] = o2.astype(o_ref.dtype)
        lse = (m + jnp.log2(r[:, LANES:LANES + 1])) * LN2
        lse_ref[res, j * sq:(j + 1) * sq, cols] = jnp.where(low, lse[:sq], lse[sq:])

    n_pairs = HEADS_PER_GROUP // 2
    for res in range(q_ref.shape[0]):
        kk = jnp.concatenate([kp_ref[res], kc_ref[res], kn_ref[res]], axis=0)
        vv = jnp.concatenate([vp_ref[res], vc_ref[res], vn_ref[res]], axis=0)
        n_sub = tq // sq
        for j in range(n_sub):
            colbias = None
            if j in (0, n_sub - 1):
                kpos = i * tq + j * sq - N_SIDE + lax.broadcasted_iota(jnp.int32, (1, nk), 1)
                colbias = jnp.where((kpos >= 0) & (kpos < sub_len), 0.0, NEG).astype(F32)
            for pr in range(0, n_pairs, ATTN_PAIRS_IN_FLIGHT):
                _lockstep(*[pair(res, kk, vv, j, pr + d, colbias) for d in range(ATTN_PAIRS_IN_FLIGHT)])


def _attn_bias(sq, dil, slopes):
    nk = sq + 2 * N_SIDE
    rel = np.arange(nk)[None, :] - N_SIDE - np.arange(sq)[:, None]
    band = np.abs(rel) <= N_SIDE
    alibi = -slopes[:, None, None] * (dil * np.abs(rel)).astype(np.float32)[None]
    bias = np.where(band[None], alibi * LOG2E, NEG).astype(np.float32)
    return jnp.asarray(bias.reshape(HEADS_PER_GROUP // 2, 2 * sq, nk))


def _attention_group(q, k, v, g):
    b, dil, sub_len, _ = q.shape
    tq = min(TQ, sub_len)
    sq = min(SQ, sub_len)
    hb = tq // N_SIDE
    n_halo = sub_len // N_SIDE
    slopes = (2.0 ** (-8.0 * (np.arange(N_HEADS_A) + 1) / N_HEADS_A)).astype(np.float32)
    bias = _attn_bias(sq, dil, slopes[g * HEADS_PER_GROUP:(g + 1) * HEADS_PER_GROUP])
    cur = lambda bi, r, i: (bi, r, i, 0)
    prev = lambda bi, r, i: (bi, r, jnp.maximum(i * hb - 1, 0), 0)
    nxt = lambda bi, r, i: (bi, r, jnp.minimum((i + 1) * hb, n_halo - 1), 0)
    n_res = max(1, min(dil, TQ // tq))
    blk_q = (None, n_res, tq, GROUP_W)
    blk_h = (None, n_res, N_SIDE, GROUP_W)
    return pl.pallas_call(
        functools.partial(_attn_kernel, tq=tq, sub_len=sub_len),
        out_shape=(jax.ShapeDtypeStruct((b, dil, sub_len, GROUP_W), BF16),
                   jax.ShapeDtypeStruct((b, dil, sub_len, GROUP_W), F32)),
        grid=(b, dil // n_res, sub_len // tq),
        in_specs=[
            pl.BlockSpec(blk_q, cur),
            pl.BlockSpec(blk_h, prev), pl.BlockSpec(blk_q, cur), pl.BlockSpec(blk_h, nxt),
            pl.BlockSpec(blk_h, prev), pl.BlockSpec(blk_q, cur), pl.BlockSpec(blk_h, nxt),
            pl.BlockSpec(bias.shape, lambda bi, r, i: (0, 0, 0)),
        ],
        out_specs=(pl.BlockSpec(blk_q, cur), pl.BlockSpec(blk_q, cur)),
        compiler_params=_cparams(("parallel", "parallel", "parallel")),
        name=f"attn_d{dil}",
    )(q, k, k, k, v, v, v, bias)


def _hgrn_kernel(qf_ref, lf_ref, vf_ref, qr_ref, lr_ref, vr_ref, of_ref, or_ref,
                 sf_ref, sr_ref, qif_ref, qir_ref, uf_ref, ur_ref, df_ref, dr_ref, *, ts):
    c_len = HGRN_CHUNK
    nc = ts // c_len

    @pl.when(pl.program_id(1) == 0)
    def _():
        sf_ref[...] = jnp.zeros_like(sf_ref)
        sr_ref[...] = jnp.zeros_like(sr_ref)

    r_i = lax.broadcasted_iota(jnp.int32, (c_len, c_len), 0)
    c_i = lax.broadcasted_iota(jnp.int32, (c_len, c_len), 1)
    lower = r_i >= c_i
    upper = r_i <= c_i
    tri_f = jnp.where(lower, 1.0, 0.0).astype(BF16)
    tri_r = jnp.where(upper, 1.0, 0.0).astype(BF16)

    def cumsum(tri, x):
        hi = x.astype(BF16)
        r1 = x - hi.astype(F32)
        mid = r1.astype(BF16)
        lo = (r1 - mid.astype(F32)).astype(BF16)
        d = lambda a: jnp.dot(tri, a, preferred_element_type=F32)
        return d(hi) + d(mid) + d(lo)

    n_sub = c_len // HGRN_SUB
    shift = HGRN_SUB.bit_length() - 1
    sub_f = jnp.right_shift(lax.broadcasted_iota(jnp.int32, (c_len, D_B), 0), shift)
    sub_tf = jnp.right_shift(lax.broadcasted_iota(jnp.int32, (D_B, c_len), 1), shift)
    t_i = lax.broadcasted_iota(jnp.int32, (2 * c_len, c_len), 0) & (c_len - 1)
    s_i = lax.broadcasted_iota(jnp.int32, (2 * c_len, c_len), 1)
    same_sub2 = jnp.right_shift(t_i, shift) == jnp.right_shift(s_i, shift)
    lower2, upper2 = t_i >= s_i, t_i <= s_i
    lower_sub2, upper_sub2 = same_sub2 & lower2, same_sub2 & upper2
    row_head = lax.broadcasted_iota(jnp.int32, (2 * c_len, 2 * DK_B), 0) // c_len
    col_head = lax.broadcasted_iota(jnp.int32, (2 * c_len, 2 * DK_B), 1) // DK_B
    pair_cols = jnp.where(row_head == col_head, 1.0, 0.0).astype(BF16)
    pair_off = jnp.concatenate([pair_cols] * (n_sub - 1), axis=1)

    def intra(q_ref, l_ref, v_ref, o_ref, qi_ref, u_ref, d_ref, c, rev):
        rows = pl.ds(pl.multiple_of(c * c_len, c_len), c_len)
        lf = l_ref[0, rows, :]
        q = q_ref[0, rows, :].astype(F32)
        v = v_ref[0, rows, :]
        k = 1.0 - jnp.exp(lf)
        b = cumsum(tri_r if rev else tri_f, lf)
        yield
        sub = (n_sub - 1 - sub_f) if rev else sub_f

        def b_at(pos):
            r = c_len - 1 - pos if rev else pos
            return b[r:r + 1, :]

        def per_sub(vals):
            vals = vals[::-1] if rev else vals
            return jnp.concatenate([jnp.broadcast_to(x, (HGRN_SUB, D_B)) for x in vals], axis=0)

        a_end = [b_at(HGRN_SUB * j + HGRN_SUB - 1) for j in range(n_sub)]
        a_start = [jnp.zeros((1, D_B), F32)] + a_end[:-1]
        btot = a_end[-1]
        end_full, start_full = per_sub(a_end), per_sub(a_start)
        stores = [(qi_ref, (rows, slice(None)), (q * jnp.exp(b)).astype(BF16))]
        q_off = [jnp.where(sub > j, q * jnp.exp(jnp.minimum(b - a_end[j], 0.0)), 0.0).astype(BF16)
                 for j in range(n_sub - 1)]
        k_end = k * jnp.exp(end_full - b)
        sub_t = (n_sub - 1 - sub_tf) if rev else sub_tf
        q_dia = (q * jnp.exp(b - start_full)).astype(BF16)
        k_upd = (k * jnp.exp(btot - b)).astype(BF16)
        k_end_t = k_end.T
        k_dia_t = (k * jnp.exp(jnp.minimum(start_full - b, HGRN_EXP_CLAMP))).T.astype(BF16)
        k_off_t = [jnp.where(sub_t == j, k_end_t, 0.0).astype(BF16) for j in range(n_sub - 1)]
        dec8_t = jnp.broadcast_to(jnp.exp(btot), (SUBLANES, D_B)).T
        keep, keep_d = (upper2, upper_sub2) if rev else (lower2, lower_sub2)
        two = lambda a: jnp.concatenate([a, a], axis=0)
        for p in range(N_HEADS_B // 2):
            ps = slice(2 * p * DK_B, (2 * p + 2) * DK_B)
            qc = two(jnp.concatenate([q_off[j][:, ps] for j in range(n_sub - 1)], axis=1)) * pair_off
            kc = jnp.concatenate([k_off_t[j][ps, :] for j in range(n_sub - 1)], axis=0)
            yield
            sc = jnp.dot(qc, kc, preferred_element_type=F32)
            sc_d = jnp.dot(two(q_dia[:, ps]) * pair_cols, k_dia_t[ps, :], preferred_element_type=F32)
            u2 = lax.dot_general(k_upd[:, ps], v[:, ps], (((0,), (0,)), ((), ())),
                                 preferred_element_type=F32)
            yield
            sc = (jnp.where(keep, sc, 0.0) + jnp.where(keep_d, sc_d, 0.0)).astype(BF16)
            o2 = jnp.dot(sc, v[:, ps], preferred_element_type=F32)
            for i in range(2):
                h = 2 * p + i
                hs = slice(h * DK_B, (h + 1) * DK_B)
                blk = slice(i * DK_B, (i + 1) * DK_B)
                stores.append((o_ref, (0, rows, hs), o2[i * c_len:(i + 1) * c_len, blk]))
                stores.append((u_ref, (c, h), u2[blk, blk]))
                stores.append((d_ref, (c, h), jnp.broadcast_to(dec8_t[hs, 0:1], (DK_B, DK_B))))
        return stores

    def carry_state(o_ref, qi_ref, u_ref, d_ref, s_ref, c):
        rows = pl.ds(pl.multiple_of(c * c_len, c_len), c_len)
        zero = jnp.zeros((DK_B, DK_B), BF16)
        stores = []
        for p in range(N_HEADS_B // 2):
            ps = slice(2 * p * DK_B, (2 * p + 2) * DK_B)
            sa, sb = s_ref[2 * p], s_ref[2 * p + 1]
            s_bd = jnp.concatenate([jnp.concatenate([sa.astype(BF16), zero], axis=1),
                                    jnp.concatenate([zero, sb.astype(BF16)], axis=1)], axis=0)
            o_new = o_ref[0, rows, ps] + jnp.dot(qi_ref[rows, ps], s_bd, preferred_element_type=F32)
            stores.append((o_ref, (0, rows, ps), o_new))
            stores.append((s_ref, (2 * p,), sa * d_ref[c, 2 * p] + u_ref[c, 2 * p]))
            stores.append((s_ref, (2 * p + 1,), sb * d_ref[c, 2 * p + 1] + u_ref[c, 2 * p + 1]))
        return stores

    def commit(stores):
        for ref, idx, val in stores:
            ref[idx] = val

    def intra_body(trip, carry):
        gens = []
        for i in range(HGRN_CHUNKS_PER_TRIP):
            c = trip * HGRN_CHUNKS_PER_TRIP + i
            gens.append(intra(qf_ref, lf_ref, vf_ref, of_ref, qif_ref, uf_ref, df_ref, c, False))
            gens.append(intra(qr_ref, lr_ref, vr_ref, or_ref, qir_ref, ur_ref, dr_ref, c, True))
        commit(sum(_lockstep(*gens), []))
        return carry

    def state_body(c, carry):
        commit(carry_state(of_ref, qif_ref, uf_ref, df_ref, sf_ref, c)
               + carry_state(or_ref, qir_ref, ur_ref, dr_ref, sr_ref, nc - 1 - c))
        return carry

    lax.fori_loop(0, nc // HGRN_CHUNKS_PER_TRIP, intra_body, 0)
    lax.fori_loop(0, nc, state_body, 0)


def _hgrn(qb, lff, lfb, ib):
    b, s, _ = qb.shape
    ts = min(TS_HGRN, s)
    nt = s // ts
    nc = ts // HGRN_CHUNK
    fwd = lambda bi, j: (bi, j, 0)
    rev = lambda bi, j: (bi, nt - 1 - j, 0)
    blk = (1, ts, D_B)
    return pl.pallas_call(
        functools.partial(_hgrn_kernel, ts=ts),
        out_shape=(jax.ShapeDtypeStruct((b, s, D_B), F32), jax.ShapeDtypeStruct((b, s, D_B), F32)),
        grid=(b, nt),
        in_specs=[pl.BlockSpec(blk, fwd), pl.BlockSpec(blk, fwd), pl.BlockSpec(blk, fwd),
                  pl.BlockSpec(blk, rev), pl.BlockSpec(blk, rev), pl.BlockSpec(blk, rev)],
        out_specs=(pl.BlockSpec(blk, fwd), pl.BlockSpec(blk, rev)),
        scratch_shapes=[pltpu.VMEM((N_HEADS_B, DK_B, DK_B), F32),
                        pltpu.VMEM((N_HEADS_B, DK_B, DK_B), F32),
                        pltpu.VMEM((ts, D_B), BF16),
                        pltpu.VMEM((ts, D_B), BF16),
                        pltpu.VMEM((nc, N_HEADS_B, DK_B, DK_B), F32),
                        pltpu.VMEM((nc, N_HEADS_B, DK_B, DK_B), F32),
                        pltpu.VMEM((nc, N_HEADS_B, DK_B, DK_B), F32),
                        pltpu.VMEM((nc, N_HEADS_B, DK_B, DK_B), F32)],
        compiler_params=_cparams(("parallel", "arbitrary")),
        name="hgrn",
    )(qb, lff, ib, qb, lfb, ib)


def _mix_kernel(x_ref, o1_ref, o2_ref, o3_ref, l1_ref, l2_ref, l3_ref, of_ref, ob_ref, og_ref,
                gt_ref, wa_ref, wb_ref, wo_ref, ogain_ref, nmoe_ref, wrh_ref, wrl_ref,
                br_ref, cnt_ref,
                h_ref, xt_ref, tw_ref, ps_ref, ct_ref, rt_ref, cnt_out_ref, run_ref, so_ref, *, tm):
    i = pl.program_id(0)

    @pl.when(i == 0)
    def _():
        run_ref[...] = cnt_ref[...]

    def token_major(src_ref, scr_ref, dil):
        if dil == 1:
            return src_ref[0].astype(F32)
        n_chunk = scr_ref.shape[0]
        for r in range(dil):
            blk = src_ref[r].astype(F32)
            for c in range(n_chunk):
                scr_ref[c, pl.ds(r, tm // dil, stride=dil), :] = blk[:, c * LANES:(c + 1) * LANES]
        return jnp.concatenate([scr_ref[c] for c in range(n_chunk)], axis=-1)

    dils = [dil for _, dil in ATTN_GROUPS]
    l1, l2, l3 = [token_major(r, so_ref, d) for r, d in zip((l1_ref, l2_ref, l3_ref), dils)]
    mx = jnp.maximum(jnp.maximum(l1, l2), l3)
    e1, e2, e3 = jnp.exp(l1 - mx), jnp.exp(l2 - mx), jnp.exp(l3 - mx)
    attn = (e1 * token_major(o1_ref, so_ref, dils[0]) + e2 * token_major(o2_ref, so_ref, dils[1])
            + e3 * token_major(o3_ref, so_ref, dils[2])) / (e1 + e2 + e3)

    o = of_ref[...] + ob_ref[...]
    parts = []
    for h in range(N_HEADS_B):
        oh = o[:, h * DK_B:(h + 1) * DK_B]
        ms = jnp.mean(oh * oh, axis=-1, keepdims=True)
        parts.append(oh * lax.rsqrt(ms + EPS))
    hg = jnp.concatenate(parts, axis=-1) * ogain_ref[...] * og_ref[...].astype(F32)

    attn_bf, hg_bf = attn.astype(BF16), hg.astype(BF16)
    tp = TM_MIX
    lane = lax.broadcasted_iota(jnp.int32, (tp, LANES), 1)
    r_i = lax.broadcasted_iota(jnp.int32, (tp, tp), 0)
    c_i = lax.broadcasted_iota(jnp.int32, (tp, tp), 1)
    tri = jnp.where(r_i > c_i, 1.0, 0.0).astype(BF16)
    e_r = lax.broadcasted_iota(jnp.int32, (LANES, LANES), 0)
    e_c = lax.broadcasted_iota(jnp.int32, (LANES, LANES), 1)
    before_e = jnp.where(e_r < e_c, 1.0, 0.0).astype(BF16)
    row_id = lax.broadcasted_iota(jnp.int32, (tp * TOP_K, tp), 0).astype(F32)

    def token_chain(part):
        rs = slice(part * tp, (part + 1) * tp)
        pa = jnp.dot(attn_bf[rs], wa_ref[...], preferred_element_type=F32)
        pb = jnp.dot(hg_bf[rs], wb_ref[...], preferred_element_type=F32)
        yield
        mixed = (gt_ref[rs, :D_MODEL].astype(F32) * pa + gt_ref[rs, D_MODEL:].astype(F32) * pb)
        h = x_ref[rs, :] + jnp.dot(mixed.astype(BF16), wo_ref[...], preferred_element_type=F32)
        yield
        h_ref[rs, :] = h
        ms = jnp.mean(h * h, axis=-1, keepdims=True)
        hn = h * lax.rsqrt(ms + EPS) * nmoe_ref[...]
        hi, lo = _split_bf16(hn)
        lg = (jnp.dot(hi, wrh_ref[...], preferred_element_type=F32)
              + jnp.dot(lo, wrh_ref[...], preferred_element_type=F32)
              + jnp.dot(hi, wrl_ref[...], preferred_element_type=F32)) + br_ref[...]
        yield
        vals, idxs = [], []
        onehot = jnp.zeros((tp, LANES), F32)
        for _ in range(TOP_K):
            m = jnp.max(lg, axis=-1, keepdims=True)
            idx = jnp.min(jnp.where(lg == m, lane, LANES), axis=-1, keepdims=True)
            sel = lane == idx
            onehot = jnp.where(sel, 1.0, onehot)
            lg = jnp.where(sel, NEG * 2, lg)
            vals.append(m)
            idxs.append(idx)
        exps = [jnp.exp(v - vals[0]) for v in vals]
        inv = 1.0 / (exps[0] + exps[1] + exps[2] + exps[3])
        local = jnp.dot(tri, onehot.astype(BF16), preferred_element_type=F32)
        cnt_tile = jnp.sum(onehot, axis=0, keepdims=True)
        off = jnp.dot(jnp.broadcast_to(cnt_tile, (8, LANES)).astype(BF16), before_e,
                      preferred_element_type=F32)[0:1]
        yield
        slot = off + local
        tw = jnp.zeros((tp, LANES), F32)
        ps = jnp.full((tp, LANES), -1.0, F32)
        for k in range(TOP_K):
            slot_k = jnp.sum(jnp.where(lane == idxs[k], slot, 0.0), axis=-1, keepdims=True)
            tw = jnp.where(lane == k, exps[k] * inv, tw)
            ps = jnp.where(lane == k, slot_k, ps)
        tw_ref[rs, :] = tw
        ps_ref[rs, :] = ps.astype(jnp.int32)
        ps_t = ps.T
        perm = jnp.zeros((tp * TOP_K, tp), F32)
        for k in range(TOP_K):
            perm = jnp.where(row_id == ps_t[k:k + 1, :], 1.0, perm)
        rows = jnp.dot(perm.astype(BF16), hi, preferred_element_type=F32)
        yield
        for c in range(D_MODEL // LANES):
            xt_ref[pl.ds(part * tp * TOP_K * ROW_SUB + c, tp * TOP_K, stride=ROW_SUB), :] = (
                rows[:, c * LANES:(c + 1) * LANES])
        return cnt_tile

    counts = _lockstep(*[token_chain(p) for p in range(tm // tp)])
    run = run_ref[...]
    for part, cnt_tile in enumerate(counts):
        ct_ref[part] = cnt_tile
        rt_ref[part] = run
        run = run + cnt_tile
    run_ref[...] = run
    cnt_out_ref[...] = run


def _mix(x2, s, o1, o2, o3, l1, l2, l3, of, ob, og, gt, wa, wb, wo, ogain, nmoe, wrh, wrl, br, cnt):
    t = x2.shape[0]
    tm = TM_MIX * MIX_TILES_PER_STEP
    tps = s // tm
    row = lambda i: (i, 0)
    const = lambda i: (0, 0)
    seq = lambda i: (i // tps, 0, i % tps, 0)
    rb = lambda w: pl.BlockSpec((tm, w), row)
    cb = lambda a: pl.BlockSpec(a.shape, const)
    gb = lambda a: pl.BlockSpec((None, a.shape[1], tm // a.shape[1], a.shape[3]), seq)
    per_tile = pl.BlockSpec((MIX_TILES_PER_STEP, 1, LANES), lambda i: (i, 0, 0))
    return pl.pallas_call(
        functools.partial(_mix_kernel, tm=tm),
        out_shape=(jax.ShapeDtypeStruct((t, D_MODEL), F32),
                   jax.ShapeDtypeStruct((t * TOP_K * ROW_SUB, LANES), F32),
                   jax.ShapeDtypeStruct((t, LANES), F32),
                   jax.ShapeDtypeStruct((t, LANES), jnp.int32),
                   jax.ShapeDtypeStruct((t // TM_MIX, 1, LANES), F32),
                   jax.ShapeDtypeStruct((t // TM_MIX, 1, LANES), F32),
                   jax.ShapeDtypeStruct((1, LANES), F32)),
        grid=(t // tm,),
        in_specs=[rb(D_MODEL), gb(o1), gb(o2), gb(o3), gb(l1), gb(l2), gb(l3),
                  rb(D_B), rb(D_B), rb(D_B), rb(2 * D_MODEL),
                  cb(wa), cb(wb), cb(wo), cb(ogain), cb(nmoe), cb(wrh), cb(wrl), cb(br), cb(cnt)],
        out_specs=(rb(D_MODEL), pl.BlockSpec((tm * TOP_K * ROW_SUB, LANES), row), rb(LANES), rb(LANES),
                   per_tile, per_tile, pl.BlockSpec((1, LANES), const)),
        scratch_shapes=[pltpu.VMEM((1, LANES), F32), pltpu.VMEM((GROUP_W // LANES, tm, LANES), F32)],
        compiler_params=_cparams(("arbitrary",)),
        name="mix",
    )(x2, o1, o2, o3, l1, l2, l3, of, ob, og, gt, wa, wb, wo, ogain, nmoe, wrh, wrl, br, cnt)


def _start_tile_segments(tile, off_tbl, row_tbl, len_tbl, make_piece):
    def segment(e, carry):
        sidx = tile * N_EXPERTS + e
        t0, r0, n = off_tbl[sidx], row_tbl[sidx], len_tbl[sidx]
        n_bulk = jnp.right_shift(n, SEG_CHUNK.bit_length() - 1)

        def bulk(j, c):
            make_piece(t0 + j * SEG_CHUNK, r0 + j * SEG_CHUNK, SEG_CHUNK).start()
            return c

        lax.fori_loop(0, n_bulk, bulk, 0)
        done = n_bulk * SEG_CHUNK
        bit = SEG_CHUNK // 2
        while bit >= 1:
            has = (n & bit) != 0

            @pl.when(has)
            def _(done=done, bit=bit):
                make_piece(t0 + done, r0 + done, bit).start()

            done = done + jnp.where(has, bit, 0)
            bit //= 2
        return carry

    lax.fori_loop(0, N_EXPERTS, segment, 0)


def _row_slice(row, n_rows):
    return pl.ds(pl.multiple_of(row * ROW_SUB, ROW_SUB), n_rows * ROW_SUB)


def _dispatch_kernel(off_tbl, row_tbl, len_tbl, fill_row, fill_len, xa_ref, xb_ref, xs_ref, buf_ref,
                     zero_ref, sems_in, sems_out, sem, *, tiles_a):
    i = pl.program_id(0)
    n_tiles = pl.num_programs(0)
    blk_rows = buf_ref.shape[1]

    def fetch(tile):
        slot = tile % DISPATCH_SLOTS

        def copy(src_ref, src_tile):
            start = pl.multiple_of(src_tile * blk_rows, blk_rows)
            pltpu.make_async_copy(src_ref.at[pl.ds(start, blk_rows)], buf_ref.at[slot], sems_in.at[slot]).start()

        @pl.when(tile < tiles_a)
        def _():
            copy(xa_ref, tile)

        @pl.when(tile >= tiles_a)
        def _():
            copy(xb_ref, tile - tiles_a)

    def scatter_done(tile):
        slot = tile % DISPATCH_SLOTS
        pltpu.make_async_copy(buf_ref.at[slot], xs_ref.at[pl.ds(0, blk_rows)], sems_out.at[slot]).wait()

    def zeros_to(row, n_rows):
        return pltpu.make_async_copy(zero_ref.at[pl.ds(0, n_rows * ROW_SUB)],
                                     xs_ref.at[_row_slice(row, n_rows)], sem)

    @pl.when(i == 0)
    def _():
        fetch(i)
        zero_ref[...] = jnp.zeros_like(zero_ref)

        def fill(e, total):
            r0, n = fill_row[e], fill_len[e]
            n_blk = jnp.right_shift(n, BM.bit_length() - 1)

            def blocks(j, c):
                zeros_to(r0 + j * BM, BM).start()
                return c

            lax.fori_loop(0, n_blk, blocks, 0)
            done = n_blk * BM
            bit = BM // 2
            while bit >= 1:
                has = (n & bit) != 0

                @pl.when(has)
                def _(done=done, bit=bit):
                    zeros_to(r0 + done, bit).start()

                done = done + jnp.where(has, bit, 0)
                bit //= 2
            return total + n

        total = lax.fori_loop(0, N_EXPERTS + 1, fill, 0)
        bit = 1
        while bit * ROW_SUB <= xs_ref.shape[0]:
            @pl.when((total & bit) != 0)
            def _(bit=bit):
                n = bit * ROW_SUB
                pltpu.make_async_copy(xs_ref.at[pl.ds(0, n)], xs_ref.at[pl.ds(0, n)], sem).wait()

            bit *= 2

    @pl.when(i + 1 < n_tiles)
    def _():
        fetch(i + 1)

    slot = i % DISPATCH_SLOTS
    pltpu.make_async_copy(xa_ref.at[pl.ds(0, blk_rows)], buf_ref.at[slot], sems_in.at[slot]).wait()

    def piece(tile_row, expert_row, n_rows):
        return pltpu.make_async_copy(buf_ref.at[slot, _row_slice(tile_row, n_rows)],
                                     xs_ref.at[_row_slice(expert_row, n_rows)], sems_out.at[slot])

    _start_tile_segments(i, off_tbl, row_tbl, len_tbl, piece)

    @pl.when(i >= 1)
    def _():
        scatter_done(i - 1)

    @pl.when(i == n_tiles - 1)
    def _():
        scatter_done(i)


def _dispatch(off_tbl, row_tbl, len_tbl, fill_row, fill_len, xt_a, xt_b, n_rows):
    blk = TM_MIX * TOP_K * ROW_SUB
    tiles_a, tiles_b = xt_a.shape[0] // blk, xt_b.shape[0] // blk
    return pl.pallas_call(
        functools.partial(_dispatch_kernel, tiles_a=tiles_a),
        out_shape=jax.ShapeDtypeStruct((n_rows * ROW_SUB, LANES), F32),
        grid_spec=pltpu.PrefetchScalarGridSpec(
            num_scalar_prefetch=5,
            grid=(tiles_a + tiles_b,),
            in_specs=[pl.BlockSpec(memory_space=pl.ANY), pl.BlockSpec(memory_space=pl.ANY)],
            out_specs=pl.BlockSpec(memory_space=pl.ANY),
            scratch_shapes=[pltpu.VMEM((DISPATCH_SLOTS, blk, LANES), F32),
                            pltpu.VMEM((BM * ROW_SUB, LANES), F32),
                            pltpu.SemaphoreType.DMA((DISPATCH_SLOTS,)),
                            pltpu.SemaphoreType.DMA((DISPATCH_SLOTS,)),
                            pltpu.SemaphoreType.DMA],
        ),
        compiler_params=_cparams(("arbitrary",)),
        name="dispatch",
    )(off_tbl, row_tbl, len_tbl, fill_row, fill_len, xt_a, xt_b)


def _combine_kernel(off_tbl, row_tbl, len_tbl, ps_ref, tw_ref, h_ref, ys_ref, y_ref, buf_ref, sems, *, tm):
    i = pl.program_id(0)
    n_rows = tm * TOP_K

    def start(tile):
        slot = tile % 2

        def piece(tile_row, expert_row, n):
            return pltpu.make_async_copy(ys_ref.at[_row_slice(expert_row, n)],
                                         buf_ref.at[slot, _row_slice(tile_row, n)], sems.at[slot])

        _start_tile_segments(tile, off_tbl, row_tbl, len_tbl, piece)

    @pl.when(i == 0)
    def _():
        start(i)

    @pl.when(i + 1 < pl.num_programs(0))
    def _():
        start(i + 1)

    slot = i % 2
    yt_ref = buf_ref.at[slot]
    pltpu.make_async_copy(ys_ref.at[pl.ds(0, n_rows * ROW_SUB)], yt_ref, sems.at[slot]).wait()
    ysorted = jnp.concatenate(
        [yt_ref[pl.ds(c, n_rows, stride=ROW_SUB), :] for c in range(ROW_SUB)], axis=-1).astype(BF16)
    col = lax.broadcasted_iota(jnp.int32, (tm, n_rows), 1)
    ps = ps_ref[...]
    tw = tw_ref[...]
    pw = jnp.zeros((tm, n_rows), F32)
    for k in range(TOP_K):
        pw = jnp.where(col == ps[:, k:k + 1], tw[:, k:k + 1], pw)
    hi, lo = _split_bf16(pw)
    y_ref[...] = (h_ref[...] + jnp.dot(hi, ysorted, preferred_element_type=F32)
                  + jnp.dot(lo, ysorted, preferred_element_type=F32))


def _combine(off_tbl, row_tbl, len_tbl, ps, tw, h, ys):
    t = h.shape[0]
    tm = TM_MIX
    row = lambda i, *_: (i, 0)
    return pl.pallas_call(
        functools.partial(_combine_kernel, tm=tm),
        out_shape=jax.ShapeDtypeStruct((t, D_MODEL), F32),
        grid_spec=pltpu.PrefetchScalarGridSpec(
            num_scalar_prefetch=3,
            grid=(t // tm,),
            in_specs=[pl.BlockSpec((tm, LANES), row),
                      pl.BlockSpec((tm, LANES), row),
                      pl.BlockSpec((tm, D_MODEL), row),
                      pl.BlockSpec(memory_space=pl.ANY)],
            out_specs=pl.BlockSpec((tm, D_MODEL), row),
            scratch_shapes=[pltpu.VMEM((2, tm * TOP_K * ROW_SUB, LANES), F32),
                            pltpu.SemaphoreType.DMA((2,))],
        ),
        compiler_params=_cparams(("arbitrary",)),
        name="combine",
    )(off_tbl, row_tbl, len_tbl, ps, tw, h, ys)


def _experts_kernel(be_ref, nused_ref, xs_ref, wgu_ref, bgu_ref, wd_ref, bd_ref, ys_ref, wgu_bf, wd_bf):
    i = pl.program_id(0)
    used = i < nused_ref[0]
    new_expert = (i == 0) | (be_ref[i] != be_ref[jnp.maximum(i - 1, 0)])

    @pl.when(used & new_expert)
    def _():
        wgu_bf[...] = wgu_ref[0].astype(BF16)
        wd_bf[...] = wd_ref[0].astype(BF16)

    @pl.when(used)
    def _():
        x = jnp.concatenate(
            [xs_ref[pl.ds(c, BM, stride=ROW_SUB), :] for c in range(ROW_SUB)], axis=-1).astype(BF16)
        hh = jnp.dot(x, wgu_bf[...], preferred_element_type=F32) + bgu_ref[0]
        gate = jnp.minimum(hh[:, :D_FF], SWIGLU_LIMIT)
        up = jnp.clip(hh[:, D_FF:], -SWIGLU_LIMIT, SWIGLU_LIMIT)
        glu = gate * jax.nn.sigmoid(SWIGLU_ALPHA * gate)
        act = ((up + 1.0) * glu).astype(BF16)
        y = jnp.dot(act, wd_bf[...], preferred_element_type=F32) + bd_ref[0]
        for c in range(ROW_SUB):
            ys_ref[pl.ds(c, BM, stride=ROW_SUB), :] = y[:, c * LANES:(c + 1) * LANES]

    @pl.when(i >= nused_ref[0])
    def _():
        ys_ref[...] = jnp.zeros_like(ys_ref)


def _experts(block_e, nused, xs, wgu, bgu, wd, bd):
    nb = xs.shape[0] // (BM * ROW_SUB)
    emap3 = lambda i, be, nu: (be[i], 0, 0)
    rows = pl.BlockSpec((BM * ROW_SUB, LANES), lambda i, be, nu: (i, 0))
    return pl.pallas_call(
        _experts_kernel,
        out_shape=jax.ShapeDtypeStruct(xs.shape, F32),
        grid_spec=pltpu.PrefetchScalarGridSpec(
            num_scalar_prefetch=2,
            grid=(nb,),
            in_specs=[rows,
                      pl.BlockSpec((1, D_MODEL, 2 * D_FF), emap3),
                      pl.BlockSpec((1, 1, 2 * D_FF), emap3),
                      pl.BlockSpec((1, D_FF, D_MODEL), emap3),
                      pl.BlockSpec((1, 1, D_MODEL), emap3)],
            out_specs=rows,
            scratch_shapes=[pltpu.VMEM((D_MODEL, 2 * D_FF), BF16), pltpu.VMEM((D_FF, D_MODEL), BF16)],
        ),
        compiler_params=_cparams(("arbitrary",)),
        name="experts",
    )(block_e, nused, xs, wgu, bgu, wd, bd)


def _head_indicator(n_cols, head_dim):
    e = np.zeros((n_cols, LANES), np.float32)
    e[np.arange(n_cols), np.arange(n_cols) // head_dim] = 1.0
    return e


def _mixer(x, prm):
    b, s, d = x.shape
    t = b * s
    x2 = x.reshape(t, d)
    res = _inproj(x2, b, s, prm["norm_mix"], prm["w_all"], prm["q_gain"], prm["k_gain"], prm["e_in"],
                  prm["lb_f"], prm["lb_b"])
    n_grp = len(ATTN_GROUPS)
    qs, ks, vs = res[:n_grp], res[n_grp:2 * n_grp], res[2 * n_grp:3 * n_grp]
    qb, lff, lfb, ib, og, gt = res[3 * n_grp:]
    r3 = lambda a: a.reshape(b, s, a.shape[-1])
    outs, lses = [], []
    for g in range(n_grp):
        o, lse = _attention_group(qs[g], ks[g], vs[g], g)
        outs.append(o)
        lses.append(lse)
    of, ob = _hgrn(r3(qb), r3(lff), r3(lfb), r3(ib))
    return x2, outs, lses, of.reshape(t, D_B), ob.reshape(t, D_B), og, gt


def kernel(x_prompt, x_sample, norm_mix, w_in, q_gain, k_gain, hgrn_lb, hgrn_o_gain, w_gate, w_proj_a,
           w_proj_b, w_out, norm_moe, w_router, b_router, w_gu, b_gu, w_down, b_down):
    l = 0
    lb = jnp.cumsum(jax.nn.softmax(hgrn_lb.astype(F32), axis=1), axis=1)
    wr = jnp.zeros((D_MODEL, LANES), F32).at[:, :N_EXPERTS].set(w_router[l])
    wr_hi = wr.astype(BF16)
    prm = {
        "norm_mix": norm_mix[l].reshape(1, D_MODEL),
        "w_all": jnp.concatenate([w_in[l], w_gate[l]], axis=1).astype(BF16),
        "q_gain": q_gain[l].reshape(1, W_A) * (HEAD_DIM_A ** -0.5 * LOG2E),
        "k_gain": k_gain[l].reshape(1, W_A),
        "e_in": jnp.asarray(_head_indicator(PIECE, HEAD_DIM_A) @ _head_indicator(PIECE, HEAD_DIM_A).T, BF16),
        "lb_f": lb[0, l].reshape(1, D_B),
        "lb_b": lb[1, l].reshape(1, D_B),
    }
    wa, wb, wo = w_proj_a[l].astype(BF16), w_proj_b[l].astype(BF16), w_out[l].astype(BF16)
    ogain = hgrn_o_gain[l].reshape(1, D_B)
    nmoe = norm_moe[l].reshape(1, D_MODEL)
    wr_lo = (wr - wr_hi.astype(F32)).astype(BF16)
    br = jnp.full((1, LANES), NEG, F32).at[0, :N_EXPERTS].set(b_router[l])

    cnt = jnp.zeros((1, LANES), F32)
    per_batch = []
    for x in (x_prompt, x_sample):
        x2, outs, lses, of, ob, og, gt = _mixer(x, prm)
        h, xt, tw, ps, ct, rt, cnt = _mix(x2, x.shape[1], outs[0], outs[1], outs[2], lses[0], lses[1], lses[2],
                                          of, ob, og, gt, wa, wb, wo, ogain, nmoe, wr_hi, wr_lo, br, cnt)
        per_batch.append((x.shape, h, xt, tw, ps, ct, rt))

    n_tok = sum(pb[1].shape[0] for pb in per_batch)
    sizes = cnt[0, :N_EXPERTS].astype(jnp.int32)
    pad_sizes = (sizes + BM - 1) // BM * BM
    pad_ends = jnp.cumsum(pad_sizes)
    pad_starts = pad_ends - pad_sizes
    nb = (n_tok * TOP_K) // BM + N_EXPERTS
    block_start = jnp.arange(nb, dtype=jnp.int32) * BM
    block_e = jnp.minimum(jnp.sum(pad_ends[None, :] <= block_start[:, None], axis=1),
                          N_EXPERTS - 1).astype(jnp.int32)
    nused = (pad_ends[-1:] // BM).astype(jnp.int32)

    tables = []
    for _, _, _, _, _, ct, rt in per_batch:
        cnt_te = ct[:, 0, :N_EXPERTS].astype(jnp.int32)
        tile_off = jnp.cumsum(cnt_te, axis=1) - cnt_te
        expert_row = pad_starts[None, :] + rt[:, 0, :N_EXPERTS].astype(jnp.int32)
        tables.append((tile_off.reshape(-1), expert_row.reshape(-1), cnt_te.reshape(-1)))

    fill_row = jnp.concatenate([pad_starts + sizes, pad_ends[-1:]]).astype(jnp.int32)
    fill_len = jnp.concatenate([pad_sizes - sizes, nb * BM - pad_ends[-1:]]).astype(jnp.int32)
    both = [jnp.concatenate(cols) for cols in zip(*tables)]
    xs = _dispatch(*both, fill_row, fill_len, per_batch[0][2], per_batch[1][2], nb * BM)
    ys = _experts(block_e, nused, xs, w_gu[l], b_gu[l].reshape(N_EXPERTS, 1, 2 * D_FF),
                  w_down[l], b_down[l].reshape(N_EXPERTS, 1, D_MODEL))
    results = []
    for (shape, h, _, tw, ps, _, _), tbl in zip(per_batch, tables):
        results.append(_combine(*tbl, ps, tw, h, ys).reshape(shape))
    return tuple(results)
```

```python
import functools
import math

import jax
import jax.numpy as jnp
import numpy as np
from jax import lax
from jax.experimental import pallas as pl
from jax.experimental.pallas import tpu as pltpu

F32 = jnp.float32
BF16 = jnp.bfloat16

D_MODEL = 1024
ATTN_GROUPS = ((128, 1), (512, 4), (2048, 16))
HEADS_PER_GROUP = 8
N_HEADS_A = 24
HEAD_DIM_A = 64
W_A = N_HEADS_A * HEAD_DIM_A
GROUP_W = HEADS_PER_GROUP * HEAD_DIM_A
N_SIDE = 64
N_HEADS_B = 4
DK_B = 128
HGRN_CHUNK = 64
HGRN_SUB = 16
HGRN_EXP_CLAMP = 80.0
HGRN_CHUNKS_PER_TRIP = 4
D_B = N_HEADS_B * DK_B
D_IN = 3 * W_A + 5 * D_B
N_EXPERTS = 32
TOP_K = 4
D_FF = 1024
SWIGLU_LIMIT = 7.0
SWIGLU_ALPHA = 1.702
EPS = 1e-6
NEG = -1e30
LOG2E = math.log2(math.e)
LN2 = math.log(2.0)

LANES = 128
VMEM_LIMIT = 56 * 1024 * 1024

TM_IN = 256
PIECE = 512
TQ = 2048
SQ = 128
ATTN_PAIRS_IN_FLIGHT = 1
TS_HGRN = 512
TM_MIX = 256
MIX_TILES_PER_STEP = 2
BM = 512
SUBLANES = 8
ROW_SUB = D_MODEL // LANES
assert ROW_SUB == SUBLANES
DISPATCH_SLOTS = 3
SEG_CHUNK = 16


def _cparams(sem):
    return pltpu.CompilerParams(dimension_semantics=sem, vmem_limit_bytes=VMEM_LIMIT)


def _lockstep(*gens):
    out = [None] * len(gens)
    live = list(range(len(gens)))
    while live:
        for g in list(live):
            try:
                next(gens[g])
            except StopIteration as done:
                out[g] = done.value
                live.remove(g)
    return out


def _split_bf16(x):
    hi = x.astype(BF16)
    lo = (x - hi.astype(F32)).astype(BF16)
    return hi, lo


def _inproj_kernel(x_ref, nw_ref, w_ref, qg_ref, kg_ref, e_ref, lbf_ref, lbb_ref,
                   q0_ref, q1_ref, q2_ref, k0_ref, k1_ref, k2_ref, v0_ref, v1_ref, v2_ref,
                   qb_ref, lff_ref, lfb_ref, ib_ref, og_ref, gt_ref, scr_ref, *, tm):
    q_refs, k_refs, v_refs = (q0_ref, q1_ref, q2_ref), (k0_ref, k1_ref, k2_ref), (v0_ref, v1_ref, v2_ref)
    x = x_ref[...]
    ms = jnp.mean(x * x, axis=-1, keepdims=True)
    xn = (x * lax.rsqrt(ms + EPS) * nw_ref[...]).astype(BF16)

    def proj(col):
        return jnp.dot(xn, w_ref[:, col:col + PIECE], preferred_element_type=F32)

    def normed(dst_ref, col, gain, dil):
        y = proj(col)
        yield
        ss = jnp.dot((y * y).astype(BF16), e_ref[...], preferred_element_type=F32)
        yield
        store_group(dst_ref, y * lax.rsqrt(ss * (1.0 / HEAD_DIM_A) + EPS) * gain, dil)

    def plain(dst_ref, col, dil):
        y = proj(col)
        yield
        yield
        store_group(dst_ref, y, dil)

    def store_group(dst_ref, y, dil):
        if dil == 1:
            dst_ref[0, 0] = y.astype(BF16)
            return
        for c in range(GROUP_W // LANES):
            scr_ref[c] = y[:, c * LANES:(c + 1) * LANES]
        for r in range(dil):
            rows = pl.ds(r, tm // dil, stride=dil)
            dst_ref[0, r] = jnp.concatenate(
                [scr_ref[c, rows, :] for c in range(GROUP_W // LANES)], axis=-1).astype(BF16)

    for g, (_, dil) in enumerate(ATTN_GROUPS):
        c = g * GROUP_W
        _lockstep(normed(q_refs[g], c, qg_ref[:, c:c + GROUP_W], dil),
                  normed(k_refs[g], W_A + c, kg_ref[:, c:c + GROUP_W], dil),
                  plain(v_refs[g], 2 * W_A + c, dil))
    base = 3 * W_A
    qb = proj(base)
    qb_ref[...] = (qb * jax.nn.sigmoid(qb) * (DK_B ** -0.5)).astype(BF16)
    for dst, lb_ref, off in ((lff_ref, lbf_ref, D_B), (lfb_ref, lbb_ref, 2 * D_B)):
        lb = lb_ref[...]
        f = lb + (1.0 - lb) * jax.nn.sigmoid(proj(base + off))
        dst[...] = jnp.log(f)
    ib_ref[...] = proj(base + 3 * D_B).astype(BF16)
    og = proj(base + 4 * D_B)
    og_ref[...] = (og * jax.nn.sigmoid(og)).astype(BF16)
    for p in range(2 * D_MODEL // PIECE):
        c = p * PIECE
        gt_ref[:, c:c + PIECE] = jax.nn.sigmoid(proj(D_IN + c)).astype(BF16)


def _inproj(x2, b, s, nw, w_all, qg, kg, e_mat, lbf, lbb):
    t = x2.shape[0]
    tm = TM_IN
    tps = s // tm
    n_all = w_all.shape[1]
    row = lambda i: (i, 0)
    const = lambda i: (0, 0)
    seq = lambda i: (i // tps, 0, i % tps, 0)
    widths = (D_B, D_B, D_B, D_B, D_B, 2 * D_MODEL)
    dtypes = (BF16, F32, F32, BF16, BF16, BF16)
    grp_shapes = [jax.ShapeDtypeStruct((b, dil, s // dil, GROUP_W), BF16) for _, dil in ATTN_GROUPS] * 3
    grp_specs = [pl.BlockSpec((1, dil, tm // dil, GROUP_W), seq) for _, dil in ATTN_GROUPS] * 3
    return pl.pallas_call(
        functools.partial(_inproj_kernel, tm=tm),
        out_shape=tuple(grp_shapes) + tuple(jax.ShapeDtypeStruct((t, w), dt) for w, dt in zip(widths, dtypes)),
        grid=(t // tm,),
        in_specs=[
            pl.BlockSpec((tm, D_MODEL), row),
            pl.BlockSpec((1, D_MODEL), const),
            pl.BlockSpec((D_MODEL, n_all), const),
            pl.BlockSpec((1, W_A), const),
            pl.BlockSpec((1, W_A), const),
            pl.BlockSpec((PIECE, PIECE), const),
            pl.BlockSpec((1, D_B), const),
            pl.BlockSpec((1, D_B), const),
        ],
        out_specs=tuple(grp_specs) + tuple(pl.BlockSpec((tm, w), row) for w in widths),
        scratch_shapes=[pltpu.VMEM((GROUP_W // LANES, tm, LANES), F32)],
        compiler_params=_cparams(("parallel",)),
        name="inproj",
    )(x2, nw, w_all, qg, kg, e_mat, lbf, lbb)


def _attn_kernel(q_ref, kp_ref, kc_ref, kn_ref, vp_ref, vc_ref, vn_ref, bias_ref,
                 o_ref, lse_ref, *, tq, sub_len):
    i = pl.program_id(2)
    sq = bias_ref.shape[1] // 2
    nk = sq + 2 * N_SIDE
    lane = lax.broadcasted_iota(jnp.int32, (sq, LANES), 1)
    low = lane < HEAD_DIM_A
    ones = jnp.ones((nk, LANES), BF16)
    zero = jnp.zeros((sq, LANES), BF16)

    def pair(res, kk, vv, j, pr, colbias):
        cols = slice(pr * LANES, (pr + 1) * LANES)
        q2 = q_ref[res, j * sq:(j + 1) * sq, cols]
        k2 = kk[j * sq:j * sq + nk, cols]
        v2 = jnp.concatenate([vv[j * sq:j * sq + nk, cols], ones], axis=1)
        q_st = jnp.concatenate([jnp.where(low, q2, zero), jnp.where(low, zero, q2)], axis=0)
        s = lax.dot_general(q_st, k2, (((1,), (1,)), ((), ())), preferred_element_type=F32)
        yield
        s = s + bias_ref[pr]
        if colbias is not None:
            s = s + colbias
        m = jnp.max(s, axis=-1, keepdims=True)
        p = jnp.exp2(s - m).astype(BF16)
        r = jnp.dot(p, v2, preferred_element_type=F32)
        yield
        o2 = jnp.where(low, r[:sq, :LANES] / r[:sq, LANES:], r[sq:, :LANES] / r[sq:, LANES:])
        o_ref[res, j * sq:(j + 1) * sq, cols] = o2.astype(o_ref.dtype)
        lse = (m + jnp.log2(r[:, LANES:LANES + 1])) * LN2
        lse_ref[res, j * sq:(j + 1) * sq, cols] = jnp.where(low, lse[:sq], lse[sq:])

    n_pairs = HEADS_PER_GROUP // 2
    for res in range(q_ref.shape[0]):
        kk = jnp.concatenate([kp_ref[res], kc_ref[res], kn_ref[res]], axis=0)
        vv = jnp.concatenate([vp_ref[res], vc_ref[res], vn_ref[res]], axis=0)
        n_sub = tq // sq
        for j in range(n_sub):
            colbias = None
            if j in (0, n_sub - 1):
                kpos = i * tq + j * sq - N_SIDE + lax.broadcasted_iota(jnp.int32, (1, nk), 1)
                colbias = jnp.where((kpos >= 0) & (kpos < sub_len), 0.0, NEG).astype(F32)
            for pr in range(0, n_pairs, ATTN_PAIRS_IN_FLIGHT):
                _lockstep(*[pair(res, kk, vv, j, pr + d, colbias) for d in range(ATTN_PAIRS_IN_FLIGHT)])


def _attn_bias(sq, dil, slopes):
    nk = sq + 2 * N_SIDE
    rel = np.arange(nk)[None, :] - N_SIDE - np.arange(sq)[:, None]
    band = np.abs(rel) <= N_SIDE
    alibi = -slopes[:, None, None] * (dil * np.abs(rel)).astype(np.float32)[None]
    bias = np.where(band[None], alibi * LOG2E, NEG).astype(np.float32)
    return jnp.asarray(bias.reshape(HEADS_PER_GROUP // 2, 2 * sq, nk))


def _attention_group(q, k, v, g):
    b, dil, sub_len, _ = q.shape
    tq = min(TQ, sub_len)
    sq = min(SQ, sub_len)
    hb = tq // N_SIDE
    n_halo = sub_len // N_SIDE
    slopes = (2.0 ** (-8.0 * (np.arange(N_HEADS_A) + 1) / N_HEADS_A)).astype(np.float32)
    bias = _attn_bias(sq, dil, slopes[g * HEADS_PER_GROUP:(g + 1) * HEADS_PER_GROUP])
    cur = lambda bi, r, i: (bi, r, i, 0)
    prev = lambda bi, r, i: (bi, r, jnp.maximum(i * hb - 1, 0), 0)
    nxt = lambda bi, r, i: (bi, r, jnp.minimum((i + 1) * hb, n_halo - 1), 0)
    n_res = max(1, min(dil, TQ // tq))
    blk_q = (None, n_res, tq, GROUP_W)
    blk_h = (None, n_res, N_SIDE, GROUP_W)
    return pl.pallas_call(
        functools.partial(_attn_kernel, tq=tq, sub_len=sub_len),
        out_shape=(jax.ShapeDtypeStruct((b, dil, sub_len, GROUP_W), BF16),
                   jax.ShapeDtypeStruct((b, dil, sub_len, GROUP_W), F32)),
        grid=(b, dil // n_res, sub_len // tq),
        in_specs=[
            pl.BlockSpec(blk_q, cur),
            pl.BlockSpec(blk_h, prev), pl.BlockSpec(blk_q, cur), pl.BlockSpec(blk_h, nxt),
            pl.BlockSpec(blk_h, prev), pl.BlockSpec(blk_q, cur), pl.BlockSpec(blk_h, nxt),
            pl.BlockSpec(bias.shape, lambda bi, r, i: (0, 0, 0)),
        ],
        out_specs=(pl.BlockSpec(blk_q, cur), pl.BlockSpec(blk_q, cur)),
        compiler_params=_cparams(("parallel", "parallel", "parallel")),
        name=f"attn_d{dil}",
    )(q, k, k, k, v, v, v, bias)


def _hgrn_kernel(qf_ref, lf_ref, vf_ref, qr_ref, lr_ref, vr_ref, of_ref, or_ref,
                 sf_ref, sr_ref, qif_ref, qir_ref, uf_ref, ur_ref, df_ref, dr_ref, *, ts):
    c_len = HGRN_CHUNK
    nc = ts // c_len

    @pl.when(pl.program_id(1) == 0)
    def _():
        sf_ref[...] = jnp.zeros_like(sf_ref)
        sr_ref[...] = jnp.zeros_like(sr_ref)

    r_i = lax.broadcasted_iota(jnp.int32, (c_len, c_len), 0)
    c_i = lax.broadcasted_iota(jnp.int32, (c_len, c_len), 1)
    lower = r_i >= c_i
    upper = r_i <= c_i
    tri_f = jnp.where(lower, 1.0, 0.0).astype(BF16)
    tri_r = jnp.where(upper, 1.0, 0.0).astype(BF16)

    def cumsum(tri, x):
        hi = x.astype(BF16)
        r1 = x - hi.astype(F32)
        mid = r1.astype(BF16)
        lo = (r1 - mid.astype(F32)).astype(BF16)
        d = lambda a: jnp.dot(tri, a, preferred_element_type=F32)
        return d(hi) + d(mid) + d(lo)

    n_sub = c_len // HGRN_SUB
    shift = HGRN_SUB.bit_length() - 1
    sub_f = jnp.right_shift(lax.broadcasted_iota(jnp.int32, (c_len, D_B), 0), shift)
    sub_tf = jnp.right_shift(lax.broadcasted_iota(jnp.int32, (D_B, c_len), 1), shift)
    t_i = lax.broadcasted_iota(jnp.int32, (2 * c_len, c_len), 0) & (c_len - 1)
    s_i = lax.broadcasted_iota(jnp.int32, (2 * c_len, c_len), 1)
    same_sub2 = jnp.right_shift(t_i, shift) == jnp.right_shift(s_i, shift)
    lower2, upper2 = t_i >= s_i, t_i <= s_i
    lower_sub2, upper_sub2 = same_sub2 & lower2, same_sub2 & upper2
    row_head = lax.broadcasted_iota(jnp.int32, (2 * c_len, 2 * DK_B), 0) // c_len
    col_head = lax.broadcasted_iota(jnp.int32, (2 * c_len, 2 * DK_B), 1) // DK_B
    pair_cols = jnp.where(row_head == col_head, 1.0, 0.0).astype(BF16)
    pair_off = jnp.concatenate([pair_cols] * (n_sub - 1), axis=1)

    def intra(q_ref, l_ref, v_ref, o_ref, qi_ref, u_ref, d_ref, c, rev):
        rows = pl.ds(pl.multiple_of(c * c_len, c_len), c_len)
        lf = l_ref[0, rows, :]
        q = q_ref[0, rows, :].astype(F32)
        v = v_ref[0, rows, :]
        k = 1.0 - jnp.exp(lf)
        b = cumsum(tri_r if rev else tri_f, lf)
        yield
        sub = (n_sub - 1 - sub_f) if rev else sub_f

        def b_at(pos):
            r = c_len - 1 - pos if rev else pos
            return b[r:r + 1, :]

        def per_sub(vals):
            vals = vals[::-1] if rev else vals
            return jnp.concatenate([jnp.broadcast_to(x, (HGRN_SUB, D_B)) for x in vals], axis=0)

        a_end = [b_at(HGRN_SUB * j + HGRN_SUB - 1) for j in range(n_sub)]
        a_start = [jnp.zeros((1, D_B), F32)] + a_end[:-1]
        btot = a_end[-1]
        end_full, start_full = per_sub(a_end), per_sub(a_start)
        stores = [(qi_ref, (rows, slice(None)), (q * jnp.exp(b)).astype(BF16))]
        q_off = [jnp.where(sub > j, q * jnp.exp(jnp.minimum(b - a_end[j], 0.0)), 0.0).astype(BF16)
                 for j in range(n_sub - 1)]
        k_end = k * jnp.exp(end_full - b)
        sub_t = (n_sub - 1 - sub_tf) if rev else sub_tf
        q_dia = (q * jnp.exp(b - start_full)).astype(BF16)
        k_upd = (k * jnp.exp(btot - b)).astype(BF16)
        k_end_t = k_end.T
        k_dia_t = (k * jnp.exp(jnp.minimum(start_full - b, HGRN_EXP_CLAMP))).T.astype(BF16)
        k_off_t = [jnp.where(sub_t == j, k_end_t, 0.0).astype(BF16) for j in range(n_sub - 1)]
        dec8_t = jnp.broadcast_to(jnp.exp(btot), (SUBLANES, D_B)).T
        keep, keep_d = (upper2, upper_sub2) if rev else (lower2, lower_sub2)
        two = lambda a: jnp.concatenate([a, a], axis=0)
        for p in range(N_HEADS_B // 2):
            ps = slice(2 * p * DK_B, (2 * p + 2) * DK_B)
            qc = two(jnp.concatenate([q_off[j][:, ps] for j in range(n_sub - 1)], axis=1)) * pair_off
            kc = jnp.concatenate([k_off_t[j][ps, :] for j in range(n_sub - 1)], axis=0)
            yield
            sc = jnp.dot(qc, kc, preferred_element_type=F32)
            sc_d = jnp.dot(two(q_dia[:, ps]) * pair_cols, k_dia_t[ps, :], preferred_element_type=F32)
            u2 = lax.dot_general(k_upd[:, ps], v[:, ps], (((0,), (0,)), ((), ())),
                                 preferred_element_type=F32)
            yield
            sc = (jnp.where(keep, sc, 0.0) + jnp.where(keep_d, sc_d, 0.0)).astype(BF16)
            o2 = jnp.dot(sc, v[:, ps], preferred_element_type=F32)
            for i in range(2):
                h = 2 * p + i
                hs = slice(h * DK_B, (h + 1) * DK_B)
                blk = slice(i * DK_B, (i + 1) * DK_B)
                stores.append((o_ref, (0, rows, hs), o2[i * c_len:(i + 1) * c_len, blk]))
                stores.append((u_ref, (c, h), u2[blk, blk]))
                stores.append((d_ref, (c, h), jnp.broadcast_to(dec8_t[hs, 0:1], (DK_B, DK_B))))
        return stores

    def carry_state(o_ref, qi_ref, u_ref, d_ref, s_ref, c):
        rows = pl.ds(pl.multiple_of(c * c_len, c_len), c_len)
        zero = jnp.zeros((DK_B, DK_B), BF16)
        stores = []
        for p in range(N_HEADS_B // 2):
            ps = slice(2 * p * DK_B, (2 * p + 2) * DK_B)
            sa, sb = s_ref[2 * p], s_ref[2 * p + 1]
            s_bd = jnp.concatenate([jnp.concatenate([sa.astype(BF16), zero], axis=1),
                                    jnp.concatenate([zero, sb.astype(BF16)], axis=1)], axis=0)
            o_new = o_ref[0, rows, ps] + jnp.dot(qi_ref[rows, ps], s_bd, preferred_element_type=F32)
            stores.append((o_ref, (0, rows, ps), o_new))
            stores.append((s_ref, (2 * p,), sa * d_ref[c, 2 * p] + u_ref[c, 2 * p]))
            stores.append((s_ref, (2 * p + 1,), sb * d_ref[c, 2 * p + 1] + u_ref[c, 2 * p + 1]))
        return stores

    def commit(stores):
        for ref, idx, val in stores:
            ref[idx] = val

    def intra_body(trip, carry):
        gens = []
        for i in range(HGRN_CHUNKS_PER_TRIP):
            c = trip * HGRN_CHUNKS_PER_TRIP + i
            gens.append(intra(qf_ref, lf_ref, vf_ref, of_ref, qif_ref, uf_ref, df_ref, c, False))
            gens.append(intra(qr_ref, lr_ref, vr_ref, or_ref, qir_ref, ur_ref, dr_ref, c, True))
        commit(sum(_lockstep(*gens), []))
        return carry

    def state_body(c, carry):
        commit(carry_state(of_ref, qif_ref, uf_ref, df_ref, sf_ref, c)
               + carry_state(or_ref, qir_ref, ur_ref, dr_ref, sr_ref, nc - 1 - c))
        return carry

    lax.fori_loop(0, nc // HGRN_CHUNKS_PER_TRIP, intra_body, 0)
    lax.fori_loop(0, nc, state_body, 0)


def _hgrn(qb, lff, lfb, ib):
    b, s, _ = qb.shape
    ts = min(TS_HGRN, s)
    nt = s // ts
    nc = ts // HGRN_CHUNK
    fwd = lambda bi, j: (bi, j, 0)
    rev = lambda bi, j: (bi, nt - 1 - j, 0)
    blk = (1, ts, D_B)
    return pl.pallas_call(
        functools.partial(_hgrn_kernel, ts=ts),
        out_shape=(jax.ShapeDtypeStruct((b, s, D_B), F32), jax.ShapeDtypeStruct((b, s, D_B), F32)),
        grid=(b, nt),
        in_specs=[pl.BlockSpec(blk, fwd), pl.BlockSpec(blk, fwd), pl.BlockSpec(blk, fwd),
                  pl.BlockSpec(blk, rev), pl.BlockSpec(blk, rev), pl.BlockSpec(blk, rev)],
        out_specs=(pl.BlockSpec(blk, fwd), pl.BlockSpec(blk, rev)),
        scratch_shapes=[pltpu.VMEM((N_HEADS_B, DK_B, DK_B), F32),
                        pltpu.VMEM((N_HEADS_B, DK_B, DK_B), F32),
                        pltpu.VMEM((ts, D_B), BF16),
                        pltpu.VMEM((ts, D_B), BF16),
                        pltpu.VMEM((nc, N_HEADS_B, DK_B, DK_B), F32),
                        pltpu.VMEM((nc, N_HEADS_B, DK_B, DK_B), F32),
                        pltpu.VMEM((nc, N_HEADS_B, DK_B, DK_B), F32),
                        pltpu.VMEM((nc, N_HEADS_B, DK_B, DK_B), F32)],
        compiler_params=_cparams(("parallel", "arbitrary")),
        name="hgrn",
    )(qb, lff, ib, qb, lfb, ib)


def _mix_kernel(x_ref, o1_ref, o2_ref, o3_ref, l1_ref, l2_ref, l3_ref, of_ref, ob_ref, og_ref,
                gt_ref, wa_ref, wb_ref, wo_ref, ogain_ref, nmoe_ref, wrh_ref, wrl_ref,
                br_ref, cnt_ref,
                h_ref, xt_ref, tw_ref, ps_ref, ct_ref, rt_ref, cnt_out_ref, run_ref, so_ref, *, tm):
    i = pl.program_id(0)

    @pl.when(i == 0)
    def _():
        run_ref[...] = cnt_ref[...]

    def token_major(src_ref, scr_ref, dil):
        if dil == 1:
            return src_ref[0].astype(F32)
        n_chunk = scr_ref.shape[0]
        for r in range(dil):
            blk = src_ref[r].astype(F32)
            for c in range(n_chunk):
                scr_ref[c, pl.ds(r, tm // dil, stride=dil), :] = blk[:, c * LANES:(c + 1) * LANES]
        return jnp.concatenate([scr_ref[c] for c in range(n_chunk)], axis=-1)

    dils = [dil for _, dil in ATTN_GROUPS]
    l1, l2, l3 = [token_major(r, so_ref, d) for r, d in zip((l1_ref, l2_ref, l3_ref), dils)]
    mx = jnp.maximum(jnp.maximum(l1, l2), l3)
    e1, e2, e3 = jnp.exp(l1 - mx), jnp.exp(l2 - mx), jnp.exp(l3 - mx)
    attn = (e1 * token_major(o1_ref, so_ref, dils[0]) + e2 * token_major(o2_ref, so_ref, dils[1])
            + e3 * token_major(o3_ref, so_ref, dils[2])) / (e1 + e2 + e3)

    o = of_ref[...] + ob_ref[...]
    parts = []
    for h in range(N_HEADS_B):
        oh = o[:, h * DK_B:(h + 1) * DK_B]
        ms = jnp.mean(oh * oh, axis=-1, keepdims=True)
        parts.append(oh * lax.rsqrt(ms + EPS))
    hg = jnp.concatenate(parts, axis=-1) * ogain_ref[...] * og_ref[...].astype(F32)

    attn_bf, hg_bf = attn.astype(BF16), hg.astype(BF16)
    tp = TM_MIX
    lane = lax.broadcasted_iota(jnp.int32, (tp, LANES), 1)
    r_i = lax.broadcasted_iota(jnp.int32, (tp, tp), 0)
    c_i = lax.broadcasted_iota(jnp.int32, (tp, tp), 1)
    tri = jnp.where(r_i > c_i, 1.0, 0.0).astype(BF16)
    e_r = lax.broadcasted_iota(jnp.int32, (LANES, LANES), 0)
    e_c = lax.broadcasted_iota(jnp.int32, (LANES, LANES), 1)
    before_e = jnp.where(e_r < e_c, 1.0, 0.0).astype(BF16)
    row_id = lax.broadcasted_iota(jnp.int32, (tp * TOP_K, tp), 0).astype(F32)

    def token_chain(part):
        rs = slice(part * tp, (part + 1) * tp)
        pa = jnp.dot(attn_bf[rs], wa_ref[...], preferred_element_type=F32)
        pb = jnp.dot(hg_bf[rs], wb_ref[...], preferred_element_type=F32)
        yield
        mixed = (gt_ref[rs, :D_MODEL].astype(F32) * pa + gt_ref[rs, D_MODEL:].astype(F32) * pb)
        h = x_ref[rs, :] + jnp.dot(mixed.astype(BF16), wo_ref[...], preferred_element_type=F32)
        yield
        h_ref[rs, :] = h
        ms = jnp.mean(h * h, axis=-1, keepdims=True)
        hn = h * lax.rsqrt(ms + EPS) * nmoe_ref[...]
        hi, lo = _split_bf16(hn)
        lg = (jnp.dot(hi, wrh_ref[...], preferred_element_type=F32)
              + jnp.dot(lo, wrh_ref[...], preferred_element_type=F32)
              + jnp.dot(hi, wrl_ref[...], preferred_element_type=F32)) + br_ref[...]
        yield
        vals, idxs = [], []
        onehot = jnp.zeros((tp, LANES), F32)
        for _ in range(TOP_K):
            m = jnp.max(lg, axis=-1, keepdims=True)
            idx = jnp.min(jnp.where(lg == m, lane, LANES), axis=-1, keepdims=True)
            sel = lane == idx
            onehot = jnp.where(sel, 1.0, onehot)
            lg = jnp.where(sel, NEG * 2, lg)
            vals.append(m)
            idxs.append(idx)
        exps = [jnp.exp(v - vals[0]) for v in vals]
        inv = 1.0 / (exps[0] + exps[1] + exps[2] + exps[3])
        local = jnp.dot(tri, onehot.astype(BF16), preferred_element_type=F32)
        cnt_tile = jnp.sum(onehot, axis=0, keepdims=True)
        off = jnp.dot(jnp.broadcast_to(cnt_tile, (8, LANES)).astype(BF16), before_e,
                      preferred_element_type=F32)[0:1]
        yield
        slot = off + local
        tw = jnp.zeros((tp, LANES), F32)
        ps = jnp.full((tp, LANES), -1.0, F32)
        for k in range(TOP_K):
            slot_k = jnp.sum(jnp.where(lane == idxs[k], slot, 0.0), axis=-1, keepdims=True)
            tw = jnp.where(lane == k, exps[k] * inv, tw)
            ps = jnp.where(lane == k, slot_k, ps)
        tw_ref[rs, :] = tw
        ps_ref[rs, :] = ps.astype(jnp.int32)
        ps_t = ps.T
        perm = jnp.zeros((tp * TOP_K, tp), F32)
        for k in range(TOP_K):
            perm = jnp.where(row_id == ps_t[k:k + 1, :], 1.0, perm)
        rows = jnp.dot(perm.astype(BF16), hi, preferred_element_type=F32)
        yield
        for c in range(D_MODEL // LANES):
            xt_ref[pl.ds(part * tp * TOP_K * ROW_SUB + c, tp * TOP_K, stride=ROW_SUB), :] = (
                rows[:, c * LANES:(c + 1) * LANES])
        return cnt_tile

    counts = _lockstep(*[token_chain(p) for p in range(tm // tp)])
    run = run_ref[...]
    for part, cnt_tile in enumerate(counts):
        ct_ref[part] = cnt_tile
        rt_ref[part] = run
        run = run + cnt_tile
    run_ref[...] = run
    cnt_out_ref[...] = run


def _mix(x2, s, o1, o2, o3, l1, l2, l3, of, ob, og, gt, wa, wb, wo, ogain, nmoe, wrh, wrl, br, cnt):
    t = x2.shape[0]
    tm = TM_MIX * MIX_TILES_PER_STEP
    tps = s // tm
    row = lambda i: (i, 0)
    const = lambda i: (0, 0)
    seq = lambda i: (i // tps, 0, i % tps, 0)
    rb = lambda w: pl.BlockSpec((tm, w), row)
    cb = lambda a: pl.BlockSpec(a.shape, const)
    gb = lambda a: pl.BlockSpec((None, a.shape[1], tm // a.shape[1], a.shape[3]), seq)
    per_tile = pl.BlockSpec((MIX_TILES_PER_STEP, 1, LANES), lambda i: (i, 0, 0))
    return pl.pallas_call(
        functools.partial(_mix_kernel, tm=tm),
        out_shape=(jax.ShapeDtypeStruct((t, D_MODEL), F32),
                   jax.ShapeDtypeStruct((t * TOP_K * ROW_SUB, LANES), F32),
                   jax.ShapeDtypeStruct((t, LANES), F32),
                   jax.ShapeDtypeStruct((t, LANES), jnp.int32),
                   jax.ShapeDtypeStruct((t // TM_MIX, 1, LANES), F32),
                   jax.ShapeDtypeStruct((t // TM_MIX, 1, LANES), F32),
                   jax.ShapeDtypeStruct((1, LANES), F32)),
        grid=(t // tm,),
        in_specs=[rb(D_MODEL), gb(o1), gb(o2), gb(o3), gb(l1), gb(l2), gb(l3),
                  rb(D_B), rb(D_B), rb(D_B), rb(2 * D_MODEL),
                  cb(wa), cb(wb), cb(wo), cb(ogain), cb(nmoe), cb(wrh), cb(wrl), cb(br), cb(cnt)],
        out_specs=(rb(D_MODEL), pl.BlockSpec((tm * TOP_K * ROW_SUB, LANES), row), rb(LANES), rb(LANES),
                   per_tile, per_tile, pl.BlockSpec((1, LANES), const)),
        scratch_shapes=[pltpu.VMEM((1, LANES), F32), pltpu.VMEM((GROUP_W // LANES, tm, LANES), F32)],
        compiler_params=_cparams(("arbitrary",)),
        name="mix",
    )(x2, o1, o2, o3, l1, l2, l3, of, ob, og, gt, wa, wb, wo, ogain, nmoe, wrh, wrl, br, cnt)


def _start_tile_segments(tile, off_tbl, row_tbl, len_tbl, make_piece):
    def segment(e, carry):
        sidx = tile * N_EXPERTS + e
        t0, r0, n = off_tbl[sidx], row_tbl[sidx], len_tbl[sidx]
        n_bulk = jnp.right_shift(n, SEG_CHUNK.bit_length() - 1)

        def bulk(j, c):
            make_piece(t0 + j * SEG_CHUNK, r0 + j * SEG_CHUNK, SEG_CHUNK).start()
            return c

        lax.fori_loop(0, n_bulk, bulk, 0)
        done = n_bulk * SEG_CHUNK
        bit = SEG_CHUNK // 2
        while bit >= 1:
            has = (n & bit) != 0

            @pl.when(has)
            def _(done=done, bit=bit):
                make_piece(t0 + done, r0 + done, bit).start()

            done = done + jnp.where(has, bit, 0)
            bit //= 2
        return carry

    lax.fori_loop(0, N_EXPERTS, segment, 0)


def _row_slice(row, n_rows):
    return pl.ds(pl.multiple_of(row * ROW_SUB, ROW_SUB), n_rows * ROW_SUB)


def _dispatch_kernel(off_tbl, row_tbl, len_tbl, fill_row, fill_len, xa_ref, xb_ref, xs_ref, buf_ref,
                     zero_ref, sems_in, sems_out, sem, *, tiles_a):
    i = pl.program_id(0)
    n_tiles = pl.num_programs(0)
    blk_rows = buf_ref.shape[1]

    def fetch(tile):
        slot = tile % DISPATCH_SLOTS

        def copy(src_ref, src_tile):
            start = pl.multiple_of(src_tile * blk_rows, blk_rows)
            pltpu.make_async_copy(src_ref.at[pl.ds(start, blk_rows)], buf_ref.at[slot], sems_in.at[slot]).start()

        @pl.when(tile < tiles_a)
        def _():
            copy(xa_ref, tile)

        @pl.when(tile >= tiles_a)
        def _():
            copy(xb_ref, tile - tiles_a)

    def scatter_done(tile):
        slot = tile % DISPATCH_SLOTS
        pltpu.make_async_copy(buf_ref.at[slot], xs_ref.at[pl.ds(0, blk_rows)], sems_out.at[slot]).wait()

    def zeros_to(row, n_rows):
        return pltpu.make_async_copy(zero_ref.at[pl.ds(0, n_rows * ROW_SUB)],
                                     xs_ref.at[_row_slice(row, n_rows)], sem)

    @pl.when(i == 0)
    def _():
        fetch(i)
        zero_ref[...] = jnp.zeros_like(zero_ref)

        def fill(e, total):
            r0, n = fill_row[e], fill_len[e]
            n_blk = jnp.right_shift(n, BM.bit_length() - 1)

            def blocks(j, c):
                zeros_to(r0 + j * BM, BM).start()
                return c

            lax.fori_loop(0, n_blk, blocks, 0)
            done = n_blk * BM
            bit = BM // 2
            while bit >= 1:
                has = (n & bit) != 0

                @pl.when(has)
                def _(done=done, bit=bit):
                    zeros_to(r0 + done, bit).start()

                done = done + jnp.where(has, bit, 0)
                bit //= 2
            return total + n

        total = lax.fori_loop(0, N_EXPERTS + 1, fill, 0)
        bit = 1
        while bit * ROW_SUB <= xs_ref.shape[0]:
            @pl.when((total & bit) != 0)
            def _(bit=bit):
                n = bit * ROW_SUB
                pltpu.make_async_copy(xs_ref.at[pl.ds(0, n)], xs_ref.at[pl.ds(0, n)], sem).wait()

            bit *= 2

    @pl.when(i + 1 < n_tiles)
    def _():
        fetch(i + 1)

    slot = i % DISPATCH_SLOTS
    pltpu.make_async_copy(xa_ref.at[pl.ds(0, blk_rows)], buf_ref.at[slot], sems_in.at[slot]).wait()

    def piece(tile_row, expert_row, n_rows):
        return pltpu.make_async_copy(buf_ref.at[slot, _row_slice(tile_row, n_rows)],
                                     xs_ref.at[_row_slice(expert_row, n_rows)], sems_out.at[slot])

    _start_tile_segments(i, off_tbl, row_tbl, len_tbl, piece)

    @pl.when(i >= 1)
    def _():
        scatter_done(i - 1)

    @pl.when(i == n_tiles - 1)
    def _():
        scatter_done(i)


def _dispatch(off_tbl, row_tbl, len_tbl, fill_row, fill_len, xt_a, xt_b, n_rows):
    blk = TM_MIX * TOP_K * ROW_SUB
    tiles_a, tiles_b = xt_a.shape[0] // blk, xt_b.shape[0] // blk
    return pl.pallas_call(
        functools.partial(_dispatch_kernel, tiles_a=tiles_a),
        out_shape=jax.ShapeDtypeStruct((n_rows * ROW_SUB, LANES), F32),
        grid_spec=pltpu.PrefetchScalarGridSpec(
            num_scalar_prefetch=5,
            grid=(tiles_a + tiles_b,),
            in_specs=[pl.BlockSpec(memory_space=pl.ANY), pl.BlockSpec(memory_space=pl.ANY)],
            out_specs=pl.BlockSpec(memory_space=pl.ANY),
            scratch_shapes=[pltpu.VMEM((DISPATCH_SLOTS, blk, LANES), F32),
                            pltpu.VMEM((BM * ROW_SUB, LANES), F32),
                            pltpu.SemaphoreType.DMA((DISPATCH_SLOTS,)),
                            pltpu.SemaphoreType.DMA((DISPATCH_SLOTS,)),
                            pltpu.SemaphoreType.DMA],
        ),
        compiler_params=_cparams(("arbitrary",)),
        name="dispatch",
    )(off_tbl, row_tbl, len_tbl, fill_row, fill_len, xt_a, xt_b)


def _combine_kernel(off_tbl, row_tbl, len_tbl, ps_ref, tw_ref, h_ref, ys_ref, y_ref, buf_ref, sems, *, tm):
    i = pl.program_id(0)
    n_rows = tm * TOP_K

    def start(tile):
        slot = tile % 2

        def piece(tile_row, expert_row, n):
            return pltpu.make_async_copy(ys_ref.at[_row_slice(expert_row, n)],
                                         buf_ref.at[slot, _row_slice(tile_row, n)], sems.at[slot])

        _start_tile_segments(tile, off_tbl, row_tbl, len_tbl, piece)

    @pl.when(i == 0)
    def _():
        start(i)

    @pl.when(i + 1 < pl.num_programs(0))
    def _():
        start(i + 1)

    slot = i % 2
    yt_ref = buf_ref.at[slot]
    pltpu.make_async_copy(ys_ref.at[pl.ds(0, n_rows * ROW_SUB)], yt_ref, sems.at[slot]).wait()
    ysorted = jnp.concatenate(
        [yt_ref[pl.ds(c, n_rows, stride=ROW_SUB), :] for c in range(ROW_SUB)], axis=-1).astype(BF16)
    col = lax.broadcasted_iota(jnp.int32, (tm, n_rows), 1)
    ps = ps_ref[...]
    tw = tw_ref[...]
    pw = jnp.zeros((tm, n_rows), F32)
    for k in range(TOP_K):
        pw = jnp.where(col == ps[:, k:k + 1], tw[:, k:k + 1], pw)
    hi, lo = _split_bf16(pw)
    y_ref[...] = (h_ref[...] + jnp.dot(hi, ysorted, preferred_element_type=F32)
                  + jnp.dot(lo, ysorted, preferred_element_type=F32))


def _combine(off_tbl, row_tbl, len_tbl, ps, tw, h, ys):
    t = h.shape[0]
    tm = TM_MIX
    row = lambda i, *_: (i, 0)
    return pl.pallas_call(
        functools.partial(_combine_kernel, tm=tm),
        out_shape=jax.ShapeDtypeStruct((t, D_MODEL), F32),
        grid_spec=pltpu.PrefetchScalarGridSpec(
            num_scalar_prefetch=3,
            grid=(t // tm,),
            in_specs=[pl.BlockSpec((tm, LANES), row),
                      pl.BlockSpec((tm, LANES), row),
                      pl.BlockSpec((tm, D_MODEL), row),
                      pl.BlockSpec(memory_space=pl.ANY)],
            out_specs=pl.BlockSpec((tm, D_MODEL), row),
            scratch_shapes=[pltpu.VMEM((2, tm * TOP_K * ROW_SUB, LANES), F32),
                            pltpu.SemaphoreType.DMA((2,))],
        ),
        compiler_params=_cparams(("arbitrary",)),
        name="combine",
    )(off_tbl, row_tbl, len_tbl, ps, tw, h, ys)


def _experts_kernel(be_ref, nused_ref, xs_ref, wgu_ref, bgu_ref, wd_ref, bd_ref, ys_ref, wgu_bf, wd_bf):
    i = pl.program_id(0)
    used = i < nused_ref[0]
    new_expert = (i == 0) | (be_ref[i] != be_ref[jnp.maximum(i - 1, 0)])

    @pl.when(used & new_expert)
    def _():
        wgu_bf[...] = wgu_ref[0].astype(BF16)
        wd_bf[...] = wd_ref[0].astype(BF16)

    @pl.when(used)
    def _():
        x = jnp.concatenate(
            [xs_ref[pl.ds(c, BM, stride=ROW_SUB), :] for c in range(ROW_SUB)], axis=-1).astype(BF16)
        hh = jnp.dot(x, wgu_bf[...], preferred_element_type=F32) + bgu_ref[0]
        gate = jnp.minimum(hh[:, :D_FF], SWIGLU_LIMIT)
        up = jnp.clip(hh[:, D_FF:], -SWIGLU_LIMIT, SWIGLU_LIMIT)
        glu = gate * jax.nn.sigmoid(SWIGLU_ALPHA * gate)
        act = ((up + 1.0) * glu).astype(BF16)
        y = jnp.dot(act, wd_bf[...], preferred_element_type=F32) + bd_ref[0]
        for c in range(ROW_SUB):
            ys_ref[pl.ds(c, BM, stride=ROW_SUB), :] = y[:, c * LANES:(c + 1) * LANES]

    @pl.when(i >= nused_ref[0])
    def _():
        ys_ref[...] = jnp.zeros_like(ys_ref)


def _experts(block_e, nused, xs, wgu, bgu, wd, bd):
    nb = xs.shape[0] // (BM * ROW_SUB)
    emap3 = lambda i, be, nu: (be[i], 0, 0)
    rows = pl.BlockSpec((BM * ROW_SUB, LANES), lambda i, be, nu: (i, 0))
    return pl.pallas_call(
        _experts_kernel,
        out_shape=jax.ShapeDtypeStruct(xs.shape, F32),
        grid_spec=pltpu.PrefetchScalarGridSpec(
            num_scalar_prefetch=2,
            grid=(nb,),
            in_specs=[rows,
                      pl.BlockSpec((1, D_MODEL, 2 * D_FF), emap3),
                      pl.BlockSpec((1, 1, 2 * D_FF), emap3),
                      pl.BlockSpec((1, D_FF, D_MODEL), emap3),
                      pl.BlockSpec((1, 1, D_MODEL), emap3)],
            out_specs=rows,
            scratch_shapes=[pltpu.VMEM((D_MODEL, 2 * D_FF), BF16), pltpu.VMEM((D_FF, D_MODEL), BF16)],
        ),
        compiler_params=_cparams(("arbitrary",)),
        name="experts",
    )(block_e, nused, xs, wgu, bgu, wd, bd)


def _head_indicator(n_cols, head_dim):
    e = np.zeros((n_cols, LANES), np.float32)
    e[np.arange(n_cols), np.arange(n_cols) // head_dim] = 1.0
    return e


def _mixer(x, prm):
    b, s, d = x.shape
    t = b * s
    x2 = x.reshape(t, d)
    res = _inproj(x2, b, s, prm["norm_mix"], prm["w_all"], prm["q_gain"], prm["k_gain"], prm["e_in"],
                  prm["lb_f"], prm["lb_b"])
    n_grp = len(ATTN_GROUPS)
    qs, ks, vs = res[:n_grp], res[n_grp:2 * n_grp], res[2 * n_grp:3 * n_grp]
    qb, lff, lfb, ib, og, gt = res[3 * n_grp:]
    r3 = lambda a: a.reshape(b, s, a.shape[-1])
    outs, lses = [], []
    for g in range(n_grp):
        o, lse = _attention_group(qs[g], ks[g], vs[g], g)
        outs.append(o)
        lses.append(lse)
    of, ob = _hgrn(r3(qb), r3(lff), r3(lfb), r3(ib))
    return x2, outs, lses, of.reshape(t, D_B), ob.reshape(t, D_B), og, gt


def kernel(x_prompt, x_sample, norm_mix, w_in, q_gain, k_gain, hgrn_lb, hgrn_o_gain, w_gate, w_proj_a,
           w_proj_b, w_out, norm_moe, w_router, b_router, w_gu, b_gu, w_down, b_down):
    l = 0
    lb = jnp.cumsum(jax.nn.softmax(hgrn_lb.astype(F32), axis=1), axis=1)
    wr = jnp.zeros((D_MODEL, LANES), F32).at[:, :N_EXPERTS].set(w_router[l])
    wr_hi = wr.astype(BF16)
    prm = {
        "norm_mix": norm_mix[l].reshape(1, D_MODEL),
        "w_all": jnp.concatenate([w_in[l], w_gate[l]], axis=1).astype(BF16),
        "q_gain": q_gain[l].reshape(1, W_A) * (HEAD_DIM_A ** -0.5 * LOG2E),
        "k_gain": k_gain[l].reshape(1, W_A),
        "e_in": jnp.asarray(_head_indicator(PIECE, HEAD_DIM_A) @ _head_indicator(PIECE, HEAD_DIM_A).T, BF16),
        "lb_f": lb[0, l].reshape(1, D_B),
        "lb_b": lb[1, l].reshape(1, D_B),
    }
    wa, wb, wo = w_proj_a[l].astype(BF16), w_proj_b[l].astype(BF16), w_out[l].astype(BF16)
    ogain = hgrn_o_gain[l].reshape(1, D_B)
    nmoe = norm_moe[l].reshape(1, D_MODEL)
    wr_lo = (wr - wr_hi.astype(F32)).astype(BF16)
    br = jnp.full((1, LANES), NEG, F32).at[0, :N_EXPERTS].set(b_router[l])

    cnt = jnp.zeros((1, LANES), F32)
    per_batch = []
    for x in (x_prompt, x_sample):
        x2, outs, lses, of, ob, og, gt = _mixer(x, prm)
        h, xt, tw, ps, ct, rt, cnt = _mix(x2, x.shape[1], outs[0], outs[1], outs[2], lses[0], lses[1], lses[2],
                                          of, ob, og, gt, wa, wb, wo, ogain, nmoe, wr_hi, wr_lo, br, cnt)
        per_batch.append((x.shape, h, xt, tw, ps, ct, rt))

    n_tok = sum(pb[1].shape[0] for pb in per_batch)
    sizes = cnt[0, :N_EXPERTS].astype(jnp.int32)
    pad_sizes = (sizes + BM - 1) // BM * BM
    pad_ends = jnp.cumsum(pad_sizes)
    pad_starts = pad_ends - pad_sizes
    nb = (n_tok * TOP_K) // BM + N_EXPERTS
    block_start = jnp.arange(nb, dtype=jnp.int32) * BM
    block_e = jnp.minimum(jnp.sum(pad_ends[None, :] <= block_start[:, None], axis=1),
                          N_EXPERTS - 1).astype(jnp.int32)
    nused = (pad_ends[-1:] // BM).astype(jnp.int32)

    tables = []
    for _, _, _, _, _, ct, rt in per_batch:
        cnt_te = ct[:, 0, :N_EXPERTS].astype(jnp.int32)
        tile_off = jnp.cumsum(cnt_te, axis=1) - cnt_te
        expert_row = pad_starts[None, :] + rt[:, 0, :N_EXPERTS].astype(jnp.int32)
        tables.append((tile_off.reshape(-1), expert_row.reshape(-1), cnt_te.reshape(-1)))

    fill_row = jnp.concatenate([pad_starts + sizes, pad_ends[-1:]]).astype(jnp.int32)
    fill_len = jnp.concatenate([pad_sizes - sizes, nb * BM - pad_ends[-1:]]).astype(jnp.int32)
    both = [jnp.concatenate(cols) for cols in zip(*tables)]
    xs = _dispatch(*both, fill_row, fill_len, per_batch[0][2], per_batch[1][2], nb * BM)
    ys = _experts(block_e, nused, xs, w_gu[l], b_gu[l].reshape(N_EXPERTS, 1, 2 * D_FF),
                  w_down[l], b_down[l].reshape(N_EXPERTS, 1, D_MODEL))
    results = []
    for (shape, h, _, tw, ps, _, _), tbl in zip(per_batch, tables):
        results.append(_combine(*tbl, ps, tw, h, ys).reshape(shape))
    return tuple(results)
```

```python
import functools
import math

import jax
import jax.numpy as jnp
import numpy as np
from jax import lax
from jax.experimental import pallas as pl
from jax.experimental.pallas import tpu as pltpu

F32 = jnp.float32
BF16 = jnp.bfloat16

D_MODEL = 1024
ATTN_GROUPS = ((128, 1), (512, 4), (2048, 16))
HEADS_PER_GROUP = 8
N_HEADS_A = 24
HEAD_DIM_A = 64
W_A = N_HEADS_A * HEAD_DIM_A
GROUP_W = HEADS_PER_GROUP * HEAD_DIM_A
N_SIDE = 64
N_HEADS_B = 4
DK_B = 128
HGRN_CHUNK = 64
HGRN_SUB = 16
HGRN_EXP_CLAMP = 80.0
HGRN_CHUNKS_PER_TRIP = 4
D_B = N_HEADS_B * DK_B
D_IN = 3 * W_A + 5 * D_B
N_EXPERTS = 32
TOP_K = 4
D_FF = 1024
SWIGLU_LIMIT = 7.0
SWIGLU_ALPHA = 1.702
EPS = 1e-6
NEG = -1e30
LOG2E = math.log2(math.e)
LN2 = math.log(2.0)

LANES = 128
VMEM_LIMIT = 56 * 1024 * 1024

TM_IN = 256
PIECE = 512
TQ = 2048
SQ = 128
ATTN_PAIRS_IN_FLIGHT = 1
TS_HGRN = 512
TM_MIX = 256
MIX_TILES_PER_STEP = 2
BM = 512
SUBLANES = 8
ROW_SUB = D_MODEL // LANES
assert ROW_SUB == SUBLANES
DISPATCH_SLOTS = 3
SEG_CHUNK = 16


def _cparams(sem):
    return pltpu.CompilerParams(dimension_semantics=sem, vmem_limit_bytes=VMEM_LIMIT)


def _lockstep(*gens):
    out = [None] * len(gens)
    live = list(range(len(gens)))
    while live:
        for g in list(live):
            try:
                next(gens[g])
            except StopIteration as done:
                out[g] = done.value
                live.remove(g)
    return out


def _split_bf16(x):
    hi = x.astype(BF16)
    lo = (x - hi.astype(F32)).astype(BF16)
    return hi, lo


def _inproj_kernel(x_ref, nw_ref, w_ref, qg_ref, kg_ref, e_ref, lbf_ref, lbb_ref,
                   q0_ref, q1_ref, q2_ref, k0_ref, k1_ref, k2_ref, v0_ref, v1_ref, v2_ref,
                   qb_ref, lff_ref, lfb_ref, ib_ref, og_ref, gt_ref, scr_ref, *, tm):
    q_refs, k_refs, v_refs = (q0_ref, q1_ref, q2_ref), (k0_ref, k1_ref, k2_ref), (v0_ref, v1_ref, v2_ref)
    x = x_ref[...]
    ms = jnp.mean(x * x, axis=-1, keepdims=True)
    xn = (x * lax.rsqrt(ms + EPS) * nw_ref[...]).astype(BF16)

    def proj(col):
        return jnp.dot(xn, w_ref[:, col:col + PIECE], preferred_element_type=F32)

    def normed(dst_ref, col, gain, dil):
        y = proj(col)
        yield
        ss = jnp.dot((y * y).astype(BF16), e_ref[...], preferred_element_type=F32)
        yield
        store_group(dst_ref, y * lax.rsqrt(ss * (1.0 / HEAD_DIM_A) + EPS) * gain, dil)

    def plain(dst_ref, col, dil):
        y = proj(col)
        yield
        yield
        store_group(dst_ref, y, dil)

    def store_group(dst_ref, y, dil):
        if dil == 1:
            dst_ref[0, 0] = y.astype(BF16)
            return
        for c in range(GROUP_W // LANES):
            scr_ref[c] = y[:, c * LANES:(c + 1) * LANES]
        for r in range(dil):
            rows = pl.ds(r, tm // dil, stride=dil)
            dst_ref[0, r] = jnp.concatenate(
                [scr_ref[c, rows, :] for c in range(GROUP_W // LANES)], axis=-1).astype(BF16)

    for g, (_, dil) in enumerate(ATTN_GROUPS):
        c = g * GROUP_W
        _lockstep(normed(q_refs[g], c, qg_ref[:, c:c + GROUP_W], dil),
                  normed(k_refs[g], W_A + c, kg_ref[:, c:c + GROUP_W], dil),
                  plain(v_refs[g], 2 * W_A + c, dil))
    base = 3 * W_A
    qb = proj(base)
    qb_ref[...] = (qb * jax.nn.sigmoid(qb) * (DK_B ** -0.5)).astype(BF16)
    for dst, lb_ref, off in ((lff_ref, lbf_ref, D_B), (lfb_ref, lbb_ref, 2 * D_B)):
        lb = lb_ref[...]
        f = lb + (1.0 - lb) * jax.nn.sigmoid(proj(base + off))
        dst[...] = jnp.log(f)
    ib_ref[...] = proj(base + 3 * D_B).astype(BF16)
    og = proj(base + 4 * D_B)
    og_ref[...] = (og * jax.nn.sigmoid(og)).astype(BF16)
    for p in range(2 * D_MODEL // PIECE):
        c = p * PIECE
        gt_ref[:, c:c + PIECE] = jax.nn.sigmoid(proj(D_IN + c)).astype(BF16)


def _inproj(x2, b, s, nw, w_all, qg, kg, e_mat, lbf, lbb):
    t = x2.shape[0]
    tm = TM_IN
    tps = s // tm
    n_all = w_all.shape[1]
    row = lambda i: (i, 0)
    const = lambda i: (0, 0)
    seq = lambda i: (i // tps, 0, i % tps, 0)
    widths = (D_B, D_B, D_B, D_B, D_B, 2 * D_MODEL)
    dtypes = (BF16, F32, F32, BF16, BF16, BF16)
    grp_shapes = [jax.ShapeDtypeStruct((b, dil, s // dil, GROUP_W), BF16) for _, dil in ATTN_GROUPS] * 3
    grp_specs = [pl.BlockSpec((1, dil, tm // dil, GROUP_W), seq) for _, dil in ATTN_GROUPS] * 3
    return pl.pallas_call(
        functools.partial(_inproj_kernel, tm=tm),
        out_shape=tuple(grp_shapes) + tuple(jax.ShapeDtypeStruct((t, w), dt) for w, dt in zip(widths, dtypes)),
        grid=(t // tm,),
        in_specs=[
            pl.BlockSpec((tm, D_MODEL), row),
            pl.BlockSpec((1, D_MODEL), const),
            pl.BlockSpec((D_MODEL, n_all), const),
            pl.BlockSpec((1, W_A), const),
            pl.BlockSpec((1, W_A), const),
            pl.BlockSpec((PIECE, PIECE), const),
            pl.BlockSpec((1, D_B), const),
            pl.BlockSpec((1, D_B), const),
        ],
        out_specs=tuple(grp_specs) + tuple(pl.BlockSpec((tm, w), row) for w in widths),
        scratch_shapes=[pltpu.VMEM((GROUP_W // LANES, tm, LANES), F32)],
        compiler_params=_cparams(("parallel",)),
        name="inproj",
    )(x2, nw, w_all, qg, kg, e_mat, lbf, lbb)


def _attn_kernel(q_ref, kp_ref, kc_ref, kn_ref, vp_ref, vc_ref, vn_ref, bias_ref,
                 o_ref, lse_ref, *, tq, sub_len):
    i = pl.program_id(2)
    sq = bias_ref.shape[1] // 2
    nk = sq + 2 * N_SIDE
    lane = lax.broadcasted_iota(jnp.int32, (sq, LANES), 1)
    low = lane < HEAD_DIM_A
    ones = jnp.ones((nk, LANES), BF16)
    zero = jnp.zeros((sq, LANES), BF16)

    def pair(res, kk, vv, j, pr, colbias):
        cols = slice(pr * LANES, (pr + 1) * LANES)
        q2 = q_ref[res, j * sq:(j + 1) * sq, cols]
        k2 = kk[j * sq:j * sq + nk, cols]
        v2 = jnp.concatenate([vv[j * sq:j * sq + nk, cols], ones], axis=1)
        q_st = jnp.concatenate([jnp.where(low, q2, zero), jnp.where(low, zero, q2)], axis=0)
        s = lax.dot_general(q_st, k2, (((1,), (1,)), ((), ())), preferred_element_type=F32)
        yield
        s = s + bias_ref[pr]
        if colbias is not None:
            s = s + colbias
        m = jnp.max(s, axis=-1, keepdims=True)
        p = jnp.exp2(s - m).astype(BF16)
        r = jnp.dot(p, v2, preferred_element_type=F32)
        yield
        o2 = jnp.where(low, r[:sq, :LANES] / r[:sq, LANES:], r[sq:, :LANES] / r[sq:, LANES:])
        o_ref[res, j * sq:(j + 1) * sq, cols] = o2.astype(o_ref.dtype)
        lse = (m + jnp.log2(r[:, LANES:LANES + 1])) * LN2
        lse_ref[res, j * sq:(j + 1) * sq, cols] = jnp.where(low, lse[:sq], lse[sq:])

    n_pairs = HEADS_PER_GROUP // 2
    for res in range(q_ref.shape[0]):
        kk = jnp.concatenate([kp_ref[res], kc_ref[res], kn_ref[res]], axis=0)
        vv = jnp.concatenate([vp_ref[res], vc_ref[res], vn_ref[res]], axis=0)
        n_sub = tq // sq
        for j in range(n_sub):
            colbias = None
            if j in (0, n_sub - 1):
                kpos = i * tq + j * sq - N_SIDE + lax.broadcasted_iota(jnp.int32, (1, nk), 1)
                colbias = jnp.where((kpos >= 0) & (kpos < sub_len), 0.0, NEG).astype(F32)
            for pr in range(0, n_pairs, ATTN_PAIRS_IN_FLIGHT):
                _lockstep(*[pair(res, kk, vv, j, pr + d, colbias) for d in range(ATTN_PAIRS_IN_FLIGHT)])


def _attn_bias(sq, dil, slopes):
    nk = sq + 2 * N_SIDE
    rel = np.arange(nk)[None, :] - N_SIDE - np.arange(sq)[:, None]
    band = np.abs(rel) <= N_SIDE
    alibi = -slopes[:, None, None] * (dil * np.abs(rel)).astype(np.float32)[None]
    bias = np.where(band[None], alibi * LOG2E, NEG).astype(np.float32)
    return jnp.asarray(bias.reshape(HEADS_PER_GROUP // 2, 2 * sq, nk))


def _attention_group(q, k, v, g):
    b, dil, sub_len, _ = q.shape
    tq = min(TQ, sub_len)
    sq = min(SQ, sub_len)
    hb = tq // N_SIDE
    n_halo = sub_len // N_SIDE
    slopes = (2.0 ** (-8.0 * (np.arange(N_HEADS_A) + 1) / N_HEADS_A)).astype(np.float32)
    bias = _attn_bias(sq, dil, slopes[g * HEADS_PER_GROUP:(g + 1) * HEADS_PER_GROUP])
    cur = lambda bi, r, i: (bi, r, i, 0)
    prev = lambda bi, r, i: (bi, r, jnp.maximum(i * hb - 1, 0), 0)
    nxt = lambda bi, r, i: (bi, r, jnp.minimum((i + 1) * hb, n_halo - 1), 0)
    n_res = max(1, min(dil, TQ // tq))
    blk_q = (None, n_res, tq, GROUP_W)
    blk_h = (None, n_res, N_SIDE, GROUP_W)
    return pl.pallas_call(
        functools.partial(_attn_kernel, tq=tq, sub_len=sub_len),
        out_shape=(jax.ShapeDtypeStruct((b, dil, sub_len, GROUP_W), BF16),
                   jax.ShapeDtypeStruct((b, dil, sub_len, GROUP_W), F32)),
        grid=(b, dil // n_res, sub_len // tq),
        in_specs=[
            pl.BlockSpec(blk_q, cur),
            pl.BlockSpec(blk_h, prev), pl.BlockSpec(blk_q, cur), pl.BlockSpec(blk_h, nxt),
            pl.BlockSpec(blk_h, prev), pl.BlockSpec(blk_q, cur), pl.BlockSpec(blk_h, nxt),
            pl.BlockSpec(bias.shape, lambda bi, r, i: (0, 0, 0)),
        ],
        out_specs=(pl.BlockSpec(blk_q, cur), pl.BlockSpec(blk_q, cur)),
        compiler_params=_cparams(("parallel", "parallel", "parallel")),
        name=f"attn_d{dil}",
    )(q, k, k, k, v, v, v, bias)


def _hgrn_kernel(qf_ref, lf_ref, vf_ref, qr_ref, lr_ref, vr_ref, of_ref, or_ref,
                 sf_ref, sr_ref, qif_ref, qir_ref, uf_ref, ur_ref, df_ref, dr_ref, *, ts):
    c_len = HGRN_CHUNK
    nc = ts // c_len

    @pl.when(pl.program_id(1) == 0)
    def _():
        sf_ref[...] = jnp.zeros_like(sf_ref)
        sr_ref[...] = jnp.zeros_like(sr_ref)

    r_i = lax.broadcasted_iota(jnp.int32, (c_len, c_len), 0)
    c_i = lax.broadcasted_iota(jnp.int32, (c_len, c_len), 1)
    lower = r_i >= c_i
    upper = r_i <= c_i
    tri_f = jnp.where(lower, 1.0, 0.0).astype(BF16)
    tri_r = jnp.where(upper, 1.0, 0.0).astype(BF16)

    def cumsum(tri, x):
        hi = x.astype(BF16)
        r1 = x - hi.astype(F32)
        mid = r1.astype(BF16)
        lo = (r1 - mid.astype(F32)).astype(BF16)
        d = lambda a: jnp.dot(tri, a, preferred_element_type=F32)
        return d(hi) + d(mid) + d(lo)

    n_sub = c_len // HGRN_SUB
    shift = HGRN_SUB.bit_length() - 1
    sub_f = jnp.right_shift(lax.broadcasted_iota(jnp.int32, (c_len, D_B), 0), shift)
    sub_tf = jnp.right_shift(lax.broadcasted_iota(jnp.int32, (D_B, c_len), 1), shift)
    t_i = lax.broadcasted_iota(jnp.int32, (2 * c_len, c_len), 0) & (c_len - 1)
    s_i = lax.broadcasted_iota(jnp.int32, (2 * c_len, c_len), 1)
    same_sub2 = jnp.right_shift(t_i, shift) == jnp.right_shift(s_i, shift)
    lower2, upper2 = t_i >= s_i, t_i <= s_i
    lower_sub2, upper_sub2 = same_sub2 & lower2, same_sub2 & upper2
    row_head = lax.broadcasted_iota(jnp.int32, (2 * c_len, 2 * DK_B), 0) // c_len
    col_head = lax.broadcasted_iota(jnp.int32, (2 * c_len, 2 * DK_B), 1) // DK_B
    pair_cols = jnp.where(row_head == col_head, 1.0, 0.0).astype(BF16)
    pair_off = jnp.concatenate([pair_cols] * (n_sub - 1), axis=1)

    def intra(q_ref, l_ref, v_ref, o_ref, qi_ref, u_ref, d_ref, c, rev):
        rows = pl.ds(pl.multiple_of(c * c_len, c_len), c_len)
        lf = l_ref[0, rows, :]
        q = q_ref[0, rows, :].astype(F32)
        v = v_ref[0, rows, :]
        k = 1.0 - jnp.exp(lf)
        b = cumsum(tri_r if rev else tri_f, lf)
        yield
        sub = (n_sub - 1 - sub_f) if rev else sub_f

        def b_at(pos):
            r = c_len - 1 - pos if rev else pos
            return b[r:r + 1, :]

        def per_sub(vals):
            vals = vals[::-1] if rev else vals
            return jnp.concatenate([jnp.broadcast_to(x, (HGRN_SUB, D_B)) for x in vals], axis=0)

        a_end = [b_at(HGRN_SUB * j + HGRN_SUB - 1) for j in range(n_sub)]
        a_start = [jnp.zeros((1, D_B), F32)] + a_end[:-1]
        btot = a_end[-1]
        end_full, start_full = per_sub(a_end), per_sub(a_start)
        stores = [(qi_ref, (rows, slice(None)), (q * jnp.exp(b)).astype(BF16))]
        q_off = [jnp.where(sub > j, q * jnp.exp(jnp.minimum(b - a_end[j], 0.0)), 0.0).astype(BF16)
                 for j in range(n_sub - 1)]
        k_end = k * jnp.exp(end_full - b)
        sub_t = (n_sub - 1 - sub_tf) if rev else sub_tf
        q_dia = (q * jnp.exp(b - start_full)).astype(BF16)
        k_upd = (k * jnp.exp(btot - b)).astype(BF16)
        k_end_t = k_end.T
        k_dia_t = (k * jnp.exp(jnp.minimum(start_full - b, HGRN_EXP_CLAMP))).T.astype(BF16)
        k_off_t = [jnp.where(sub_t == j, k_end_t, 0.0).astype(BF16) for j in range(n_sub - 1)]
        dec8_t = jnp.broadcast_to(jnp.exp(btot), (SUBLANES, D_B)).T
        keep, keep_d = (upper2, upper_sub2) if rev else (lower2, lower_sub2)
        two = lambda a: jnp.concatenate([a, a], axis=0)
        for p in range(N_HEADS_B // 2):
            ps = slice(2 * p * DK_B, (2 * p + 2) * DK_B)
            qc = two(jnp.concatenate([q_off[j][:, ps] for j in range(n_sub - 1)], axis=1)) * pair_off
            kc = jnp.concatenate([k_off_t[j][ps, :] for j in range(n_sub - 1)], axis=0)
            yield
            sc = jnp.dot(qc, kc, preferred_element_type=F32)
            sc_d = jnp.dot(two(q_dia[:, ps]) * pair_cols, k_dia_t[ps, :], preferred_element_type=F32)
            u2 = lax.dot_general(k_upd[:, ps], v[:, ps], (((0,), (0,)), ((), ())),
                                 preferred_element_type=F32)
            yield
            sc = (jnp.where(keep, sc, 0.0) + jnp.where(keep_d, sc_d, 0.0)).astype(BF16)
            o2 = jnp.dot(sc, v[:, ps], preferred_element_type=F32)
            for i in range(2):
                h = 2 * p + i
                hs = slice(h * DK_B, (h + 1) * DK_B)
                blk = slice(i * DK_B, (i + 1) * DK_B)
                stores.append((o_ref, (0, rows, hs), o2[i * c_len:(i + 1) * c_len, blk]))
                stores.append((u_ref, (c, h), u2[blk, blk]))
                stores.append((d_ref, (c, h), jnp.broadcast_to(dec8_t[hs, 0:1], (DK_B, DK_B))))
        return stores

    def carry_state(o_ref, qi_ref, u_ref, d_ref, s_ref, c):
        rows = pl.ds(pl.multiple_of(c * c_len, c_len), c_len)
        zero = jnp.zeros((DK_B, DK_B), BF16)
        stores = []
        for p in range(N_HEADS_B // 2):
            ps = slice(2 * p * DK_B, (2 * p + 2) * DK_B)
            sa, sb = s_ref[2 * p], s_ref[2 * p + 1]
            s_bd = jnp.concatenate([jnp.concatenate([sa.astype(BF16), zero], axis=1),
                                    jnp.concatenate([zero, sb.astype(BF16)], axis=1)], axis=0)
            o_new = o_ref[0, rows, ps] + jnp.dot(qi_ref[rows, ps], s_bd, preferred_element_type=F32)
            stores.append((o_ref, (0, rows, ps), o_new))
            stores.append((s_ref, (2 * p,), sa * d_ref[c, 2 * p] + u_ref[c, 2 * p]))
            stores.append((s_ref, (2 * p + 1,), sb * d_ref[c, 2 * p + 1] + u_ref[c, 2 * p + 1]))
        return stores

    def commit(stores):
        for ref, idx, val in stores:
            ref[idx] = val

    def intra_body(trip, carry):
        gens = []
        for i in range(HGRN_CHUNKS_PER_TRIP):
            c = trip * HGRN_CHUNKS_PER_TRIP + i
            gens.append(intra(qf_ref, lf_ref, vf_ref, of_ref, qif_ref, uf_ref, df_ref, c, False))
            gens.append(intra(qr_ref, lr_ref, vr_ref, or_ref, qir_ref, ur_ref, dr_ref, c, True))
        commit(sum(_lockstep(*gens), []))
        return carry

    def state_body(c, carry):
        commit(carry_state(of_ref, qif_ref, uf_ref, df_ref, sf_ref, c)
               + carry_state(or_ref, qir_ref, ur_ref, dr_ref, sr_ref, nc - 1 - c))
        return carry

    lax.fori_loop(0, nc // HGRN_CHUNKS_PER_TRIP, intra_body, 0)
    lax.fori_loop(0, nc, state_body, 0)


def _hgrn(qb, lff, lfb, ib):
    b, s, _ = qb.shape
    ts = min(TS_HGRN, s)
    nt = s // ts
    nc = ts // HGRN_CHUNK
    fwd = lambda bi, j: (bi, j, 0)
    rev = lambda bi, j: (bi, nt - 1 - j, 0)
    blk = (1, ts, D_B)
    return pl.pallas_call(
        functools.partial(_hgrn_kernel, ts=ts),
        out_shape=(jax.ShapeDtypeStruct((b, s, D_B), F32), jax.ShapeDtypeStruct((b, s, D_B), F32)),
        grid=(b, nt),
        in_specs=[pl.BlockSpec(blk, fwd), pl.BlockSpec(blk, fwd), pl.BlockSpec(blk, fwd),
                  pl.BlockSpec(blk, rev), pl.BlockSpec(blk, rev), pl.BlockSpec(blk, rev)],
        out_specs=(pl.BlockSpec(blk, fwd), pl.BlockSpec(blk, rev)),
        scratch_shapes=[pltpu.VMEM((N_HEADS_B, DK_B, DK_B), F32),
                        pltpu.VMEM((N_HEADS_B, DK_B, DK_B), F32),
                        pltpu.VMEM((ts, D_B), BF16),
                        pltpu.VMEM((ts, D_B), BF16),
                        pltpu.VMEM((nc, N_HEADS_B, DK_B, DK_B), F32),
                        pltpu.VMEM((nc, N_HEADS_B, DK_B, DK_B), F32),
                        pltpu.VMEM((nc, N_HEADS_B, DK_B, DK_B), F32),
                        pltpu.VMEM((nc, N_HEADS_B, DK_B, DK_B), F32)],
        compiler_params=_cparams(("parallel", "arbitrary")),
        name="hgrn",
    )(qb, lff, ib, qb, lfb, ib)


def _mix_kernel(x_ref, o1_ref, o2_ref, o3_ref, l1_ref, l2_ref, l3_ref, of_ref, ob_ref, og_ref,
                gt_ref, wa_ref, wb_ref, wo_ref, ogain_ref, nmoe_ref, wrh_ref, wrl_ref,
                br_ref, cnt_ref,
                h_ref, xt_ref, tw_ref, ps_ref, ct_ref, rt_ref, cnt_out_ref, run_ref, so_ref, *, tm):
    i = pl.program_id(0)

    @pl.when(i == 0)
    def _():
        run_ref[...] = cnt_ref[...]

    def token_major(src_ref, scr_ref, dil):
        if dil == 1:
            return src_ref[0].astype(F32)
        n_chunk = scr_ref.shape[0]
        for r in range(dil):
            blk = src_ref[r].astype(F32)
            for c in range(n_chunk):
                scr_ref[c, pl.ds(r, tm // dil, stride=dil), :] = blk[:, c * LANES:(c + 1) * LANES]
        return jnp.concatenate([scr_ref[c] for c in range(n_chunk)], axis=-1)

    dils = [dil for _, dil in ATTN_GROUPS]
    l1, l2, l3 = [token_major(r, so_ref, d) for r, d in zip((l1_ref, l2_ref, l3_ref), dils)]
    mx = jnp.maximum(jnp.maximum(l1, l2), l3)
    e1, e2, e3 = jnp.exp(l1 - mx), jnp.exp(l2 - mx), jnp.exp(l3 - mx)
    attn = (e1 * token_major(o1_ref, so_ref, dils[0]) + e2 * token_major(o2_ref, so_ref, dils[1])
            + e3 * token_major(o3_ref, so_ref, dils[2])) / (e1 + e2 + e3)

    o = of_ref[...] + ob_ref[...]
    parts = []
    for h in range(N_HEADS_B):
        oh = o[:, h * DK_B:(h + 1) * DK_B]
        ms = jnp.mean(oh * oh, axis=-1, keepdims=True)
        parts.append(oh * lax.rsqrt(ms + EPS))
    hg = jnp.concatenate(parts, axis=-1) * ogain_ref[...] * og_ref[...].astype(F32)

    attn_bf, hg_bf = attn.astype(BF16), hg.astype(BF16)
    tp = TM_MIX
    lane = lax.broadcasted_iota(jnp.int32, (tp, LANES), 1)
    r_i = lax.broadcasted_iota(jnp.int32, (tp, tp), 0)
    c_i = lax.broadcasted_iota(jnp.int32, (tp, tp), 1)
    tri = jnp.where(r_i > c_i, 1.0, 0.0).astype(BF16)
    e_r = lax.broadcasted_iota(jnp.int32, (LANES, LANES), 0)
    e_c = lax.broadcasted_iota(jnp.int32, (LANES, LANES), 1)
    before_e = jnp.where(e_r < e_c, 1.0, 0.0).astype(BF16)
    row_id = lax.broadcasted_iota(jnp.int32, (tp * TOP_K, tp), 0).astype(F32)

    def token_chain(part):
        rs = slice(part * tp, (part + 1) * tp)
        pa = jnp.dot(attn_bf[rs], wa_ref[...], preferred_element_type=F32)
        pb = jnp.dot(hg_bf[rs], wb_ref[...], preferred_element_type=F32)
        yield
        mixed = (gt_ref[rs, :D_MODEL].astype(F32) * pa + gt_ref[rs, D_MODEL:].astype(F32) * pb)
        h = x_ref[rs, :] + jnp.dot(mixed.astype(BF16), wo_ref[...], preferred_element_type=F32)
        yield
        h_ref[rs, :] = h
        ms = jnp.mean(h * h, axis=-1, keepdims=True)
        hn = h * lax.rsqrt(ms + EPS) * nmoe_ref[...]
        hi, lo = _split_bf16(hn)
        lg = (jnp.dot(hi, wrh_ref[...], preferred_element_type=F32)
              + jnp.dot(lo, wrh_ref[...], preferred_element_type=F32)
              + jnp.dot(hi, wrl_ref[...], preferred_element_type=F32)) + br_ref[...]
        yield
        vals, idxs = [], []
        onehot = jnp.zeros((tp, LANES), F32)
        for _ in range(TOP_K):
            m = jnp.max(lg, axis=-1, keepdims=True)
            idx = jnp.min(jnp.where(lg == m, lane, LANES), axis=-1, keepdims=True)
            sel = lane == idx
            onehot = jnp.where(sel, 1.0, onehot)
            lg = jnp.where(sel, NEG * 2, lg)
            vals.append(m)
            idxs.append(idx)
        exps = [jnp.exp(v - vals[0]) for v in vals]
        inv = 1.0 / (exps[0] + exps[1] + exps[2] + exps[3])
        local = jnp.dot(tri, onehot.astype(BF16), preferred_element_type=F32)
        cnt_tile = jnp.sum(onehot, axis=0, keepdims=True)
        off = jnp.dot(jnp.broadcast_to(cnt_tile, (8, LANES)).astype(BF16), before_e,
                      preferred_element_type=F32)[0:1]
        yield
        slot = off + local
        tw = jnp.zeros((tp, LANES), F32)
        ps = jnp.full((tp, LANES), -1.0, F32)
        for k in range(TOP_K):
            slot_k = jnp.sum(jnp.where(lane == idxs[k], slot, 0.0), axis=-1, keepdims=True)
            tw = jnp.where(lane == k, exps[k] * inv, tw)
            ps = jnp.where(lane == k, slot_k, ps)
        tw_ref[rs, :] = tw
        ps_ref[rs, :] = ps.astype(jnp.int32)
        ps_t = ps.T
        perm = jnp.zeros((tp * TOP_K, tp), F32)
        for k in range(TOP_K):
            perm = jnp.where(row_id == ps_t[k:k + 1, :], 1.0, perm)
        rows = jnp.dot(perm.astype(BF16), hi, preferred_element_type=F32)
        yield
        for c in range(D_MODEL // LANES):
            xt_ref[pl.ds(part * tp * TOP_K * ROW_SUB + c, tp * TOP_K, stride=ROW_SUB), :] = (
                rows[:, c * LANES:(c + 1) * LANES])
        return cnt_tile

    counts = _lockstep(*[token_chain(p) for p in range(tm // tp)])
    run = run_ref[...]
    for part, cnt_tile in enumerate(counts):
        ct_ref[part] = cnt_tile
        rt_ref[part] = run
        run = run + cnt_tile
    run_ref[...] = run
    cnt_out_ref[...] = run


def _mix(x2, s, o1, o2, o3, l1, l2, l3, of, ob, og, gt, wa, wb, wo, ogain, nmoe, wrh, wrl, br, cnt):
    t = x2.shape[0]
    tm = TM_MIX * MIX_TILES_PER_STEP
    tps = s // tm
    row = lambda i: (i, 0)
    const = lambda i: (0, 0)
    seq = lambda i: (i // tps, 0, i % tps, 0)
    rb = lambda w: pl.BlockSpec((tm, w), row)
    cb = lambda a: pl.BlockSpec(a.shape, const)
    gb = lambda a: pl.BlockSpec((None, a.shape[1], tm // a.shape[1], a.shape[3]), seq)
    per_tile = pl.BlockSpec((MIX_TILES_PER_STEP, 1, LANES), lambda i: (i, 0, 0))
    return pl.pallas_call(
        functools.partial(_mix_kernel, tm=tm),
        out_shape=(jax.ShapeDtypeStruct((t, D_MODEL), F32),
                   jax.ShapeDtypeStruct((t * TOP_K * ROW_SUB, LANES), F32),
                   jax.ShapeDtypeStruct((t, LANES), F32),
                   jax.ShapeDtypeStruct((t, LANES), jnp.int32),
                   jax.ShapeDtypeStruct((t // TM_MIX, 1, LANES), F32),
                   jax.ShapeDtypeStruct((t // TM_MIX, 1, LANES), F32),
                   jax.ShapeDtypeStruct((1, LANES), F32)),
        grid=(t // tm,),
        in_specs=[rb(D_MODEL), gb(o1), gb(o2), gb(o3), gb(l1), gb(l2), gb(l3),
                  rb(D_B), rb(D_B), rb(D_B), rb(2 * D_MODEL),
                  cb(wa), cb(wb), cb(wo), cb(ogain), cb(nmoe), cb(wrh), cb(wrl), cb(br), cb(cnt)],
        out_specs=(rb(D_MODEL), pl.BlockSpec((tm * TOP_K * ROW_SUB, LANES), row), rb(LANES), rb(LANES),
                   per_tile, per_tile, pl.BlockSpec((1, LANES), const)),
        scratch_shapes=[pltpu.VMEM((1, LANES), F32), pltpu.VMEM((GROUP_W // LANES, tm, LANES), F32)],
        compiler_params=_cparams(("arbitrary",)),
        name="mix",
    )(x2, o1, o2, o3, l1, l2, l3, of, ob, og, gt, wa, wb, wo, ogain, nmoe, wrh, wrl, br, cnt)


def _start_tile_segments(tile, off_tbl, row_tbl, len_tbl, make_piece):
    def segment(e, carry):
        sidx = tile * N_EXPERTS + e
        t0, r0, n = off_tbl[sidx], row_tbl[sidx], len_tbl[sidx]
        n_bulk = jnp.right_shift(n, SEG_CHUNK.bit_length() - 1)

        def bulk(j, c):
            make_piece(t0 + j * SEG_CHUNK, r0 + j * SEG_CHUNK, SEG_CHUNK).start(priority=0)
            return c

        lax.fori_loop(0, n_bulk, bulk, 0)
        done = n_bulk * SEG_CHUNK
        bit = SEG_CHUNK // 2
        while bit >= 1:
            has = (n & bit) != 0

            @pl.when(has)
            def _(done=done, bit=bit):
                make_piece(t0 + done, r0 + done, bit).start(priority=1)

            done = done + jnp.where(has, bit, 0)
            bit //= 2
        return carry

    lax.fori_loop(0, N_EXPERTS, segment, 0)


def _row_slice(row, n_rows):
    return pl.ds(pl.multiple_of(row * ROW_SUB, ROW_SUB), n_rows * ROW_SUB)


def _dispatch_kernel(off_tbl, row_tbl, len_tbl, fill_row, fill_len, xa_ref, xb_ref, xs_ref, buf_ref,
                     zero_ref, sems_in, sems_out, sem, *, tiles_a):
    i = pl.program_id(0)
    n_tiles = pl.num_programs(0)
    blk_rows = buf_ref.shape[1]

    def fetch(tile):
        slot = tile % DISPATCH_SLOTS

        def copy(src_ref, src_tile):
            start = pl.multiple_of(src_tile * blk_rows, blk_rows)
            pltpu.make_async_copy(src_ref.at[pl.ds(start, blk_rows)], buf_ref.at[slot],
                                  sems_in.at[slot]).start(priority=1)

        @pl.when(tile < tiles_a)
        def _():
            copy(xa_ref, tile)

        @pl.when(tile >= tiles_a)
        def _():
            copy(xb_ref, tile - tiles_a)

    def scatter_done(tile):
        slot = tile % DISPATCH_SLOTS
        pltpu.make_async_copy(buf_ref.at[slot], xs_ref.at[pl.ds(0, blk_rows)], sems_out.at[slot]).wait()

    def zeros_to(row, n_rows):
        return pltpu.make_async_copy(zero_ref.at[pl.ds(0, n_rows * ROW_SUB)],
                                     xs_ref.at[_row_slice(row, n_rows)], sem)

    @pl.when(i == 0)
    def _():
        fetch(i)
        zero_ref[...] = jnp.zeros_like(zero_ref)

        def fill(e, total):
            r0, n = fill_row[e], fill_len[e]
            n_blk = jnp.right_shift(n, BM.bit_length() - 1)

            def blocks(j, c):
                zeros_to(r0 + j * BM, BM).start()
                return c

            lax.fori_loop(0, n_blk, blocks, 0)
            done = n_blk * BM
            bit = BM // 2
            while bit >= 1:
                has = (n & bit) != 0

                @pl.when(has)
                def _(done=done, bit=bit):
                    zeros_to(r0 + done, bit).start()

                done = done + jnp.where(has, bit, 0)
                bit //= 2
            return total + n

        total = lax.fori_loop(0, N_EXPERTS + 1, fill, 0)
        bit = 1
        while bit * ROW_SUB <= xs_ref.shape[0]:
            @pl.when((total & bit) != 0)
            def _(bit=bit):
                n = bit * ROW_SUB
                pltpu.make_async_copy(xs_ref.at[pl.ds(0, n)], xs_ref.at[pl.ds(0, n)], sem).wait()

            bit *= 2

    @pl.when(i + 1 < n_tiles)
    def _():
        fetch(i + 1)

    slot = i % DISPATCH_SLOTS
    pltpu.make_async_copy(xa_ref.at[pl.ds(0, blk_rows)], buf_ref.at[slot], sems_in.at[slot]).wait()

    def piece(tile_row, expert_row, n_rows):
        return pltpu.make_async_copy(buf_ref.at[slot, _row_slice(tile_row, n_rows)],
                                     xs_ref.at[_row_slice(expert_row, n_rows)], sems_out.at[slot])

    _start_tile_segments(i, off_tbl, row_tbl, len_tbl, piece)

    @pl.when(i >= 1)
    def _():
        scatter_done(i - 1)

    @pl.when(i == n_tiles - 1)
    def _():
        scatter_done(i)


def _dispatch(off_tbl, row_tbl, len_tbl, fill_row, fill_len, xt_a, xt_b, n_rows):
    blk = TM_MIX * TOP_K * ROW_SUB
    tiles_a, tiles_b = xt_a.shape[0] // blk, xt_b.shape[0] // blk
    return pl.pallas_call(
        functools.partial(_dispatch_kernel, tiles_a=tiles_a),
        out_shape=jax.ShapeDtypeStruct((n_rows * ROW_SUB, LANES), F32),
        grid_spec=pltpu.PrefetchScalarGridSpec(
            num_scalar_prefetch=5,
            grid=(tiles_a + tiles_b,),
            in_specs=[pl.BlockSpec(memory_space=pl.ANY), pl.BlockSpec(memory_space=pl.ANY)],
            out_specs=pl.BlockSpec(memory_space=pl.ANY),
            scratch_shapes=[pltpu.VMEM((DISPATCH_SLOTS, blk, LANES), F32),
                            pltpu.VMEM((BM * ROW_SUB, LANES), F32),
                            pltpu.SemaphoreType.DMA((DISPATCH_SLOTS,)),
                            pltpu.SemaphoreType.DMA((DISPATCH_SLOTS,)),
                            pltpu.SemaphoreType.DMA],
        ),
        compiler_params=_cparams(("arbitrary",)),
        name="dispatch",
    )(off_tbl, row_tbl, len_tbl, fill_row, fill_len, xt_a, xt_b)


def _combine_kernel(off_tbl, row_tbl, len_tbl, ps_ref, tw_ref, h_ref, ys_ref, y_ref, buf_ref, sems, *, tm):
    i = pl.program_id(0)
    n_rows = tm * TOP_K

    def start(tile):
        slot = tile % 2

        def piece(tile_row, expert_row, n):
            return pltpu.make_async_copy(ys_ref.at[_row_slice(expert_row, n)],
                                         buf_ref.at[slot, _row_slice(tile_row, n)], sems.at[slot])

        _start_tile_segments(tile, off_tbl, row_tbl, len_tbl, piece)

    @pl.when(i == 0)
    def _():
        start(i)

    @pl.when(i + 1 < pl.num_programs(0))
    def _():
        start(i + 1)

    slot = i % 2
    yt_ref = buf_ref.at[slot]
    pltpu.make_async_copy(ys_ref.at[pl.ds(0, n_rows * ROW_SUB)], yt_ref, sems.at[slot]).wait()
    ysorted = jnp.concatenate(
        [yt_ref[pl.ds(c, n_rows, stride=ROW_SUB), :] for c in range(ROW_SUB)], axis=-1).astype(BF16)
    col = lax.broadcasted_iota(jnp.int32, (tm, n_rows), 1)
    ps = ps_ref[...]
    tw = tw_ref[...]
    pw = jnp.zeros((tm, n_rows), F32)
    for k in range(TOP_K):
        pw = jnp.where(col == ps[:, k:k + 1], tw[:, k:k + 1], pw)
    hi, lo = _split_bf16(pw)
    y_ref[...] = (h_ref[...] + jnp.dot(hi, ysorted, preferred_element_type=F32)
                  + jnp.dot(lo, ysorted, preferred_element_type=F32))


def _combine(off_tbl, row_tbl, len_tbl, ps, tw, h, ys):
    t = h.shape[0]
    tm = TM_MIX
    row = lambda i, *_: (i, 0)
    return pl.pallas_call(
        functools.partial(_combine_kernel, tm=tm),
        out_shape=jax.ShapeDtypeStruct((t, D_MODEL), F32),
        grid_spec=pltpu.PrefetchScalarGridSpec(
            num_scalar_prefetch=3,
            grid=(t // tm,),
            in_specs=[pl.BlockSpec((tm, LANES), row),
                      pl.BlockSpec((tm, LANES), row),
                      pl.BlockSpec((tm, D_MODEL), row),
                      pl.BlockSpec(memory_space=pl.ANY)],
            out_specs=pl.BlockSpec((tm, D_MODEL), row),
            scratch_shapes=[pltpu.VMEM((2, tm * TOP_K * ROW_SUB, LANES), F32),
                            pltpu.SemaphoreType.DMA((2,))],
        ),
        compiler_params=_cparams(("arbitrary",)),
        name="combine",
    )(off_tbl, row_tbl, len_tbl, ps, tw, h, ys)


def _experts_kernel(be_ref, nused_ref, xs_ref, wgu_ref, bgu_ref, wd_ref, bd_ref, ys_ref, wgu_bf, wd_bf):
    i = pl.program_id(0)
    used = i < nused_ref[0]
    new_expert = (i == 0) | (be_ref[i] != be_ref[jnp.maximum(i - 1, 0)])

    @pl.when(used & new_expert)
    def _():
        wgu_bf[...] = wgu_ref[0].astype(BF16)
        wd_bf[...] = wd_ref[0].astype(BF16)

    @pl.when(used)
    def _():
        x = jnp.concatenate(
            [xs_ref[pl.ds(c, BM, stride=ROW_SUB), :] for c in range(ROW_SUB)], axis=-1).astype(BF16)
        hh = jnp.dot(x, wgu_bf[...], preferred_element_type=F32) + bgu_ref[0]
        gate = jnp.minimum(hh[:, :D_FF], SWIGLU_LIMIT)
        up = jnp.clip(hh[:, D_FF:], -SWIGLU_LIMIT, SWIGLU_LIMIT)
        glu = gate * jax.nn.sigmoid(SWIGLU_ALPHA * gate)
        act = ((up + 1.0) * glu).astype(BF16)
        y = jnp.dot(act, wd_bf[...], preferred_element_type=F32) + bd_ref[0]
        for c in range(ROW_SUB):
            ys_ref[pl.ds(c, BM, stride=ROW_SUB), :] = y[:, c * LANES:(c + 1) * LANES]

    @pl.when(i >= nused_ref[0])
    def _():
        ys_ref[...] = jnp.zeros_like(ys_ref)


def _experts(block_e, nused, xs, wgu, bgu, wd, bd):
    nb = xs.shape[0] // (BM * ROW_SUB)
    emap3 = lambda i, be, nu: (be[i], 0, 0)
    rows = pl.BlockSpec((BM * ROW_SUB, LANES), lambda i, be, nu: (i, 0))
    return pl.pallas_call(
        _experts_kernel,
        out_shape=jax.ShapeDtypeStruct(xs.shape, F32),
        grid_spec=pltpu.PrefetchScalarGridSpec(
            num_scalar_prefetch=2,
            grid=(nb,),
            in_specs=[rows,
                      pl.BlockSpec((1, D_MODEL, 2 * D_FF), emap3),
                      pl.BlockSpec((1, 1, 2 * D_FF), emap3),
                      pl.BlockSpec((1, D_FF, D_MODEL), emap3),
                      pl.BlockSpec((1, 1, D_MODEL), emap3)],
            out_specs=rows,
            scratch_shapes=[pltpu.VMEM((D_MODEL, 2 * D_FF), BF16), pltpu.VMEM((D_FF, D_MODEL), BF16)],
        ),
        compiler_params=_cparams(("arbitrary",)),
        name="experts",
    )(block_e, nused, xs, wgu, bgu, wd, bd)


def _head_indicator(n_cols, head_dim):
    e = np.zeros((n_cols, LANES), np.float32)
    e[np.arange(n_cols), np.arange(n_cols) // head_dim] = 1.0
    return e


def _mixer(x, prm):
    b, s, d = x.shape
    t = b * s
    x2 = x.reshape(t, d)
    res = _inproj(x2, b, s, prm["norm_mix"], prm["w_all"], prm["q_gain"], prm["k_gain"], prm["e_in"],
                  prm["lb_f"], prm["lb_b"])
    n_grp = len(ATTN_GROUPS)
    qs, ks, vs = res[:n_grp], res[n_grp:2 * n_grp], res[2 * n_grp:3 * n_grp]
    qb, lff, lfb, ib, og, gt = res[3 * n_grp:]
    r3 = lambda a: a.reshape(b, s, a.shape[-1])
    outs, lses = [], []
    for g in range(n_grp):
        o, lse = _attention_group(qs[g], ks[g], vs[g], g)
        outs.append(o)
        lses.append(lse)
    of, ob = _hgrn(r3(qb), r3(lff), r3(lfb), r3(ib))
    return x2, outs, lses, of.reshape(t, D_B), ob.reshape(t, D_B), og, gt


def kernel(x_prompt, x_sample, norm_mix, w_in, q_gain, k_gain, hgrn_lb, hgrn_o_gain, w_gate, w_proj_a,
           w_proj_b, w_out, norm_moe, w_router, b_router, w_gu, b_gu, w_down, b_down):
    l = 0
    lb = jnp.cumsum(jax.nn.softmax(hgrn_lb.astype(F32), axis=1), axis=1)
    wr = jnp.zeros((D_MODEL, LANES), F32).at[:, :N_EXPERTS].set(w_router[l])
    wr_hi = wr.astype(BF16)
    prm = {
        "norm_mix": norm_mix[l].reshape(1, D_MODEL),
        "w_all": jnp.concatenate([w_in[l], w_gate[l]], axis=1).astype(BF16),
        "q_gain": q_gain[l].reshape(1, W_A) * (HEAD_DIM_A ** -0.5 * LOG2E),
        "k_gain": k_gain[l].reshape(1, W_A),
        "e_in": jnp.asarray(_head_indicator(PIECE, HEAD_DIM_A) @ _head_indicator(PIECE, HEAD_DIM_A).T, BF16),
        "lb_f": lb[0, l].reshape(1, D_B),
        "lb_b": lb[1, l].reshape(1, D_B),
    }
    wa, wb, wo = w_proj_a[l].astype(BF16), w_proj_b[l].astype(BF16), w_out[l].astype(BF16)
    ogain = hgrn_o_gain[l].reshape(1, D_B)
    nmoe = norm_moe[l].reshape(1, D_MODEL)
    wr_lo = (wr - wr_hi.astype(F32)).astype(BF16)
    br = jnp.full((1, LANES), NEG, F32).at[0, :N_EXPERTS].set(b_router[l])

    cnt = jnp.zeros((1, LANES), F32)
    per_batch = []
    for x in (x_prompt, x_sample):
        x2, outs, lses, of, ob, og, gt = _mixer(x, prm)
        h, xt, tw, ps, ct, rt, cnt = _mix(x2, x.shape[1], outs[0], outs[1], outs[2], lses[0], lses[1], lses[2],
                                          of, ob, og, gt, wa, wb, wo, ogain, nmoe, wr_hi, wr_lo, br, cnt)
        per_batch.append((x.shape, h, xt, tw, ps, ct, rt))

    n_tok = sum(pb[1].shape[0] for pb in per_batch)
    sizes = cnt[0, :N_EXPERTS].astype(jnp.int32)
    pad_sizes = (sizes + BM - 1) // BM * BM
    pad_ends = jnp.cumsum(pad_sizes)
    pad_starts = pad_ends - pad_sizes
    nb = (n_tok * TOP_K) // BM + N_EXPERTS
    block_start = jnp.arange(nb, dtype=jnp.int32) * BM
    block_e = jnp.minimum(jnp.sum(pad_ends[None, :] <= block_start[:, None], axis=1),
                          N_EXPERTS - 1).astype(jnp.int32)
    nused = (pad_ends[-1:] // BM).astype(jnp.int32)

    tables = []
    for _, _, _, _, _, ct, rt in per_batch:
        cnt_te = ct[:, 0, :N_EXPERTS].astype(jnp.int32)
        tile_off = jnp.cumsum(cnt_te, axis=1) - cnt_te
        expert_row = pad_starts[None, :] + rt[:, 0, :N_EXPERTS].astype(jnp.int32)
        tables.append((tile_off.reshape(-1), expert_row.reshape(-1), cnt_te.reshape(-1)))

    fill_row = jnp.concatenate([pad_starts + sizes, pad_ends[-1:]]).astype(jnp.int32)
    fill_len = jnp.concatenate([pad_sizes - sizes, nb * BM - pad_ends[-1:]]).astype(jnp.int32)
    both = [jnp.concatenate(cols) for cols in zip(*tables)]
    xs = _dispatch(*both, fill_row, fill_len, per_batch[0][2], per_batch[1][2], nb * BM)
    ys = _experts(block_e, nused, xs, w_gu[l], b_gu[l].reshape(N_EXPERTS, 1, 2 * D_FF),
                  w_down[l], b_down[l].reshape(N_EXPERTS, 1, D_MODEL))
    results = []
    for (shape, h, _, tw, ps, _, _), tbl in zip(per_batch, tables):
        results.append(_combine(*tbl, ps, tw, h, ys).reshape(shape))
    return tuple(results)
```
